```python
import math
import jax, jax.numpy as jnp
from jax import lax
import numpy as np

D_MODEL = 2048
BATCH = 8
SEQ = 2048
DEPTH = 4

CHUNK = 64
Q_BLOCK = 128
MIX_W = D_MODEL // 2
N_BRANCH = 3
MLA_HEADS = 8
MLA_NOPE = 128
MLA_ROPE = 64
MLA_QK = MLA_NOPE + MLA_ROPE
MLA_V = MIX_W // MLA_HEADS
Q_LORA = D_MODEL // 4
KV_LORA = D_MODEL // 8
ROPE_THETA = 10000.0
FOX_DH = 128
FOX_HEADS = MIX_W // FOX_DH
CH_DH = 128
CH_HEADS = MIX_W // CH_DH
LEFT_CHUNKS = 8
BAND = (LEFT_CHUNKS + 1) * CHUNK
REL_CLIP = 128
N_REL = 2 * REL_CLIP + 1
MEM_LEN = 256
X_HEADS = 4
X_DH = 128
D_FF = 4 * D_MODEL
EPS = 1e-6
NEG = -1e30

SPLIT_SIZES = (Q_LORA, KV_LORA, MLA_ROPE, 3 * FOX_HEADS * FOX_DH, FOX_HEADS,
               3 * CH_HEADS * CH_DH, N_BRANCH * D_MODEL)
D_IN = sum(SPLIT_SIZES)
SPLIT_CUTS = tuple(int(c) for c in np.cumsum(SPLIT_SIZES)[:-1])

kernel_name = 'hybrid_mla_fox_chunkrel_gated_encoder'


def rms_norm(x, g):
    xf = x.astype(jnp.float32)
    y = xf * lax.rsqrt(jnp.mean(xf * xf, axis=-1, keepdims=True) + EPS)
    return (y * g.astype(jnp.float32)).astype(x.dtype)


def rope_tables(seq):
    pos = jnp.arange(seq, dtype=jnp.float32)
    inv = ROPE_THETA ** (-jnp.arange(0, MLA_ROPE, 2, dtype=jnp.float32) / MLA_ROPE)
    ang = pos[:, None] * inv[None, :]
    return jnp.cos(ang), jnp.sin(ang)


def apply_rope(x, cos, sin):
    x1, x2 = jnp.split(x, 2, axis=-1)
    c = cos.astype(x.dtype)[None, :, None, :]
    s = sin.astype(x.dtype)[None, :, None, :]
    return jnp.concatenate([x1 * c - x2 * s, x1 * s + x2 * c], axis=-1)


def block_sweep_attention(q, k, v, cum=None):
    B, S, H, Dk = q.shape
    scale = Dk ** -0.5
    nq = S // Q_BLOCK
    qb = jnp.moveaxis(q.reshape(B, nq, Q_BLOCK, H, Dk), 1, 0)
    k_pos = jnp.arange(S)
    if cum is None:
        xs = (jnp.arange(nq), qb)
    else:
        cb = jnp.moveaxis(cum.reshape(B, nq, Q_BLOCK, H), 1, 0)
        cum_k = jnp.transpose(cum, (0, 2, 1))[:, :, None, :]
        xs = (jnp.arange(nq), qb, cb)

    def body(xs_):
        i, q_i = xs_[0], xs_[1]
        q_pos = i * Q_BLOCK + jnp.arange(Q_BLOCK)
        s = jnp.einsum('bqhd,bkhd->bhqk', q_i, k).astype(jnp.float32) * scale
        if cum is None:
            allowed = (k_pos // CHUNK)[None, :] <= (q_pos // CHUNK)[:, None]
        else:
            allowed = k_pos[None, :] <= q_pos[:, None]
            s = s + jnp.transpose(xs_[2], (0, 2, 1))[..., None] - cum_k
        s = jnp.where(allowed[None, None], s, NEG)
        p = jax.nn.softmax(s, axis=-1).astype(v.dtype)
        return jnp.einsum('bhqk,bkhd->bqhd', p, v)

    out = lax.map(body, xs)
    return jnp.moveaxis(out, 0, 1).reshape(B, S, H, v.shape[-1])


def mla_branch(c_q_raw, c_kv_raw, k_r_raw, g_cq, w_uq, g_ckv, w_ukv, g_qn, g_kn, cos, sin):
    B, S, _ = c_q_raw.shape
    q = (rms_norm(c_q_raw, g_cq) @ w_uq).reshape(B, S, MLA_HEADS, MLA_QK)
    kv = (rms_norm(c_kv_raw, g_ckv) @ w_ukv).reshape(B, S, MLA_HEADS, MLA_NOPE + MLA_V)
    k_nope, v = kv[..., :MLA_NOPE], kv[..., MLA_NOPE:]
    k_rope = jnp.broadcast_to(k_r_raw[:, :, None, :], (B, S, MLA_HEADS, MLA_ROPE))
    k = jnp.concatenate([k_nope, k_rope], axis=-1)
    q = rms_norm(q, g_qn)
    k = rms_norm(k, g_kn)
    q = jnp.concatenate([q[..., :MLA_NOPE], apply_rope(q[..., MLA_NOPE:], cos, sin)], axis=-1)
    k = jnp.concatenate([k[..., :MLA_NOPE], apply_rope(k[..., MLA_NOPE:], cos, sin)], axis=-1)
    o = block_sweep_attention(q, k, v)
    return o.reshape(B, S, MIX_W)


def fox_branch(qkv, f_logit, b_f, g_qn, g_kn):
    B, S, _ = qkv.shape
    qkv = qkv.reshape(B, S, 3, FOX_HEADS, FOX_DH)
    q = rms_norm(qkv[:, :, 0], g_qn)
    k = rms_norm(qkv[:, :, 1], g_kn)
    v = qkv[:, :, 2]
    log_f = jax.nn.log_sigmoid(f_logit.astype(jnp.float32) + b_f.astype(jnp.float32))
    cum = jnp.cumsum(log_f, axis=1)
    o = block_sweep_attention(q, k, v, cum)
    return o.reshape(B, S, MIX_W)


def chunk_band_branch(qkv, rel_bias, g_qn, g_kn):
    B, S, _ = qkv.shape
    n_chunks = S // CHUNK
    pad = LEFT_CHUNKS * CHUNK
    qkv = qkv.reshape(B, S, 3, CH_HEADS, CH_DH)
    q = rms_norm(qkv[:, :, 0], g_qn)
    k = rms_norm(qkv[:, :, 1], g_kn)
    v = qkv[:, :, 2]
    band_idx = (jnp.arange(n_chunks) * CHUNK)[:, None] + jnp.arange(BAND)[None, :]
    kp = jnp.pad(k, ((0, 0), (pad, 0), (0, 0), (0, 0)))
    vp = jnp.pad(v, ((0, 0), (pad, 0), (0, 0), (0, 0)))
    kb = kp[:, band_idx]
    vb = vp[:, band_idx]
    qc = q.reshape(B, n_chunks, CHUNK, CH_HEADS, CH_DH)
    s = jnp.einsum('bcqhd,bckhd->bchqk', qc, kb).astype(jnp.float32) * (CH_DH ** -0.5)
    rel = (jnp.arange(CHUNK)[:, None] + pad) - jnp.arange(BAND)[None, :]
    bias = rel_bias.astype(jnp.float32)[:, jnp.clip(rel, -REL_CLIP, REL_CLIP) + REL_CLIP]
    valid = band_idx >= pad
    s = jnp.where(valid[None, :, None, None, :], s + bias[None, None], NEG)
    p = jax.nn.softmax(s, axis=-1).astype(v.dtype)
    o = jnp.einsum('bchqk,bckhd->bcqhd', p, vb)
    return o.reshape(B, S, MIX_W)


def memory_cross_attention(h, mem_n, w_xq, w_xkv, g_qn, g_kn, w_xo):
    B, S, _ = h.shape
    M = mem_n.shape[1]
    q = rms_norm((h @ w_xq).reshape(B, S, X_HEADS, X_DH), g_qn)
    kv = (mem_n @ w_xkv).reshape(B, M, 2, X_HEADS, X_DH)
    k = rms_norm(kv[:, :, 0], g_kn)
    v = kv[:, :, 1]
    s = jnp.einsum('bshd,bmhd->bhsm', q, k).astype(jnp.float32) * (X_DH ** -0.5)
    p = jax.nn.softmax(s, axis=-1).astype(v.dtype)
    o = jnp.einsum('bhsm,bmhd->bshd', p, v).reshape(B, S, X_HEADS * X_DH)
    return o @ w_xo


def _fwd_setup_inputs(seed: int = 0) -> dict:
    key = jax.random.key(seed)
    ks = jax.random.split(key, 32)
    f32 = jnp.float32

    def nrm(k, shape, scale):
        return jax.random.normal(k, shape, f32) * scale

    def gain(k, shape):
        return 1.0 + 0.02 * jax.random.normal(k, shape, f32)

    L = DEPTH
    return {
        'x': nrm(ks[0], (BATCH, SEQ, D_MODEL), 1.0),
        'mem': nrm(ks[1], (BATCH, MEM_LEN, D_MODEL), 1.0),
        'g_mix': gain(ks[2], (L, D_MODEL)),
        'w_in': nrm(ks[3], (L, D_MODEL, D_IN), D_MODEL ** -0.5),
        'g_cq': gain(ks[4], (L, Q_LORA)),
        'w_uq': nrm(ks[5], (L, Q_LORA, MLA_HEADS * MLA_QK), Q_LORA ** -0.5),
        'g_ckv': gain(ks[6], (L, KV_LORA)),
        'w_ukv': nrm(ks[7], (L, KV_LORA, MLA_HEADS * (MLA_NOPE + MLA_V)), KV_LORA ** -0.5),
        'g_mla_q': gain(ks[8], (L, MLA_QK)),
        'g_mla_k': gain(ks[9], (L, MLA_QK)),
        'b_f': 3.0 + 0.1 * jax.random.normal(ks[10], (L, FOX_HEADS), f32),
        'g_fox_q': gain(ks[11], (L, FOX_DH)),
        'g_fox_k': gain(ks[12], (L, FOX_DH)),
        'rel_bias': nrm(ks[13], (L, CH_HEADS, N_REL), 0.5),
        'g_ch_q': gain(ks[14], (L, CH_DH)),
        'g_ch_k': gain(ks[15], (L, CH_DH)),
        'w_br': nrm(ks[16], (L, N_BRANCH, MIX_W, D_MODEL), MIX_W ** -0.5),
        'w_out': nrm(ks[17], (L, D_MODEL, D_MODEL), D_MODEL ** -0.5),
        'g_cross': gain(ks[18], (L, D_MODEL)),
        'g_mem': gain(ks[19], (L, D_MODEL)),
        'w_xq': nrm(ks[20], (L, D_MODEL, X_HEADS * X_DH), D_MODEL ** -0.5),
        'w_xkv': nrm(ks[21], (L, D_MODEL, 2 * X_HEADS * X_DH), D_MODEL ** -0.5),
        'g_x_q': gain(ks[22], (L, X_DH)),
        'g_x_k': gain(ks[23], (L, X_DH)),
        'w_xo': nrm(ks[24], (L, X_HEADS * X_DH, D_MODEL), (X_HEADS * X_DH) ** -0.5),
        'g_mlp': gain(ks[25], (L, D_MODEL)),
        'w_1': nrm(ks[26], (L, D_MODEL, D_FF), D_MODEL ** -0.5),
        'w_2': nrm(ks[27], (L, D_FF, D_MODEL), D_FF ** -0.5),
    }


def _fwd_reference(x, mem, g_mix, w_in, g_cq, w_uq, g_ckv, w_ukv, g_mla_q, g_mla_k, b_f,
              g_fox_q, g_fox_k, rel_bias, g_ch_q, g_ch_k, w_br, w_out, g_cross, g_mem,
              w_xq, w_xkv, g_x_q, g_x_k, w_xo, g_mlp, w_1, w_2):
    B, S, _ = x.shape
    cos, sin = rope_tables(S)
    for l in range(DEPTH):
        h = rms_norm(x, g_mix[l])
        z = h @ w_in[l]
        c_q, c_kv, k_r, fox_qkv, fox_f, ch_qkv, gate_logits = jnp.split(z, SPLIT_CUTS, axis=-1)
        y_a = mla_branch(c_q, c_kv, k_r, g_cq[l], w_uq[l], g_ckv[l], w_ukv[l],
                         g_mla_q[l], g_mla_k[l], cos, sin)
        y_b = fox_branch(fox_qkv, fox_f, b_f[l], g_fox_q[l], g_fox_k[l])
        y_c = chunk_band_branch(ch_qkv, rel_bias[l], g_ch_q[l], g_ch_k[l])
        ys = jnp.stack([y_a, y_b, y_c], axis=2)
        proj = jnp.einsum('bsnc,ncd->bsnd', ys, w_br[l])
        gates = jax.nn.sigmoid(gate_logits.astype(jnp.float32)).astype(x.dtype)
        gates = gates.reshape(B, S, N_BRANCH, D_MODEL)
        merged = jnp.einsum('bsnd,bsnd->bsd', gates, proj)
        x = x + merged @ w_out[l]
        x = x + memory_cross_attention(rms_norm(x, g_cross[l]), rms_norm(mem, g_mem[l]),
                                       w_xq[l], w_xkv[l], g_x_q[l], g_x_k[l], w_xo[l])
        hm = rms_norm(x, g_mlp[l])
        x = x + jnp.square(jax.nn.relu(hm @ w_1[l])) @ w_2[l]
    return x


import jax as _jax
import jax.numpy as _jnp

TWIN_FORMAT = 'train_step'
FWD_PARAMS = ['x', 'mem', 'g_mix', 'w_in', 'g_cq', 'w_uq', 'g_ckv', 'w_ukv', 'g_mla_q', 'g_mla_k', 'b_f', 'g_fox_q', 'g_fox_k', 'rel_bias', 'g_ch_q', 'g_ch_k', 'w_br', 'w_out', 'g_cross', 'g_mem', 'w_xq', 'w_xkv', 'g_x_q', 'g_x_k', 'w_xo', 'g_mlp', 'w_1', 'w_2']
TWIN_WEIGHTS = ['g_mix', 'w_in', 'g_cq', 'w_uq', 'g_ckv', 'w_ukv', 'g_mla_q', 'g_mla_k', 'b_f', 'g_fox_q', 'g_fox_k', 'rel_bias', 'g_ch_q', 'g_ch_k', 'w_br', 'w_out', 'g_cross', 'g_mem', 'w_xq', 'w_xkv', 'g_x_q', 'g_x_k', 'w_xo', 'g_mlp', 'w_1', 'w_2']
TWIN_DIFF_INPUT = 'x'
TWIN_INPUTS = ['x', 'mem', 'g_mix', 'w_in', 'g_cq', 'w_uq', 'g_ckv', 'w_ukv', 'g_mla_q', 'g_mla_k', 'b_f', 'g_fox_q', 'g_fox_k', 'rel_bias', 'g_ch_q', 'g_ch_k', 'w_br', 'w_out', 'g_cross', 'g_mem', 'w_xq', 'w_xkv', 'g_x_q', 'g_x_k', 'w_xo', 'g_mlp', 'w_1', 'w_2', 'loss_target', 'm_g_mix', 'm_w_in', 'm_g_cq', 'm_w_uq', 'm_g_ckv', 'm_w_ukv', 'm_g_mla_q', 'm_g_mla_k', 'm_b_f', 'm_g_fox_q', 'm_g_fox_k', 'm_rel_bias', 'm_g_ch_q', 'm_g_ch_k', 'm_w_br', 'm_w_out', 'm_g_cross', 'm_g_mem', 'm_w_xq', 'm_w_xkv', 'm_g_x_q', 'm_g_x_k', 'm_w_xo', 'm_g_mlp', 'm_w_1', 'm_w_2', 'v_g_mix', 'v_w_in', 'v_g_cq', 'v_w_uq', 'v_g_ckv', 'v_w_ukv', 'v_g_mla_q', 'v_g_mla_k', 'v_b_f', 'v_g_fox_q', 'v_g_fox_k', 'v_rel_bias', 'v_g_ch_q', 'v_g_ch_k', 'v_w_br', 'v_w_out', 'v_g_cross', 'v_g_mem', 'v_w_xq', 'v_w_xkv', 'v_g_x_q', 'v_g_x_k', 'v_w_xo', 'v_g_mlp', 'v_w_1', 'v_w_2']
TWIN_OUTPUTS = ['loss', 'grad_x', 'grad_g_mix', 'grad_w_in', 'grad_g_cq', 'grad_w_uq', 'grad_g_ckv', 'grad_w_ukv', 'grad_g_mla_q', 'grad_g_mla_k', 'grad_b_f', 'grad_g_fox_q', 'grad_g_fox_k', 'grad_rel_bias', 'grad_g_ch_q', 'grad_g_ch_k', 'grad_w_br', 'grad_w_out', 'grad_g_cross', 'grad_g_mem', 'grad_w_xq', 'grad_w_xkv', 'grad_g_x_q', 'grad_g_x_k', 'grad_w_xo', 'grad_g_mlp', 'grad_w_1', 'grad_w_2', 'delta_g_mix', 'delta_w_in', 'delta_g_cq', 'delta_w_uq', 'delta_g_ckv', 'delta_w_ukv', 'delta_g_mla_q', 'delta_g_mla_k', 'delta_b_f', 'delta_g_fox_q', 'delta_g_fox_k', 'delta_rel_bias', 'delta_g_ch_q', 'delta_g_ch_k', 'delta_w_br', 'delta_w_out', 'delta_g_cross', 'delta_g_mem', 'delta_w_xq', 'delta_w_xkv', 'delta_g_x_q', 'delta_g_x_k', 'delta_w_xo', 'delta_g_mlp', 'delta_w_1', 'delta_w_2', 'new_m_g_mix', 'new_m_w_in', 'new_m_g_cq', 'new_m_w_uq', 'new_m_g_ckv', 'new_m_w_ukv', 'new_m_g_mla_q', 'new_m_g_mla_k', 'new_m_b_f', 'new_m_g_fox_q', 'new_m_g_fox_k', 'new_m_rel_bias', 'new_m_g_ch_q', 'new_m_g_ch_k', 'new_m_w_br', 'new_m_w_out', 'new_m_g_cross', 'new_m_g_mem', 'new_m_w_xq', 'new_m_w_xkv', 'new_m_g_x_q', 'new_m_g_x_k', 'new_m_w_xo', 'new_m_g_mlp', 'new_m_w_1', 'new_m_w_2', 'new_v_g_mix', 'new_v_w_in', 'new_v_g_cq', 'new_v_w_uq', 'new_v_g_ckv', 'new_v_w_ukv', 'new_v_g_mla_q', 'new_v_g_mla_k', 'new_v_b_f', 'new_v_g_fox_q', 'new_v_g_fox_k', 'new_v_rel_bias', 'new_v_g_ch_q', 'new_v_g_ch_k', 'new_v_w_br', 'new_v_w_out', 'new_v_g_cross', 'new_v_g_mem', 'new_v_w_xq', 'new_v_w_xkv', 'new_v_g_x_q', 'new_v_g_x_k', 'new_v_w_xo', 'new_v_g_mlp', 'new_v_w_1', 'new_v_w_2']
TWIN_LEAF_KINDS = {'loss': 'loss', 'grad_x': 'grad_x', 'grad_g_mix': 'grad_w', 'grad_w_in': 'grad_w', 'grad_g_cq': 'grad_w', 'grad_w_uq': 'grad_w', 'grad_g_ckv': 'grad_w', 'grad_w_ukv': 'grad_w', 'grad_g_mla_q': 'grad_w', 'grad_g_mla_k': 'grad_w', 'grad_b_f': 'grad_w', 'grad_g_fox_q': 'grad_w', 'grad_g_fox_k': 'grad_w', 'grad_rel_bias': 'grad_w', 'grad_g_ch_q': 'grad_w', 'grad_g_ch_k': 'grad_w', 'grad_w_br': 'grad_w', 'grad_w_out': 'grad_w', 'grad_g_cross': 'grad_w', 'grad_g_mem': 'grad_w', 'grad_w_xq': 'grad_w', 'grad_w_xkv': 'grad_w', 'grad_g_x_q': 'grad_w', 'grad_g_x_k': 'grad_w', 'grad_w_xo': 'grad_w', 'grad_g_mlp': 'grad_w', 'grad_w_1': 'grad_w', 'grad_w_2': 'grad_w', 'delta_g_mix': 'delta_w', 'delta_w_in': 'delta_w', 'delta_g_cq': 'delta_w', 'delta_w_uq': 'delta_w', 'delta_g_ckv': 'delta_w', 'delta_w_ukv': 'delta_w', 'delta_g_mla_q': 'delta_w', 'delta_g_mla_k': 'delta_w', 'delta_b_f': 'delta_w', 'delta_g_fox_q': 'delta_w', 'delta_g_fox_k': 'delta_w', 'delta_rel_bias': 'delta_w', 'delta_g_ch_q': 'delta_w', 'delta_g_ch_k': 'delta_w', 'delta_w_br': 'delta_w', 'delta_w_out': 'delta_w', 'delta_g_cross': 'delta_w', 'delta_g_mem': 'delta_w', 'delta_w_xq': 'delta_w', 'delta_w_xkv': 'delta_w', 'delta_g_x_q': 'delta_w', 'delta_g_x_k': 'delta_w', 'delta_w_xo': 'delta_w', 'delta_g_mlp': 'delta_w', 'delta_w_1': 'delta_w', 'delta_w_2': 'delta_w', 'new_m_g_mix': 'new_m', 'new_m_w_in': 'new_m', 'new_m_g_cq': 'new_m', 'new_m_w_uq': 'new_m', 'new_m_g_ckv': 'new_m', 'new_m_w_ukv': 'new_m', 'new_m_g_mla_q': 'new_m', 'new_m_g_mla_k': 'new_m', 'new_m_b_f': 'new_m', 'new_m_g_fox_q': 'new_m', 'new_m_g_fox_k': 'new_m', 'new_m_rel_bias': 'new_m', 'new_m_g_ch_q': 'new_m', 'new_m_g_ch_k': 'new_m', 'new_m_w_br': 'new_m', 'new_m_w_out': 'new_m', 'new_m_g_cross': 'new_m', 'new_m_g_mem': 'new_m', 'new_m_w_xq': 'new_m', 'new_m_w_xkv': 'new_m', 'new_m_g_x_q': 'new_m', 'new_m_g_x_k': 'new_m', 'new_m_w_xo': 'new_m', 'new_m_g_mlp': 'new_m', 'new_m_w_1': 'new_m', 'new_m_w_2': 'new_m', 'new_v_g_mix': 'new_v', 'new_v_w_in': 'new_v', 'new_v_g_cq': 'new_v', 'new_v_w_uq': 'new_v', 'new_v_g_ckv': 'new_v', 'new_v_w_ukv': 'new_v', 'new_v_g_mla_q': 'new_v', 'new_v_g_mla_k': 'new_v', 'new_v_b_f': 'new_v', 'new_v_g_fox_q': 'new_v', 'new_v_g_fox_k': 'new_v', 'new_v_rel_bias': 'new_v', 'new_v_g_ch_q': 'new_v', 'new_v_g_ch_k': 'new_v', 'new_v_w_br': 'new_v', 'new_v_w_out': 'new_v', 'new_v_g_cross': 'new_v', 'new_v_g_mem': 'new_v', 'new_v_w_xq': 'new_v', 'new_v_w_xkv': 'new_v', 'new_v_g_x_q': 'new_v', 'new_v_g_x_k': 'new_v', 'new_v_w_xo': 'new_v', 'new_v_g_mlp': 'new_v', 'new_v_w_1': 'new_v', 'new_v_w_2': 'new_v'}


def _forward(args):
    return _fwd_reference(*[args[k] for k in FWD_PARAMS])


def _output_shape():
    out = _jax.eval_shape(lambda: _forward(_fwd_setup_inputs(0)))
    return out.shape, out.dtype

N_MICROBATCH = 1
ADAM_LR = 0.001
ADAM_B1 = 0.9
ADAM_B2 = 0.999
ADAM_EPS = 1e-08
ADAM_WD = 0.01
ADAM_STEP = 10
PER_EXAMPLE_BATCH_AXIS = {'x': 0, 'mem': 0, 'loss_target': 0}
SHARED_INPUTS = []
_WEIGHT_DTYPES = {'g_mix': _jnp.float32, 'w_in': _jnp.float32, 'g_cq': _jnp.float32, 'w_uq': _jnp.float32, 'g_ckv': _jnp.float32, 'w_ukv': _jnp.float32, 'g_mla_q': _jnp.float32, 'g_mla_k': _jnp.float32, 'b_f': _jnp.float32, 'g_fox_q': _jnp.float32, 'g_fox_k': _jnp.float32, 'rel_bias': _jnp.float32, 'g_ch_q': _jnp.float32, 'g_ch_k': _jnp.float32, 'w_br': _jnp.float32, 'w_out': _jnp.float32, 'g_cross': _jnp.float32, 'g_mem': _jnp.float32, 'w_xq': _jnp.float32, 'w_xkv': _jnp.float32, 'g_x_q': _jnp.float32, 'g_x_k': _jnp.float32, 'w_xo': _jnp.float32, 'g_mlp': _jnp.float32, 'w_1': _jnp.float32, 'w_2': _jnp.float32}
MOMENT_SCALE = {'g_mix': 4.967250e+00, 'w_in': 1.849998e+00, 'g_cq': 2.042616e-01, 'w_uq': 1.227414e-01, 'g_ckv': 8.694775e+00, 'w_ukv': 2.654862e+00, 'g_mla_q': 3.543035e-01, 'g_mla_k': 3.556621e-01, 'b_f': 1.435973e+01, 'g_fox_q': 1.117499e+00, 'g_fox_k': 1.115317e+00, 'rel_bias': 3.826637e-02, 'g_ch_q': 2.271098e-01, 'g_ch_k': 2.273907e-01, 'w_br': 2.622751e+00, 'w_out': 4.511875e+00, 'g_cross': 2.153061e-01, 'g_mem': 7.212420e-01, 'w_xq': 4.052155e-01, 'w_xkv': 9.990326e-01, 'g_x_q': 1.285292e+00, 'g_x_k': 1.285837e+00, 'w_xo': 6.552973e-01, 'g_mlp': 2.536520e+01, 'w_1': 2.666790e+00, 'w_2': 9.237594e+00}


def _to_microbatches(a, axis):
    t = _jnp.moveaxis(a, axis, 0)
    t = t.reshape((N_MICROBATCH, t.shape[0] // N_MICROBATCH) + t.shape[1:])
    return _jnp.moveaxis(t, 1, axis + 1)


def setup_inputs(seed: int = 0) -> dict:
    inp = _fwd_setup_inputs(seed)
    key = _jax.random.fold_in(_jax.random.key(seed), 7919)
    shape, _ = _output_shape()
    out = dict(inp)
    out["loss_target"] = _jax.random.normal(_jax.random.fold_in(key, 0), shape, _jnp.float32)
    for i, name in enumerate(TWIN_WEIGHTS):
        w = inp[name].astype(_jnp.float32)
        if MOMENT_SCALE is None:
            s = _jnp.sqrt(_jnp.mean(_jnp.square(w)) + 1e-30)
        else:
            s = MOMENT_SCALE[name]
        km, kv = _jax.random.split(_jax.random.fold_in(key, i + 1))
        out[name] = w
        out["m_" + name] = s * _jax.random.normal(km, w.shape, _jnp.float32)
        out["v_" + name] = (s * s) * _jax.random.uniform(kv, w.shape, _jnp.float32, 0.5, 1.5)
    if N_MICROBATCH > 1:
        for name, axis in PER_EXAMPLE_BATCH_AXIS.items():
            out[name] = _to_microbatches(out[name], axis)
    return {'x': out['x'], 'mem': out['mem'], 'g_mix': out['g_mix'], 'w_in': out['w_in'], 'g_cq': out['g_cq'], 'w_uq': out['w_uq'], 'g_ckv': out['g_ckv'], 'w_ukv': out['w_ukv'], 'g_mla_q': out['g_mla_q'], 'g_mla_k': out['g_mla_k'], 'b_f': out['b_f'], 'g_fox_q': out['g_fox_q'], 'g_fox_k': out['g_fox_k'], 'rel_bias': out['rel_bias'], 'g_ch_q': out['g_ch_q'], 'g_ch_k': out['g_ch_k'], 'w_br': out['w_br'], 'w_out': out['w_out'], 'g_cross': out['g_cross'], 'g_mem': out['g_mem'], 'w_xq': out['w_xq'], 'w_xkv': out['w_xkv'], 'g_x_q': out['g_x_q'], 'g_x_k': out['g_x_k'], 'w_xo': out['w_xo'], 'g_mlp': out['g_mlp'], 'w_1': out['w_1'], 'w_2': out['w_2'], 'loss_target': out['loss_target'], 'm_g_mix': out['m_g_mix'], 'm_w_in': out['m_w_in'], 'm_g_cq': out['m_g_cq'], 'm_w_uq': out['m_w_uq'], 'm_g_ckv': out['m_g_ckv'], 'm_w_ukv': out['m_w_ukv'], 'm_g_mla_q': out['m_g_mla_q'], 'm_g_mla_k': out['m_g_mla_k'], 'm_b_f': out['m_b_f'], 'm_g_fox_q': out['m_g_fox_q'], 'm_g_fox_k': out['m_g_fox_k'], 'm_rel_bias': out['m_rel_bias'], 'm_g_ch_q': out['m_g_ch_q'], 'm_g_ch_k': out['m_g_ch_k'], 'm_w_br': out['m_w_br'], 'm_w_out': out['m_w_out'], 'm_g_cross': out['m_g_cross'], 'm_g_mem': out['m_g_mem'], 'm_w_xq': out['m_w_xq'], 'm_w_xkv': out['m_w_xkv'], 'm_g_x_q': out['m_g_x_q'], 'm_g_x_k': out['m_g_x_k'], 'm_w_xo': out['m_w_xo'], 'm_g_mlp': out['m_g_mlp'], 'm_w_1': out['m_w_1'], 'm_w_2': out['m_w_2'], 'v_g_mix': out['v_g_mix'], 'v_w_in': out['v_w_in'], 'v_g_cq': out['v_g_cq'], 'v_w_uq': out['v_w_uq'], 'v_g_ckv': out['v_g_ckv'], 'v_w_ukv': out['v_w_ukv'], 'v_g_mla_q': out['v_g_mla_q'], 'v_g_mla_k': out['v_g_mla_k'], 'v_b_f': out['v_b_f'], 'v_g_fox_q': out['v_g_fox_q'], 'v_g_fox_k': out['v_g_fox_k'], 'v_rel_bias': out['v_rel_bias'], 'v_g_ch_q': out['v_g_ch_q'], 'v_g_ch_k': out['v_g_ch_k'], 'v_w_br': out['v_w_br'], 'v_w_out': out['v_w_out'], 'v_g_cross': out['v_g_cross'], 'v_g_mem': out['v_g_mem'], 'v_w_xq': out['v_w_xq'], 'v_w_xkv': out['v_w_xkv'], 'v_g_x_q': out['v_g_x_q'], 'v_g_x_k': out['v_g_x_k'], 'v_w_xo': out['v_w_xo'], 'v_g_mlp': out['v_g_mlp'], 'v_w_1': out['v_w_1'], 'v_w_2': out['v_w_2']}


def _loss(weights, diff, rest, loss_target):
    with _jax.named_scope("forward"):
        args = {**rest, TWIN_DIFF_INPUT: diff, **{k: w.astype(_WEIGHT_DTYPES[k]) for k, w in weights.items()}}
        y = _forward(args)
    with _jax.named_scope("loss_head"):
        err = _jnp.square(y.astype(_jnp.float32) - loss_target)
        return 0.5 * _jnp.sum(_jnp.mean(err, axis=-1)) if err.ndim else 0.5 * err


def _adamw(w, g, m, v):
    m = ADAM_B1 * m + (1.0 - ADAM_B1) * g
    v = ADAM_B2 * v + (1.0 - ADAM_B2) * _jnp.square(g)
    m_hat = m / (1.0 - ADAM_B1 ** ADAM_STEP)
    v_hat = v / (1.0 - ADAM_B2 ** ADAM_STEP)
    delta = -ADAM_LR * (m_hat / (_jnp.sqrt(v_hat) + ADAM_EPS) + ADAM_WD * w)
    return delta, m, v


def reference(x, mem, g_mix, w_in, g_cq, w_uq, g_ckv, w_ukv, g_mla_q, g_mla_k, b_f, g_fox_q, g_fox_k, rel_bias, g_ch_q, g_ch_k, w_br, w_out, g_cross, g_mem, w_xq, w_xkv, g_x_q, g_x_k, w_xo, g_mlp, w_1, w_2, loss_target, m_g_mix, m_w_in, m_g_cq, m_w_uq, m_g_ckv, m_w_ukv, m_g_mla_q, m_g_mla_k, m_b_f, m_g_fox_q, m_g_fox_k, m_rel_bias, m_g_ch_q, m_g_ch_k, m_w_br, m_w_out, m_g_cross, m_g_mem, m_w_xq, m_w_xkv, m_g_x_q, m_g_x_k, m_w_xo, m_g_mlp, m_w_1, m_w_2, v_g_mix, v_w_in, v_g_cq, v_w_uq, v_g_ckv, v_w_ukv, v_g_mla_q, v_g_mla_k, v_b_f, v_g_fox_q, v_g_fox_k, v_rel_bias, v_g_ch_q, v_g_ch_k, v_w_br, v_w_out, v_g_cross, v_g_mem, v_w_xq, v_w_xkv, v_g_x_q, v_g_x_k, v_w_xo, v_g_mlp, v_w_1, v_w_2):
    given = dict(x=x, mem=mem, g_mix=g_mix, w_in=w_in, g_cq=g_cq, w_uq=w_uq, g_ckv=g_ckv, w_ukv=w_ukv, g_mla_q=g_mla_q, g_mla_k=g_mla_k, b_f=b_f, g_fox_q=g_fox_q, g_fox_k=g_fox_k, rel_bias=rel_bias, g_ch_q=g_ch_q, g_ch_k=g_ch_k, w_br=w_br, w_out=w_out, g_cross=g_cross, g_mem=g_mem, w_xq=w_xq, w_xkv=w_xkv, g_x_q=g_x_q, g_x_k=g_x_k, w_xo=w_xo, g_mlp=g_mlp, w_1=w_1, w_2=w_2, loss_target=loss_target, m_g_mix=m_g_mix, m_w_in=m_w_in, m_g_cq=m_g_cq, m_w_uq=m_w_uq, m_g_ckv=m_g_ckv, m_w_ukv=m_w_ukv, m_g_mla_q=m_g_mla_q, m_g_mla_k=m_g_mla_k, m_b_f=m_b_f, m_g_fox_q=m_g_fox_q, m_g_fox_k=m_g_fox_k, m_rel_bias=m_rel_bias, m_g_ch_q=m_g_ch_q, m_g_ch_k=m_g_ch_k, m_w_br=m_w_br, m_w_out=m_w_out, m_g_cross=m_g_cross, m_g_mem=m_g_mem, m_w_xq=m_w_xq, m_w_xkv=m_w_xkv, m_g_x_q=m_g_x_q, m_g_x_k=m_g_x_k, m_w_xo=m_w_xo, m_g_mlp=m_g_mlp, m_w_1=m_w_1, m_w_2=m_w_2, v_g_mix=v_g_mix, v_w_in=v_w_in, v_g_cq=v_g_cq, v_w_uq=v_w_uq, v_g_ckv=v_g_ckv, v_w_ukv=v_w_ukv, v_g_mla_q=v_g_mla_q, v_g_mla_k=v_g_mla_k, v_b_f=v_b_f, v_g_fox_q=v_g_fox_q, v_g_fox_k=v_g_fox_k, v_rel_bias=v_rel_bias, v_g_ch_q=v_g_ch_q, v_g_ch_k=v_g_ch_k, v_w_br=v_w_br, v_w_out=v_w_out, v_g_cross=v_g_cross, v_g_mem=v_g_mem, v_w_xq=v_w_xq, v_w_xkv=v_w_xkv, v_g_x_q=v_g_x_q, v_g_x_k=v_g_x_k, v_w_xo=v_w_xo, v_g_mlp=v_g_mlp, v_w_1=v_w_1, v_w_2=v_w_2)
    weights = {n: given[n] for n in TWIN_WEIGHTS}
    shared = {n: given[n] for n in SHARED_INPUTS}
    per_example = {n: given[n] for n in ['x', 'mem']}
    grad_fn = _jax.value_and_grad(_loss, argnums=(0, 1))

    def one_microbatch(ex, loss_target):
        ex = dict(ex)
        diff = ex.pop(TWIN_DIFF_INPUT)
        return grad_fn(weights, diff, {**shared, **ex}, loss_target)

    if N_MICROBATCH == 1:
        loss, (grad_w, grad_x) = one_microbatch(per_example, given["loss_target"])
    else:
        def body(carry, xs):
            loss_sum, grad_sum = carry
            l_k, (gw_k, gx_k) = one_microbatch(xs[0], xs[1])
            with _jax.named_scope("update"):
                return (loss_sum + l_k, _jax.tree.map(_jnp.add, grad_sum, gw_k)), gx_k

        init = (_jnp.zeros((), _jnp.float32), _jax.tree.map(_jnp.zeros_like, weights))
        (loss, grad_w), grad_x = _jax.lax.scan(body, init, (per_example, given["loss_target"]))
    with _jax.named_scope("update"):
        delta_w, new_m, new_v = {}, {}, {}
        for n in TWIN_WEIGHTS:
            delta_w[n], new_m[n], new_v[n] = _adamw(weights[n], grad_w[n], given["m_" + n], given["v_" + n])
    return (loss, grad_x, *[grad_w[n] for n in TWIN_WEIGHTS], *[delta_w[n] for n in TWIN_WEIGHTS],
            *[new_m[n] for n in TWIN_WEIGHTS], *[new_v[n] for n in TWIN_WEIGHTS])
```

```python
import numpy as np
import jax
import jax.numpy as jnp
from jax import lax
from jax.experimental import pallas as pl
from jax.experimental.pallas import tpu as pltpu

F32, BF16 = jnp.float32, jnp.bfloat16
EPS = 1e-6
NEG = -1e30
LANE = 128
VMEM_LIMIT_BYTES = 56 * 2**20
MESH = pl.DeviceIdType.MESH

D_MODEL = 2048
CHUNK = 64
BAND = 9 * CHUNK
PAD = 8 * CHUNK
REL_CLIP = 128
MLA_HEADS, MLA_NOPE, MLA_ROPE, MLA_QK = 8, 128, 64, 192
N_HEADS = 8
X_HEADS = 4
ROPE_THETA = 10000.0
ADAM_LR, ADAM_B1, ADAM_B2, ADAM_EPS, ADAM_WD, ADAM_STEP = 0.001, 0.9, 0.999, 1e-08, 0.01, 10

Z_MAIN, Z_FOX, Z_CH, Z_GATE, Z_W = 0, 1024, 4096, 7168, 13312
KR_COL, FF_COL = 768, 896

BIG = ('w_in', 'w_uq', 'w_ukv', 'w_br', 'w_out', 'w_xq', 'w_xkv', 'w_xo', 'w_1', 'w_2')
COL_SHARDED = ('w_in', 'w_uq', 'w_ukv', 'w_br', 'w_xo', 'w_1')
SMALL = ('g_mix', 'g_cq', 'g_ckv', 'g_mla_q', 'g_mla_k', 'b_f', 'g_fox_q', 'g_fox_k', 'rel_bias', 'g_ch_q',
         'g_ch_k', 'g_cross', 'g_mem', 'g_x_q', 'g_x_k', 'g_mlp')
WEIGHTS = ('g_mix', 'w_in', 'g_cq', 'w_uq', 'g_ckv', 'w_ukv', 'g_mla_q', 'g_mla_k', 'b_f', 'g_fox_q', 'g_fox_k',
           'rel_bias', 'g_ch_q', 'g_ch_k', 'w_br', 'w_out', 'g_cross', 'g_mem', 'w_xq', 'w_xkv', 'g_x_q', 'g_x_k',
           'w_xo', 'g_mlp', 'w_1', 'w_2')


def _params(*sem):
    return pltpu.CompilerParams(dimension_semantics=sem, vmem_limit_bytes=VMEM_LIMIT_BYTES)


def _tile(dim, pref):
    if dim <= pref:
        return dim
    for t in range(pref - pref % LANE, 0, -LANE):
        if dim % t == 0:
            return t
    raise ValueError((dim, pref))


def mm(a, b, *, ta=False, tb=False, out_dtype=F32, epi=None, aux=None, name, tm=1024, tn=512, tk=2048):
    M, K = (a.shape[1], a.shape[0]) if ta else a.shape
    N = b.shape[0] if tb else b.shape[1]
    assert (b.shape[1] if tb else b.shape[0]) == K, (a.shape, b.shape, ta, tb)
    tm, tn, tk = _tile(M, tm), _tile(N, tn), _tile(K, tk)
    nk = K // tk
    dn = (((0 if ta else 1,), (1 if tb else 0,)), ((), ()))
    n_aux = 0 if aux is None else 1

    def finish(acc, aux_refs, o_refs):
        if epi is None:
            o_refs[0][...] = acc.astype(o_refs[0].dtype)
        elif epi == 'add':
            o_refs[0][...] = (acc + aux_refs[0][...]).astype(o_refs[0].dtype)
        elif epi == 'relu2':
            o_refs[0][...] = acc
            r = jnp.maximum(acc, 0.0)
            o_refs[1][...] = (r * r).astype(o_refs[1].dtype)
        elif epi == 'mul_drelu2':
            o_refs[0][...] = (acc * (2.0 * jnp.maximum(aux_refs[0][...], 0.0))).astype(o_refs[0].dtype)

    def body(a_ref, b_ref, *rest):
        aux_refs = rest[:n_aux]
        o_refs = rest[n_aux:n_aux + (2 if epi == 'relu2' else 1)]
        part = lax.dot_general(a_ref[...].astype(BF16), b_ref[...].astype(BF16), dn, preferred_element_type=F32)
        if nk == 1:
            finish(part, aux_refs, o_refs)
        else:
            acc_ref = rest[-1]
            k = pl.program_id(2)

            @pl.when(k == 0)
            def _():
                acc_ref[...] = part

            @pl.when(k > 0)
            def _():
                acc_ref[...] += part

            @pl.when(k == nk - 1)
            def _():
                finish(acc_ref[...], aux_refs, o_refs)

    a_spec = pl.BlockSpec((tk, tm), lambda i, j, k: (k, i)) if ta else pl.BlockSpec((tm, tk), lambda i, j, k: (i, k))
    b_spec = pl.BlockSpec((tn, tk), lambda i, j, k: (j, k)) if tb else pl.BlockSpec((tk, tn), lambda i, j, k: (k, j))
    o_spec = pl.BlockSpec((tm, tn), lambda i, j, k: (i, j))
    if epi == 'relu2':
        out_shape = (jax.ShapeDtypeStruct((M, N), F32), jax.ShapeDtypeStruct((M, N), out_dtype))
        out_specs = (o_spec, o_spec)
    else:
        out_shape, out_specs = jax.ShapeDtypeStruct((M, N), out_dtype), o_spec
    return pl.pallas_call(
        body, name=name, out_shape=out_shape, grid=(M // tm, N // tn, nk),
        in_specs=[a_spec, b_spec] + [o_spec] * n_aux, out_specs=out_specs,
        scratch_shapes=[pltpu.VMEM((tm, tn), F32)] if nk > 1 else [],
        compiler_params=_params("parallel", "parallel", "arbitrary"),
    )(a, b, *([aux] if n_aux else []))


def rmsnorm_fwd(x, g, *, col=0, width=None, out_dtype=BF16, name, ts=256):
    S = x.shape[0]
    width = x.shape[1] if width is None else width
    ts, cb = _tile(S, ts), col // width

    def body(x_ref, g_ref, o_ref):
        xf = x_ref[...]
        r = lax.rsqrt(jnp.mean(xf * xf, axis=-1, keepdims=True) + EPS)
        o_ref[...] = (xf * r * g_ref[...]).astype(o_ref.dtype)

    return pl.pallas_call(
        body, name=name, out_shape=jax.ShapeDtypeStruct((S, width), out_dtype), grid=(S // ts,),
        in_specs=[pl.BlockSpec((ts, width), lambda i: (i, cb)), pl.BlockSpec((1, width), lambda i: (0, 0))],
        out_specs=pl.BlockSpec((ts, width), lambda i: (i, 0)), compiler_params=_params("parallel"),
    )(x, g.reshape(1, width))


def rmsnorm_bwd(x, g, dy, *, col=0, width=None, res=None, name, ts=256):
    S = x.shape[0]
    width = x.shape[1] if width is None else width
    ts, cb = _tile(S, ts), col // width
    has_res = res is not None

    def body(x_ref, g_ref, dy_ref, *rest):
        dx_ref, dg_ref = rest[-2:]
        xf = x_ref[...]
        r = lax.rsqrt(jnp.mean(xf * xf, axis=-1, keepdims=True) + EPS)
        dyf = dy_ref[...].astype(F32)
        dyg = dyf * g_ref[...]
        dx = r * dyg - xf * (r * r * r) * jnp.mean(dyg * xf, axis=-1, keepdims=True)
        if has_res:
            dx = dx + rest[0][...]
        dx_ref[...] = dx
        part = jnp.sum(dyf * xf * r, axis=0, keepdims=True)

        @pl.when(pl.program_id(0) == 0)
        def _():
            dg_ref[...] = part

        @pl.when(pl.program_id(0) > 0)
        def _():
            dg_ref[...] += part

    blk = pl.BlockSpec((ts, width), lambda i: (i, 0))
    dx, dg = pl.pallas_call(
        body, name=name,
        out_shape=(jax.ShapeDtypeStruct((S, width), F32), jax.ShapeDtypeStruct((1, width), F32)), grid=(S // ts,),
        in_specs=[pl.BlockSpec((ts, width), lambda i: (i, cb)), pl.BlockSpec((1, width), lambda i: (0, 0)), blk]
        + ([blk] if has_res else []),
        out_specs=(blk, pl.BlockSpec((1, width), lambda i: (0, 0))), compiler_params=_params("arbitrary"),
    )(x, g.reshape(1, width), dy, *([res] if has_res else []))
    return dx, dg.reshape(width)


def headnorm_fwd(x, g, *, col, heads, name, ts=256):
    S = x.shape[0]
    ts, cb = _tile(S, ts), col // LANE

    def body(x_ref, g_ref, o_ref):
        xf = x_ref[...]
        r = lax.rsqrt(jnp.mean(xf * xf, axis=-1, keepdims=True) + EPS)
        o_ref[0] = (xf * r * g_ref[...]).astype(o_ref.dtype)

    return pl.pallas_call(
        body, name=name, out_shape=jax.ShapeDtypeStruct((heads, S, LANE), BF16), grid=(heads, S // ts),
        in_specs=[pl.BlockSpec((ts, LANE), lambda h, i: (i, cb + h)), pl.BlockSpec((1, LANE), lambda h, i: (0, 0))],
        out_specs=pl.BlockSpec((1, ts, LANE), lambda h, i: (h, i, 0)), compiler_params=_params("parallel", "parallel"),
    )(x, g.reshape(1, LANE))


def headnorm_bwd(x, g, dy, *, col, heads, name, ts=256):
    S = x.shape[0]
    ts, cb = _tile(S, ts), col // LANE

    def body(x_ref, g_ref, dy_ref, dx_ref, dg_ref):
        xf = x_ref[...]
        r = lax.rsqrt(jnp.mean(xf * xf, axis=-1, keepdims=True) + EPS)
        dyf = dy_ref[0]
        dyg = dyf * g_ref[...]
        dx_ref[...] = r * dyg - xf * (r * r * r) * jnp.mean(dyg * xf, axis=-1, keepdims=True)
        part = jnp.sum(dyf * xf * r, axis=0, keepdims=True)
        first = jnp.logical_and(pl.program_id(0) == 0, pl.program_id(1) == 0)

        @pl.when(first)
        def _():
            dg_ref[...] = part

        @pl.when(jnp.logical_not(first))
        def _():
            dg_ref[...] += part

    dx, dg = pl.pallas_call(
        body, name=name,
        out_shape=(jax.ShapeDtypeStruct((S, heads * LANE), F32), jax.ShapeDtypeStruct((1, LANE), F32)),
        grid=(heads, S // ts),
        in_specs=[pl.BlockSpec((ts, LANE), lambda h, i: (i, cb + h)), pl.BlockSpec((1, LANE), lambda h, i: (0, 0)),
                  pl.BlockSpec((1, ts, LANE), lambda h, i: (h, i, 0))],
        out_specs=(pl.BlockSpec((ts, LANE), lambda h, i: (i, h)), pl.BlockSpec((1, LANE), lambda h, i: (0, 0))),
        compiler_params=_params("arbitrary", "arbitrary"),
    )(x, g.reshape(1, LANE), dy)
    return dx, dg.reshape(LANE)


def _rope_tables(S):
    pos = jnp.arange(S, dtype=F32)
    inv = ROPE_THETA ** (-jnp.arange(0, MLA_ROPE, 2, dtype=F32) / MLA_ROPE)
    ang = pos[:, None] * inv[None, :]
    c, s, z = jnp.cos(ang), jnp.sin(ang), jnp.zeros((S, 64), F32)
    return jnp.concatenate([c, c, z], axis=1), jnp.concatenate([-s, s, z], axis=1)


def _rope(v, cos, ssin, lane):
    partner = jnp.where(lane < 32, pltpu.roll(v, 96, 1), pltpu.roll(v, 32, 1))
    return v * cos + partner * ssin


def mla_prep_fwd(xn, xr, g, cos, ssin, *, n_col, n_stride, r_col, r_stride, heads, name, ts=256):
    S = xn.shape[0]
    ts = _tile(S, ts)
    nb, ns, rb, rs = n_col // LANE, n_stride // LANE, r_col // LANE, r_stride // LANE
    gn = g[:MLA_NOPE].reshape(1, LANE)
    gr = jnp.concatenate([g[MLA_NOPE:], jnp.zeros((64,), F32)]).reshape(1, LANE)

    def body(n_ref, r_ref, gn_ref, gr_ref, c_ref, s_ref, o_ref):
        n, rr = n_ref[...], r_ref[...]
        ss = jnp.sum(n * n, axis=-1, keepdims=True) + jnp.sum(rr * rr, axis=-1, keepdims=True)
        r = lax.rsqrt(ss * (1.0 / MLA_QK) + EPS)
        lane = lax.broadcasted_iota(jnp.int32, rr.shape, 1)
        o_ref[0, :, :LANE] = (n * r * gn_ref[...]).astype(o_ref.dtype)
        o_ref[0, :, LANE:] = _rope(rr * r * gr_ref[...], c_ref[...], s_ref[...], lane).astype(o_ref.dtype)

    row = lambda h, i: (0, 0)
    return pl.pallas_call(
        body, name=name, out_shape=jax.ShapeDtypeStruct((heads, S, 2 * LANE), BF16), grid=(heads, S // ts),
        in_specs=[pl.BlockSpec((ts, LANE), lambda h, i: (i, nb + ns * h)),
                  pl.BlockSpec((ts, LANE), lambda h, i: (i, rb + rs * h)),
                  pl.BlockSpec((1, LANE), row), pl.BlockSpec((1, LANE), row),
                  pl.BlockSpec((ts, LANE), lambda h, i: (i, 0)), pl.BlockSpec((ts, LANE), lambda h, i: (i, 0))],
        out_specs=pl.BlockSpec((1, ts, 2 * LANE), lambda h, i: (h, i, 0)),
        compiler_params=_params("parallel", "parallel"),
    )(xn, xr, gn, gr, cos, ssin)


def mla_prep_bwd(xn, xr, g, cos, ssin, dy, *, n_col, n_stride, r_col, r_stride, heads, name, ts=256):
    S = xn.shape[0]
    ts = _tile(S, ts)
    nb, ns, rb, rs = n_col // LANE, n_stride // LANE, r_col // LANE, r_stride // LANE
    shared = r_stride == 0
    gn = g[:MLA_NOPE].reshape(1, LANE)
    gr = jnp.concatenate([g[MLA_NOPE:], jnp.zeros((64,), F32)]).reshape(1, LANE)

    def body(n_ref, r_ref, gn_ref, gr_ref, c_ref, s_ref, dy_ref, dn_ref, dr_ref, dgn_ref, dgr_ref):
        i, h = pl.program_id(0), pl.program_id(1)
        n, rr = n_ref[...], r_ref[...]
        ss = jnp.sum(n * n, axis=-1, keepdims=True) + jnp.sum(rr * rr, axis=-1, keepdims=True)
        r = lax.rsqrt(ss * (1.0 / MLA_QK) + EPS)
        lane = lax.broadcasted_iota(jnp.int32, rr.shape, 1)
        dyn = dy_ref[0, :, :LANE]
        dyr = dy_ref[0, :, LANE:]
        t = dyr * s_ref[...]
        dvr = dyr * c_ref[...] + jnp.where(lane < 32, pltpu.roll(t, 96, 1), pltpu.roll(t, 32, 1))
        dvr = jnp.where(lane < 64, dvr, 0.0)
        dgn_part = jnp.sum(dyn * n * r, axis=0, keepdims=True)
        dgr_part = jnp.sum(dvr * rr * r, axis=0, keepdims=True)
        dyn_g, dvr_g = dyn * gn_ref[...], dvr * gr_ref[...]
        proj = (jnp.sum(dyn_g * n, axis=-1, keepdims=True) + jnp.sum(dvr_g * rr, axis=-1, keepdims=True)) * (1.0 / MLA_QK)
        r3 = r * r * r
        dn_ref[...] = r * dyn_g - n * r3 * proj
        dr = r * dvr_g - rr * r3 * proj
        if shared:
            @pl.when(h == 0)
            def _():
                dr_ref[...] = dr

            @pl.when(h > 0)
            def _():
                dr_ref[...] += dr
        else:
            dr_ref[...] = dr
        first = jnp.logical_and(i == 0, h == 0)

        @pl.when(first)
        def _():
            dgn_ref[...] = dgn_part
            dgr_ref[...] = dgr_part

        @pl.when(jnp.logical_not(first))
        def _():
            dgn_ref[...] += dgn_part
            dgr_ref[...] += dgr_part

    row = lambda i, h: (0, 0)
    dr_cols = LANE if shared else heads * LANE
    dn, dr, dgn, dgr = pl.pallas_call(
        body, name=name,
        out_shape=(jax.ShapeDtypeStruct((S, heads * LANE), F32), jax.ShapeDtypeStruct((S, dr_cols), F32),
                   jax.ShapeDtypeStruct((1, LANE), F32), jax.ShapeDtypeStruct((1, LANE), F32)),
        grid=(S // ts, heads),
        in_specs=[pl.BlockSpec((ts, LANE), lambda i, h: (i, nb + ns * h)),
                  pl.BlockSpec((ts, LANE), lambda i, h: (i, rb + rs * h)),
                  pl.BlockSpec((1, LANE), row), pl.BlockSpec((1, LANE), row),
                  pl.BlockSpec((ts, LANE), lambda i, h: (i, 0)), pl.BlockSpec((ts, LANE), lambda i, h: (i, 0)),
                  pl.BlockSpec((1, ts, 2 * LANE), lambda i, h: (h, i, 0))],
        out_specs=(pl.BlockSpec((ts, LANE), lambda i, h: (i, h)),
                   pl.BlockSpec((ts, LANE), (lambda i, h: (i, 0)) if shared else (lambda i, h: (i, h))),
                   pl.BlockSpec((1, LANE), row), pl.BlockSpec((1, LANE), row)),
        compiler_params=_params("arbitrary", "arbitrary"),
    )(xn, xr, gn, gr, cos, ssin, dy)
    return dn, dr, jnp.concatenate([dgn.reshape(LANE), dgr.reshape(LANE)[:MLA_ROPE]])


def _probs(s, mode, i, bq, cq, ck):
    if mode == 'fox':
        s = s + cq - ck
    if mode != 'none':
        qpos = i * bq + lax.broadcasted_iota(jnp.int32, s.shape, 0)
        kpos = lax.broadcasted_iota(jnp.int32, s.shape, 1)
        ok = (kpos >> 6) <= (qpos >> 6) if mode == 'mla' else kpos <= qpos
        s = jnp.where(ok, s, NEG)
    e = jnp.exp(s - jnp.max(s, axis=-1, keepdims=True))
    return e * (1.0 / jnp.sum(e, axis=-1, keepdims=True))


_NT = (((1,), (1,)), ((), ()))
_TN = (((0,), (0,)), ((), ()))


def attn_fwd(q, k, v, *, v_col, mode, scale, cq=None, ck=None, name, bq=256):
    H, S, dk = q.shape
    Sk = k.shape[1]
    bq, vb = _tile(S, bq), v_col // LANE
    fox = mode == 'fox'

    def body(q_ref, k_ref, v_ref, *rest):
        o_ref = rest[-1]
        i = pl.program_id(1)
        s = lax.dot_general(q_ref[0], k_ref[0], _NT, preferred_element_type=F32) * scale
        p = _probs(s, mode, i, bq, rest[0][0] if fox else None, rest[1][0] if fox else None)
        o = jnp.dot(p.astype(BF16), v_ref[...].astype(BF16), preferred_element_type=F32)
        o_ref[...] = o.astype(o_ref.dtype)

    in_specs = [pl.BlockSpec((1, bq, dk), lambda h, i: (h, i, 0)), pl.BlockSpec((1, Sk, dk), lambda h, i: (h, 0, 0)),
                pl.BlockSpec((Sk, LANE), lambda h, i: (0, vb + h))]
    args = [q, k, v]
    if fox:
        in_specs += [pl.BlockSpec((1, bq, 1), lambda h, i: (h, i, 0)), pl.BlockSpec((1, 1, Sk), lambda h, i: (h, 0, 0))]
        args += [cq, ck]
    return pl.pallas_call(
        body, name=name, out_shape=jax.ShapeDtypeStruct((S, H * LANE), BF16), grid=(H, S // bq),
        in_specs=in_specs, out_specs=pl.BlockSpec((bq, LANE), lambda h, i: (i, h)),
        compiler_params=_params("parallel", "parallel"),
    )(*args)


def attn_bwd(q, k, v, do, *, v_col, mode, scale, cq=None, ck=None, name, bq=256):
    H, S, dk = q.shape
    Sk = k.shape[1]
    bq, vb = _tile(S, bq), v_col // LANE
    fox = mode == 'fox'

    def body(q_ref, k_ref, v_ref, do_ref, *rest):
        i = pl.program_id(1)
        if fox:
            cq_ref, ck_ref, dq_ref, dk_ref, dv_ref, dcq_ref, dck_ref = rest
        else:
            dq_ref, dk_ref, dv_ref = rest
        qb, kb, vv = q_ref[0], k_ref[0], v_ref[...].astype(BF16)
        s = lax.dot_general(qb, kb, _NT, preferred_element_type=F32) * scale
        p = _probs(s, mode, i, bq, cq_ref[0] if fox else None, ck_ref[0] if fox else None)
        dob = do_ref[...].astype(BF16)
        dv_part = lax.dot_general(p.astype(BF16), dob, _TN, preferred_element_type=F32)
        dp = lax.dot_general(dob, vv, _NT, preferred_element_type=F32)
        ds = p * (dp - jnp.sum(p * dp, axis=-1, keepdims=True))
        dsb = (ds * scale).astype(BF16)
        dq_ref[0] = jnp.dot(dsb, kb, preferred_element_type=F32)
        dk_part = lax.dot_general(dsb, qb, _TN, preferred_element_type=F32)
        if fox:
            dcq_ref[0] = jnp.sum(ds, axis=-1, keepdims=True)
            dck_part = -jnp.sum(ds, axis=0, keepdims=True)

        @pl.when(i == 0)
        def _():
            dk_ref[0] = dk_part
            dv_ref[...] = dv_part
            if fox:
                dck_ref[0] = dck_part

        @pl.when(i > 0)
        def _():
            dk_ref[0] += dk_part
            dv_ref[...] += dv_part
            if fox:
                dck_ref[0] += dck_part

    in_specs = [pl.BlockSpec((1, bq, dk), lambda h, i: (h, i, 0)), pl.BlockSpec((1, Sk, dk), lambda h, i: (h, 0, 0)),
                pl.BlockSpec((Sk, LANE), lambda h, i: (0, vb + h)), pl.BlockSpec((bq, LANE), lambda h, i: (i, h))]
    args = [q, k, v, do]
    out_shape = [jax.ShapeDtypeStruct((H, S, dk), F32), jax.ShapeDtypeStruct((H, Sk, dk), F32),
                 jax.ShapeDtypeStruct((Sk, H * LANE), F32)]
    out_specs = [pl.BlockSpec((1, bq, dk), lambda h, i: (h, i, 0)), pl.BlockSpec((1, Sk, dk), lambda h, i: (h, 0, 0)),
                 pl.BlockSpec((Sk, LANE), lambda h, i: (0, h))]
    if fox:
        fox_specs = [pl.BlockSpec((1, bq, 1), lambda h, i: (h, i, 0)), pl.BlockSpec((1, 1, Sk), lambda h, i: (h, 0, 0))]
        in_specs += fox_specs
        args += [cq, ck]
        out_shape += [jax.ShapeDtypeStruct((H, S, 1), F32), jax.ShapeDtypeStruct((H, 1, Sk), F32)]
        out_specs += fox_specs
    return pl.pallas_call(
        body, name=name, out_shape=tuple(out_shape), grid=(H, S // bq), in_specs=in_specs, out_specs=tuple(out_specs),
        compiler_params=_params("parallel", "arbitrary"),
    )(*args)


CPB = 4
BANDW = BAND + CHUNK


def _band_probs(qc, kb, bias, start, scale):
    s = lax.dot_general(qc, kb, _NT, preferred_element_type=F32) * scale
    col = lax.broadcasted_iota(jnp.int32, s.shape, 1)
    valid = jnp.logical_and(start + col >= PAD, col < BAND)
    s = jnp.where(valid, s + bias, NEG)
    e = jnp.exp(s - jnp.max(s, axis=-1, keepdims=True))
    return e * (1.0 / jnp.sum(e, axis=-1, keepdims=True))


def band_fwd(q, kp, vp, bias, *, scale, name):
    H, S, _ = q.shape
    Sp, rows = S + PAD + CHUNK, CPB * CHUNK

    def body(q_ref, k_ref, v_ref, b_ref, o_ref):
        j = pl.program_id(1)
        for cc in range(CPB):
            start = pl.multiple_of((j * CPB + cc) * CHUNK, CHUNK)
            kb = k_ref[0, pl.ds(start, BANDW), :]
            vb = v_ref[pl.ds(start, BANDW), :].astype(BF16)
            p = _band_probs(q_ref[0, cc * CHUNK:(cc + 1) * CHUNK, :], kb, b_ref[0], start, scale)
            o_ref[cc * CHUNK:(cc + 1) * CHUNK, :] = jnp.dot(p.astype(BF16), vb, preferred_element_type=F32).astype(o_ref.dtype)

    return pl.pallas_call(
        body, name=name, out_shape=jax.ShapeDtypeStruct((S, H * LANE), BF16), grid=(H, S // rows),
        in_specs=[pl.BlockSpec((1, rows, LANE), lambda h, j: (h, j, 0)), pl.BlockSpec((1, Sp, LANE), lambda h, j: (h, 0, 0)),
                  pl.BlockSpec((Sp, LANE), lambda h, j: (0, h)), pl.BlockSpec((1, CHUNK, BANDW), lambda h, j: (h, 0, 0))],
        out_specs=pl.BlockSpec((rows, LANE), lambda h, j: (j, h)), compiler_params=_params("parallel", "parallel"),
    )(q, kp, vp, bias)


def band_bwd(q, kp, vp, bias, do, *, scale, name):
    H, S, _ = q.shape
    Sp, rows = S + PAD + CHUNK, CPB * CHUNK

    def body(q_ref, k_ref, v_ref, b_ref, do_ref, dq_ref, dk_ref, dv_ref, db_ref):
        j = pl.program_id(1)

        @pl.when(j == 0)
        def _():
            dk_ref[...] = jnp.zeros_like(dk_ref)
            dv_ref[...] = jnp.zeros_like(dv_ref)
            db_ref[...] = jnp.zeros_like(db_ref)

        for cc in range(CPB):
            start = pl.multiple_of((j * CPB + cc) * CHUNK, CHUNK)
            sl = slice(cc * CHUNK, (cc + 1) * CHUNK)
            qc = q_ref[0, sl, :]
            kb = k_ref[0, pl.ds(start, BANDW), :]
            vb = v_ref[pl.ds(start, BANDW), :].astype(BF16)
            p = _band_probs(qc, kb, b_ref[0], start, scale)
            dob = do_ref[sl, :].astype(BF16)
            dv_ref[pl.ds(start, BANDW), :] += lax.dot_general(p.astype(BF16), dob, _TN, preferred_element_type=F32)
            dp = lax.dot_general(dob, vb, _NT, preferred_element_type=F32)
            ds = p * (dp - jnp.sum(p * dp, axis=-1, keepdims=True))
            db_ref[0] += ds
            dsb = (ds * scale).astype(BF16)
            dq_ref[0, sl, :] = jnp.dot(dsb, kb, preferred_element_type=F32)
            dk_ref[0, pl.ds(start, BANDW), :] += lax.dot_general(dsb, qc, _TN, preferred_element_type=F32)

    return pl.pallas_call(
        body, name=name,
        out_shape=(jax.ShapeDtypeStruct((H, S, LANE), F32), jax.ShapeDtypeStruct((H, Sp, LANE), F32),
                   jax.ShapeDtypeStruct((Sp, H * LANE), F32), jax.ShapeDtypeStruct((H, CHUNK, BANDW), F32)),
        grid=(H, S // rows),
        in_specs=[pl.BlockSpec((1, rows, LANE), lambda h, j: (h, j, 0)), pl.BlockSpec((1, Sp, LANE), lambda h, j: (h, 0, 0)),
                  pl.BlockSpec((Sp, LANE), lambda h, j: (0, h)), pl.BlockSpec((1, CHUNK, BANDW), lambda h, j: (h, 0, 0)),
                  pl.BlockSpec((rows, LANE), lambda h, j: (j, h))],
        out_specs=(pl.BlockSpec((1, rows, LANE), lambda h, j: (h, j, 0)), pl.BlockSpec((1, Sp, LANE), lambda h, j: (h, 0, 0)),
                   pl.BlockSpec((Sp, LANE), lambda h, j: (0, h)), pl.BlockSpec((1, CHUNK, BANDW), lambda h, j: (h, 0, 0))),
        compiler_params=_params("parallel", "arbitrary"),
    )(q, kp, vp, bias, do)


def band_bias(rel_bias, *, name):
    H = rel_bias.shape[0]
    last = rel_bias[:, 2 * REL_CLIP:]
    row0 = jnp.concatenate([jnp.tile(last, (1, PAD - REL_CLIP)), rel_bias[:, CHUNK + 1:][:, ::-1],
                            jnp.tile(last, (1, CHUNK))], axis=1)

    def body(r_ref, o_ref):
        o_ref[0] = pltpu.roll(jnp.broadcast_to(r_ref[0], (CHUNK, BANDW)), 0, 1, stride=1, stride_axis=0)

    return pl.pallas_call(
        body, name=name, out_shape=jax.ShapeDtypeStruct((H, CHUNK, BANDW), F32), grid=(H,),
        in_specs=[pl.BlockSpec((1, 1, BANDW), lambda h: (h, 0, 0))], out_specs=pl.BlockSpec((1, CHUNK, BANDW), lambda h: (h, 0, 0)),
        compiler_params=_params("parallel"),
    )(row0.reshape(H, 1, BANDW))


def relbias_bwd(dbias, *, name):
    H, W = dbias.shape[0], BANDW
    x = jnp.pad(dbias[:, :, :BAND][:, :, ::-1], ((0, 0), (0, 0), (0, CHUNK)))

    def body(x_ref, o_ref):
        skew = pltpu.roll(x_ref[0], 0, 1, stride=1, stride_axis=0)
        f = jnp.broadcast_to(jnp.sum(skew, axis=0, keepdims=True), (8, W))
        lane = lax.broadcasted_iota(jnp.int32, (8, W), 1)
        direct = jnp.where(jnp.logical_and(lane >= 65, lane <= 255), pltpu.roll(f, 65, 1), 0.0)
        tail = jnp.sum(jnp.where(lane >= 191, f, 0.0), axis=-1, keepdims=True)
        o_ref[0] = direct + jnp.where(lane == 2 * REL_CLIP, tail, 0.0)

    out = pl.pallas_call(
        body, name=name, out_shape=jax.ShapeDtypeStruct((H, 8, W), F32), grid=(H,),
        in_specs=[pl.BlockSpec((1, CHUNK, W), lambda h: (h, 0, 0))], out_specs=pl.BlockSpec((1, 8, W), lambda h: (h, 0, 0)),
        compiler_params=_params("parallel"),
    )(x)
    return out[:, 0, :2 * REL_CLIP + 1]


def _split_dot(x, u, dn):
    hi = x.astype(BF16)
    r1 = x - hi.astype(F32)
    mid = r1.astype(BF16)
    lo = (r1 - mid.astype(F32)).astype(BF16)
    d = lambda t: lax.dot_general(t, u, dn, preferred_element_type=F32)
    return d(hi) + d(mid) + d(lo)


def _upper_ones(S):
    return (np.arange(S)[:, None] <= np.arange(S)[None, :]).astype(np.float32)


def foxgate_fwd(fl, b, *, name):
    H, S = fl.shape
    u = jnp.asarray(_upper_ones(S), BF16)

    def body(f_ref, b_ref, u_ref, o_ref):
        x = f_ref[...] + b_ref[...]
        lf = jnp.minimum(x, 0.0) - jnp.log(1.0 + jnp.exp(-jnp.abs(x)))
        o_ref[...] = _split_dot(lf, u_ref[...], (((1,), (0,)), ((), ())))

    return pl.pallas_call(body, name=name, out_shape=jax.ShapeDtypeStruct((H, S), F32),
                          compiler_params=pltpu.CompilerParams(vmem_limit_bytes=VMEM_LIMIT_BYTES))(fl, b.reshape(H, 1), u)


def foxgate_bwd(fl, b, dcum, *, name):
    H, S = fl.shape
    u = jnp.asarray(_upper_ones(S), BF16)

    def body(f_ref, b_ref, u_ref, dc_ref, df_ref, db_ref):
        x = f_ref[...] + b_ref[...]
        dlf = _split_dot(dc_ref[...], u_ref[...], _NT)
        df = dlf * (1.0 / (1.0 + jnp.exp(x)))
        df_ref[...] = df
        db_ref[...] = jnp.sum(df, axis=-1, keepdims=True)

    df, db = pl.pallas_call(body, name=name,
                            out_shape=(jax.ShapeDtypeStruct((H, S), F32), jax.ShapeDtypeStruct((H, 1), F32)),
                            compiler_params=pltpu.CompilerParams(vmem_limit_bytes=VMEM_LIMIT_BYTES))(fl, b.reshape(H, 1), u, dcum)
    return df, db.reshape(H)


def gate_fwd(z, proj, *, name, ts=256, tc=512):
    S, D = proj[0].shape
    ts, gb, nb = _tile(S, ts), Z_GATE // tc, D // tc

    def body(g0, g1, g2, p0, p1, p2, o_ref):
        acc = None
        for g_ref, p_ref in zip((g0, g1, g2), (p0, p1, p2)):
            t = (1.0 / (1.0 + jnp.exp(-g_ref[...]))) * p_ref[...]
            acc = t if acc is None else acc + t
        o_ref[...] = acc.astype(o_ref.dtype)

    blk = pl.BlockSpec((ts, tc), lambda i, j: (i, j))
    return pl.pallas_call(
        body, name=name, out_shape=jax.ShapeDtypeStruct((S, D), BF16), grid=(S // ts, nb),
        in_specs=[pl.BlockSpec((ts, tc), lambda i, j, n=n: (i, gb + n * nb + j)) for n in range(3)] + [blk] * 3,
        out_specs=blk, compiler_params=_params("parallel", "parallel"),
    )(z, z, z, *proj)


def gate_bwd(z, proj, dm, *, name, ts=256, tc=512):
    S, D = proj[0].shape
    ts, gb, nb = _tile(S, ts), Z_GATE // tc, D // tc

    def body(g0, g1, g2, p0, p1, p2, dm_ref, *outs):
        dmv = dm_ref[...]
        for n, (g_ref, p_ref) in enumerate(zip((g0, g1, g2), (p0, p1, p2))):
            sg = 1.0 / (1.0 + jnp.exp(-g_ref[...]))
            outs[n][...] = (dmv * sg).astype(BF16)
            outs[3 + n][...] = (dmv * p_ref[...] * sg * (1.0 - sg)).astype(BF16)

    blk = pl.BlockSpec((ts, tc), lambda i, j: (i, j))
    outs = pl.pallas_call(
        body, name=name, out_shape=tuple(jax.ShapeDtypeStruct((S, D), BF16) for _ in range(6)), grid=(S // ts, nb),
        in_specs=[pl.BlockSpec((ts, tc), lambda i, j, n=n: (i, gb + n * nb + j)) for n in range(3)] + [blk] * 4,
        out_specs=(blk,) * 6, compiler_params=_params("parallel", "parallel"),
    )(z, z, z, *proj, dm)
    return outs[:3], outs[3:]


def loss_head(y, target, *, name, ts=256):
    S, D = y.shape
    ts = _tile(S, ts)

    def body(y_ref, t_ref, l_ref, dy_ref):
        err = y_ref[...] - t_ref[...]
        dy_ref[...] = err * (1.0 / D)
        part = 0.5 * jnp.sum(jnp.mean(err * err, axis=-1, keepdims=True), axis=0, keepdims=True)

        @pl.when(pl.program_id(0) == 0)
        def _():
            l_ref[...] = part

        @pl.when(pl.program_id(0) > 0)
        def _():
            l_ref[...] += part

    blk = pl.BlockSpec((ts, D), lambda i: (i, 0))
    return pl.pallas_call(
        body, name=name, out_shape=(jax.ShapeDtypeStruct((1, 1), F32), jax.ShapeDtypeStruct((S, D), F32)), grid=(S // ts,),
        in_specs=[blk, blk], out_specs=(pl.BlockSpec((1, 1), lambda i: (0, 0)), blk), compiler_params=_params("arbitrary"),
    )(y, target)


def adamw(w, g, m, v, *, name):
    shape = w.shape
    C = shape[-1]
    R = int(np.prod(shape[:-1]))
    br = R
    while br % 16 == 0 and br * C * 4 > 2**20:
        br //= 2
    w2, g2, m2, v2 = (t.reshape(R, C) for t in (w, g, m, v))

    def body(w_ref, g_ref, m_ref, v_ref, d_ref, nm_ref, nv_ref):
        gg = g_ref[...]
        nm = ADAM_B1 * m_ref[...] + (1.0 - ADAM_B1) * gg
        nv = ADAM_B2 * v_ref[...] + (1.0 - ADAM_B2) * (gg * gg)
        m_hat = nm / (1.0 - ADAM_B1 ** ADAM_STEP)
        v_hat = nv / (1.0 - ADAM_B2 ** ADAM_STEP)
        d_ref[...] = -ADAM_LR * (m_hat / (jnp.sqrt(v_hat) + ADAM_EPS) + ADAM_WD * w_ref[...])
        nm_ref[...] = nm
        nv_ref[...] = nv

    blk = pl.BlockSpec((br, C), lambda i: (i, 0))
    outs = pl.pallas_call(
        body, name=name, out_shape=tuple(jax.ShapeDtypeStruct((R, C), F32) for _ in range(3)), grid=(R // br,),
        in_specs=[blk] * 4, out_specs=(blk,) * 3, compiler_params=_params("parallel"),
    )(w2, g2, m2, v2)
    return tuple(o.reshape(shape) for o in outs)


_ANY = pl.BlockSpec(memory_space=pl.ANY)


def _place():
    return lax.axis_index("x"), lax.axis_index("y"), lax.axis_index("c")


def all_gather(xs, *, name):
    n = len(xs)

    def body(*refs):
        x_refs, o_refs = refs[:n], refs[n:2 * n]
        send_sems, recv_sems, local_sems = refs[2 * n:]
        px, py, pc = _place()
        me, sibling = (px, py, pc), (px, py, 1 - pc)
        chips = [(1 - px, py), (px, 1 - py), (1 - px, 1 - py)]

        def slot(t, dev):
            return o_refs[t].at[4 * dev[0] + 2 * dev[1] + dev[2]]

        def copy(t, k, block, to, src=None):
            return pltpu.make_async_remote_copy(
                src_ref=slot(t, block) if src is None else src, dst_ref=slot(t, block),
                send_sem=send_sems.at[t, k], recv_sem=recv_sems.at[t, k], device_id=to, device_id_type=MESH)

        mine = [pltpu.make_async_copy(x_refs[t], slot(t, me), local_sems.at[t]) for t in range(n)]
        first = []
        for t in range(n):
            mine[t].start()
            first += [copy(t, 1 + j, me, (*chip, pc), src=x_refs[t]) for j, chip in enumerate(chips)]
            first.append(copy(t, 0, me, sibling, src=x_refs[t]))
        for cp in first:
            cp.start()
        passed = []
        for t in range(n):
            for j, chip in enumerate(chips):
                copy(t, 1 + j, (*chip, pc), me).wait_recv()
                fwd = copy(t, 4 + j, (*chip, pc), sibling)
                fwd.start()
                passed.append(fwd)
        for t in range(n):
            copy(t, 0, sibling, me).wait_recv()
            for j, chip in enumerate(chips):
                copy(t, 4 + j, (*chip, 1 - pc), me).wait_recv()
        for cp in first + passed:
            cp.wait_send()
        for cp in mine:
            cp.wait()

    return pl.pallas_call(
        body, name=name, out_shape=tuple(jax.ShapeDtypeStruct((8,) + x.shape, x.dtype) for x in xs),
        in_specs=[_ANY] * n, out_specs=(_ANY,) * n,
        scratch_shapes=[pltpu.SemaphoreType.DMA((n, 7)), pltpu.SemaphoreType.DMA((n, 7)), pltpu.SemaphoreType.DMA((n,))],
    )(*xs)


def exchange_sibling(gs, *, name):
    n = len(gs)

    def body(*refs):
        g_refs, o_refs, (send_sems, recv_sems) = refs[:n], refs[n:2 * n], refs[2 * n:]
        px, py, pc = _place()
        copies = []
        for t in range(n):
            for j in range(4):
                copies.append(pltpu.make_async_remote_copy(
                    src_ref=g_refs[t].at[2 * j + 1 - pc], dst_ref=o_refs[t].at[j],
                    send_sem=send_sems.at[t, j], recv_sem=recv_sems.at[t, j],
                    device_id=(px, py, 1 - pc), device_id_type=MESH))
        for cp in copies:
            cp.start()
        for cp in copies:
            cp.wait()

    return pl.pallas_call(
        body, name=name, out_shape=tuple(jax.ShapeDtypeStruct((4,) + g.shape[1:], g.dtype) for g in gs),
        in_specs=[_ANY] * n, out_specs=(_ANY,) * n,
        scratch_shapes=[pltpu.SemaphoreType.DMA((n, 4)), pltpu.SemaphoreType.DMA((n, 4))],
    )(*gs)


def exchange_chips(hs, *, name):
    n = len(hs)

    def body(*refs):
        h_refs, o_refs, (send_sems, recv_sems) = refs[:n], refs[n:2 * n], refs[2 * n:]
        px, py, pc = _place()
        chips = [(1 - px, py), (px, 1 - py), (1 - px, 1 - py)]
        copies = []
        for t in range(n):
            for k, chip in enumerate(chips):
                copies.append(pltpu.make_async_remote_copy(
                    src_ref=h_refs[t].at[2 * chip[0] + chip[1]], dst_ref=o_refs[t].at[k],
                    send_sem=send_sems.at[t, k], recv_sem=recv_sems.at[t, k],
                    device_id=(*chip, pc), device_id_type=MESH))
        for cp in copies:
            cp.start()
        for cp in copies:
            cp.wait()

    return pl.pallas_call(
        body, name=name, out_shape=tuple(jax.ShapeDtypeStruct((3,) + h.shape[1:], h.dtype) for h in hs),
        in_specs=[_ANY] * n, out_specs=(_ANY,) * n,
        scratch_shapes=[pltpu.SemaphoreType.DMA((n, 3)), pltpu.SemaphoreType.DMA((n, 3))],
    )(*hs)


def _row_block(rows, cols, itemsize, budget=2**20):
    br = rows
    while br % 32 == 0 and br * cols * itemsize > budget:
        br //= 2
    return br


def pair_sum(g, other, *, name):
    shape, C = g.shape[1:], g.shape[-1]
    R = int(np.prod(shape[:-1]))
    br = _row_block(R, C, 2)
    pc = lax.axis_index("c").astype(jnp.int32).reshape(1)

    def body(c_ref, g_ref, o_ref, out_ref):
        out_ref[...] = (g_ref[...].astype(F32) + o_ref[...].astype(F32)).astype(out_ref.dtype)

    blk = lambda f: pl.BlockSpec((1, br, C), f)
    out = pl.pallas_call(
        body, name=name, out_shape=jax.ShapeDtypeStruct((4, R, C), BF16),
        grid_spec=pltpu.PrefetchScalarGridSpec(
            num_scalar_prefetch=1, grid=(4, R // br),
            in_specs=[blk(lambda j, r, c: (2 * j + c[0], r, 0)), blk(lambda j, r, c: (j, r, 0))],
            out_specs=blk(lambda j, r, c: (j, r, 0))),
        compiler_params=_params("parallel", "parallel"),
    )(pc, g.reshape(8, R, C), other.reshape(4, R, C))
    return out.reshape((4,) + shape)


def chip_sum(h, recv, *, name):
    shape, C = h.shape[1:], h.shape[-1]
    R = int(np.prod(shape[:-1]))
    br = _row_block(R, C, 4)
    chip = (2 * lax.axis_index("x") + lax.axis_index("y")).astype(jnp.int32).reshape(1)

    def body(c_ref, h_ref, r_ref, out_ref):
        acc = h_ref[0].astype(F32)
        for k in range(3):
            acc = acc + r_ref[k].astype(F32)
        out_ref[...] = acc

    out = pl.pallas_call(
        body, name=name, out_shape=jax.ShapeDtypeStruct((R, C), F32),
        grid_spec=pltpu.PrefetchScalarGridSpec(
            num_scalar_prefetch=1, grid=(R // br,),
            in_specs=[pl.BlockSpec((1, br, C), lambda r, c: (c[0], r, 0)), pl.BlockSpec((3, br, C), lambda r, c: (0, r, 0))],
            out_specs=pl.BlockSpec((br, C), lambda r, c: (r, 0))),
        compiler_params=_params("parallel"),
    )(chip, h.reshape(4, R, C), recv.reshape(3, R, C))
    return out.reshape(shape)


def ordered_sum(parts, *, name):
    _, R, C = parts.shape

    def body(p_ref, o_ref):
        acc = p_ref[0]
        for d in range(1, 8):
            acc = acc + p_ref[d]
        o_ref[...] = acc

    return pl.pallas_call(body, name=name, out_shape=jax.ShapeDtypeStruct((R, C), F32))(parts)


W_IN_COLS, W_IN_SHARD = 13128, 1641
W_IN_SEGMENTS = ((0, 832, 0), (832, 3904, Z_FOX), (3904, 3912, FF_COL), (3912, 6984, Z_CH), (6984, 13128, Z_GATE))


def col_gather(src, table, pieces, out_shape, *, name, tr=512):
    R = src.shape[1]
    tr = _tile(R, tr)
    width = 2 + 6 * pieces
    nb = table.shape[0] // width

    def body(tab, *refs):
        o_ref = refs[-1]
        base = pl.program_id(0) * width
        lane = lax.broadcasted_iota(jnp.int32, (tr, LANE), 1)
        acc = jnp.zeros((tr, LANE), F32)
        for p in range(pieces):
            e = base + 2 + 6 * p
            x = jnp.concatenate([refs[2 * p][0].astype(F32), refs[2 * p + 1][0].astype(F32)], axis=1)
            y = pltpu.roll(x, tab[e + 3], 1)[:, :LANE]
            acc = jnp.where(jnp.logical_and(lane >= tab[e + 4], lane < tab[e + 5]), y, acc)
        o_ref[0] = acc.astype(o_ref.dtype)

    in_specs = []
    for p in range(pieces):
        for tcol in (1, 2):
            in_specs.append(pl.BlockSpec(
                (1, tr, LANE), lambda b, i, tab, p=p, tcol=tcol: (tab[b * width + 2 + 6 * p], i, tab[b * width + 2 + 6 * p + tcol])))
    return pl.pallas_call(
        body, name=name, out_shape=jax.ShapeDtypeStruct(out_shape, src.dtype),
        grid_spec=pltpu.PrefetchScalarGridSpec(
            num_scalar_prefetch=1, grid=(nb, R // tr), in_specs=in_specs,
            out_specs=pl.BlockSpec((1, tr, LANE), lambda b, i, tab: (tab[b * width], i, tab[b * width + 1]))),
        compiler_params=_params("parallel", "parallel"),
    )(jnp.asarray(table, jnp.int32), *([src] * (2 * pieces)))


def _piece(sd, start, lo, hi, last_tile):
    t0 = start // LANE
    return [sd, t0, min(t0 + 1, last_tile), (lo - start % LANE) % (2 * LANE), lo, hi]


def _w_in_table(layer, L):
    rows = []
    for b in range(Z_W // LANE):
        pcs = []
        for first, last, col in W_IN_SEGMENTS:
            lo, hi = max(LANE * b, col), min(LANE * (b + 1), col + last - first)
            while lo < hi:
                c = first + lo - col
                n = min(hi - lo, W_IN_SHARD - c % W_IN_SHARD)
                pcs += _piece((c // W_IN_SHARD) * L + layer, c % W_IN_SHARD, lo - LANE * b, lo - LANE * b + n, W_IN_SHARD // LANE)
                lo += n
        assert len(pcs) <= 12
        rows.append([0, b] + pcs + [0] * (12 - len(pcs)))
    return np.asarray(rows, np.int32).reshape(-1)


def _w_in_grad_table():
    rows = []
    for d in range(8):
        for t in range(-(-W_IN_SHARD // LANE)):
            pcs = []
            c0 = d * W_IN_SHARD + LANE * t
            c1 = min(c0 + LANE, (d + 1) * W_IN_SHARD)
            for first, last, col in W_IN_SEGMENTS:
                lo, hi = max(c0, first), min(c1, last)
                if lo < hi:
                    pcs += _piece(0, col + lo - first, lo - c0, hi - c0, Z_W // LANE - 1)
            assert len(pcs) <= 18
            rows.append([d, t] + pcs + [0] * (18 - len(pcs)))
    return np.asarray(rows, np.int32).reshape(-1)


def _full_from_shards(k, sh):
    if k not in COL_SHARDED:
        return sh.reshape((-1, sh.shape[-1]))
    if k == 'w_br':
        return [jnp.transpose(sh[:, n], (1, 0, 2)).reshape(1024, 2048) for n in range(3)]
    full = jnp.transpose(sh, (1, 0, 2)).reshape(sh.shape[1], -1)
    if k == 'w_uq':
        return jnp.pad(full.reshape(512, MLA_HEADS, MLA_QK), ((0, 0), (0, 0), (0, 64))).reshape(512, 2048)
    if k == 'w_ukv':
        return jnp.transpose(full.reshape(256, MLA_HEADS, 2, LANE), (0, 2, 1, 3)).reshape(256, 2048)
    return full


def _shards_from_full(k, g):
    if k not in COL_SHARDED:
        return g.reshape((8, g.shape[0] // 8, g.shape[1]))
    if k == 'w_br':
        return jnp.stack([jnp.transpose(g[n].reshape(1024, 8, 256), (1, 0, 2)) for n in range(3)], axis=1)
    if k == 'w_uq':
        g = g.reshape(512, MLA_HEADS, 256)[:, :, :MLA_QK].reshape(512, 1536)
    elif k == 'w_ukv':
        g = jnp.transpose(g.reshape(256, 2, MLA_HEADS, LANE), (0, 2, 1, 3)).reshape(256, 2048)
    return jnp.transpose(g.reshape(g.shape[0], 8, g.shape[1] // 8), (1, 0, 2))


def w_in_full(gathered, layer, *, name):
    _, L, K, c = gathered.shape
    return col_gather(gathered.reshape(8 * L, K, c), _w_in_table(layer, L), 2, (1, K, Z_W), name=name)[0]


def w_in_shards(g, *, name):
    return col_gather(g[None], _w_in_grad_table(), 3, (8, g.shape[0], W_IN_SHARD), name=name)


def _layer_fwd(x, mem, W, P, cos, ssin, tag):
    S = x.shape[0]
    sv = {'x0': x}
    h = rmsnorm_fwd(x, P['g_mix'], name=f"{tag}_norm_mix")
    z = mm(h, W['w_in'], name=f"{tag}_mm_in")
    sv.update(h=h, z=z)
    cqn = rmsnorm_fwd(z, P['g_cq'], col=0, width=512, name=f"{tag}_norm_cq")
    ckvn = rmsnorm_fwd(z, P['g_ckv'], col=512, width=256, name=f"{tag}_norm_ckv")
    qf = mm(cqn, W['w_uq'], name=f"{tag}_mm_uq")
    kvf = mm(ckvn, W['w_ukv'], name=f"{tag}_mm_ukv")
    qa = mla_prep_fwd(qf, qf, P['g_mla_q'], cos, ssin, n_col=0, n_stride=2 * LANE, r_col=LANE, r_stride=2 * LANE,
                      heads=8, name=f"{tag}_mla_q")
    ka = mla_prep_fwd(kvf, z, P['g_mla_k'], cos, ssin, n_col=0, n_stride=LANE, r_col=KR_COL, r_stride=0,
                      heads=8, name=f"{tag}_mla_k")
    ya = attn_fwd(qa, ka, kvf, v_col=1024, mode='mla', scale=MLA_QK ** -0.5, name=f"{tag}_mla_attn")
    sv.update(cqn=cqn, ckvn=ckvn, qf=qf, kvf=kvf, qa=qa, ka=ka)
    qb = headnorm_fwd(z, P['g_fox_q'], col=Z_FOX, heads=8, name=f"{tag}_fox_qn")
    kb = headnorm_fwd(z, P['g_fox_k'], col=Z_FOX + 1024, heads=8, name=f"{tag}_fox_kn")
    fl = z[:, FF_COL:FF_COL + 8].T
    cum = foxgate_fwd(fl, P['b_f'], name=f"{tag}_fox_gate")
    cq, ck = cum.reshape(8, S, 1), cum.reshape(8, 1, S)
    yb = attn_fwd(qb, kb, z, v_col=Z_FOX + 2048, mode='fox', scale=LANE ** -0.5, cq=cq, ck=ck, name=f"{tag}_fox_attn")
    sv.update(qb=qb, kb=kb, fl=fl, cq=cq, ck=ck)
    qc = headnorm_fwd(z, P['g_ch_q'], col=Z_CH, heads=8, name=f"{tag}_ch_qn")
    kc = headnorm_fwd(z, P['g_ch_k'], col=Z_CH + 1024, heads=8, name=f"{tag}_ch_kn")
    kcp = jnp.pad(kc, ((0, 0), (PAD, CHUNK), (0, 0)))
    vcp = jnp.pad(z[:, Z_CH + 2048:Z_CH + 3072], ((PAD, CHUNK), (0, 0)))
    bias = band_bias(P['rel_bias'], name=f"{tag}_ch_bias")
    yc = band_fwd(qc, kcp, vcp, bias, scale=LANE ** -0.5, name=f"{tag}_ch_attn")
    sv.update(qc=qc, kcp=kcp, vcp=vcp, bias=bias)
    ys = (ya, yb, yc)
    proj = [mm(ys[n], W['w_br'][n], name=f"{tag}_mm_br{n}") for n in range(3)]
    merged = gate_fwd(z, proj, name=f"{tag}_gate")
    x1 = mm(merged, W['w_out'], epi='add', aux=x, name=f"{tag}_mm_out")
    sv.update(ys=ys, proj=proj, merged=merged, x1=x1)
    hc = rmsnorm_fwd(x1, P['g_cross'], name=f"{tag}_norm_cross")
    memn = rmsnorm_fwd(mem, P['g_mem'], name=f"{tag}_norm_mem")
    qx_raw = mm(hc, W['w_xq'], name=f"{tag}_mm_xq")
    memkv = mm(memn, W['w_xkv'], name=f"{tag}_mm_xkv")
    qx = headnorm_fwd(qx_raw, P['g_x_q'], col=0, heads=4, name=f"{tag}_x_qn")
    kx = headnorm_fwd(memkv, P['g_x_k'], col=0, heads=4, name=f"{tag}_x_kn")
    ox = attn_fwd(qx, kx, memkv, v_col=512, mode='none', scale=LANE ** -0.5, name=f"{tag}_x_attn")
    x2 = mm(ox, W['w_xo'], epi='add', aux=x1, name=f"{tag}_mm_xo")
    sv.update(hc=hc, memn=memn, qx_raw=qx_raw, memkv=memkv, qx=qx, kx=kx, ox=ox, x2=x2)
    hm = rmsnorm_fwd(x2, P['g_mlp'], name=f"{tag}_norm_mlp")
    u, a = mm(hm, W['w_1'], epi='relu2', out_dtype=BF16, name=f"{tag}_mm_w1")
    x3 = mm(a, W['w_2'], epi='add', aux=x2, name=f"{tag}_mm_w2")
    sv.update(hm=hm, u=u, a=a)
    return x3, sv


def _layer_bwd(dx, mem, W, P, sv, cos, ssin, tag):
    S = dx.shape[0]
    z = sv['z']
    gw, gs = {}, {}
    wgrad = lambda a, d, name: mm(a, d, ta=True, out_dtype=BF16, name=name)
    gw['w_2'] = wgrad(sv['a'], dx, f"{tag}_dw2")
    du = mm(dx, W['w_2'], tb=True, epi='mul_drelu2', aux=sv['u'], out_dtype=BF16, name=f"{tag}_du")
    gw['w_1'] = wgrad(sv['hm'], du, f"{tag}_dw1")
    dhm = mm(du, W['w_1'], tb=True, name=f"{tag}_dhm")
    dx, gs['g_mlp'] = rmsnorm_bwd(sv['x2'], P['g_mlp'], dhm, res=dx, name=f"{tag}_dnorm_mlp")
    gw['w_xo'] = wgrad(sv['ox'], dx, f"{tag}_dwxo")
    dox = mm(dx, W['w_xo'], tb=True, out_dtype=BF16, name=f"{tag}_dox")
    dqx, dkx, dvx = attn_bwd(sv['qx'], sv['kx'], sv['memkv'], dox, v_col=512, mode='none', scale=LANE ** -0.5,
                             name=f"{tag}_x_attn_bwd")
    dqx_raw, gs['g_x_q'] = headnorm_bwd(sv['qx_raw'], P['g_x_q'], dqx, col=0, heads=4, name=f"{tag}_x_qn_bwd")
    dkx_raw, gs['g_x_k'] = headnorm_bwd(sv['memkv'], P['g_x_k'], dkx, col=0, heads=4, name=f"{tag}_x_kn_bwd")
    dqx_b = dqx_raw.astype(BF16)
    gw['w_xq'] = wgrad(sv['hc'], dqx_b, f"{tag}_dwxq")
    dhc = mm(dqx_b, W['w_xq'], tb=True, name=f"{tag}_dhc")
    dx, gs['g_cross'] = rmsnorm_bwd(sv['x1'], P['g_cross'], dhc, res=dx, name=f"{tag}_dnorm_cross")
    dmemkv = jnp.concatenate([dkx_raw, dvx], axis=1).astype(BF16)
    gw['w_xkv'] = wgrad(sv['memn'], dmemkv, f"{tag}_dwxkv")
    dmemn = mm(dmemkv, W['w_xkv'], tb=True, name=f"{tag}_dmemn")
    _, gs['g_mem'] = rmsnorm_bwd(mem, P['g_mem'], dmemn, name=f"{tag}_dnorm_mem")
    gw['w_out'] = wgrad(sv['merged'], dx, f"{tag}_dwout")
    dmerged = mm(dx, W['w_out'], tb=True, name=f"{tag}_dmerged")
    dproj, dgl = gate_bwd(z, sv['proj'], dmerged, name=f"{tag}_gate_bwd")
    gw['w_br'] = [wgrad(sv['ys'][n], dproj[n], f"{tag}_dwbr{n}") for n in range(3)]
    dys = [mm(dproj[n], W['w_br'][n], tb=True, out_dtype=BF16, name=f"{tag}_dys{n}") for n in range(3)]
    dqa, dka, dva = attn_bwd(sv['qa'], sv['ka'], sv['kvf'], dys[0], v_col=1024, mode='mla', scale=MLA_QK ** -0.5,
                             name=f"{tag}_mla_attn_bwd")
    dqn, dqr, gs['g_mla_q'] = mla_prep_bwd(sv['qf'], sv['qf'], P['g_mla_q'], cos, ssin, dqa, n_col=0, n_stride=2 * LANE,
                                           r_col=LANE, r_stride=2 * LANE, heads=8, name=f"{tag}_mla_q_bwd")
    dkn, dkr, gs['g_mla_k'] = mla_prep_bwd(sv['kvf'], z, P['g_mla_k'], cos, ssin, dka, n_col=0, n_stride=LANE,
                                           r_col=KR_COL, r_stride=0, heads=8, name=f"{tag}_mla_k_bwd")
    dqf = jnp.stack([dqn.reshape(S, 8, LANE), dqr.reshape(S, 8, LANE)], axis=2).reshape(S, 2048).astype(BF16)
    dkvf = jnp.concatenate([dkn, dva], axis=1).astype(BF16)
    gw['w_uq'] = wgrad(sv['cqn'], dqf, f"{tag}_dwuq")
    gw['w_ukv'] = wgrad(sv['ckvn'], dkvf, f"{tag}_dwukv")
    dcqn = mm(dqf, W['w_uq'], tb=True, name=f"{tag}_dcqn")
    dckvn = mm(dkvf, W['w_ukv'], tb=True, name=f"{tag}_dckvn")
    dcq_raw, gs['g_cq'] = rmsnorm_bwd(z, P['g_cq'], dcqn, col=0, width=512, name=f"{tag}_dnorm_cq")
    dckv_raw, gs['g_ckv'] = rmsnorm_bwd(z, P['g_ckv'], dckvn, col=512, width=256, name=f"{tag}_dnorm_ckv")
    dqb, dkb, dvb, dcq, dck = attn_bwd(sv['qb'], sv['kb'], z, dys[1], v_col=Z_FOX + 2048, mode='fox', scale=LANE ** -0.5,
                                       cq=sv['cq'], ck=sv['ck'], name=f"{tag}_fox_attn_bwd")
    dqb_raw, gs['g_fox_q'] = headnorm_bwd(z, P['g_fox_q'], dqb, col=Z_FOX, heads=8, name=f"{tag}_fox_qn_bwd")
    dkb_raw, gs['g_fox_k'] = headnorm_bwd(z, P['g_fox_k'], dkb, col=Z_FOX + 1024, heads=8, name=f"{tag}_fox_kn_bwd")
    dfl, gs['b_f'] = foxgate_bwd(sv['fl'], P['b_f'], dcq.reshape(8, S) + dck.reshape(8, S), name=f"{tag}_fox_gate_bwd")
    dqc, dkcp, dvcp, dbias = band_bwd(sv['qc'], sv['kcp'], sv['vcp'], sv['bias'], dys[2], scale=LANE ** -0.5,
                                      name=f"{tag}_ch_attn_bwd")
    dqc_raw, gs['g_ch_q'] = headnorm_bwd(z, P['g_ch_q'], dqc, col=Z_CH, heads=8, name=f"{tag}_ch_qn_bwd")
    dkc_raw, gs['g_ch_k'] = headnorm_bwd(z, P['g_ch_k'], dkcp[:, PAD:PAD + S, :], col=Z_CH + 1024, heads=8,
                                         name=f"{tag}_ch_kn_bwd")
    gs['rel_bias'] = relbias_bwd(dbias, name=f"{tag}_relbias_bwd")
    b16 = lambda t: t.astype(BF16)
    dz = jnp.concatenate([b16(dcq_raw), b16(dckv_raw), b16(dkr), b16(dfl.T), jnp.zeros((S, 120), BF16),
                          b16(dqb_raw), b16(dkb_raw), b16(dvb), b16(dqc_raw), b16(dkc_raw), b16(dvcp[PAD:PAD + S]),
                          dgl[0], dgl[1], dgl[2]], axis=1)
    gw['w_in'] = wgrad(sv['h'], dz, f"{tag}_dwin")
    dh = mm(dz, W['w_in'], tb=True, name=f"{tag}_dh")
    dx, gs['g_mix'] = rmsnorm_bwd(sv['x0'], P['g_mix'], dh, res=dx, name=f"{tag}_dnorm_mix")
    return dx, gw, gs


def _local_step(x, mem, target, Ws, Ps):
    S = x.shape[0]
    cos, ssin = _rope_tables(S)
    L = len(Ws)
    saved = []
    for l in range(L):
        x, sv = _layer_fwd(x, mem, Ws[l], Ps[l], cos, ssin, f"l{l}")
        saved.append(sv)
    loss, dx = loss_head(x, target, name="loss_head")
    gws, gss = [None] * L, [None] * L
    for l in reversed(range(L)):
        dx, gws[l], gss[l] = _layer_bwd(dx, mem, Ws[l], Ps[l], saved[l], cos, ssin, f"l{l}")
    return loss, dx, gws, gss


def _pack_small(d):
    flat = jnp.concatenate([d[k].reshape(-1) for k in SMALL])
    n = flat.shape[0]
    rows = -(-n // (8 * LANE)) * 8
    return jnp.pad(flat, (0, rows * LANE - n)).reshape(rows, LANE)


def _unpack_small(packed, like):
    flat, out, off = packed.reshape(-1), {}, 0
    for k in SMALL:
        n = int(np.prod(like[k].shape))
        out[k] = flat[off:off + n].reshape(like[k].shape)
        off += n
    return out


def kernel(x, mem, g_mix, w_in, g_cq, w_uq, g_ckv, w_ukv, g_mla_q, g_mla_k, b_f, g_fox_q, g_fox_k, rel_bias, g_ch_q, g_ch_k, w_br, w_out, g_cross, g_mem, w_xq, w_xkv, g_x_q, g_x_k, w_xo, g_mlp, w_1, w_2, loss_target, m_g_mix, m_w_in, m_g_cq, m_w_uq, m_g_ckv, m_w_ukv, m_g_mla_q, m_g_mla_k, m_b_f, m_g_fox_q, m_g_fox_k, m_rel_bias, m_g_ch_q, m_g_ch_k, m_w_br, m_w_out, m_g_cross, m_g_mem, m_w_xq, m_w_xkv, m_g_x_q, m_g_x_k, m_w_xo, m_g_mlp, m_w_1, m_w_2, v_g_mix, v_w_in, v_g_cq, v_w_uq, v_g_ckv, v_w_ukv, v_g_mla_q, v_g_mla_k, v_b_f, v_g_fox_q, v_g_fox_k, v_rel_bias, v_g_ch_q, v_g_ch_k, v_w_br, v_w_out, v_g_cross, v_g_mem, v_w_xq, v_w_xkv, v_g_x_q, v_g_x_k, v_w_xo, v_g_mlp, v_w_1, v_w_2):
    args = locals()
    w = {k: args[k] for k in WEIGHTS}
    m = {k: args['m_' + k] for k in WEIGHTS}
    v = {k: args['v_' + k] for k in WEIGHTS}
    L = w_in.shape[0]

    gathered = dict(zip(BIG, all_gather([w[k].astype(BF16) for k in BIG], name="ag_weights")))
    Ws = [{k: w_in_full(gathered[k], l, name=f"l{l}_w_in_layout") if k == 'w_in' else _full_from_shards(k, gathered[k][:, l])
           for k in BIG} for l in range(L)]
    Ps = [{k: w[k][l] for k in SMALL} for l in range(L)]

    loss, grad_x, gws, gss = _local_step(x[0], mem[0], loss_target[0], Ws, Ps)
    loss = lax.psum(loss[0, 0], ("x", "y", "c"))

    def owner_major(k, l):
        return w_in_shards(gws[l][k], name=f"l{l}_dw_in_layout") if k == 'w_in' else _shards_from_full(k, gws[l][k])

    gdst = [jnp.stack([owner_major(k, l) for l in range(L)], axis=1) for k in BIG]
    from_sibling = exchange_sibling(gdst, name="rs_sibling")
    chip_part = [pair_sum(g, o, name=f"rs_pair_sum_{k}") for k, g, o in zip(BIG, gdst, from_sibling)]
    from_chips = exchange_chips(chip_part, name="rs_chips")
    grads = {k: chip_sum(h, r, name=f"rs_chip_sum_{k}") for k, h, r in zip(BIG, chip_part, from_chips)}

    small_part = _pack_small({k: jnp.stack([gss[l][k] for l in range(L)]) for k in SMALL})
    small_all = all_gather([small_part], name="ag_small")[0]
    grads.update(_unpack_small(ordered_sum(small_all, name="small_sum"), {k: w[k] for k in SMALL}))

    delta, new_m, new_v = {}, {}, {}
    for k in BIG:
        delta[k], new_m[k], new_v[k] = adamw(w[k], grads[k], m[k], v[k], name=f"adamw_{k}")
    sd, sm, sv_ = adamw(_pack_small({k: w[k] for k in SMALL}), _pack_small({k: grads[k] for k in SMALL}),
                        _pack_small({k: m[k] for k in SMALL}), _pack_small({k: v[k] for k in SMALL}), name="adamw_small")
    like = {k: w[k] for k in SMALL}
    delta.update(_unpack_small(sd, like))
    new_m.update(_unpack_small(sm, like))
    new_v.update(_unpack_small(sv_, like))

    return (loss, grad_x[None], *[grads[k] for k in WEIGHTS], *[delta[k] for k in WEIGHTS],
            *[new_m[k] for k in WEIGHTS], *[new_v[k] for k in WEIGHTS])
```

```python
import numpy as np
import jax
import jax.numpy as jnp
from jax import lax
from jax.experimental import pallas as pl
from jax.experimental.pallas import tpu as pltpu

F32, BF16 = jnp.float32, jnp.bfloat16
EPS = 1e-6
NEG = -1e30
LANE = 128
VMEM_LIMIT_BYTES = 56 * 2**20
MESH = pl.DeviceIdType.MESH

D_MODEL = 2048
CHUNK = 64
BAND = 9 * CHUNK
PAD = 8 * CHUNK
REL_CLIP = 128
MLA_HEADS, MLA_NOPE, MLA_ROPE, MLA_QK = 8, 128, 64, 192
N_HEADS = 8
X_HEADS = 4
ROPE_THETA = 10000.0
ADAM_LR, ADAM_B1, ADAM_B2, ADAM_EPS, ADAM_WD, ADAM_STEP = 0.001, 0.9, 0.999, 1e-08, 0.01, 10

Z_MAIN, Z_FOX, Z_CH, Z_GATE, Z_W = 0, 1024, 4096, 7168, 13312
KR_COL, FF_COL = 768, 896

BIG = ('w_in', 'w_uq', 'w_ukv', 'w_br', 'w_out', 'w_xq', 'w_xkv', 'w_xo', 'w_1', 'w_2')
COL_SHARDED = ('w_in', 'w_uq', 'w_ukv', 'w_br', 'w_xo', 'w_1')
SMALL = ('g_mix', 'g_cq', 'g_ckv', 'g_mla_q', 'g_mla_k', 'b_f', 'g_fox_q', 'g_fox_k', 'rel_bias', 'g_ch_q',
         'g_ch_k', 'g_cross', 'g_mem', 'g_x_q', 'g_x_k', 'g_mlp')
WEIGHTS = ('g_mix', 'w_in', 'g_cq', 'w_uq', 'g_ckv', 'w_ukv', 'g_mla_q', 'g_mla_k', 'b_f', 'g_fox_q', 'g_fox_k',
           'rel_bias', 'g_ch_q', 'g_ch_k', 'w_br', 'w_out', 'g_cross', 'g_mem', 'w_xq', 'w_xkv', 'g_x_q', 'g_x_k',
           'w_xo', 'g_mlp', 'w_1', 'w_2')


def _params(*sem):
    return pltpu.CompilerParams(dimension_semantics=sem, vmem_limit_bytes=VMEM_LIMIT_BYTES)


def _tile(dim, pref):
    if dim <= pref:
        return dim
    for t in range(pref - pref % LANE, 0, -LANE):
        if dim % t == 0:
            return t
    raise ValueError((dim, pref))


def mm(a, b, *, ta=False, tb=False, out_dtype=F32, epi=None, aux=None, name, tm=1024, tn=512, tk=2048):
    M, K = (a.shape[1], a.shape[0]) if ta else a.shape
    N = b.shape[0] if tb else b.shape[1]
    assert (b.shape[1] if tb else b.shape[0]) == K, (a.shape, b.shape, ta, tb)
    tm, tn, tk = _tile(M, tm), _tile(N, tn), _tile(K, tk)
    nk = K // tk
    dn = (((0 if ta else 1,), (1 if tb else 0,)), ((), ()))
    n_aux = 0 if aux is None else 1

    def finish(acc, aux_refs, o_refs):
        if epi is None:
            o_refs[0][...] = acc.astype(o_refs[0].dtype)
        elif epi == 'add':
            o_refs[0][...] = (acc + aux_refs[0][...]).astype(o_refs[0].dtype)
        elif epi == 'relu2':
            o_refs[0][...] = acc
            r = jnp.maximum(acc, 0.0)
            o_refs[1][...] = (r * r).astype(o_refs[1].dtype)
        elif epi == 'mul_drelu2':
            o_refs[0][...] = (acc * (2.0 * jnp.maximum(aux_refs[0][...], 0.0))).astype(o_refs[0].dtype)

    def body(a_ref, b_ref, *rest):
        aux_refs = rest[:n_aux]
        o_refs = rest[n_aux:n_aux + (2 if epi == 'relu2' else 1)]
        part = lax.dot_general(a_ref[...].astype(BF16), b_ref[...].astype(BF16), dn, preferred_element_type=F32)
        if nk == 1:
            finish(part, aux_refs, o_refs)
        else:
            acc_ref = rest[-1]
            k = pl.program_id(2)

            @pl.when(k == 0)
            def _():
                acc_ref[...] = part

            @pl.when(k > 0)
            def _():
                acc_ref[...] += part

            @pl.when(k == nk - 1)
            def _():
                finish(acc_ref[...], aux_refs, o_refs)

    a_spec = pl.BlockSpec((tk, tm), lambda i, j, k: (k, i)) if ta else pl.BlockSpec((tm, tk), lambda i, j, k: (i, k))
    b_spec = pl.BlockSpec((tn, tk), lambda i, j, k: (j, k)) if tb else pl.BlockSpec((tk, tn), lambda i, j, k: (k, j))
    o_spec = pl.BlockSpec((tm, tn), lambda i, j, k: (i, j))
    if epi == 'relu2':
        out_shape = (jax.ShapeDtypeStruct((M, N), F32), jax.ShapeDtypeStruct((M, N), out_dtype))
        out_specs = (o_spec, o_spec)
    else:
        out_shape, out_specs = jax.ShapeDtypeStruct((M, N), out_dtype), o_spec
    return pl.pallas_call(
        body, name=name, out_shape=out_shape, grid=(M // tm, N // tn, nk),
        in_specs=[a_spec, b_spec] + [o_spec] * n_aux, out_specs=out_specs,
        scratch_shapes=[pltpu.VMEM((tm, tn), F32)] if nk > 1 else [],
        compiler_params=_params("parallel", "parallel", "arbitrary"),
    )(a, b, *([aux] if n_aux else []))


def rmsnorm_fwd(x, g, *, col=0, width=None, out_dtype=BF16, name, ts=256):
    S = x.shape[0]
    width = x.shape[1] if width is None else width
    ts, cb = _tile(S, ts), col // width

    def body(x_ref, g_ref, o_ref):
        xf = x_ref[...]
        r = lax.rsqrt(jnp.mean(xf * xf, axis=-1, keepdims=True) + EPS)
        o_ref[...] = (xf * r * g_ref[...]).astype(o_ref.dtype)

    return pl.pallas_call(
        body, name=name, out_shape=jax.ShapeDtypeStruct((S, width), out_dtype), grid=(S // ts,),
        in_specs=[pl.BlockSpec((ts, width), lambda i: (i, cb)), pl.BlockSpec((1, width), lambda i: (0, 0))],
        out_specs=pl.BlockSpec((ts, width), lambda i: (i, 0)), compiler_params=_params("parallel"),
    )(x, g.reshape(1, width))


def rmsnorm_bwd(x, g, dy, *, col=0, width=None, res=None, name, ts=256):
    S = x.shape[0]
    width = x.shape[1] if width is None else width
    ts, cb = _tile(S, ts), col // width
    has_res = res is not None

    def body(x_ref, g_ref, dy_ref, *rest):
        dx_ref, dg_ref = rest[-2:]
        xf = x_ref[...]
        r = lax.rsqrt(jnp.mean(xf * xf, axis=-1, keepdims=True) + EPS)
        dyf = dy_ref[...].astype(F32)
        dyg = dyf * g_ref[...]
        dx = r * dyg - xf * (r * r * r) * jnp.mean(dyg * xf, axis=-1, keepdims=True)
        if has_res:
            dx = dx + rest[0][...]
        dx_ref[...] = dx
        part = jnp.sum(dyf * xf * r, axis=0, keepdims=True)

        @pl.when(pl.program_id(0) == 0)
        def _():
            dg_ref[...] = part

        @pl.when(pl.program_id(0) > 0)
        def _():
            dg_ref[...] += part

    blk = pl.BlockSpec((ts, width), lambda i: (i, 0))
    dx, dg = pl.pallas_call(
        body, name=name,
        out_shape=(jax.ShapeDtypeStruct((S, width), F32), jax.ShapeDtypeStruct((1, width), F32)), grid=(S // ts,),
        in_specs=[pl.BlockSpec((ts, width), lambda i: (i, cb)), pl.BlockSpec((1, width), lambda i: (0, 0)), blk]
        + ([blk] if has_res else []),
        out_specs=(blk, pl.BlockSpec((1, width), lambda i: (0, 0))), compiler_params=_params("arbitrary"),
    )(x, g.reshape(1, width), dy, *([res] if has_res else []))
    return dx, dg.reshape(width)


def headnorm_fwd(x, g, *, col, heads, name, ts=1024):
    S = x.shape[0]
    ts, cb = _tile(S, ts), col // LANE

    def body(x_ref, g_ref, o_ref):
        xf = x_ref[...]
        r = lax.rsqrt(jnp.mean(xf * xf, axis=-1, keepdims=True) + EPS)
        o_ref[0] = (xf * r * g_ref[...]).astype(o_ref.dtype)

    return pl.pallas_call(
        body, name=name, out_shape=jax.ShapeDtypeStruct((heads, S, LANE), BF16), grid=(heads, S // ts),
        in_specs=[pl.BlockSpec((ts, LANE), lambda h, i: (i, cb + h)), pl.BlockSpec((1, LANE), lambda h, i: (0, 0))],
        out_specs=pl.BlockSpec((1, ts, LANE), lambda h, i: (h, i, 0)), compiler_params=_params("parallel", "parallel"),
    )(x, g.reshape(1, LANE))


def headnorm_bwd(x, g, dy, *, col, heads, name, ts=1024):
    S = x.shape[0]
    ts, cb = _tile(S, ts), col // LANE

    def body(x_ref, g_ref, dy_ref, dx_ref, dg_ref):
        xf = x_ref[...]
        r = lax.rsqrt(jnp.mean(xf * xf, axis=-1, keepdims=True) + EPS)
        dyf = dy_ref[0]
        dyg = dyf * g_ref[...]
        dx_ref[...] = r * dyg - xf * (r * r * r) * jnp.mean(dyg * xf, axis=-1, keepdims=True)
        part = jnp.sum(dyf * xf * r, axis=0, keepdims=True)
        first = jnp.logical_and(pl.program_id(0) == 0, pl.program_id(1) == 0)

        @pl.when(first)
        def _():
            dg_ref[...] = part

        @pl.when(jnp.logical_not(first))
        def _():
            dg_ref[...] += part

    dx, dg = pl.pallas_call(
        body, name=name,
        out_shape=(jax.ShapeDtypeStruct((S, heads * LANE), F32), jax.ShapeDtypeStruct((1, LANE), F32)),
        grid=(heads, S // ts),
        in_specs=[pl.BlockSpec((ts, LANE), lambda h, i: (i, cb + h)), pl.BlockSpec((1, LANE), lambda h, i: (0, 0)),
                  pl.BlockSpec((1, ts, LANE), lambda h, i: (h, i, 0))],
        out_specs=(pl.BlockSpec((ts, LANE), lambda h, i: (i, h)), pl.BlockSpec((1, LANE), lambda h, i: (0, 0))),
        compiler_params=_params("arbitrary", "arbitrary"),
    )(x, g.reshape(1, LANE), dy)
    return dx, dg.reshape(LANE)


def _rope_tables(S):
    pos = jnp.arange(S, dtype=F32)
    inv = ROPE_THETA ** (-jnp.arange(0, MLA_ROPE, 2, dtype=F32) / MLA_ROPE)
    ang = pos[:, None] * inv[None, :]
    c, s, z = jnp.cos(ang), jnp.sin(ang), jnp.zeros((S, 64), F32)
    return jnp.concatenate([c, c, z], axis=1), jnp.concatenate([-s, s, z], axis=1)


def _rope(v, cos, ssin, lane):
    partner = jnp.where(lane < 32, pltpu.roll(v, 96, 1), pltpu.roll(v, 32, 1))
    return v * cos + partner * ssin


def mla_prep_fwd(xn, xr, g, cos, ssin, *, n_col, n_stride, r_col, r_stride, heads, name, ts=1024):
    S = xn.shape[0]
    ts = _tile(S, ts)
    nb, ns, rb, rs = n_col // LANE, n_stride // LANE, r_col // LANE, r_stride // LANE
    gn = g[:MLA_NOPE].reshape(1, LANE)
    gr = jnp.concatenate([g[MLA_NOPE:], jnp.zeros((64,), F32)]).reshape(1, LANE)

    def body(n_ref, r_ref, gn_ref, gr_ref, c_ref, s_ref, o_ref):
        n, rr = n_ref[...], r_ref[...]
        ss = jnp.sum(n * n, axis=-1, keepdims=True) + jnp.sum(rr * rr, axis=-1, keepdims=True)
        r = lax.rsqrt(ss * (1.0 / MLA_QK) + EPS)
        lane = lax.broadcasted_iota(jnp.int32, rr.shape, 1)
        o_ref[0, :, :LANE] = (n * r * gn_ref[...]).astype(o_ref.dtype)
        o_ref[0, :, LANE:] = _rope(rr * r * gr_ref[...], c_ref[...], s_ref[...], lane).astype(o_ref.dtype)

    row = lambda h, i: (0, 0)
    return pl.pallas_call(
        body, name=name, out_shape=jax.ShapeDtypeStruct((heads, S, 2 * LANE), BF16), grid=(heads, S // ts),
        in_specs=[pl.BlockSpec((ts, LANE), lambda h, i: (i, nb + ns * h)),
                  pl.BlockSpec((ts, LANE), lambda h, i: (i, rb + rs * h)),
                  pl.BlockSpec((1, LANE), row), pl.BlockSpec((1, LANE), row),
                  pl.BlockSpec((ts, LANE), lambda h, i: (i, 0)), pl.BlockSpec((ts, LANE), lambda h, i: (i, 0))],
        out_specs=pl.BlockSpec((1, ts, 2 * LANE), lambda h, i: (h, i, 0)),
        compiler_params=_params("parallel", "parallel"),
    )(xn, xr, gn, gr, cos, ssin)


def mla_prep_bwd(xn, xr, g, cos, ssin, dy, *, n_col, n_stride, r_col, r_stride, heads, name, ts=1024):
    S = xn.shape[0]
    ts = _tile(S, ts)
    nb, ns, rb, rs = n_col // LANE, n_stride // LANE, r_col // LANE, r_stride // LANE
    shared = r_stride == 0
    gn = g[:MLA_NOPE].reshape(1, LANE)
    gr = jnp.concatenate([g[MLA_NOPE:], jnp.zeros((64,), F32)]).reshape(1, LANE)

    def body(n_ref, r_ref, gn_ref, gr_ref, c_ref, s_ref, dy_ref, dn_ref, dr_ref, dgn_ref, dgr_ref):
        i, h = pl.program_id(0), pl.program_id(1)
        n, rr = n_ref[...], r_ref[...]
        ss = jnp.sum(n * n, axis=-1, keepdims=True) + jnp.sum(rr * rr, axis=-1, keepdims=True)
        r = lax.rsqrt(ss * (1.0 / MLA_QK) + EPS)
        lane = lax.broadcasted_iota(jnp.int32, rr.shape, 1)
        dyn = dy_ref[0, :, :LANE]
        dyr = dy_ref[0, :, LANE:]
        t = dyr * s_ref[...]
        dvr = dyr * c_ref[...] + jnp.where(lane < 32, pltpu.roll(t, 96, 1), pltpu.roll(t, 32, 1))
        dvr = jnp.where(lane < 64, dvr, 0.0)
        dgn_part = jnp.sum(dyn * n * r, axis=0, keepdims=True)
        dgr_part = jnp.sum(dvr * rr * r, axis=0, keepdims=True)
        dyn_g, dvr_g = dyn * gn_ref[...], dvr * gr_ref[...]
        proj = (jnp.sum(dyn_g * n, axis=-1, keepdims=True) + jnp.sum(dvr_g * rr, axis=-1, keepdims=True)) * (1.0 / MLA_QK)
        r3 = r * r * r
        dn_ref[...] = r * dyn_g - n * r3 * proj
        dr = r * dvr_g - rr * r3 * proj
        if shared:
            @pl.when(h == 0)
            def _():
                dr_ref[...] = dr

            @pl.when(h > 0)
            def _():
                dr_ref[...] += dr
        else:
            dr_ref[...] = dr
        first = jnp.logical_and(i == 0, h == 0)

        @pl.when(first)
        def _():
            dgn_ref[...] = dgn_part
            dgr_ref[...] = dgr_part

        @pl.when(jnp.logical_not(first))
        def _():
            dgn_ref[...] += dgn_part
            dgr_ref[...] += dgr_part

    row = lambda i, h: (0, 0)
    dr_cols = LANE if shared else heads * LANE
    dn, dr, dgn, dgr = pl.pallas_call(
        body, name=name,
        out_shape=(jax.ShapeDtypeStruct((S, heads * LANE), F32), jax.ShapeDtypeStruct((S, dr_cols), F32),
                   jax.ShapeDtypeStruct((1, LANE), F32), jax.ShapeDtypeStruct((1, LANE), F32)),
        grid=(S // ts, heads),
        in_specs=[pl.BlockSpec((ts, LANE), lambda i, h: (i, nb + ns * h)),
                  pl.BlockSpec((ts, LANE), lambda i, h: (i, rb + rs * h)),
                  pl.BlockSpec((1, LANE), row), pl.BlockSpec((1, LANE), row),
                  pl.BlockSpec((ts, LANE), lambda i, h: (i, 0)), pl.BlockSpec((ts, LANE), lambda i, h: (i, 0)),
                  pl.BlockSpec((1, ts, 2 * LANE), lambda i, h: (h, i, 0))],
        out_specs=(pl.BlockSpec((ts, LANE), lambda i, h: (i, h)),
                   pl.BlockSpec((ts, LANE), (lambda i, h: (i, 0)) if shared else (lambda i, h: (i, h))),
                   pl.BlockSpec((1, LANE), row), pl.BlockSpec((1, LANE), row)),
        compiler_params=_params("arbitrary", "arbitrary"),
    )(xn, xr, gn, gr, cos, ssin, dy)
    return dn, dr, jnp.concatenate([dgn.reshape(LANE), dgr.reshape(LANE)[:MLA_ROPE]])


_NT = (((1,), (1,)), ((), ()))
_TN = (((0,), (0,)), ((), ()))


def attn_fwd(q, k, v, *, v_col, scale, name, bq=256):
    H, S, dk = q.shape
    Sk = k.shape[1]
    bq, vb = _tile(S, bq), v_col // LANE

    def body(q_ref, k_ref, v_ref, o_ref):
        s = lax.dot_general(q_ref[0], k_ref[0], _NT, preferred_element_type=F32) * scale
        e = jnp.exp(s - jnp.max(s, axis=-1, keepdims=True))
        p = e * (1.0 / jnp.sum(e, axis=-1, keepdims=True))
        o_ref[...] = jnp.dot(p.astype(BF16), v_ref[...].astype(BF16), preferred_element_type=F32).astype(o_ref.dtype)

    return pl.pallas_call(
        body, name=name, out_shape=jax.ShapeDtypeStruct((S, H * LANE), BF16), grid=(H, S // bq),
        in_specs=[pl.BlockSpec((1, bq, dk), lambda h, i: (h, i, 0)), pl.BlockSpec((1, Sk, dk), lambda h, i: (h, 0, 0)),
                  pl.BlockSpec((Sk, LANE), lambda h, i: (0, vb + h))],
        out_specs=pl.BlockSpec((bq, LANE), lambda h, i: (i, h)), compiler_params=_params("parallel", "parallel"),
    )(q, k, v)


def attn_bwd(q, k, v, do, *, v_col, scale, name, bq=256):
    H, S, dk = q.shape
    Sk = k.shape[1]
    bq, vb = _tile(S, bq), v_col // LANE

    def body(q_ref, k_ref, v_ref, do_ref, dq_ref, dk_ref, dv_ref):
        i = pl.program_id(1)
        qb, kb, vv = q_ref[0], k_ref[0], v_ref[...].astype(BF16)
        s = lax.dot_general(qb, kb, _NT, preferred_element_type=F32) * scale
        e = jnp.exp(s - jnp.max(s, axis=-1, keepdims=True))
        p = e * (1.0 / jnp.sum(e, axis=-1, keepdims=True))
        dob = do_ref[...].astype(BF16)
        dv_part = lax.dot_general(p.astype(BF16), dob, _TN, preferred_element_type=F32)
        dp = lax.dot_general(dob, vv, _NT, preferred_element_type=F32)
        ds = p * (dp - jnp.sum(p * dp, axis=-1, keepdims=True))
        dsb = (ds * scale).astype(BF16)
        dq_ref[0] = jnp.dot(dsb, kb, preferred_element_type=F32)
        dk_part = lax.dot_general(dsb, qb, _TN, preferred_element_type=F32)

        @pl.when(i == 0)
        def _():
            dk_ref[0] = dk_part
            dv_ref[...] = dv_part

        @pl.when(i > 0)
        def _():
            dk_ref[0] += dk_part
            dv_ref[...] += dv_part

    return pl.pallas_call(
        body, name=name,
        out_shape=(jax.ShapeDtypeStruct((H, S, dk), F32), jax.ShapeDtypeStruct((H, Sk, dk), F32),
                   jax.ShapeDtypeStruct((Sk, H * LANE), F32)),
        grid=(H, S // bq),
        in_specs=[pl.BlockSpec((1, bq, dk), lambda h, i: (h, i, 0)), pl.BlockSpec((1, Sk, dk), lambda h, i: (h, 0, 0)),
                  pl.BlockSpec((Sk, LANE), lambda h, i: (0, vb + h)), pl.BlockSpec((bq, LANE), lambda h, i: (i, h))],
        out_specs=(pl.BlockSpec((1, bq, dk), lambda h, i: (h, i, 0)), pl.BlockSpec((1, Sk, dk), lambda h, i: (h, 0, 0)),
                   pl.BlockSpec((Sk, LANE), lambda h, i: (0, h))),
        compiler_params=_params("parallel", "arbitrary"),
    )(q, k, v, do)


def _causal_scores(q, kblk, i, start, blk, scale, chunked, cq, ckblk):
    s = lax.dot_general(q, kblk, _NT, preferred_element_type=F32) * scale
    if cq is not None:
        s = s + cq - ckblk
    qpos = i * blk + lax.broadcasted_iota(jnp.int32, s.shape, 0)
    kpos = start + lax.broadcasted_iota(jnp.int32, s.shape, 1)
    ok = (kpos >> 6) <= (qpos >> 6) if chunked else kpos <= qpos
    return jnp.where(ok, s, NEG)


def causal_attn_fwd(q, k, v, *, v_col, chunked, scale, cq=None, ck=None, name, blk=256):
    H, S, dk = q.shape
    blk, vb = _tile(S, blk), v_col // LANE
    fox = cq is not None

    def body(q_ref, k_ref, v_ref, *rest):
        o_ref, lse_ref = rest[-2:]
        i = pl.program_id(1)
        qb = q_ref[0]

        def step(kb, carry):
            m, l, acc = carry
            start = pl.multiple_of(kb * blk, blk)
            s = _causal_scores(qb, k_ref[0, pl.ds(start, blk), :], i, start, blk, scale, chunked,
                               rest[0][0] if fox else None, rest[1][0, :, pl.ds(start, blk)] if fox else None)
            m_new = jnp.maximum(m, jnp.max(s, axis=-1, keepdims=True))
            alpha, p = jnp.exp(m - m_new), jnp.exp(s - m_new)
            pv = jnp.dot(p.astype(BF16), v_ref[pl.ds(start, blk), :].astype(BF16), preferred_element_type=F32)
            return m_new, alpha * l + jnp.sum(p, axis=-1, keepdims=True), alpha * acc + pv

        init = (jnp.full((blk, 1), NEG, F32), jnp.zeros((blk, 1), F32), jnp.zeros((blk, LANE), F32))
        m, l, acc = lax.fori_loop(0, i + 1, step, init)
        o_ref[...] = (acc * (1.0 / l)).astype(o_ref.dtype)
        lse_ref[0] = m + jnp.log(l)

    in_specs = [pl.BlockSpec((1, blk, dk), lambda h, i: (h, i, 0)), pl.BlockSpec((1, S, dk), lambda h, i: (h, 0, 0)),
                pl.BlockSpec((S, LANE), lambda h, i: (0, vb + h))]
    args = [q, k, v]
    if fox:
        in_specs += [pl.BlockSpec((1, blk, 1), lambda h, i: (h, i, 0)), pl.BlockSpec((1, 1, S), lambda h, i: (h, 0, 0))]
        args += [cq, ck]
    return pl.pallas_call(
        body, name=name, out_shape=(jax.ShapeDtypeStruct((S, H * LANE), BF16), jax.ShapeDtypeStruct((H, S, 1), F32)),
        grid=(H, S // blk), in_specs=in_specs,
        out_specs=(pl.BlockSpec((blk, LANE), lambda h, i: (i, h)), pl.BlockSpec((1, blk, 1), lambda h, i: (h, i, 0))),
        compiler_params=_params("parallel", "parallel"),
    )(*args)


def causal_attn_bwd(q, k, v, o, do, lse, *, v_col, chunked, scale, cq=None, ck=None, name, blk=256):
    H, S, dk = q.shape
    blk, vb = _tile(S, blk), v_col // LANE
    fox = cq is not None

    def body(q_ref, k_ref, v_ref, o_ref, do_ref, lse_ref, *rest):
        i = pl.program_id(1)
        if fox:
            cq_ref, ck_ref, dq_ref, dk_ref, dv_ref, dcq_ref, dck_ref = rest
        else:
            dq_ref, dk_ref, dv_ref = rest

        @pl.when(i == 0)
        def _():
            dk_ref[...] = jnp.zeros_like(dk_ref)
            dv_ref[...] = jnp.zeros_like(dv_ref)
            if fox:
                dck_ref[...] = jnp.zeros_like(dck_ref)

        qb, dob, lse_b = q_ref[0], do_ref[...].astype(BF16), lse_ref[0]
        delta = jnp.sum(do_ref[...].astype(F32) * o_ref[...].astype(F32), axis=-1, keepdims=True)

        def step(kb, carry):
            dq, dcq = carry
            start = pl.multiple_of(kb * blk, blk)
            kblk = k_ref[0, pl.ds(start, blk), :]
            s = _causal_scores(qb, kblk, i, start, blk, scale, chunked,
                               cq_ref[0] if fox else None, ck_ref[0, :, pl.ds(start, blk)] if fox else None)
            p = jnp.exp(s - lse_b)
            dv_ref[pl.ds(start, blk), :] += lax.dot_general(p.astype(BF16), dob, _TN, preferred_element_type=F32)
            dp = lax.dot_general(dob, v_ref[pl.ds(start, blk), :].astype(BF16), _NT, preferred_element_type=F32)
            ds = p * (dp - delta)
            dsb = (ds * scale).astype(BF16)
            dk_ref[0, pl.ds(start, blk), :] += lax.dot_general(dsb, qb, _TN, preferred_element_type=F32)
            if fox:
                dck_ref[0, :, pl.ds(start, blk)] += -jnp.sum(ds, axis=0, keepdims=True)
                dcq = dcq + jnp.sum(ds, axis=-1, keepdims=True)
            return dq + jnp.dot(dsb, kblk, preferred_element_type=F32), dcq

        dq, dcq = lax.fori_loop(0, i + 1, step, (jnp.zeros((blk, dk), F32), jnp.zeros((blk, 1), F32)))
        dq_ref[0] = dq
        if fox:
            dcq_ref[0] = dcq

    row = pl.BlockSpec((blk, LANE), lambda h, i: (i, h))
    in_specs = [pl.BlockSpec((1, blk, dk), lambda h, i: (h, i, 0)), pl.BlockSpec((1, S, dk), lambda h, i: (h, 0, 0)),
                pl.BlockSpec((S, LANE), lambda h, i: (0, vb + h)), row, row, pl.BlockSpec((1, blk, 1), lambda h, i: (h, i, 0))]
    args = [q, k, v, o, do, lse]
    out_shape = [jax.ShapeDtypeStruct((H, S, dk), F32), jax.ShapeDtypeStruct((H, S, dk), F32),
                 jax.ShapeDtypeStruct((S, H * LANE), F32)]
    out_specs = [pl.BlockSpec((1, blk, dk), lambda h, i: (h, i, 0)), pl.BlockSpec((1, S, dk), lambda h, i: (h, 0, 0)),
                 pl.BlockSpec((S, LANE), lambda h, i: (0, h))]
    if fox:
        fox_specs = [pl.BlockSpec((1, blk, 1), lambda h, i: (h, i, 0)), pl.BlockSpec((1, 1, S), lambda h, i: (h, 0, 0))]
        in_specs += fox_specs
        args += [cq, ck]
        out_shape += [jax.ShapeDtypeStruct((H, S, 1), F32), jax.ShapeDtypeStruct((H, 1, S), F32)]
        out_specs += fox_specs
    return pl.pallas_call(
        body, name=name, out_shape=tuple(out_shape), grid=(H, S // blk), in_specs=in_specs, out_specs=tuple(out_specs),
        compiler_params=_params("parallel", "arbitrary"),
    )(*args)


CPB = 4
BANDW = BAND + CHUNK


def _band_probs(qc, kb, bias, start, scale):
    s = lax.dot_general(qc, kb, _NT, preferred_element_type=F32) * scale
    col = lax.broadcasted_iota(jnp.int32, s.shape, 1)
    valid = jnp.logical_and(start + col >= PAD, col < BAND)
    s = jnp.where(valid, s + bias, NEG)
    e = jnp.exp(s - jnp.max(s, axis=-1, keepdims=True))
    return e * (1.0 / jnp.sum(e, axis=-1, keepdims=True))


def band_fwd(q, kp, vp, bias, *, scale, name):
    H, S, _ = q.shape
    Sp, rows = S + PAD + CHUNK, CPB * CHUNK

    def body(q_ref, k_ref, v_ref, b_ref, o_ref):
        j = pl.program_id(1)
        for cc in range(CPB):
            start = pl.multiple_of((j * CPB + cc) * CHUNK, CHUNK)
            kb = k_ref[0, pl.ds(start, BANDW), :]
            vb = v_ref[pl.ds(start, BANDW), :].astype(BF16)
            p = _band_probs(q_ref[0, cc * CHUNK:(cc + 1) * CHUNK, :], kb, b_ref[0], start, scale)
            o_ref[cc * CHUNK:(cc + 1) * CHUNK, :] = jnp.dot(p.astype(BF16), vb, preferred_element_type=F32).astype(o_ref.dtype)

    return pl.pallas_call(
        body, name=name, out_shape=jax.ShapeDtypeStruct((S, H * LANE), BF16), grid=(H, S // rows),
        in_specs=[pl.BlockSpec((1, rows, LANE), lambda h, j: (h, j, 0)), pl.BlockSpec((1, Sp, LANE), lambda h, j: (h, 0, 0)),
                  pl.BlockSpec((Sp, LANE), lambda h, j: (0, h)), pl.BlockSpec((1, CHUNK, BANDW), lambda h, j: (h, 0, 0))],
        out_specs=pl.BlockSpec((rows, LANE), lambda h, j: (j, h)), compiler_params=_params("parallel", "parallel"),
    )(q, kp, vp, bias)


def band_bwd(q, kp, vp, bias, do, *, scale, name):
    H, S, _ = q.shape
    Sp, rows = S + PAD + CHUNK, CPB * CHUNK

    def body(q_ref, k_ref, v_ref, b_ref, do_ref, dq_ref, dk_ref, dv_ref, db_ref):
        j = pl.program_id(1)

        @pl.when(j == 0)
        def _():
            dk_ref[...] = jnp.zeros_like(dk_ref)
            dv_ref[...] = jnp.zeros_like(dv_ref)
            db_ref[...] = jnp.zeros_like(db_ref)

        for cc in range(CPB):
            start = pl.multiple_of((j * CPB + cc) * CHUNK, CHUNK)
            sl = slice(cc * CHUNK, (cc + 1) * CHUNK)
            qc = q_ref[0, sl, :]
            kb = k_ref[0, pl.ds(start, BANDW), :]
            vb = v_ref[pl.ds(start, BANDW), :].astype(BF16)
            p = _band_probs(qc, kb, b_ref[0], start, scale)
            dob = do_ref[sl, :].astype(BF16)
            dv_ref[pl.ds(start, BANDW), :] += lax.dot_general(p.astype(BF16), dob, _TN, preferred_element_type=F32)
            dp = lax.dot_general(dob, vb, _NT, preferred_element_type=F32)
            ds = p * (dp - jnp.sum(p * dp, axis=-1, keepdims=True))
            db_ref[0] += ds
            dsb = (ds * scale).astype(BF16)
            dq_ref[0, sl, :] = jnp.dot(dsb, kb, preferred_element_type=F32)
            dk_ref[0, pl.ds(start, BANDW), :] += lax.dot_general(dsb, qc, _TN, preferred_element_type=F32)

    return pl.pallas_call(
        body, name=name,
        out_shape=(jax.ShapeDtypeStruct((H, S, LANE), F32), jax.ShapeDtypeStruct((H, Sp, LANE), F32),
                   jax.ShapeDtypeStruct((Sp, H * LANE), F32), jax.ShapeDtypeStruct((H, CHUNK, BANDW), F32)),
        grid=(H, S // rows),
        in_specs=[pl.BlockSpec((1, rows, LANE), lambda h, j: (h, j, 0)), pl.BlockSpec((1, Sp, LANE), lambda h, j: (h, 0, 0)),
                  pl.BlockSpec((Sp, LANE), lambda h, j: (0, h)), pl.BlockSpec((1, CHUNK, BANDW), lambda h, j: (h, 0, 0)),
                  pl.BlockSpec((rows, LANE), lambda h, j: (j, h))],
        out_specs=(pl.BlockSpec((1, rows, LANE), lambda h, j: (h, j, 0)), pl.BlockSpec((1, Sp, LANE), lambda h, j: (h, 0, 0)),
                   pl.BlockSpec((Sp, LANE), lambda h, j: (0, h)), pl.BlockSpec((1, CHUNK, BANDW), lambda h, j: (h, 0, 0))),
        compiler_params=_params("parallel", "arbitrary"),
    )(q, kp, vp, bias, do)


def band_bias(rel_bias, *, name):
    H = rel_bias.shape[0]
    last = rel_bias[:, 2 * REL_CLIP:]
    row0 = jnp.concatenate([jnp.tile(last, (1, PAD - REL_CLIP)), rel_bias[:, CHUNK + 1:][:, ::-1],
                            jnp.tile(last, (1, CHUNK))], axis=1)

    def body(r_ref, o_ref):
        o_ref[0] = pltpu.roll(jnp.broadcast_to(r_ref[0], (CHUNK, BANDW)), 0, 1, stride=1, stride_axis=0)

    return pl.pallas_call(
        body, name=name, out_shape=jax.ShapeDtypeStruct((H, CHUNK, BANDW), F32), grid=(H,),
        in_specs=[pl.BlockSpec((1, 1, BANDW), lambda h: (h, 0, 0))], out_specs=pl.BlockSpec((1, CHUNK, BANDW), lambda h: (h, 0, 0)),
        compiler_params=_params("parallel"),
    )(row0.reshape(H, 1, BANDW))


def relbias_bwd(dbias, *, name):
    H, W = dbias.shape[0], BANDW
    x = jnp.pad(dbias[:, :, :BAND][:, :, ::-1], ((0, 0), (0, 0), (0, CHUNK)))

    def body(x_ref, o_ref):
        skew = pltpu.roll(x_ref[0], 0, 1, stride=1, stride_axis=0)
        f = jnp.broadcast_to(jnp.sum(skew, axis=0, keepdims=True), (8, W))
        lane = lax.broadcasted_iota(jnp.int32, (8, W), 1)
        direct = jnp.where(jnp.logical_and(lane >= 65, lane <= 255), pltpu.roll(f, 65, 1), 0.0)
        tail = jnp.sum(jnp.where(lane >= 191, f, 0.0), axis=-1, keepdims=True)
        o_ref[0] = direct + jnp.where(lane == 2 * REL_CLIP, tail, 0.0)

    out = pl.pallas_call(
        body, name=name, out_shape=jax.ShapeDtypeStruct((H, 8, W), F32), grid=(H,),
        in_specs=[pl.BlockSpec((1, CHUNK, W), lambda h: (h, 0, 0))], out_specs=pl.BlockSpec((1, 8, W), lambda h: (h, 0, 0)),
        compiler_params=_params("parallel"),
    )(x)
    return out[:, 0, :2 * REL_CLIP + 1]


def _split_dot(x, u, dn):
    hi = x.astype(BF16)
    r1 = x - hi.astype(F32)
    mid = r1.astype(BF16)
    lo = (r1 - mid.astype(F32)).astype(BF16)
    d = lambda t: lax.dot_general(t, u, dn, preferred_element_type=F32)
    return d(hi) + d(mid) + d(lo)


def _upper_ones(S):
    return (np.arange(S)[:, None] <= np.arange(S)[None, :]).astype(np.float32)


def foxgate_fwd(fl, b, *, name):
    H, S = fl.shape
    u = jnp.asarray(_upper_ones(S), BF16)

    def body(f_ref, b_ref, u_ref, o_ref):
        x = f_ref[...] + b_ref[...]
        lf = jnp.minimum(x, 0.0) - jnp.log(1.0 + jnp.exp(-jnp.abs(x)))
        o_ref[...] = _split_dot(lf, u_ref[...], (((1,), (0,)), ((), ())))

    return pl.pallas_call(body, name=name, out_shape=jax.ShapeDtypeStruct((H, S), F32),
                          compiler_params=pltpu.CompilerParams(vmem_limit_bytes=VMEM_LIMIT_BYTES))(fl, b.reshape(H, 1), u)


def foxgate_bwd(fl, b, dcum, *, name):
    H, S = fl.shape
    u = jnp.asarray(_upper_ones(S), BF16)

    def body(f_ref, b_ref, u_ref, dc_ref, df_ref, db_ref):
        x = f_ref[...] + b_ref[...]
        dlf = _split_dot(dc_ref[...], u_ref[...], _NT)
        df = dlf * (1.0 / (1.0 + jnp.exp(x)))
        df_ref[...] = df
        db_ref[...] = jnp.sum(df, axis=-1, keepdims=True)

    df, db = pl.pallas_call(body, name=name,
                            out_shape=(jax.ShapeDtypeStruct((H, S), F32), jax.ShapeDtypeStruct((H, 1), F32)),
                            compiler_params=pltpu.CompilerParams(vmem_limit_bytes=VMEM_LIMIT_BYTES))(fl, b.reshape(H, 1), u, dcum)
    return df, db.reshape(H)


def gate_fwd(z, proj, *, name, ts=256, tc=512):
    S, D = proj[0].shape
    ts, gb, nb = _tile(S, ts), Z_GATE // tc, D // tc

    def body(g0, g1, g2, p0, p1, p2, o_ref):
        acc = None
        for g_ref, p_ref in zip((g0, g1, g2), (p0, p1, p2)):
            t = (1.0 / (1.0 + jnp.exp(-g_ref[...]))) * p_ref[...]
            acc = t if acc is None else acc + t
        o_ref[...] = acc.astype(o_ref.dtype)

    blk = pl.BlockSpec((ts, tc), lambda i, j: (i, j))
    return pl.pallas_call(
        body, name=name, out_shape=jax.ShapeDtypeStruct((S, D), BF16), grid=(S // ts, nb),
        in_specs=[pl.BlockSpec((ts, tc), lambda i, j, n=n: (i, gb + n * nb + j)) for n in range(3)] + [blk] * 3,
        out_specs=blk, compiler_params=_params("parallel", "parallel"),
    )(z, z, z, *proj)


def gate_bwd(z, proj, dm, *, name, ts=256, tc=512):
    S, D = proj[0].shape
    ts, gb, nb = _tile(S, ts), Z_GATE // tc, D // tc

    def body(g0, g1, g2, p0, p1, p2, dm_ref, *outs):
        dmv = dm_ref[...]
        for n, (g_ref, p_ref) in enumerate(zip((g0, g1, g2), (p0, p1, p2))):
            sg = 1.0 / (1.0 + jnp.exp(-g_ref[...]))
            outs[n][...] = (dmv * sg).astype(BF16)
            outs[3 + n][...] = (dmv * p_ref[...] * sg * (1.0 - sg)).astype(BF16)

    blk = pl.BlockSpec((ts, tc), lambda i, j: (i, j))
    outs = pl.pallas_call(
        body, name=name, out_shape=tuple(jax.ShapeDtypeStruct((S, D), BF16) for _ in range(6)), grid=(S // ts, nb),
        in_specs=[pl.BlockSpec((ts, tc), lambda i, j, n=n: (i, gb + n * nb + j)) for n in range(3)] + [blk] * 4,
        out_specs=(blk,) * 6, compiler_params=_params("parallel", "parallel"),
    )(z, z, z, *proj, dm)
    return outs[:3], outs[3:]


def loss_head(y, target, *, name, ts=256):
    S, D = y.shape
    ts = _tile(S, ts)

    def body(y_ref, t_ref, l_ref, dy_ref):
        err = y_ref[...] - t_ref[...]
        dy_ref[...] = err * (1.0 / D)
        part = 0.5 * jnp.sum(jnp.mean(err * err, axis=-1, keepdims=True), axis=0, keepdims=True)

        @pl.when(pl.program_id(0) == 0)
        def _():
            l_ref[...] = part

        @pl.when(pl.program_id(0) > 0)
        def _():
            l_ref[...] += part

    blk = pl.BlockSpec((ts, D), lambda i: (i, 0))
    return pl.pallas_call(
        body, name=name, out_shape=(jax.ShapeDtypeStruct((1, 1), F32), jax.ShapeDtypeStruct((S, D), F32)), grid=(S // ts,),
        in_specs=[blk, blk], out_specs=(pl.BlockSpec((1, 1), lambda i: (0, 0)), blk), compiler_params=_params("arbitrary"),
    )(y, target)


def adamw(w, g, m, v, *, name):
    shape = w.shape
    C = shape[-1]
    R = int(np.prod(shape[:-1]))
    br = R
    while br % 16 == 0 and br * C * 4 > 2**20:
        br //= 2
    w2, g2, m2, v2 = (t.reshape(R, C) for t in (w, g, m, v))

    def body(w_ref, g_ref, m_ref, v_ref, d_ref, nm_ref, nv_ref):
        gg = g_ref[...]
        nm = ADAM_B1 * m_ref[...] + (1.0 - ADAM_B1) * gg
        nv = ADAM_B2 * v_ref[...] + (1.0 - ADAM_B2) * (gg * gg)
        m_hat = nm / (1.0 - ADAM_B1 ** ADAM_STEP)
        v_hat = nv / (1.0 - ADAM_B2 ** ADAM_STEP)
        d_ref[...] = -ADAM_LR * (m_hat / (jnp.sqrt(v_hat) + ADAM_EPS) + ADAM_WD * w_ref[...])
        nm_ref[...] = nm
        nv_ref[...] = nv

    blk = pl.BlockSpec((br, C), lambda i: (i, 0))
    outs = pl.pallas_call(
        body, name=name, out_shape=tuple(jax.ShapeDtypeStruct((R, C), F32) for _ in range(3)), grid=(R // br,),
        in_specs=[blk] * 4, out_specs=(blk,) * 3, compiler_params=_params("parallel"),
    )(w2, g2, m2, v2)
    return tuple(o.reshape(shape) for o in outs)


_ANY = pl.BlockSpec(memory_space=pl.ANY)


def _place():
    return lax.axis_index("x"), lax.axis_index("y"), lax.axis_index("c")


def all_gather(xs, *, name):
    n = len(xs)

    def body(*refs):
        x_refs, o_refs = refs[:n], refs[n:2 * n]
        send_sems, recv_sems, local_sems = refs[2 * n:]
        px, py, pc = _place()
        me, sibling = (px, py, pc), (px, py, 1 - pc)
        chips = [(1 - px, py), (px, 1 - py), (1 - px, 1 - py)]

        def slot(t, dev):
            return o_refs[t].at[4 * dev[0] + 2 * dev[1] + dev[2]]

        def copy(t, k, block, to, src=None):
            return pltpu.make_async_remote_copy(
                src_ref=slot(t, block) if src is None else src, dst_ref=slot(t, block),
                send_sem=send_sems.at[t, k], recv_sem=recv_sems.at[t, k], device_id=to, device_id_type=MESH)

        mine = [pltpu.make_async_copy(x_refs[t], slot(t, me), local_sems.at[t]) for t in range(n)]
        first = []
        for t in range(n):
            mine[t].start()
            first += [copy(t, 1 + j, me, (*chip, pc), src=x_refs[t]) for j, chip in enumerate(chips)]
            first.append(copy(t, 0, me, sibling, src=x_refs[t]))
        for cp in first:
            cp.start()
        passed = []
        for t in range(n):
            for j, chip in enumerate(chips):
                copy(t, 1 + j, (*chip, pc), me).wait_recv()
                fwd = copy(t, 4 + j, (*chip, pc), sibling)
                fwd.start()
                passed.append(fwd)
        for t in range(n):
            copy(t, 0, sibling, me).wait_recv()
            for j, chip in enumerate(chips):
                copy(t, 4 + j, (*chip, 1 - pc), me).wait_recv()
        for cp in first + passed:
            cp.wait_send()
        for cp in mine:
            cp.wait()

    return pl.pallas_call(
        body, name=name, out_shape=tuple(jax.ShapeDtypeStruct((8,) + x.shape, x.dtype) for x in xs),
        in_specs=[_ANY] * n, out_specs=(_ANY,) * n,
        scratch_shapes=[pltpu.SemaphoreType.DMA((n, 7)), pltpu.SemaphoreType.DMA((n, 7)), pltpu.SemaphoreType.DMA((n,))],
    )(*xs)


def exchange_sibling(gs, *, name):
    n = len(gs)

    def body(*refs):
        g_refs, o_refs, (send_sems, recv_sems) = refs[:n], refs[n:2 * n], refs[2 * n:]
        px, py, pc = _place()
        copies = []
        for t in range(n):
            for j in range(4):
                copies.append(pltpu.make_async_remote_copy(
                    src_ref=g_refs[t].at[2 * j + 1 - pc], dst_ref=o_refs[t].at[j],
                    send_sem=send_sems.at[t, j], recv_sem=recv_sems.at[t, j],
                    device_id=(px, py, 1 - pc), device_id_type=MESH))
        for cp in copies:
            cp.start()
        for cp in copies:
            cp.wait()

    return pl.pallas_call(
        body, name=name, out_shape=tuple(jax.ShapeDtypeStruct((4,) + g.shape[1:], g.dtype) for g in gs),
        in_specs=[_ANY] * n, out_specs=(_ANY,) * n,
        scratch_shapes=[pltpu.SemaphoreType.DMA((n, 4)), pltpu.SemaphoreType.DMA((n, 4))],
    )(*gs)


def exchange_chips(hs, *, name):
    n = len(hs)

    def body(*refs):
        h_refs, o_refs, (send_sems, recv_sems) = refs[:n], refs[n:2 * n], refs[2 * n:]
        px, py, pc = _place()
        chips = [(1 - px, py), (px, 1 - py), (1 - px, 1 - py)]
        copies = []
        for t in range(n):
            for k, chip in enumerate(chips):
                copies.append(pltpu.make_async_remote_copy(
                    src_ref=h_refs[t].at[2 * chip[0] + chip[1]], dst_ref=o_refs[t].at[k],
                    send_sem=send_sems.at[t, k], recv_sem=recv_sems.at[t, k],
                    device_id=(*chip, pc), device_id_type=MESH))
        for cp in copies:
            cp.start()
        for cp in copies:
            cp.wait()

    return pl.pallas_call(
        body, name=name, out_shape=tuple(jax.ShapeDtypeStruct((3,) + h.shape[1:], h.dtype) for h in hs),
        in_specs=[_ANY] * n, out_specs=(_ANY,) * n,
        scratch_shapes=[pltpu.SemaphoreType.DMA((n, 3)), pltpu.SemaphoreType.DMA((n, 3))],
    )(*hs)


def _row_block(rows, cols, itemsize, budget=2**20):
    br = rows
    while br % 32 == 0 and br * cols * itemsize > budget:
        br //= 2
    return br


def pair_sum(g, other, *, name):
    shape, C = g.shape[1:], g.shape[-1]
    R = int(np.prod(shape[:-1]))
    br = _row_block(R, C, 2)
    pc = lax.axis_index("c").astype(jnp.int32).reshape(1)

    def body(c_ref, g_ref, o_ref, out_ref):
        out_ref[...] = (g_ref[...].astype(F32) + o_ref[...].astype(F32)).astype(out_ref.dtype)

    blk = lambda f: pl.BlockSpec((1, br, C), f)
    out = pl.pallas_call(
        body, name=name, out_shape=jax.ShapeDtypeStruct((4, R, C), BF16),
        grid_spec=pltpu.PrefetchScalarGridSpec(
            num_scalar_prefetch=1, grid=(4, R // br),
            in_specs=[blk(lambda j, r, c: (2 * j + c[0], r, 0)), blk(lambda j, r, c: (j, r, 0))],
            out_specs=blk(lambda j, r, c: (j, r, 0))),
        compiler_params=_params("parallel", "parallel"),
    )(pc, g.reshape(8, R, C), other.reshape(4, R, C))
    return out.reshape((4,) + shape)


def chip_sum(h, recv, *, name):
    shape, C = h.shape[1:], h.shape[-1]
    R = int(np.prod(shape[:-1]))
    br = _row_block(R, C, 4)
    chip = (2 * lax.axis_index("x") + lax.axis_index("y")).astype(jnp.int32).reshape(1)

    def body(c_ref, h_ref, r_ref, out_ref):
        acc = h_ref[0].astype(F32)
        for k in range(3):
            acc = acc + r_ref[k].astype(F32)
        out_ref[...] = acc

    out = pl.pallas_call(
        body, name=name, out_shape=jax.ShapeDtypeStruct((R, C), F32),
        grid_spec=pltpu.PrefetchScalarGridSpec(
            num_scalar_prefetch=1, grid=(R // br,),
            in_specs=[pl.BlockSpec((1, br, C), lambda r, c: (c[0], r, 0)), pl.BlockSpec((3, br, C), lambda r, c: (0, r, 0))],
            out_specs=pl.BlockSpec((br, C), lambda r, c: (r, 0))),
        compiler_params=_params("parallel"),
    )(chip, h.reshape(4, R, C), recv.reshape(3, R, C))
    return out.reshape(shape)


def ordered_sum(parts, *, name):
    _, R, C = parts.shape

    def body(p_ref, o_ref):
        acc = p_ref[0]
        for d in range(1, 8):
            acc = acc + p_ref[d]
        o_ref[...] = acc

    return pl.pallas_call(body, name=name, out_shape=jax.ShapeDtypeStruct((R, C), F32))(parts)


W_IN_COLS, W_IN_SHARD = 13128, 1641
W_IN_SEGMENTS = ((0, 832, 0), (832, 3904, Z_FOX), (3904, 3912, FF_COL), (3912, 6984, Z_CH), (6984, 13128, Z_GATE))


def col_gather(src, table, pieces, out_shape, *, name, tr=1024):
    R, C = src.shape[1:]
    tr = _tile(R, tr)
    width = 2 + 6 * pieces
    nb = table.shape[0] // width
    last_tile, last_valid = C // LANE, C % LANE

    def body(tab, *refs):
        o_ref = refs[-1]
        base = pl.program_id(1) * width
        lane = lax.broadcasted_iota(jnp.int32, (tr, LANE), 1)
        row = lax.broadcasted_iota(jnp.int32, (2 * LANE * pieces, LANE), 0)
        col = lax.broadcasted_iota(jnp.int32, (2 * LANE * pieces, LANE), 1)
        tiles, hit = [], None
        for p in range(pieces):
            e = base + 2 + 6 * p
            for tcol in (1, 2):
                x = refs[2 * p + tcol - 1][0]
                if last_valid:
                    x = jnp.where(jnp.logical_or(tab[e + tcol] < last_tile, lane < last_valid), x, jnp.zeros_like(x))
                tiles.append(x)
            lo, hi = tab[e + 4], tab[e + 5]
            cond = jnp.logical_and(row - 2 * LANE * p == col + tab[e + 3], jnp.logical_and(col >= lo, col < hi))
            hit = cond if hit is None else jnp.logical_or(hit, cond)
        sel = jnp.where(hit, 1.0, 0.0).astype(src.dtype)
        o_ref[0] = jnp.dot(jnp.concatenate(tiles, axis=1), sel, preferred_element_type=F32).astype(o_ref.dtype)

    in_specs = []
    for p in range(pieces):
        for tcol in (1, 2):
            in_specs.append(pl.BlockSpec(
                (1, tr, LANE), lambda i, b, tab, p=p, tcol=tcol: (tab[b * width + 2 + 6 * p], i, tab[b * width + 2 + 6 * p + tcol])))
    return pl.pallas_call(
        body, name=name, out_shape=jax.ShapeDtypeStruct(out_shape, src.dtype),
        grid_spec=pltpu.PrefetchScalarGridSpec(
            num_scalar_prefetch=1, grid=(R // tr, nb), in_specs=in_specs,
            out_specs=pl.BlockSpec((1, tr, LANE), lambda i, b, tab: (tab[b * width], i, tab[b * width + 1]))),
        compiler_params=_params("parallel", "parallel"),
    )(jnp.asarray(table, jnp.int32), *([src] * (2 * pieces)))


def _piece(sd, start, lo, hi, last_tile):
    t0 = start // LANE
    return [sd, t0, min(t0 + 1, last_tile), start % LANE - lo, lo, hi]


def _pad_pieces(rows, pieces):
    out, prev = [], [0, 0, 0, 0, 0, 0] * pieces
    for head, pcs in rows:
        full = list(pcs)
        for p in range(len(pcs) // 6, pieces):
            full += prev[6 * p:6 * p + 3] + [0, 0, 0]
        out.append(head + full)
        prev = full
    return np.asarray(out, np.int32).reshape(-1)


def _w_in_table(layer, L):
    rows = []
    for b in range(Z_W // LANE):
        pcs = []
        for first, last, col in W_IN_SEGMENTS:
            lo, hi = max(LANE * b, col), min(LANE * (b + 1), col + last - first)
            while lo < hi:
                c = first + lo - col
                n = min(hi - lo, W_IN_SHARD - c % W_IN_SHARD)
                pcs += _piece((c // W_IN_SHARD) * L + layer, c % W_IN_SHARD, lo - LANE * b, lo - LANE * b + n, W_IN_SHARD // LANE)
                lo += n
        assert len(pcs) <= 12
        rows.append(([0, b], pcs))
    return _pad_pieces(rows, 2)


def _w_in_grad_table():
    rows = []
    for d in range(8):
        for t in range(-(-W_IN_SHARD // LANE)):
            pcs = []
            c0 = d * W_IN_SHARD + LANE * t
            c1 = min(c0 + LANE, (d + 1) * W_IN_SHARD)
            for first, last, col in W_IN_SEGMENTS:
                lo, hi = max(c0, first), min(c1, last)
                if lo < hi:
                    pcs += _piece(0, col + lo - first, lo - c0, hi - c0, Z_W // LANE - 1)
            assert len(pcs) <= 18
            rows.append(([d, t], pcs))
    return _pad_pieces(rows, 3)


def _full_from_shards(k, sh):
    if k not in COL_SHARDED:
        return sh.reshape((-1, sh.shape[-1]))
    if k == 'w_br':
        return [jnp.transpose(sh[:, n], (1, 0, 2)).reshape(1024, 2048) for n in range(3)]
    full = jnp.transpose(sh, (1, 0, 2)).reshape(sh.shape[1], -1)
    if k == 'w_uq':
        return jnp.pad(full.reshape(512, MLA_HEADS, MLA_QK), ((0, 0), (0, 0), (0, 64))).reshape(512, 2048)
    if k == 'w_ukv':
        return jnp.transpose(full.reshape(256, MLA_HEADS, 2, LANE), (0, 2, 1, 3)).reshape(256, 2048)
    return full


def _shards_from_full(k, g):
    if k not in COL_SHARDED:
        return g.reshape((8, g.shape[0] // 8, g.shape[1]))
    if k == 'w_br':
        return jnp.stack([jnp.transpose(g[n].reshape(1024, 8, 256), (1, 0, 2)) for n in range(3)], axis=1)
    if k == 'w_uq':
        g = g.reshape(512, MLA_HEADS, 256)[:, :, :MLA_QK].reshape(512, 1536)
    elif k == 'w_ukv':
        g = jnp.transpose(g.reshape(256, 2, MLA_HEADS, LANE), (0, 2, 1, 3)).reshape(256, 2048)
    return jnp.transpose(g.reshape(g.shape[0], 8, g.shape[1] // 8), (1, 0, 2))


def w_in_full(gathered, layer, *, name):
    _, L, K, c = gathered.shape
    return col_gather(gathered.reshape(8 * L, K, c), _w_in_table(layer, L), 2, (1, K, Z_W), name=name)[0]


def w_in_shards(g, *, name):
    return col_gather(g[None], _w_in_grad_table(), 3, (8, g.shape[0], W_IN_SHARD), name=name)


def _layer_fwd(x, mem, W, P, cos, ssin, tag):
    S = x.shape[0]
    sv = {'x0': x}
    h = rmsnorm_fwd(x, P['g_mix'], name=f"{tag}_norm_mix")
    z = mm(h, W['w_in'], name=f"{tag}_mm_in")
    sv.update(h=h, z=z)
    cqn = rmsnorm_fwd(z, P['g_cq'], col=0, width=512, name=f"{tag}_norm_cq")
    ckvn = rmsnorm_fwd(z, P['g_ckv'], col=512, width=256, name=f"{tag}_norm_ckv")
    qf = mm(cqn, W['w_uq'], name=f"{tag}_mm_uq")
    kvf = mm(ckvn, W['w_ukv'], name=f"{tag}_mm_ukv")
    qa = mla_prep_fwd(qf, qf, P['g_mla_q'], cos, ssin, n_col=0, n_stride=2 * LANE, r_col=LANE, r_stride=2 * LANE,
                      heads=8, name=f"{tag}_mla_q")
    ka = mla_prep_fwd(kvf, z, P['g_mla_k'], cos, ssin, n_col=0, n_stride=LANE, r_col=KR_COL, r_stride=0,
                      heads=8, name=f"{tag}_mla_k")
    ya, lse_a = causal_attn_fwd(qa, ka, kvf, v_col=1024, chunked=True, scale=MLA_QK ** -0.5, name=f"{tag}_mla_attn")
    sv.update(cqn=cqn, ckvn=ckvn, qf=qf, kvf=kvf, qa=qa, ka=ka, lse_a=lse_a)
    qb = headnorm_fwd(z, P['g_fox_q'], col=Z_FOX, heads=8, name=f"{tag}_fox_qn")
    kb = headnorm_fwd(z, P['g_fox_k'], col=Z_FOX + 1024, heads=8, name=f"{tag}_fox_kn")
    fl = z[:, FF_COL:FF_COL + 8].T
    cum = foxgate_fwd(fl, P['b_f'], name=f"{tag}_fox_gate")
    cq, ck = cum.reshape(8, S, 1), cum.reshape(8, 1, S)
    yb, lse_b = causal_attn_fwd(qb, kb, z, v_col=Z_FOX + 2048, chunked=False, scale=LANE ** -0.5, cq=cq, ck=ck,
                                name=f"{tag}_fox_attn")
    sv.update(qb=qb, kb=kb, fl=fl, cq=cq, ck=ck, lse_b=lse_b)
    qc = headnorm_fwd(z, P['g_ch_q'], col=Z_CH, heads=8, name=f"{tag}_ch_qn")
    kc = headnorm_fwd(z, P['g_ch_k'], col=Z_CH + 1024, heads=8, name=f"{tag}_ch_kn")
    kcp = jnp.pad(kc, ((0, 0), (PAD, CHUNK), (0, 0)))
    vcp = jnp.pad(z[:, Z_CH + 2048:Z_CH + 3072], ((PAD, CHUNK), (0, 0)))
    bias = band_bias(P['rel_bias'], name=f"{tag}_ch_bias")
    yc = band_fwd(qc, kcp, vcp, bias, scale=LANE ** -0.5, name=f"{tag}_ch_attn")
    sv.update(qc=qc, kcp=kcp, vcp=vcp, bias=bias)
    ys = (ya, yb, yc)
    proj = [mm(ys[n], W['w_br'][n], name=f"{tag}_mm_br{n}") for n in range(3)]
    merged = gate_fwd(z, proj, name=f"{tag}_gate")
    x1 = mm(merged, W['w_out'], epi='add', aux=x, name=f"{tag}_mm_out")
    sv.update(ys=ys, proj=proj, merged=merged, x1=x1)
    hc = rmsnorm_fwd(x1, P['g_cross'], name=f"{tag}_norm_cross")
    memn = rmsnorm_fwd(mem, P['g_mem'], name=f"{tag}_norm_mem")
    qx_raw = mm(hc, W['w_xq'], name=f"{tag}_mm_xq")
    memkv = mm(memn, W['w_xkv'], name=f"{tag}_mm_xkv")
    qx = headnorm_fwd(qx_raw, P['g_x_q'], col=0, heads=4, name=f"{tag}_x_qn")
    kx = headnorm_fwd(memkv, P['g_x_k'], col=0, heads=4, name=f"{tag}_x_kn")
    ox = attn_fwd(qx, kx, memkv, v_col=512, scale=LANE ** -0.5, name=f"{tag}_x_attn")
    x2 = mm(ox, W['w_xo'], epi='add', aux=x1, name=f"{tag}_mm_xo")
    sv.update(hc=hc, memn=memn, qx_raw=qx_raw, memkv=memkv, qx=qx, kx=kx, ox=ox, x2=x2)
    hm = rmsnorm_fwd(x2, P['g_mlp'], name=f"{tag}_norm_mlp")
    u, a = mm(hm, W['w_1'], epi='relu2', out_dtype=BF16, name=f"{tag}_mm_w1")
    x3 = mm(a, W['w_2'], epi='add', aux=x2, name=f"{tag}_mm_w2")
    sv.update(hm=hm, u=u, a=a)
    return x3, sv


def _layer_bwd(dx, mem, W, P, sv, cos, ssin, tag):
    S = dx.shape[0]
    z = sv['z']
    gw, gs = {}, {}
    wgrad = lambda a, d, name: mm(a, d, ta=True, out_dtype=BF16, name=name)
    gw['w_2'] = wgrad(sv['a'], dx, f"{tag}_dw2")
    du = mm(dx, W['w_2'], tb=True, epi='mul_drelu2', aux=sv['u'], out_dtype=BF16, name=f"{tag}_du")
    gw['w_1'] = wgrad(sv['hm'], du, f"{tag}_dw1")
    dhm = mm(du, W['w_1'], tb=True, name=f"{tag}_dhm")
    dx, gs['g_mlp'] = rmsnorm_bwd(sv['x2'], P['g_mlp'], dhm, res=dx, name=f"{tag}_dnorm_mlp")
    gw['w_xo'] = wgrad(sv['ox'], dx, f"{tag}_dwxo")
    dox = mm(dx, W['w_xo'], tb=True, out_dtype=BF16, name=f"{tag}_dox")
    dqx, dkx, dvx = attn_bwd(sv['qx'], sv['kx'], sv['memkv'], dox, v_col=512, scale=LANE ** -0.5, name=f"{tag}_x_attn_bwd")
    dqx_raw, gs['g_x_q'] = headnorm_bwd(sv['qx_raw'], P['g_x_q'], dqx, col=0, heads=4, name=f"{tag}_x_qn_bwd")
    dkx_raw, gs['g_x_k'] = headnorm_bwd(sv['memkv'], P['g_x_k'], dkx, col=0, heads=4, name=f"{tag}_x_kn_bwd")
    dqx_b = dqx_raw.astype(BF16)
    gw['w_xq'] = wgrad(sv['hc'], dqx_b, f"{tag}_dwxq")
    dhc = mm(dqx_b, W['w_xq'], tb=True, name=f"{tag}_dhc")
    dx, gs['g_cross'] = rmsnorm_bwd(sv['x1'], P['g_cross'], dhc, res=dx, name=f"{tag}_dnorm_cross")
    dmemkv = jnp.concatenate([dkx_raw, dvx], axis=1).astype(BF16)
    gw['w_xkv'] = wgrad(sv['memn'], dmemkv, f"{tag}_dwxkv")
    dmemn = mm(dmemkv, W['w_xkv'], tb=True, name=f"{tag}_dmemn")
    _, gs['g_mem'] = rmsnorm_bwd(mem, P['g_mem'], dmemn, name=f"{tag}_dnorm_mem")
    gw['w_out'] = wgrad(sv['merged'], dx, f"{tag}_dwout")
    dmerged = mm(dx, W['w_out'], tb=True, name=f"{tag}_dmerged")
    dproj, dgl = gate_bwd(z, sv['proj'], dmerged, name=f"{tag}_gate_bwd")
    gw['w_br'] = [wgrad(sv['ys'][n], dproj[n], f"{tag}_dwbr{n}") for n in range(3)]
    dys = [mm(dproj[n], W['w_br'][n], tb=True, out_dtype=BF16, name=f"{tag}_dys{n}") for n in range(3)]
    dqa, dka, dva = causal_attn_bwd(sv['qa'], sv['ka'], sv['kvf'], sv['ys'][0], dys[0], sv['lse_a'], v_col=1024, chunked=True,
                                    scale=MLA_QK ** -0.5, name=f"{tag}_mla_attn_bwd")
    dqn, dqr, gs['g_mla_q'] = mla_prep_bwd(sv['qf'], sv['qf'], P['g_mla_q'], cos, ssin, dqa, n_col=0, n_stride=2 * LANE,
                                           r_col=LANE, r_stride=2 * LANE, heads=8, name=f"{tag}_mla_q_bwd")
    dkn, dkr, gs['g_mla_k'] = mla_prep_bwd(sv['kvf'], z, P['g_mla_k'], cos, ssin, dka, n_col=0, n_stride=LANE,
                                           r_col=KR_COL, r_stride=0, heads=8, name=f"{tag}_mla_k_bwd")
    dqf = jnp.stack([dqn.reshape(S, 8, LANE), dqr.reshape(S, 8, LANE)], axis=2).reshape(S, 2048).astype(BF16)
    dkvf = jnp.concatenate([dkn, dva], axis=1).astype(BF16)
    gw['w_uq'] = wgrad(sv['cqn'], dqf, f"{tag}_dwuq")
    gw['w_ukv'] = wgrad(sv['ckvn'], dkvf, f"{tag}_dwukv")
    dcqn = mm(dqf, W['w_uq'], tb=True, name=f"{tag}_dcqn")
    dckvn = mm(dkvf, W['w_ukv'], tb=True, name=f"{tag}_dckvn")
    dcq_raw, gs['g_cq'] = rmsnorm_bwd(z, P['g_cq'], dcqn, col=0, width=512, name=f"{tag}_dnorm_cq")
    dckv_raw, gs['g_ckv'] = rmsnorm_bwd(z, P['g_ckv'], dckvn, col=512, width=256, name=f"{tag}_dnorm_ckv")
    dqb, dkb, dvb, dcq, dck = causal_attn_bwd(sv['qb'], sv['kb'], z, sv['ys'][1], dys[1], sv['lse_b'], v_col=Z_FOX + 2048,
                                              chunked=False, scale=LANE ** -0.5, cq=sv['cq'], ck=sv['ck'],
                                              name=f"{tag}_fox_attn_bwd")
    dqb_raw, gs['g_fox_q'] = headnorm_bwd(z, P['g_fox_q'], dqb, col=Z_FOX, heads=8, name=f"{tag}_fox_qn_bwd")
    dkb_raw, gs['g_fox_k'] = headnorm_bwd(z, P['g_fox_k'], dkb, col=Z_FOX + 1024, heads=8, name=f"{tag}_fox_kn_bwd")
    dfl, gs['b_f'] = foxgate_bwd(sv['fl'], P['b_f'], dcq.reshape(8, S) + dck.reshape(8, S), name=f"{tag}_fox_gate_bwd")
    dqc, dkcp, dvcp, dbias = band_bwd(sv['qc'], sv['kcp'], sv['vcp'], sv['bias'], dys[2], scale=LANE ** -0.5,
                                      name=f"{tag}_ch_attn_bwd")
    dqc_raw, gs['g_ch_q'] = headnorm_bwd(z, P['g_ch_q'], dqc, col=Z_CH, heads=8, name=f"{tag}_ch_qn_bwd")
    dkc_raw, gs['g_ch_k'] = headnorm_bwd(z, P['g_ch_k'], dkcp[:, PAD:PAD + S, :], col=Z_CH + 1024, heads=8,
                                         name=f"{tag}_ch_kn_bwd")
    gs['rel_bias'] = relbias_bwd(dbias, name=f"{tag}_relbias_bwd")
    b16 = lambda t: t.astype(BF16)
    dz = jnp.concatenate([b16(dcq_raw), b16(dckv_raw), b16(dkr), b16(dfl.T), jnp.zeros((S, 120), BF16),
                          b16(dqb_raw), b16(dkb_raw), b16(dvb), b16(dqc_raw), b16(dkc_raw), b16(dvcp[PAD:PAD + S]),
                          dgl[0], dgl[1], dgl[2]], axis=1)
    gw['w_in'] = wgrad(sv['h'], dz, f"{tag}_dwin")
    dh = mm(dz, W['w_in'], tb=True, name=f"{tag}_dh")
    dx, gs['g_mix'] = rmsnorm_bwd(sv['x0'], P['g_mix'], dh, res=dx, name=f"{tag}_dnorm_mix")
    return dx, gw, gs


def _local_step(x, mem, target, Ws, Ps):
    S = x.shape[0]
    cos, ssin = _rope_tables(S)
    L = len(Ws)
    saved = []
    for l in range(L):
        x, sv = _layer_fwd(x, mem, Ws[l], Ps[l], cos, ssin, f"l{l}")
        saved.append(sv)
    loss, dx = loss_head(x, target, name="loss_head")
    gws, gss = [None] * L, [None] * L
    for l in reversed(range(L)):
        dx, gws[l], gss[l] = _layer_bwd(dx, mem, Ws[l], Ps[l], saved[l], cos, ssin, f"l{l}")
    return loss, dx, gws, gss


def _pack_small(d):
    flat = jnp.concatenate([d[k].reshape(-1) for k in SMALL])
    n = flat.shape[0]
    rows = -(-n // (8 * LANE)) * 8
    return jnp.pad(flat, (0, rows * LANE - n)).reshape(rows, LANE)


def _unpack_small(packed, like):
    flat, out, off = packed.reshape(-1), {}, 0
    for k in SMALL:
        n = int(np.prod(like[k].shape))
        out[k] = flat[off:off + n].reshape(like[k].shape)
        off += n
    return out


def kernel(x, mem, g_mix, w_in, g_cq, w_uq, g_ckv, w_ukv, g_mla_q, g_mla_k, b_f, g_fox_q, g_fox_k, rel_bias, g_ch_q, g_ch_k, w_br, w_out, g_cross, g_mem, w_xq, w_xkv, g_x_q, g_x_k, w_xo, g_mlp, w_1, w_2, loss_target, m_g_mix, m_w_in, m_g_cq, m_w_uq, m_g_ckv, m_w_ukv, m_g_mla_q, m_g_mla_k, m_b_f, m_g_fox_q, m_g_fox_k, m_rel_bias, m_g_ch_q, m_g_ch_k, m_w_br, m_w_out, m_g_cross, m_g_mem, m_w_xq, m_w_xkv, m_g_x_q, m_g_x_k, m_w_xo, m_g_mlp, m_w_1, m_w_2, v_g_mix, v_w_in, v_g_cq, v_w_uq, v_g_ckv, v_w_ukv, v_g_mla_q, v_g_mla_k, v_b_f, v_g_fox_q, v_g_fox_k, v_rel_bias, v_g_ch_q, v_g_ch_k, v_w_br, v_w_out, v_g_cross, v_g_mem, v_w_xq, v_w_xkv, v_g_x_q, v_g_x_k, v_w_xo, v_g_mlp, v_w_1, v_w_2):
    args = locals()
    w = {k: args[k] for k in WEIGHTS}
    m = {k: args['m_' + k] for k in WEIGHTS}
    v = {k: args['v_' + k] for k in WEIGHTS}
    L = w_in.shape[0]

    gathered = dict(zip(BIG, all_gather([w[k].astype(BF16) for k in BIG], name="ag_weights")))
    Ws = [{k: w_in_full(gathered[k], l, name=f"l{l}_w_in_layout") if k == 'w_in' else _full_from_shards(k, gathered[k][:, l])
           for k in BIG} for l in range(L)]
    Ps = [{k: w[k][l] for k in SMALL} for l in range(L)]

    loss, grad_x, gws, gss = _local_step(x[0], mem[0], loss_target[0], Ws, Ps)
    loss = lax.psum(loss[0, 0], ("x", "y", "c"))

    def owner_major(k, l):
        return w_in_shards(gws[l][k], name=f"l{l}_dw_in_layout") if k == 'w_in' else _shards_from_full(k, gws[l][k])

    gdst = [jnp.stack([owner_major(k, l) for l in range(L)], axis=1) for k in BIG]
    from_sibling = exchange_sibling(gdst, name="rs_sibling")
    chip_part = [pair_sum(g, o, name=f"rs_pair_sum_{k}") for k, g, o in zip(BIG, gdst, from_sibling)]
    from_chips = exchange_chips(chip_part, name="rs_chips")
    grads = {k: chip_sum(h, r, name=f"rs_chip_sum_{k}") for k, h, r in zip(BIG, chip_part, from_chips)}

    small_part = _pack_small({k: jnp.stack([gss[l][k] for l in range(L)]) for k in SMALL})
    small_all = all_gather([small_part], name="ag_small")[0]
    grads.update(_unpack_small(ordered_sum(small_all, name="small_sum"), {k: w[k] for k in SMALL}))

    delta, new_m, new_v = {}, {}, {}
    for k in BIG:
        delta[k], new_m[k], new_v[k] = adamw(w[k], grads[k], m[k], v[k], name=f"adamw_{k}")
    sd, sm, sv_ = adamw(_pack_small({k: w[k] for k in SMALL}), _pack_small({k: grads[k] for k in SMALL}),
                        _pack_small({k: m[k] for k in SMALL}), _pack_small({k: v[k] for k in SMALL}), name="adamw_small")
    like = {k: w[k] for k in SMALL}
    delta.update(_unpack_small(sd, like))
    new_m.update(_unpack_small(sm, like))
    new_v.update(_unpack_small(sv_, like))

    return (loss, grad_x[None], *[grads[k] for k in WEIGHTS], *[delta[k] for k in WEIGHTS],
            *[new_m[k] for k in WEIGHTS], *[new_v[k] for k in WEIGHTS])
```

```python
import numpy as np
import jax
import jax.numpy as jnp
from jax import lax
from jax.experimental import pallas as pl
from jax.experimental.pallas import tpu as pltpu

F32, BF16 = jnp.float32, jnp.bfloat16
EPS = 1e-6
NEG = -1e30
LANE = 128
VMEM_LIMIT_BYTES = 56 * 2**20
MESH = pl.DeviceIdType.MESH

D_MODEL = 2048
CHUNK = 64
BAND = 9 * CHUNK
PAD = 8 * CHUNK
REL_CLIP = 128
MLA_HEADS, MLA_NOPE, MLA_ROPE, MLA_QK = 8, 128, 64, 192
N_HEADS = 8
X_HEADS = 4
ROPE_THETA = 10000.0
ADAM_LR, ADAM_B1, ADAM_B2, ADAM_EPS, ADAM_WD, ADAM_STEP = 0.001, 0.9, 0.999, 1e-08, 0.01, 10

Z_MAIN, Z_FOX, Z_CH, Z_GATE, Z_W = 0, 1024, 4096, 7168, 13312
KR_COL, FF_COL = 768, 896

BIG = ('w_in', 'w_uq', 'w_ukv', 'w_br', 'w_out', 'w_xq', 'w_xkv', 'w_xo', 'w_1', 'w_2')
COL_SHARDED = ('w_in', 'w_uq', 'w_ukv', 'w_br', 'w_xo', 'w_1')
SMALL = ('g_mix', 'g_cq', 'g_ckv', 'g_mla_q', 'g_mla_k', 'b_f', 'g_fox_q', 'g_fox_k', 'rel_bias', 'g_ch_q',
         'g_ch_k', 'g_cross', 'g_mem', 'g_x_q', 'g_x_k', 'g_mlp')
WEIGHTS = ('g_mix', 'w_in', 'g_cq', 'w_uq', 'g_ckv', 'w_ukv', 'g_mla_q', 'g_mla_k', 'b_f', 'g_fox_q', 'g_fox_k',
           'rel_bias', 'g_ch_q', 'g_ch_k', 'w_br', 'w_out', 'g_cross', 'g_mem', 'w_xq', 'w_xkv', 'g_x_q', 'g_x_k',
           'w_xo', 'g_mlp', 'w_1', 'w_2')


def _params(*sem):
    return pltpu.CompilerParams(dimension_semantics=sem, vmem_limit_bytes=VMEM_LIMIT_BYTES)


def _tile(dim, pref):
    if dim <= pref:
        return dim
    for t in range(pref - pref % LANE, 0, -LANE):
        if dim % t == 0:
            return t
    raise ValueError((dim, pref))


def mm(a, b, *, ta=False, tb=False, out_dtype=F32, epi=None, aux=None, name, tm=1024, tn=512, tk=2048):
    M, K = (a.shape[1], a.shape[0]) if ta else a.shape
    N = b.shape[0] if tb else b.shape[1]
    assert (b.shape[1] if tb else b.shape[0]) == K, (a.shape, b.shape, ta, tb)
    tm, tn, tk = _tile(M, tm), _tile(N, tn), _tile(K, tk)
    nk = K // tk
    dn = (((0 if ta else 1,), (1 if tb else 0,)), ((), ()))
    n_aux = 0 if aux is None else 1

    def finish(acc, aux_refs, o_refs):
        if epi is None:
            o_refs[0][...] = acc.astype(o_refs[0].dtype)
        elif epi == 'add':
            o_refs[0][...] = (acc + aux_refs[0][...]).astype(o_refs[0].dtype)
        elif epi == 'relu2':
            o_refs[0][...] = acc
            r = jnp.maximum(acc, 0.0)
            o_refs[1][...] = (r * r).astype(o_refs[1].dtype)
        elif epi == 'mul_drelu2':
            o_refs[0][...] = (acc * (2.0 * jnp.maximum(aux_refs[0][...], 0.0))).astype(o_refs[0].dtype)

    def body(a_ref, b_ref, *rest):
        aux_refs = rest[:n_aux]
        o_refs = rest[n_aux:n_aux + (2 if epi == 'relu2' else 1)]
        part = lax.dot_general(a_ref[...].astype(BF16), b_ref[...].astype(BF16), dn, preferred_element_type=F32)
        if nk == 1:
            finish(part, aux_refs, o_refs)
        else:
            acc_ref = rest[-1]
            k = pl.program_id(2)

            @pl.when(k == 0)
            def _():
                acc_ref[...] = part

            @pl.when(k > 0)
            def _():
                acc_ref[...] += part

            @pl.when(k == nk - 1)
            def _():
                finish(acc_ref[...], aux_refs, o_refs)

    a_spec = pl.BlockSpec((tk, tm), lambda i, j, k: (k, i)) if ta else pl.BlockSpec((tm, tk), lambda i, j, k: (i, k))
    b_spec = pl.BlockSpec((tn, tk), lambda i, j, k: (j, k)) if tb else pl.BlockSpec((tk, tn), lambda i, j, k: (k, j))
    o_spec = pl.BlockSpec((tm, tn), lambda i, j, k: (i, j))
    if epi == 'relu2':
        out_shape = (jax.ShapeDtypeStruct((M, N), F32), jax.ShapeDtypeStruct((M, N), out_dtype))
        out_specs = (o_spec, o_spec)
    else:
        out_shape, out_specs = jax.ShapeDtypeStruct((M, N), out_dtype), o_spec
    return pl.pallas_call(
        body, name=name, out_shape=out_shape, grid=(M // tm, N // tn, nk),
        in_specs=[a_spec, b_spec] + [o_spec] * n_aux, out_specs=out_specs,
        scratch_shapes=[pltpu.VMEM((tm, tn), F32)] if nk > 1 else [],
        compiler_params=_params("parallel", "parallel", "arbitrary"),
    )(a, b, *([aux] if n_aux else []))


def rmsnorm_fwd(x, g, *, col=0, width=None, out_dtype=BF16, name, ts=256):
    S = x.shape[0]
    width = x.shape[1] if width is None else width
    ts, cb = _tile(S, ts), col // width

    def body(x_ref, g_ref, o_ref):
        xf = x_ref[...]
        r = lax.rsqrt(jnp.mean(xf * xf, axis=-1, keepdims=True) + EPS)
        o_ref[...] = (xf * r * g_ref[...]).astype(o_ref.dtype)

    return pl.pallas_call(
        body, name=name, out_shape=jax.ShapeDtypeStruct((S, width), out_dtype), grid=(S // ts,),
        in_specs=[pl.BlockSpec((ts, width), lambda i: (i, cb)), pl.BlockSpec((1, width), lambda i: (0, 0))],
        out_specs=pl.BlockSpec((ts, width), lambda i: (i, 0)), compiler_params=_params("parallel"),
    )(x, g.reshape(1, width))


def rmsnorm_bwd(x, g, dy, *, col=0, width=None, res=None, name, ts=256):
    S = x.shape[0]
    width = x.shape[1] if width is None else width
    ts, cb = _tile(S, ts), col // width
    has_res = res is not None

    def body(x_ref, g_ref, dy_ref, *rest):
        dx_ref, dg_ref = rest[-2:]
        xf = x_ref[...]
        r = lax.rsqrt(jnp.mean(xf * xf, axis=-1, keepdims=True) + EPS)
        dyf = dy_ref[...].astype(F32)
        dyg = dyf * g_ref[...]
        dx = r * dyg - xf * (r * r * r) * jnp.mean(dyg * xf, axis=-1, keepdims=True)
        if has_res:
            dx = dx + rest[0][...]
        dx_ref[...] = dx
        part = jnp.sum(dyf * xf * r, axis=0, keepdims=True)

        @pl.when(pl.program_id(0) == 0)
        def _():
            dg_ref[...] = part

        @pl.when(pl.program_id(0) > 0)
        def _():
            dg_ref[...] += part

    blk = pl.BlockSpec((ts, width), lambda i: (i, 0))
    dx, dg = pl.pallas_call(
        body, name=name,
        out_shape=(jax.ShapeDtypeStruct((S, width), F32), jax.ShapeDtypeStruct((1, width), F32)), grid=(S // ts,),
        in_specs=[pl.BlockSpec((ts, width), lambda i: (i, cb)), pl.BlockSpec((1, width), lambda i: (0, 0)), blk]
        + ([blk] if has_res else []),
        out_specs=(blk, pl.BlockSpec((1, width), lambda i: (0, 0))), compiler_params=_params("arbitrary"),
    )(x, g.reshape(1, width), dy, *([res] if has_res else []))
    return dx, dg.reshape(width)


def headnorm_fwd(x, g, *, col, heads, name, ts=1024):
    S = x.shape[0]
    ts, cb = _tile(S, ts), col // LANE

    def body(x_ref, g_ref, o_ref):
        xf = x_ref[...]
        r = lax.rsqrt(jnp.mean(xf * xf, axis=-1, keepdims=True) + EPS)
        o_ref[0] = (xf * r * g_ref[...]).astype(o_ref.dtype)

    return pl.pallas_call(
        body, name=name, out_shape=jax.ShapeDtypeStruct((heads, S, LANE), BF16), grid=(heads, S // ts),
        in_specs=[pl.BlockSpec((ts, LANE), lambda h, i: (i, cb + h)), pl.BlockSpec((1, LANE), lambda h, i: (0, 0))],
        out_specs=pl.BlockSpec((1, ts, LANE), lambda h, i: (h, i, 0)), compiler_params=_params("parallel", "parallel"),
    )(x, g.reshape(1, LANE))


def headnorm_bwd(x, g, dy, *, col, heads, name, ts=1024):
    S = x.shape[0]
    ts, cb = _tile(S, ts), col // LANE

    def body(x_ref, g_ref, dy_ref, dx_ref, dg_ref):
        xf = x_ref[...]
        r = lax.rsqrt(jnp.mean(xf * xf, axis=-1, keepdims=True) + EPS)
        dyf = dy_ref[0]
        dyg = dyf * g_ref[...]
        dx_ref[...] = r * dyg - xf * (r * r * r) * jnp.mean(dyg * xf, axis=-1, keepdims=True)
        part = jnp.sum(dyf * xf * r, axis=0, keepdims=True)
        first = jnp.logical_and(pl.program_id(0) == 0, pl.program_id(1) == 0)

        @pl.when(first)
        def _():
            dg_ref[...] = part

        @pl.when(jnp.logical_not(first))
        def _():
            dg_ref[...] += part

    dx, dg = pl.pallas_call(
        body, name=name,
        out_shape=(jax.ShapeDtypeStruct((S, heads * LANE), F32), jax.ShapeDtypeStruct((1, LANE), F32)),
        grid=(heads, S // ts),
        in_specs=[pl.BlockSpec((ts, LANE), lambda h, i: (i, cb + h)), pl.BlockSpec((1, LANE), lambda h, i: (0, 0)),
                  pl.BlockSpec((1, ts, LANE), lambda h, i: (h, i, 0))],
        out_specs=(pl.BlockSpec((ts, LANE), lambda h, i: (i, h)), pl.BlockSpec((1, LANE), lambda h, i: (0, 0))),
        compiler_params=_params("arbitrary", "arbitrary"),
    )(x, g.reshape(1, LANE), dy)
    return dx, dg.reshape(LANE)


def _rope_tables(S):
    pos = jnp.arange(S, dtype=F32)
    inv = ROPE_THETA ** (-jnp.arange(0, MLA_ROPE, 2, dtype=F32) / MLA_ROPE)
    ang = pos[:, None] * inv[None, :]
    c, s, z = jnp.cos(ang), jnp.sin(ang), jnp.zeros((S, 64), F32)
    return jnp.concatenate([c, c, z], axis=1), jnp.concatenate([-s, s, z], axis=1)


def _rope(v, cos, ssin, lane):
    partner = jnp.where(lane < 32, pltpu.roll(v, 96, 1), pltpu.roll(v, 32, 1))
    return v * cos + partner * ssin


def mla_prep_fwd(xn, xr, g, cos, ssin, *, n_col, n_stride, r_col, r_stride, heads, name, ts=1024):
    S = xn.shape[0]
    ts = _tile(S, ts)
    nb, ns, rb, rs = n_col // LANE, n_stride // LANE, r_col // LANE, r_stride // LANE
    gn = g[:MLA_NOPE].reshape(1, LANE)
    gr = jnp.concatenate([g[MLA_NOPE:], jnp.zeros((64,), F32)]).reshape(1, LANE)

    def body(n_ref, r_ref, gn_ref, gr_ref, c_ref, s_ref, o_ref):
        n, rr = n_ref[...], r_ref[...]
        ss = jnp.sum(n * n, axis=-1, keepdims=True) + jnp.sum(rr * rr, axis=-1, keepdims=True)
        r = lax.rsqrt(ss * (1.0 / MLA_QK) + EPS)
        lane = lax.broadcasted_iota(jnp.int32, rr.shape, 1)
        o_ref[0, :, :LANE] = (n * r * gn_ref[...]).astype(o_ref.dtype)
        o_ref[0, :, LANE:] = _rope(rr * r * gr_ref[...], c_ref[...], s_ref[...], lane).astype(o_ref.dtype)

    row = lambda h, i: (0, 0)
    return pl.pallas_call(
        body, name=name, out_shape=jax.ShapeDtypeStruct((heads, S, 2 * LANE), BF16), grid=(heads, S // ts),
        in_specs=[pl.BlockSpec((ts, LANE), lambda h, i: (i, nb + ns * h)),
                  pl.BlockSpec((ts, LANE), lambda h, i: (i, rb + rs * h)),
                  pl.BlockSpec((1, LANE), row), pl.BlockSpec((1, LANE), row),
                  pl.BlockSpec((ts, LANE), lambda h, i: (i, 0)), pl.BlockSpec((ts, LANE), lambda h, i: (i, 0))],
        out_specs=pl.BlockSpec((1, ts, 2 * LANE), lambda h, i: (h, i, 0)),
        compiler_params=_params("parallel", "parallel"),
    )(xn, xr, gn, gr, cos, ssin)


def mla_prep_bwd(xn, xr, g, cos, ssin, dy, *, n_col, n_stride, r_col, r_stride, heads, name, ts=1024):
    S = xn.shape[0]
    ts = _tile(S, ts)
    nb, ns, rb, rs = n_col // LANE, n_stride // LANE, r_col // LANE, r_stride // LANE
    shared = r_stride == 0
    gn = g[:MLA_NOPE].reshape(1, LANE)
    gr = jnp.concatenate([g[MLA_NOPE:], jnp.zeros((64,), F32)]).reshape(1, LANE)

    def body(n_ref, r_ref, gn_ref, gr_ref, c_ref, s_ref, dy_ref, dn_ref, dr_ref, dgn_ref, dgr_ref):
        i, h = pl.program_id(0), pl.program_id(1)
        n, rr = n_ref[...], r_ref[...]
        ss = jnp.sum(n * n, axis=-1, keepdims=True) + jnp.sum(rr * rr, axis=-1, keepdims=True)
        r = lax.rsqrt(ss * (1.0 / MLA_QK) + EPS)
        lane = lax.broadcasted_iota(jnp.int32, rr.shape, 1)
        dyn = dy_ref[0, :, :LANE]
        dyr = dy_ref[0, :, LANE:]
        t = dyr * s_ref[...]
        dvr = dyr * c_ref[...] + jnp.where(lane < 32, pltpu.roll(t, 96, 1), pltpu.roll(t, 32, 1))
        dvr = jnp.where(lane < 64, dvr, 0.0)
        dgn_part = jnp.sum(dyn * n * r, axis=0, keepdims=True)
        dgr_part = jnp.sum(dvr * rr * r, axis=0, keepdims=True)
        dyn_g, dvr_g = dyn * gn_ref[...], dvr * gr_ref[...]
        proj = (jnp.sum(dyn_g * n, axis=-1, keepdims=True) + jnp.sum(dvr_g * rr, axis=-1, keepdims=True)) * (1.0 / MLA_QK)
        r3 = r * r * r
        dn_ref[...] = r * dyn_g - n * r3 * proj
        dr = r * dvr_g - rr * r3 * proj
        if shared:
            @pl.when(h == 0)
            def _():
                dr_ref[...] = dr

            @pl.when(h > 0)
            def _():
                dr_ref[...] += dr
        else:
            dr_ref[...] = dr
        first = jnp.logical_and(i == 0, h == 0)

        @pl.when(first)
        def _():
            dgn_ref[...] = dgn_part
            dgr_ref[...] = dgr_part

        @pl.when(jnp.logical_not(first))
        def _():
            dgn_ref[...] += dgn_part
            dgr_ref[...] += dgr_part

    row = lambda i, h: (0, 0)
    dr_cols = LANE if shared else heads * LANE
    dn, dr, dgn, dgr = pl.pallas_call(
        body, name=name,
        out_shape=(jax.ShapeDtypeStruct((S, heads * LANE), F32), jax.ShapeDtypeStruct((S, dr_cols), F32),
                   jax.ShapeDtypeStruct((1, LANE), F32), jax.ShapeDtypeStruct((1, LANE), F32)),
        grid=(S // ts, heads),
        in_specs=[pl.BlockSpec((ts, LANE), lambda i, h: (i, nb + ns * h)),
                  pl.BlockSpec((ts, LANE), lambda i, h: (i, rb + rs * h)),
                  pl.BlockSpec((1, LANE), row), pl.BlockSpec((1, LANE), row),
                  pl.BlockSpec((ts, LANE), lambda i, h: (i, 0)), pl.BlockSpec((ts, LANE), lambda i, h: (i, 0)),
                  pl.BlockSpec((1, ts, 2 * LANE), lambda i, h: (h, i, 0))],
        out_specs=(pl.BlockSpec((ts, LANE), lambda i, h: (i, h)),
                   pl.BlockSpec((ts, LANE), (lambda i, h: (i, 0)) if shared else (lambda i, h: (i, h))),
                   pl.BlockSpec((1, LANE), row), pl.BlockSpec((1, LANE), row)),
        compiler_params=_params("arbitrary", "arbitrary"),
    )(xn, xr, gn, gr, cos, ssin, dy)
    return dn, dr, jnp.concatenate([dgn.reshape(LANE), dgr.reshape(LANE)[:MLA_ROPE]])


_NT = (((1,), (1,)), ((), ()))
_TN = (((0,), (0,)), ((), ()))


def attn_fwd(q, k, v, *, v_col, scale, name, bq=256):
    H, S, dk = q.shape
    Sk = k.shape[1]
    bq, vb = _tile(S, bq), v_col // LANE

    def body(q_ref, k_ref, v_ref, o_ref):
        s = lax.dot_general(q_ref[0], k_ref[0], _NT, preferred_element_type=F32) * scale
        e = jnp.exp(s - jnp.max(s, axis=-1, keepdims=True))
        p = e * (1.0 / jnp.sum(e, axis=-1, keepdims=True))
        o_ref[...] = jnp.dot(p.astype(BF16), v_ref[...].astype(BF16), preferred_element_type=F32).astype(o_ref.dtype)

    return pl.pallas_call(
        body, name=name, out_shape=jax.ShapeDtypeStruct((S, H * LANE), BF16), grid=(H, S // bq),
        in_specs=[pl.BlockSpec((1, bq, dk), lambda h, i: (h, i, 0)), pl.BlockSpec((1, Sk, dk), lambda h, i: (h, 0, 0)),
                  pl.BlockSpec((Sk, LANE), lambda h, i: (0, vb + h))],
        out_specs=pl.BlockSpec((bq, LANE), lambda h, i: (i, h)), compiler_params=_params("parallel", "parallel"),
    )(q, k, v)


def attn_bwd(q, k, v, do, *, v_col, scale, name, bq=256):
    H, S, dk = q.shape
    Sk = k.shape[1]
    bq, vb = _tile(S, bq), v_col // LANE

    def body(q_ref, k_ref, v_ref, do_ref, dq_ref, dk_ref, dv_ref):
        i = pl.program_id(1)
        qb, kb, vv = q_ref[0], k_ref[0], v_ref[...].astype(BF16)
        s = lax.dot_general(qb, kb, _NT, preferred_element_type=F32) * scale
        e = jnp.exp(s - jnp.max(s, axis=-1, keepdims=True))
        p = e * (1.0 / jnp.sum(e, axis=-1, keepdims=True))
        dob = do_ref[...].astype(BF16)
        dv_part = lax.dot_general(p.astype(BF16), dob, _TN, preferred_element_type=F32)
        dp = lax.dot_general(dob, vv, _NT, preferred_element_type=F32)
        ds = p * (dp - jnp.sum(p * dp, axis=-1, keepdims=True))
        dsb = (ds * scale).astype(BF16)
        dq_ref[0] = jnp.dot(dsb, kb, preferred_element_type=F32)
        dk_part = lax.dot_general(dsb, qb, _TN, preferred_element_type=F32)

        @pl.when(i == 0)
        def _():
            dk_ref[0] = dk_part
            dv_ref[...] = dv_part

        @pl.when(i > 0)
        def _():
            dk_ref[0] += dk_part
            dv_ref[...] += dv_part

    return pl.pallas_call(
        body, name=name,
        out_shape=(jax.ShapeDtypeStruct((H, S, dk), F32), jax.ShapeDtypeStruct((H, Sk, dk), F32),
                   jax.ShapeDtypeStruct((Sk, H * LANE), F32)),
        grid=(H, S // bq),
        in_specs=[pl.BlockSpec((1, bq, dk), lambda h, i: (h, i, 0)), pl.BlockSpec((1, Sk, dk), lambda h, i: (h, 0, 0)),
                  pl.BlockSpec((Sk, LANE), lambda h, i: (0, vb + h)), pl.BlockSpec((bq, LANE), lambda h, i: (i, h))],
        out_specs=(pl.BlockSpec((1, bq, dk), lambda h, i: (h, i, 0)), pl.BlockSpec((1, Sk, dk), lambda h, i: (h, 0, 0)),
                   pl.BlockSpec((Sk, LANE), lambda h, i: (0, h))),
        compiler_params=_params("parallel", "arbitrary"),
    )(q, k, v, do)


def _causal_scores(q, kblk, i, start, blk, scale, chunked, cq, ckblk):
    s = lax.dot_general(q, kblk, _NT, preferred_element_type=F32) * scale
    if cq is not None:
        s = s + cq - ckblk
    qpos = i * blk + lax.broadcasted_iota(jnp.int32, s.shape, 0)
    kpos = start + lax.broadcasted_iota(jnp.int32, s.shape, 1)
    ok = (kpos >> 6) <= (qpos >> 6) if chunked else kpos <= qpos
    return jnp.where(ok, s, NEG)


def causal_attn_fwd(q, k, v, *, v_col, chunked, scale, cq=None, ck=None, name, blk=256):
    H, S, dk = q.shape
    blk, vb = _tile(S, blk), v_col // LANE
    fox = cq is not None

    def body(q_ref, k_ref, v_ref, *rest):
        o_ref, lse_ref = rest[-2:]
        s = _causal_scores(q_ref[0], k_ref[0], pl.program_id(1), 0, blk, scale, chunked,
                           rest[0][0] if fox else None, rest[1][0] if fox else None)
        m = jnp.max(s, axis=-1, keepdims=True)
        p = jnp.exp(s - m)
        l = jnp.sum(p, axis=-1, keepdims=True)
        pv = jnp.dot(p.astype(BF16), v_ref[...].astype(BF16), preferred_element_type=F32)
        o_ref[...] = (pv * (1.0 / l)).astype(o_ref.dtype)
        lse_ref[0] = m + jnp.log(l)

    in_specs = [pl.BlockSpec((1, blk, dk), lambda h, i: (h, i, 0)), pl.BlockSpec((1, S, dk), lambda h, i: (h, 0, 0)),
                pl.BlockSpec((S, LANE), lambda h, i: (0, vb + h))]
    args = [q, k, v]
    if fox:
        in_specs += [pl.BlockSpec((1, blk, 1), lambda h, i: (h, i, 0)), pl.BlockSpec((1, 1, S), lambda h, i: (h, 0, 0))]
        args += [cq, ck]
    return pl.pallas_call(
        body, name=name, out_shape=(jax.ShapeDtypeStruct((S, H * LANE), BF16), jax.ShapeDtypeStruct((H, S, 1), F32)),
        grid=(H, S // blk), in_specs=in_specs,
        out_specs=(pl.BlockSpec((blk, LANE), lambda h, i: (i, h)), pl.BlockSpec((1, blk, 1), lambda h, i: (h, i, 0))),
        compiler_params=_params("parallel", "parallel"),
    )(*args)


def causal_attn_bwd(q, k, v, o, do, lse, *, v_col, chunked, scale, cq=None, ck=None, name, blk=256):
    H, S, dk = q.shape
    blk, vb = _tile(S, blk), v_col // LANE
    fox = cq is not None

    def body(q_ref, k_ref, v_ref, o_ref, do_ref, lse_ref, *rest):
        i = pl.program_id(1)
        if fox:
            cq_ref, ck_ref, dq_ref, dk_ref, dv_ref, dcq_ref, dck_ref = rest
        else:
            dq_ref, dk_ref, dv_ref = rest

        qb, kb, dob = q_ref[0], k_ref[0], do_ref[...].astype(BF16)
        delta = jnp.sum(do_ref[...].astype(F32) * o_ref[...].astype(F32), axis=-1, keepdims=True)
        s = _causal_scores(qb, kb, i, 0, blk, scale, chunked, cq_ref[0] if fox else None, ck_ref[0] if fox else None)
        p = jnp.exp(s - lse_ref[0])
        dv_part = lax.dot_general(p.astype(BF16), dob, _TN, preferred_element_type=F32)
        dp = lax.dot_general(dob, v_ref[...].astype(BF16), _NT, preferred_element_type=F32)
        ds = p * (dp - delta)
        dsb = (ds * scale).astype(BF16)
        dq_ref[0] = jnp.dot(dsb, kb, preferred_element_type=F32)
        dk_part = lax.dot_general(dsb, qb, _TN, preferred_element_type=F32)
        if fox:
            dcq_ref[0] = jnp.sum(ds, axis=-1, keepdims=True)
            dck_part = -jnp.sum(ds, axis=0, keepdims=True)

        @pl.when(i == 0)
        def _():
            dk_ref[0] = dk_part
            dv_ref[...] = dv_part
            if fox:
                dck_ref[0] = dck_part

        @pl.when(i > 0)
        def _():
            dk_ref[0] += dk_part
            dv_ref[...] += dv_part
            if fox:
                dck_ref[0] += dck_part

    row = pl.BlockSpec((blk, LANE), lambda h, i: (i, h))
    in_specs = [pl.BlockSpec((1, blk, dk), lambda h, i: (h, i, 0)), pl.BlockSpec((1, S, dk), lambda h, i: (h, 0, 0)),
                pl.BlockSpec((S, LANE), lambda h, i: (0, vb + h)), row, row, pl.BlockSpec((1, blk, 1), lambda h, i: (h, i, 0))]
    args = [q, k, v, o, do, lse]
    out_shape = [jax.ShapeDtypeStruct((H, S, dk), F32), jax.ShapeDtypeStruct((H, S, dk), F32),
                 jax.ShapeDtypeStruct((S, H * LANE), F32)]
    out_specs = [pl.BlockSpec((1, blk, dk), lambda h, i: (h, i, 0)), pl.BlockSpec((1, S, dk), lambda h, i: (h, 0, 0)),
                 pl.BlockSpec((S, LANE), lambda h, i: (0, h))]
    if fox:
        fox_specs = [pl.BlockSpec((1, blk, 1), lambda h, i: (h, i, 0)), pl.BlockSpec((1, 1, S), lambda h, i: (h, 0, 0))]
        in_specs += fox_specs
        args += [cq, ck]
        out_shape += [jax.ShapeDtypeStruct((H, S, 1), F32), jax.ShapeDtypeStruct((H, 1, S), F32)]
        out_specs += fox_specs
    return pl.pallas_call(
        body, name=name, out_shape=tuple(out_shape), grid=(H, S // blk), in_specs=in_specs, out_specs=tuple(out_specs),
        compiler_params=_params("parallel", "arbitrary"),
    )(*args)


CPB = 4
BANDW = BAND + CHUNK


def _band_probs(qc, kb, bias, start, scale):
    s = lax.dot_general(qc, kb, _NT, preferred_element_type=F32) * scale
    col = lax.broadcasted_iota(jnp.int32, s.shape, 1)
    valid = jnp.logical_and(start + col >= PAD, col < BAND)
    s = jnp.where(valid, s + bias, NEG)
    e = jnp.exp(s - jnp.max(s, axis=-1, keepdims=True))
    return e * (1.0 / jnp.sum(e, axis=-1, keepdims=True))


def band_fwd(q, kp, vp, bias, *, scale, name):
    H, S, _ = q.shape
    Sp, rows = S + PAD + CHUNK, CPB * CHUNK

    def body(q_ref, k_ref, v_ref, b_ref, o_ref):
        j = pl.program_id(1)
        for cc in range(CPB):
            start = pl.multiple_of((j * CPB + cc) * CHUNK, CHUNK)
            kb = k_ref[0, pl.ds(start, BANDW), :]
            vb = v_ref[pl.ds(start, BANDW), :].astype(BF16)
            p = _band_probs(q_ref[0, cc * CHUNK:(cc + 1) * CHUNK, :], kb, b_ref[0], start, scale)
            o_ref[cc * CHUNK:(cc + 1) * CHUNK, :] = jnp.dot(p.astype(BF16), vb, preferred_element_type=F32).astype(o_ref.dtype)

    return pl.pallas_call(
        body, name=name, out_shape=jax.ShapeDtypeStruct((S, H * LANE), BF16), grid=(H, S // rows),
        in_specs=[pl.BlockSpec((1, rows, LANE), lambda h, j: (h, j, 0)), pl.BlockSpec((1, Sp, LANE), lambda h, j: (h, 0, 0)),
                  pl.BlockSpec((Sp, LANE), lambda h, j: (0, h)), pl.BlockSpec((1, CHUNK, BANDW), lambda h, j: (h, 0, 0))],
        out_specs=pl.BlockSpec((rows, LANE), lambda h, j: (j, h)), compiler_params=_params("parallel", "parallel"),
    )(q, kp, vp, bias)


def band_bwd(q, kp, vp, bias, do, *, scale, name):
    H, S, _ = q.shape
    Sp, rows = S + PAD + CHUNK, CPB * CHUNK

    def body(q_ref, k_ref, v_ref, b_ref, do_ref, dq_ref, dk_ref, dv_ref, db_ref):
        j = pl.program_id(1)

        @pl.when(j == 0)
        def _():
            dk_ref[...] = jnp.zeros_like(dk_ref)
            dv_ref[...] = jnp.zeros_like(dv_ref)
            db_ref[...] = jnp.zeros_like(db_ref)

        for cc in range(CPB):
            start = pl.multiple_of((j * CPB + cc) * CHUNK, CHUNK)
            sl = slice(cc * CHUNK, (cc + 1) * CHUNK)
            qc = q_ref[0, sl, :]
            kb = k_ref[0, pl.ds(start, BANDW), :]
            vb = v_ref[pl.ds(start, BANDW), :].astype(BF16)
            p = _band_probs(qc, kb, b_ref[0], start, scale)
            dob = do_ref[sl, :].astype(BF16)
            dv_ref[pl.ds(start, BANDW), :] += lax.dot_general(p.astype(BF16), dob, _TN, preferred_element_type=F32)
            dp = lax.dot_general(dob, vb, _NT, preferred_element_type=F32)
            ds = p * (dp - jnp.sum(p * dp, axis=-1, keepdims=True))
            db_ref[0] += ds
            dsb = (ds * scale).astype(BF16)
            dq_ref[0, sl, :] = jnp.dot(dsb, kb, preferred_element_type=F32)
            dk_ref[0, pl.ds(start, BANDW), :] += lax.dot_general(dsb, qc, _TN, preferred_element_type=F32)

    return pl.pallas_call(
        body, name=name,
        out_shape=(jax.ShapeDtypeStruct((H, S, LANE), F32), jax.ShapeDtypeStruct((H, Sp, LANE), F32),
                   jax.ShapeDtypeStruct((Sp, H * LANE), F32), jax.ShapeDtypeStruct((H, CHUNK, BANDW), F32)),
        grid=(H, S // rows),
        in_specs=[pl.BlockSpec((1, rows, LANE), lambda h, j: (h, j, 0)), pl.BlockSpec((1, Sp, LANE), lambda h, j: (h, 0, 0)),
                  pl.BlockSpec((Sp, LANE), lambda h, j: (0, h)), pl.BlockSpec((1, CHUNK, BANDW), lambda h, j: (h, 0, 0)),
                  pl.BlockSpec((rows, LANE), lambda h, j: (j, h))],
        out_specs=(pl.BlockSpec((1, rows, LANE), lambda h, j: (h, j, 0)), pl.BlockSpec((1, Sp, LANE), lambda h, j: (h, 0, 0)),
                   pl.BlockSpec((Sp, LANE), lambda h, j: (0, h)), pl.BlockSpec((1, CHUNK, BANDW), lambda h, j: (h, 0, 0))),
        compiler_params=_params("parallel", "arbitrary"),
    )(q, kp, vp, bias, do)


def band_bias(rel_bias, *, name):
    H = rel_bias.shape[0]
    last = rel_bias[:, 2 * REL_CLIP:]
    row0 = jnp.concatenate([jnp.tile(last, (1, PAD - REL_CLIP)), rel_bias[:, CHUNK + 1:][:, ::-1],
                            jnp.tile(last, (1, CHUNK))], axis=1)

    def body(r_ref, o_ref):
        o_ref[0] = pltpu.roll(jnp.broadcast_to(r_ref[0], (CHUNK, BANDW)), 0, 1, stride=1, stride_axis=0)

    return pl.pallas_call(
        body, name=name, out_shape=jax.ShapeDtypeStruct((H, CHUNK, BANDW), F32), grid=(H,),
        in_specs=[pl.BlockSpec((1, 1, BANDW), lambda h: (h, 0, 0))], out_specs=pl.BlockSpec((1, CHUNK, BANDW), lambda h: (h, 0, 0)),
        compiler_params=_params("parallel"),
    )(row0.reshape(H, 1, BANDW))


def relbias_bwd(dbias, *, name):
    H, W = dbias.shape[0], BANDW
    x = jnp.pad(dbias[:, :, :BAND][:, :, ::-1], ((0, 0), (0, 0), (0, CHUNK)))

    def body(x_ref, o_ref):
        skew = pltpu.roll(x_ref[0], 0, 1, stride=1, stride_axis=0)
        f = jnp.broadcast_to(jnp.sum(skew, axis=0, keepdims=True), (8, W))
        lane = lax.broadcasted_iota(jnp.int32, (8, W), 1)
        direct = jnp.where(jnp.logical_and(lane >= 65, lane <= 255), pltpu.roll(f, 65, 1), 0.0)
        tail = jnp.sum(jnp.where(lane >= 191, f, 0.0), axis=-1, keepdims=True)
        o_ref[0] = direct + jnp.where(lane == 2 * REL_CLIP, tail, 0.0)

    out = pl.pallas_call(
        body, name=name, out_shape=jax.ShapeDtypeStruct((H, 8, W), F32), grid=(H,),
        in_specs=[pl.BlockSpec((1, CHUNK, W), lambda h: (h, 0, 0))], out_specs=pl.BlockSpec((1, 8, W), lambda h: (h, 0, 0)),
        compiler_params=_params("parallel"),
    )(x)
    return out[:, 0, :2 * REL_CLIP + 1]


def _split_dot(x, u, dn):
    hi = x.astype(BF16)
    r1 = x - hi.astype(F32)
    mid = r1.astype(BF16)
    lo = (r1 - mid.astype(F32)).astype(BF16)
    d = lambda t: lax.dot_general(t, u, dn, preferred_element_type=F32)
    return d(hi) + d(mid) + d(lo)


def _upper_ones(S):
    return (np.arange(S)[:, None] <= np.arange(S)[None, :]).astype(np.float32)


def foxgate_fwd(fl, b, *, name):
    H, S = fl.shape
    u = jnp.asarray(_upper_ones(S), BF16)

    def body(f_ref, b_ref, u_ref, o_ref):
        x = f_ref[...] + b_ref[...]
        lf = jnp.minimum(x, 0.0) - jnp.log(1.0 + jnp.exp(-jnp.abs(x)))
        o_ref[...] = _split_dot(lf, u_ref[...], (((1,), (0,)), ((), ())))

    return pl.pallas_call(body, name=name, out_shape=jax.ShapeDtypeStruct((H, S), F32),
                          compiler_params=pltpu.CompilerParams(vmem_limit_bytes=VMEM_LIMIT_BYTES))(fl, b.reshape(H, 1), u)


def foxgate_bwd(fl, b, dcum, *, name):
    H, S = fl.shape
    u = jnp.asarray(_upper_ones(S), BF16)

    def body(f_ref, b_ref, u_ref, dc_ref, df_ref, db_ref):
        x = f_ref[...] + b_ref[...]
        dlf = _split_dot(dc_ref[...], u_ref[...], _NT)
        df = dlf * (1.0 / (1.0 + jnp.exp(x)))
        df_ref[...] = df
        db_ref[...] = jnp.sum(df, axis=-1, keepdims=True)

    df, db = pl.pallas_call(body, name=name,
                            out_shape=(jax.ShapeDtypeStruct((H, S), F32), jax.ShapeDtypeStruct((H, 1), F32)),
                            compiler_params=pltpu.CompilerParams(vmem_limit_bytes=VMEM_LIMIT_BYTES))(fl, b.reshape(H, 1), u, dcum)
    return df, db.reshape(H)


def gate_fwd(z, proj, *, name, ts=256, tc=512):
    S, D = proj[0].shape
    ts, gb, nb = _tile(S, ts), Z_GATE // tc, D // tc

    def body(g0, g1, g2, p0, p1, p2, o_ref):
        acc = None
        for g_ref, p_ref in zip((g0, g1, g2), (p0, p1, p2)):
            t = (1.0 / (1.0 + jnp.exp(-g_ref[...]))) * p_ref[...]
            acc = t if acc is None else acc + t
        o_ref[...] = acc.astype(o_ref.dtype)

    blk = pl.BlockSpec((ts, tc), lambda i, j: (i, j))
    return pl.pallas_call(
        body, name=name, out_shape=jax.ShapeDtypeStruct((S, D), BF16), grid=(S // ts, nb),
        in_specs=[pl.BlockSpec((ts, tc), lambda i, j, n=n: (i, gb + n * nb + j)) for n in range(3)] + [blk] * 3,
        out_specs=blk, compiler_params=_params("parallel", "parallel"),
    )(z, z, z, *proj)


def gate_bwd(z, proj, dm, *, name, ts=256, tc=512):
    S, D = proj[0].shape
    ts, gb, nb = _tile(S, ts), Z_GATE // tc, D // tc

    def body(g0, g1, g2, p0, p1, p2, dm_ref, *outs):
        dmv = dm_ref[...]
        for n, (g_ref, p_ref) in enumerate(zip((g0, g1, g2), (p0, p1, p2))):
            sg = 1.0 / (1.0 + jnp.exp(-g_ref[...]))
            outs[n][...] = (dmv * sg).astype(BF16)
            outs[3 + n][...] = (dmv * p_ref[...] * sg * (1.0 - sg)).astype(BF16)

    blk = pl.BlockSpec((ts, tc), lambda i, j: (i, j))
    outs = pl.pallas_call(
        body, name=name, out_shape=tuple(jax.ShapeDtypeStruct((S, D), BF16) for _ in range(6)), grid=(S // ts, nb),
        in_specs=[pl.BlockSpec((ts, tc), lambda i, j, n=n: (i, gb + n * nb + j)) for n in range(3)] + [blk] * 4,
        out_specs=(blk,) * 6, compiler_params=_params("parallel", "parallel"),
    )(z, z, z, *proj, dm)
    return outs[:3], outs[3:]


def loss_head(y, target, *, name, ts=256):
    S, D = y.shape
    ts = _tile(S, ts)

    def body(y_ref, t_ref, l_ref, dy_ref):
        err = y_ref[...] - t_ref[...]
        dy_ref[...] = err * (1.0 / D)
        part = 0.5 * jnp.sum(jnp.mean(err * err, axis=-1, keepdims=True), axis=0, keepdims=True)

        @pl.when(pl.program_id(0) == 0)
        def _():
            l_ref[...] = part

        @pl.when(pl.program_id(0) > 0)
        def _():
            l_ref[...] += part

    blk = pl.BlockSpec((ts, D), lambda i: (i, 0))
    return pl.pallas_call(
        body, name=name, out_shape=(jax.ShapeDtypeStruct((1, 1), F32), jax.ShapeDtypeStruct((S, D), F32)), grid=(S // ts,),
        in_specs=[blk, blk], out_specs=(pl.BlockSpec((1, 1), lambda i: (0, 0)), blk), compiler_params=_params("arbitrary"),
    )(y, target)


def adamw(w, g, m, v, *, name):
    shape = w.shape
    C = shape[-1]
    R = int(np.prod(shape[:-1]))
    br = R
    while br % 16 == 0 and br * C * 4 > 2**20:
        br //= 2
    w2, g2, m2, v2 = (t.reshape(R, C) for t in (w, g, m, v))

    def body(w_ref, g_ref, m_ref, v_ref, d_ref, nm_ref, nv_ref):
        d_ref[...], nm_ref[...], nv_ref[...] = _adamw_update(w_ref[...], g_ref[...], m_ref[...], v_ref[...])

    blk = pl.BlockSpec((br, C), lambda i: (i, 0))
    outs = pl.pallas_call(
        body, name=name, out_shape=tuple(jax.ShapeDtypeStruct((R, C), F32) for _ in range(3)), grid=(R // br,),
        in_specs=[blk] * 4, out_specs=(blk,) * 3, compiler_params=_params("parallel"),
    )(w2, g2, m2, v2)
    return tuple(o.reshape(shape) for o in outs)


_ANY = pl.BlockSpec(memory_space=pl.ANY)


def _place():
    return lax.axis_index("x"), lax.axis_index("y"), lax.axis_index("c")


def all_gather(xs, *, name):
    n = len(xs)

    def body(*refs):
        x_refs, o_refs = refs[:n], refs[n:2 * n]
        send_sems, recv_sems, local_sems = refs[2 * n:]
        px, py, pc = _place()
        me, sibling = (px, py, pc), (px, py, 1 - pc)
        chips = [(1 - px, py), (px, 1 - py), (1 - px, 1 - py)]

        def slot(t, dev):
            return o_refs[t].at[4 * dev[0] + 2 * dev[1] + dev[2]]

        def copy(t, k, block, to, src=None):
            return pltpu.make_async_remote_copy(
                src_ref=slot(t, block) if src is None else src, dst_ref=slot(t, block),
                send_sem=send_sems.at[t, k], recv_sem=recv_sems.at[t, k], device_id=to, device_id_type=MESH)

        mine = [pltpu.make_async_copy(x_refs[t], slot(t, me), local_sems.at[t]) for t in range(n)]
        first = []
        for t in range(n):
            mine[t].start()
            first += [copy(t, 1 + j, me, (*chip, pc), src=x_refs[t]) for j, chip in enumerate(chips)]
            first.append(copy(t, 0, me, sibling, src=x_refs[t]))
        for cp in first:
            cp.start()
        passed = []
        for t in range(n):
            for j, chip in enumerate(chips):
                copy(t, 1 + j, (*chip, pc), me).wait_recv()
                fwd = copy(t, 4 + j, (*chip, pc), sibling)
                fwd.start()
                passed.append(fwd)
        for t in range(n):
            copy(t, 0, sibling, me).wait_recv()
            for j, chip in enumerate(chips):
                copy(t, 4 + j, (*chip, 1 - pc), me).wait_recv()
        for cp in first + passed:
            cp.wait_send()
        for cp in mine:
            cp.wait()

    return pl.pallas_call(
        body, name=name, out_shape=tuple(jax.ShapeDtypeStruct((8,) + x.shape, x.dtype) for x in xs),
        in_specs=[_ANY] * n, out_specs=(_ANY,) * n,
        scratch_shapes=[pltpu.SemaphoreType.DMA((n, 7)), pltpu.SemaphoreType.DMA((n, 7)), pltpu.SemaphoreType.DMA((n,))],
    )(*xs)


def exchange_sibling(gs, *, name):
    n = len(gs)

    def body(*refs):
        g_refs, o_refs, (send_sems, recv_sems) = refs[:n], refs[n:2 * n], refs[2 * n:]
        px, py, pc = _place()
        copies = []
        for t in range(n):
            for j in range(4):
                copies.append(pltpu.make_async_remote_copy(
                    src_ref=g_refs[t].at[2 * j + 1 - pc], dst_ref=o_refs[t].at[j],
                    send_sem=send_sems.at[t, j], recv_sem=recv_sems.at[t, j],
                    device_id=(px, py, 1 - pc), device_id_type=MESH))
        for cp in copies:
            cp.start()
        for cp in copies:
            cp.wait()

    return pl.pallas_call(
        body, name=name, out_shape=tuple(jax.ShapeDtypeStruct((4,) + g.shape[1:], g.dtype) for g in gs),
        in_specs=[_ANY] * n, out_specs=(_ANY,) * n,
        scratch_shapes=[pltpu.SemaphoreType.DMA((n, 4)), pltpu.SemaphoreType.DMA((n, 4))],
    )(*gs)


def forward_to_sibling(zones, mine, *, name):
    n = len(zones)

    def body(*refs):
        m_refs, z_refs = refs[n:2 * n], refs[2 * n:3 * n]
        send_sems, recv_sems, local_sems = refs[3 * n:]
        px, py, pc = _place()
        local = [pltpu.make_async_copy(m_refs[t], z_refs[t].at[4 * px + 2 * py + pc], local_sems.at[t]) for t in range(n)]
        copies = []
        for t in range(n):
            local[t].start()
            for j, chip in enumerate([(1 - px, py), (px, 1 - py), (1 - px, 1 - py)]):
                slot = z_refs[t].at[4 * chip[0] + 2 * chip[1] + pc]
                copies.append(pltpu.make_async_remote_copy(
                    src_ref=slot, dst_ref=slot, send_sem=send_sems.at[t, j], recv_sem=recv_sems.at[t, j],
                    device_id=(px, py, 1 - pc), device_id_type=MESH))
        for cp in copies:
            cp.start()
        for cp in copies:
            cp.wait()
        for cp in local:
            cp.wait()

    return pl.pallas_call(
        body, name=name, out_shape=tuple(jax.ShapeDtypeStruct(z.shape, z.dtype) for z in zones),
        in_specs=[_ANY] * (2 * n), out_specs=(_ANY,) * n, input_output_aliases={t: t for t in range(n)},
        scratch_shapes=[pltpu.SemaphoreType.DMA((n, 3)), pltpu.SemaphoreType.DMA((n, 3)), pltpu.SemaphoreType.DMA((n,))],
    )(*zones, *mine)


_HBM = pl.BlockSpec(memory_space=pltpu.HBM)
_SEM = pl.BlockSpec(memory_space=pltpu.SEMAPHORE)
_EFFECT = pltpu.SideEffectType.DATAFLOW_SIDE_EFFECTING


N_PEERS = 7


def _peers():
    px, py, pc = _place()
    flip = lambda p, bit: 1 - p if bit else p
    return [(flip(px, m >> 2 & 1), flip(py, m >> 1 & 1), flip(pc, m & 1)) for m in range(1, N_PEERS + 1)]


def _gather_copies(src_refs, zone_refs, send_sems, recv_sems):
    px, py, pc = _place()
    return [pltpu.make_async_remote_copy(src_ref=s, dst_ref=z.at[4 * px + 2 * py + pc], send_sem=send_sems.at[k],
                                         recv_sem=recv_sems.at[k], device_id=peer, device_id_type=MESH)
            for k, peer in enumerate(_peers()) for s, z in zip(src_refs, zone_refs)]


def _scatter_copies(part_refs, zone_refs, send_sems, recv_sems):
    return [pltpu.make_async_remote_copy(src_ref=p.at[4 * peer[0] + 2 * peer[1] + peer[2]], dst_ref=z.at[k],
                                         send_sem=send_sems.at[k], recv_sem=recv_sems.at[k], device_id=peer, device_id_type=MESH)
            for k, peer in enumerate(_peers()) for p, z in zip(part_refs, zone_refs)]


def copies_start(make, srcs, zones, *, name):
    n = len(srcs)

    def body(*refs):
        for cp in make(refs[:n], refs[n:2 * n], refs[2 * n], refs[2 * n + 1]):
            cp.start()
        refs[-1][...] = jnp.zeros_like(refs[-1])

    arrays = list(srcs) + list(zones)
    outs = pl.pallas_call(
        body, name=name,
        out_shape=(pltpu.SemaphoreType.DMA((N_PEERS,)), pltpu.SemaphoreType.DMA((N_PEERS,)),
                   *[pltpu.HBM(a.shape, a.dtype) for a in arrays], jax.ShapeDtypeStruct((8, LANE), F32)),
        in_specs=[_HBM] * (2 * n), out_specs=(_SEM, _SEM, *[_HBM] * (2 * n), pl.BlockSpec(memory_space=pltpu.VMEM)),
        input_output_aliases={i: 2 + i for i in range(2 * n)},
        compiler_params=pltpu.CompilerParams(has_side_effects=_EFFECT),
    )(*[pltpu.with_memory_space_constraint(a, pltpu.HBM) for a in arrays])
    return outs[0], outs[1], list(outs[2:2 + n]), list(outs[2 + n:2 + 2 * n]), outs[-1]


def copies_wait(make, send_sems, recv_sems, srcs, zones, after, *, name):
    n = len(srcs)

    def body(*refs):
        for cp in make(refs[:n], refs[n:2 * n], refs[2 * n], refs[2 * n + 1]):
            cp.wait_send()
            cp.wait_recv()

    arrays = list(srcs) + list(zones)
    outs = pl.pallas_call(
        body, name=name, out_shape=tuple(pltpu.HBM(a.shape, a.dtype) for a in arrays),
        in_specs=[_HBM] * (2 * n) + [_SEM, _SEM, _ANY], out_specs=(_HBM,) * (2 * n),
        input_output_aliases={i: i for i in range(2 * n)},
        compiler_params=pltpu.CompilerParams(has_side_effects=_EFFECT),
    )(*arrays, send_sems, recv_sems, after)
    return list(outs[:n]), list(outs[n:])


def place_mine(zone, mine, *, name):
    C = mine.shape[-1]
    R = int(np.prod(mine.shape[:-1]))
    br = _row_block(R, C, mine.dtype.itemsize, budget=2**21)
    me = (4 * lax.axis_index("x") + 2 * lax.axis_index("y") + lax.axis_index("c")).astype(jnp.int32).reshape(1)

    def body(me_ref, m_ref, z_ref, o_ref):
        o_ref[0] = m_ref[...]

    out = pl.pallas_call(
        body, name=name, out_shape=jax.ShapeDtypeStruct((8, R, C), zone.dtype),
        grid_spec=pltpu.PrefetchScalarGridSpec(
            num_scalar_prefetch=1, grid=(R // br,), in_specs=[pl.BlockSpec((br, C), lambda i, me: (i, 0)), _ANY],
            out_specs=pl.BlockSpec((1, br, C), lambda i, me: (me[0], i, 0))),
        input_output_aliases={2: 0}, compiler_params=_params("parallel"),
    )(me, mine.reshape(R, C), zone.reshape(8, R, C))
    return out.reshape(zone.shape)


def _row_block(rows, cols, itemsize, budget=2**20):
    br = rows
    while br % 32 == 0 and br * cols * itemsize > budget:
        br //= 2
    return br


def pair_sum(g, other, *, name):
    shape, C = g.shape[1:], g.shape[-1]
    R = int(np.prod(shape[:-1]))
    br = _row_block(R, C, 2)
    pc = lax.axis_index("c").astype(jnp.int32).reshape(1)

    def body(c_ref, g_ref, o_ref, out_ref):
        out_ref[...] = (g_ref[...].astype(F32) + o_ref[...].astype(F32)).astype(out_ref.dtype)

    blk = lambda f: pl.BlockSpec((1, br, C), f)
    out = pl.pallas_call(
        body, name=name, out_shape=jax.ShapeDtypeStruct((4, R, C), BF16),
        grid_spec=pltpu.PrefetchScalarGridSpec(
            num_scalar_prefetch=1, grid=(4, R // br),
            in_specs=[blk(lambda j, r, c: (2 * j + c[0], r, 0)), blk(lambda j, r, c: (j, r, 0))],
            out_specs=blk(lambda j, r, c: (j, r, 0))),
        compiler_params=_params("parallel", "parallel"),
    )(pc, g.reshape(8, R, C), other.reshape(4, R, C))
    return out.reshape((4,) + shape)


def _adamw_update(w, g, m, v):
    nm = ADAM_B1 * m + (1.0 - ADAM_B1) * g
    nv = ADAM_B2 * v + (1.0 - ADAM_B2) * (g * g)
    m_hat = nm / (1.0 - ADAM_B1 ** ADAM_STEP)
    v_hat = nv / (1.0 - ADAM_B2 ** ADAM_STEP)
    return -ADAM_LR * (m_hat / (jnp.sqrt(v_hat) + ADAM_EPS) + ADAM_WD * w), nm, nv


def grad_sum_adamw(parts, recvs, w, m, v, *, name):
    L, C = len(parts), w.shape[-1]
    R = int(np.prod(w.shape[1:-1]))
    br = _row_block(R, C, 4, budget=2**19)
    chip = (4 * lax.axis_index("x") + 2 * lax.axis_index("y") + lax.axis_index("c")).astype(jnp.int32).reshape(1)

    def body(c_ref, *refs):
        p_refs, r_refs = refs[:L], refs[L:2 * L]
        w_ref, m_ref, v_ref, g_out, d_out, nm_out, nv_out = refs[2 * L:]
        for j in range(L):
            @pl.when(pl.program_id(0) == j)
            def _(j=j):
                g = p_refs[j][0].astype(F32)
                for k in range(N_PEERS):
                    g = g + r_refs[j][k].astype(F32)
                g_out[0] = g
                d_out[0], nm_out[0], nv_out[0] = _adamw_update(w_ref[0], g, m_ref[0], v_ref[0])

    row = lambda j: (lambda l, r, c: jnp.where(l == j, r, 0))
    part_specs = [pl.BlockSpec((1, br, C), lambda l, r, c, f=row(j): (c[0], f(l, r, c), 0)) for j in range(L)]
    recv_specs = [pl.BlockSpec((N_PEERS, br, C), lambda l, r, c, f=row(j): (0, f(l, r, c), 0)) for j in range(L)]
    blk = pl.BlockSpec((1, br, C), lambda l, r, c: (l, r, 0))
    outs = pl.pallas_call(
        body, name=name, out_shape=tuple(jax.ShapeDtypeStruct((L, R, C), F32) for _ in range(4)),
        grid_spec=pltpu.PrefetchScalarGridSpec(
            num_scalar_prefetch=1, grid=(L, R // br), in_specs=part_specs + recv_specs + [blk] * 3, out_specs=(blk,) * 4),
        compiler_params=_params("arbitrary", "arbitrary"),
    )(chip, *[p.reshape(8, R, C) for p in parts], *[r.reshape(N_PEERS, R, C) for r in recvs],
      *[t.reshape(L, R, C) for t in (w, m, v)])
    return tuple(o.reshape(w.shape) for o in outs)


def ordered_sum(parts, *, name):
    _, R, C = parts.shape

    def body(p_ref, o_ref):
        acc = p_ref[0]
        for d in range(1, 8):
            acc = acc + p_ref[d]
        o_ref[...] = acc

    return pl.pallas_call(body, name=name, out_shape=jax.ShapeDtypeStruct((R, C), F32))(parts)


W_IN_COLS, W_IN_SHARD = 13128, 1641
W_IN_SEGMENTS = ((0, 832, 0), (832, 3904, Z_FOX), (3904, 3912, FF_COL), (3912, 6984, Z_CH), (6984, 13128, Z_GATE))


def col_gather(src, table, pieces, out_shape, *, name, tr=1024):
    R, C = src.shape[1:]
    tr = _tile(R, tr)
    width = 2 + 6 * pieces
    nb = table.shape[0] // width
    last_tile, last_valid = C // LANE, C % LANE

    def body(tab, *refs):
        o_ref = refs[-1]
        base = pl.program_id(1) * width
        lane = lax.broadcasted_iota(jnp.int32, (tr, LANE), 1)
        row = lax.broadcasted_iota(jnp.int32, (2 * LANE * pieces, LANE), 0)
        col = lax.broadcasted_iota(jnp.int32, (2 * LANE * pieces, LANE), 1)
        tiles, hit = [], None
        for p in range(pieces):
            e = base + 2 + 6 * p
            for tcol in (1, 2):
                x = refs[2 * p + tcol - 1][0]
                if last_valid:
                    x = jnp.where(jnp.logical_or(tab[e + tcol] < last_tile, lane < last_valid), x, jnp.zeros_like(x))
                tiles.append(x)
            lo, hi = tab[e + 4], tab[e + 5]
            cond = jnp.logical_and(row - 2 * LANE * p == col + tab[e + 3], jnp.logical_and(col >= lo, col < hi))
            hit = cond if hit is None else jnp.logical_or(hit, cond)
        sel = jnp.where(hit, 1.0, 0.0).astype(src.dtype)
        o_ref[0] = jnp.dot(jnp.concatenate(tiles, axis=1), sel, preferred_element_type=F32).astype(o_ref.dtype)

    in_specs = []
    for p in range(pieces):
        for tcol in (1, 2):
            in_specs.append(pl.BlockSpec(
                (1, tr, LANE), lambda i, b, tab, p=p, tcol=tcol: (tab[b * width + 2 + 6 * p], i, tab[b * width + 2 + 6 * p + tcol])))
    return pl.pallas_call(
        body, name=name, out_shape=jax.ShapeDtypeStruct(out_shape, src.dtype),
        grid_spec=pltpu.PrefetchScalarGridSpec(
            num_scalar_prefetch=1, grid=(R // tr, nb), in_specs=in_specs,
            out_specs=pl.BlockSpec((1, tr, LANE), lambda i, b, tab: (tab[b * width], i, tab[b * width + 1]))),
        compiler_params=_params("parallel", "parallel"),
    )(jnp.asarray(table, jnp.int32), *([src] * (2 * pieces)))


def _piece(sd, start, lo, hi, last_tile):
    t0 = start // LANE
    return [sd, t0, min(t0 + 1, last_tile), start % LANE - lo, lo, hi]


def _pad_pieces(rows, pieces):
    out, prev = [], [0, 0, 0, 0, 0, 0] * pieces
    for head, pcs in rows:
        full = list(pcs)
        for p in range(len(pcs) // 6, pieces):
            full += prev[6 * p:6 * p + 3] + [0, 0, 0]
        out.append(head + full)
        prev = full
    return np.asarray(out, np.int32).reshape(-1)


def _w_in_table(layer, L):
    rows = []
    for b in range(Z_W // LANE):
        pcs = []
        for first, last, col in W_IN_SEGMENTS:
            lo, hi = max(LANE * b, col), min(LANE * (b + 1), col + last - first)
            while lo < hi:
                c = first + lo - col
                n = min(hi - lo, W_IN_SHARD - c % W_IN_SHARD)
                pcs += _piece((c // W_IN_SHARD) * L + layer, c % W_IN_SHARD, lo - LANE * b, lo - LANE * b + n, W_IN_SHARD // LANE)
                lo += n
        assert len(pcs) <= 12
        rows.append(([0, b], pcs))
    return _pad_pieces(rows, 2)


def _w_in_grad_table():
    rows = []
    for d in range(8):
        for t in range(-(-W_IN_SHARD // LANE)):
            pcs = []
            c0 = d * W_IN_SHARD + LANE * t
            c1 = min(c0 + LANE, (d + 1) * W_IN_SHARD)
            for first, last, col in W_IN_SEGMENTS:
                lo, hi = max(c0, first), min(c1, last)
                if lo < hi:
                    pcs += _piece(0, col + lo - first, lo - c0, hi - c0, Z_W // LANE - 1)
            assert len(pcs) <= 18
            rows.append(([d, t], pcs))
    return _pad_pieces(rows, 3)


def block_copy(src, out_shape, in_blk, out_blk, grid, in_map, out_map, *, name):
    def body(x_ref, o_ref):
        o_ref[(0,) * (len(out_blk) - 2) + (Ellipsis,)] = x_ref[(0,) * (len(in_blk) - 2) + (Ellipsis,)]

    return pl.pallas_call(
        body, name=name, out_shape=jax.ShapeDtypeStruct(out_shape, src.dtype), grid=grid,
        in_specs=[pl.BlockSpec(in_blk, in_map)], out_specs=pl.BlockSpec(out_blk, out_map),
        compiler_params=_params("parallel", "parallel"),
    )(src)


def _columns_from_owners(z, *, name, lead=()):
    K, c = z.shape[-2:]
    tr, nl = _tile(K, 1024), len(lead)
    return block_copy(z, (K, 8 * c), (1,) * (1 + nl) + (tr, c), (tr, c), (8, K // tr),
                      lambda d, i: (d, *lead, i, 0), lambda d, i: (i, d), name=name)


def _owners_from_columns(g, *, name):
    K, c = g.shape[0], g.shape[1] // 8
    tr = _tile(K, 1024)
    return block_copy(g, (8, K, c), (tr, c), (1, tr, c), (8, K // tr), lambda d, i: (i, d), lambda d, i: (d, i, 0), name=name)


def _full_from_shards(k, sh, tag):
    if k not in COL_SHARDED:
        return sh.reshape((-1, sh.shape[-1]))
    if k == 'w_br':
        return [_columns_from_owners(sh, lead=(n,), name=f"{tag}_w_br{n}_layout") for n in range(3)]
    if k == 'w_uq':
        return _columns_from_owners(jnp.pad(sh, ((0, 0), (0, 0), (0, 64))), name=f"{tag}_w_uq_layout")
    if k == 'w_ukv':
        return block_copy(sh, (256, 2048), (1, 256, LANE), (256, LANE), (2, MLA_HEADS),
                          lambda t, h: (h, 0, t), lambda t, h: (0, t * MLA_HEADS + h), name=f"{tag}_w_ukv_layout")
    return _columns_from_owners(sh, name=f"{tag}_{k}_layout")


def _shards_from_full(k, g, tag):
    if k not in COL_SHARDED:
        return g.reshape((8, g.shape[0] // 8, g.shape[1]))
    if k == 'w_br':
        return jnp.stack([_owners_from_columns(g[n], name=f"{tag}_dw_br{n}_layout") for n in range(3)], axis=1)
    if k == 'w_uq':
        return _owners_from_columns(g, name=f"{tag}_dw_uq_layout")[:, :, :MLA_QK]
    if k == 'w_ukv':
        return block_copy(g, (8, 256, 256), (256, LANE), (1, 256, LANE), (2, MLA_HEADS),
                          lambda t, h: (0, t * MLA_HEADS + h), lambda t, h: (h, 0, t), name=f"{tag}_dw_ukv_layout")
    return _owners_from_columns(g, name=f"{tag}_d{k}_layout")


def w_in_full(gathered, layer, *, name):
    _, L, K, c = gathered.shape
    return col_gather(gathered.reshape(8 * L, K, c), _w_in_table(layer, L), 2, (1, K, Z_W), name=name)[0]


def w_in_shards(g, *, name):
    return col_gather(g[None], _w_in_grad_table(), 3, (8, g.shape[0], W_IN_SHARD), name=name)


def _layer_fwd(x, mem, W, P, cos, ssin, tag):
    S = x.shape[0]
    sv = {'x0': x}
    h = rmsnorm_fwd(x, P['g_mix'], name=f"{tag}_norm_mix")
    z = mm(h, W['w_in'], name=f"{tag}_mm_in")
    sv.update(h=h, z=z)
    cqn = rmsnorm_fwd(z, P['g_cq'], col=0, width=512, name=f"{tag}_norm_cq")
    ckvn = rmsnorm_fwd(z, P['g_ckv'], col=512, width=256, name=f"{tag}_norm_ckv")
    qf = mm(cqn, W['w_uq'], name=f"{tag}_mm_uq")
    kvf = mm(ckvn, W['w_ukv'], name=f"{tag}_mm_ukv")
    qa = mla_prep_fwd(qf, qf, P['g_mla_q'], cos, ssin, n_col=0, n_stride=2 * LANE, r_col=LANE, r_stride=2 * LANE,
                      heads=8, name=f"{tag}_mla_q")
    ka = mla_prep_fwd(kvf, z, P['g_mla_k'], cos, ssin, n_col=0, n_stride=LANE, r_col=KR_COL, r_stride=0,
                      heads=8, name=f"{tag}_mla_k")
    ya, lse_a = causal_attn_fwd(qa, ka, kvf, v_col=1024, chunked=True, scale=MLA_QK ** -0.5, name=f"{tag}_mla_attn")
    sv.update(cqn=cqn, ckvn=ckvn, qf=qf, kvf=kvf, qa=qa, ka=ka, lse_a=lse_a)
    qb = headnorm_fwd(z, P['g_fox_q'], col=Z_FOX, heads=8, name=f"{tag}_fox_qn")
    kb = headnorm_fwd(z, P['g_fox_k'], col=Z_FOX + 1024, heads=8, name=f"{tag}_fox_kn")
    fl = z[:, FF_COL:FF_COL + 8].T
    cum = foxgate_fwd(fl, P['b_f'], name=f"{tag}_fox_gate")
    cq, ck = cum.reshape(8, S, 1), cum.reshape(8, 1, S)
    yb, lse_b = causal_attn_fwd(qb, kb, z, v_col=Z_FOX + 2048, chunked=False, scale=LANE ** -0.5, cq=cq, ck=ck,
                                name=f"{tag}_fox_attn")
    sv.update(qb=qb, kb=kb, fl=fl, cq=cq, ck=ck, lse_b=lse_b)
    qc = headnorm_fwd(z, P['g_ch_q'], col=Z_CH, heads=8, name=f"{tag}_ch_qn")
    kc = headnorm_fwd(z, P['g_ch_k'], col=Z_CH + 1024, heads=8, name=f"{tag}_ch_kn")
    kcp = jnp.pad(kc, ((0, 0), (PAD, CHUNK), (0, 0)))
    vcp = jnp.pad(z[:, Z_CH + 2048:Z_CH + 3072], ((PAD, CHUNK), (0, 0)))
    bias = band_bias(P['rel_bias'], name=f"{tag}_ch_bias")
    yc = band_fwd(qc, kcp, vcp, bias, scale=LANE ** -0.5, name=f"{tag}_ch_attn")
    sv.update(qc=qc, kcp=kcp, vcp=vcp, bias=bias)
    ys = (ya, yb, yc)
    proj = [mm(ys[n], W['w_br'][n], name=f"{tag}_mm_br{n}") for n in range(3)]
    merged = gate_fwd(z, proj, name=f"{tag}_gate")
    x1 = mm(merged, W['w_out'], epi='add', aux=x, name=f"{tag}_mm_out")
    sv.update(ys=ys, proj=proj, merged=merged, x1=x1)
    hc = rmsnorm_fwd(x1, P['g_cross'], name=f"{tag}_norm_cross")
    memn = rmsnorm_fwd(mem, P['g_mem'], name=f"{tag}_norm_mem")
    qx_raw = mm(hc, W['w_xq'], name=f"{tag}_mm_xq")
    memkv = mm(memn, W['w_xkv'], name=f"{tag}_mm_xkv")
    qx = headnorm_fwd(qx_raw, P['g_x_q'], col=0, heads=4, name=f"{tag}_x_qn")
    kx = headnorm_fwd(memkv, P['g_x_k'], col=0, heads=4, name=f"{tag}_x_kn")
    ox = attn_fwd(qx, kx, memkv, v_col=512, scale=LANE ** -0.5, name=f"{tag}_x_attn")
    x2 = mm(ox, W['w_xo'], epi='add', aux=x1, name=f"{tag}_mm_xo")
    sv.update(hc=hc, memn=memn, qx_raw=qx_raw, memkv=memkv, qx=qx, kx=kx, ox=ox, x2=x2)
    hm = rmsnorm_fwd(x2, P['g_mlp'], name=f"{tag}_norm_mlp")
    u, a = mm(hm, W['w_1'], epi='relu2', out_dtype=BF16, name=f"{tag}_mm_w1")
    x3 = mm(a, W['w_2'], epi='add', aux=x2, name=f"{tag}_mm_w2")
    sv.update(hm=hm, u=u, a=a)
    return x3, sv


def _layer_bwd(dx, mem, W, P, sv, cos, ssin, tag, send_off=None):
    S = dx.shape[0]
    z = sv['z']
    gw, gs = {}, {}
    wgrad = lambda a, d, name: mm(a, d, ta=True, out_dtype=BF16, name=name)
    gw['w_2'] = wgrad(sv['a'], dx, f"{tag}_dw2")
    du = mm(dx, W['w_2'], tb=True, epi='mul_drelu2', aux=sv['u'], out_dtype=BF16, name=f"{tag}_du")
    gw['w_1'] = wgrad(sv['hm'], du, f"{tag}_dw1")
    dhm = mm(du, W['w_1'], tb=True, name=f"{tag}_dhm")
    dx, gs['g_mlp'] = rmsnorm_bwd(sv['x2'], P['g_mlp'], dhm, res=dx, name=f"{tag}_dnorm_mlp")
    gw['w_xo'] = wgrad(sv['ox'], dx, f"{tag}_dwxo")
    dox = mm(dx, W['w_xo'], tb=True, out_dtype=BF16, name=f"{tag}_dox")
    dqx, dkx, dvx = attn_bwd(sv['qx'], sv['kx'], sv['memkv'], dox, v_col=512, scale=LANE ** -0.5, name=f"{tag}_x_attn_bwd")
    dqx_raw, gs['g_x_q'] = headnorm_bwd(sv['qx_raw'], P['g_x_q'], dqx, col=0, heads=4, name=f"{tag}_x_qn_bwd")
    dkx_raw, gs['g_x_k'] = headnorm_bwd(sv['memkv'], P['g_x_k'], dkx, col=0, heads=4, name=f"{tag}_x_kn_bwd")
    dqx_b = dqx_raw.astype(BF16)
    gw['w_xq'] = wgrad(sv['hc'], dqx_b, f"{tag}_dwxq")
    dhc = mm(dqx_b, W['w_xq'], tb=True, name=f"{tag}_dhc")
    dx, gs['g_cross'] = rmsnorm_bwd(sv['x1'], P['g_cross'], dhc, res=dx, name=f"{tag}_dnorm_cross")
    dmemkv = jnp.concatenate([dkx_raw, dvx], axis=1).astype(BF16)
    gw['w_xkv'] = wgrad(sv['memn'], dmemkv, f"{tag}_dwxkv")
    dmemn = mm(dmemkv, W['w_xkv'], tb=True, name=f"{tag}_dmemn")
    _, gs['g_mem'] = rmsnorm_bwd(mem, P['g_mem'], dmemn, name=f"{tag}_dnorm_mem")
    gw['w_out'] = wgrad(sv['merged'], dx, f"{tag}_dwout")
    dmerged = mm(dx, W['w_out'], tb=True, name=f"{tag}_dmerged")
    dproj, dgl = gate_bwd(z, sv['proj'], dmerged, name=f"{tag}_gate_bwd")
    gw['w_br'] = [wgrad(sv['ys'][n], dproj[n], f"{tag}_dwbr{n}") for n in range(3)]
    dys = [mm(dproj[n], W['w_br'][n], tb=True, out_dtype=BF16, name=f"{tag}_dys{n}") for n in range(3)]
    dqa, dka, dva = causal_attn_bwd(sv['qa'], sv['ka'], sv['kvf'], sv['ys'][0], dys[0], sv['lse_a'], v_col=1024, chunked=True,
                                    scale=MLA_QK ** -0.5, name=f"{tag}_mla_attn_bwd")
    dqn, dqr, gs['g_mla_q'] = mla_prep_bwd(sv['qf'], sv['qf'], P['g_mla_q'], cos, ssin, dqa, n_col=0, n_stride=2 * LANE,
                                           r_col=LANE, r_stride=2 * LANE, heads=8, name=f"{tag}_mla_q_bwd")
    dkn, dkr, gs['g_mla_k'] = mla_prep_bwd(sv['kvf'], z, P['g_mla_k'], cos, ssin, dka, n_col=0, n_stride=LANE,
                                           r_col=KR_COL, r_stride=0, heads=8, name=f"{tag}_mla_k_bwd")
    dqf = jnp.stack([dqn.reshape(S, 8, LANE), dqr.reshape(S, 8, LANE)], axis=2).reshape(S, 2048).astype(BF16)
    dkvf = jnp.concatenate([dkn, dva], axis=1).astype(BF16)
    gw['w_uq'] = wgrad(sv['cqn'], dqf, f"{tag}_dwuq")
    gw['w_ukv'] = wgrad(sv['ckvn'], dkvf, f"{tag}_dwukv")
    dcqn = mm(dqf, W['w_uq'], tb=True, name=f"{tag}_dcqn")
    dckvn = mm(dkvf, W['w_ukv'], tb=True, name=f"{tag}_dckvn")
    dcq_raw, gs['g_cq'] = rmsnorm_bwd(z, P['g_cq'], dcqn, col=0, width=512, name=f"{tag}_dnorm_cq")
    dckv_raw, gs['g_ckv'] = rmsnorm_bwd(z, P['g_ckv'], dckvn, col=512, width=256, name=f"{tag}_dnorm_ckv")
    dqb, dkb, dvb, dcq, dck = causal_attn_bwd(sv['qb'], sv['kb'], z, sv['ys'][1], dys[1], sv['lse_b'], v_col=Z_FOX + 2048,
                                              chunked=False, scale=LANE ** -0.5, cq=sv['cq'], ck=sv['ck'],
                                              name=f"{tag}_fox_attn_bwd")
    dqb_raw, gs['g_fox_q'] = headnorm_bwd(z, P['g_fox_q'], dqb, col=Z_FOX, heads=8, name=f"{tag}_fox_qn_bwd")
    dkb_raw, gs['g_fox_k'] = headnorm_bwd(z, P['g_fox_k'], dkb, col=Z_FOX + 1024, heads=8, name=f"{tag}_fox_kn_bwd")
    dfl, gs['b_f'] = foxgate_bwd(sv['fl'], P['b_f'], dcq.reshape(8, S) + dck.reshape(8, S), name=f"{tag}_fox_gate_bwd")
    dqc, dkcp, dvcp, dbias = band_bwd(sv['qc'], sv['kcp'], sv['vcp'], sv['bias'], dys[2], scale=LANE ** -0.5,
                                      name=f"{tag}_ch_attn_bwd")
    dqc_raw, gs['g_ch_q'] = headnorm_bwd(z, P['g_ch_q'], dqc, col=Z_CH, heads=8, name=f"{tag}_ch_qn_bwd")
    dkc_raw, gs['g_ch_k'] = headnorm_bwd(z, P['g_ch_k'], dkcp[:, PAD:PAD + S, :], col=Z_CH + 1024, heads=8,
                                         name=f"{tag}_ch_kn_bwd")
    gs['rel_bias'] = relbias_bwd(dbias, name=f"{tag}_relbias_bwd")
    b16 = lambda t: t.astype(BF16)
    dz = jnp.concatenate([b16(dcq_raw), b16(dckv_raw), b16(dkr), b16(dfl.T), jnp.zeros((S, 120), BF16),
                          b16(dqb_raw), b16(dkb_raw), b16(dvb), b16(dqc_raw), b16(dkc_raw), b16(dvcp[PAD:PAD + S]),
                          dgl[0], dgl[1], dgl[2]], axis=1)
    gw['w_in'] = wgrad(sv['h'], dz, f"{tag}_dwin")
    dh = mm(dz, W['w_in'], tb=True, name=f"{tag}_dh")
    g_mix = P['g_mix'] if send_off is None else P['g_mix'] + send_off(gw)
    dx, gs['g_mix'] = rmsnorm_bwd(sv['x0'], g_mix, dh, res=dx, name=f"{tag}_dnorm_mix")
    return dx, gw, gs


def _local_step(x, mem, target, Ws, Ps):
    S = x.shape[0]
    cos, ssin = _rope_tables(S)
    L = len(Ws)
    saved = []
    for l in range(L):
        x, sv = _layer_fwd(x, mem, Ws[l], Ps[l], cos, ssin, f"l{l}")
        saved.append(sv)
    loss, dx = loss_head(x, target, name="loss_head")
    gws, gss = [None] * L, [None] * L
    for l in reversed(range(L)):
        dx, gws[l], gss[l] = _layer_bwd(dx, mem, Ws[l], Ps[l], saved[l], cos, ssin, f"l{l}")
    return loss, dx, gws, gss


def _pack_small(d):
    flat = jnp.concatenate([d[k].reshape(-1) for k in SMALL])
    n = flat.shape[0]
    rows = -(-n // (8 * LANE)) * 8
    return jnp.pad(flat, (0, rows * LANE - n)).reshape(rows, LANE)


def _unpack_small(packed, like):
    flat, out, off = packed.reshape(-1), {}, 0
    for k in SMALL:
        n = int(np.prod(like[k].shape))
        out[k] = flat[off:off + n].reshape(like[k].shape)
        off += n
    return out


def kernel(x, mem, g_mix, w_in, g_cq, w_uq, g_ckv, w_ukv, g_mla_q, g_mla_k, b_f, g_fox_q, g_fox_k, rel_bias, g_ch_q, g_ch_k, w_br, w_out, g_cross, g_mem, w_xq, w_xkv, g_x_q, g_x_k, w_xo, g_mlp, w_1, w_2, loss_target, m_g_mix, m_w_in, m_g_cq, m_w_uq, m_g_ckv, m_w_ukv, m_g_mla_q, m_g_mla_k, m_b_f, m_g_fox_q, m_g_fox_k, m_rel_bias, m_g_ch_q, m_g_ch_k, m_w_br, m_w_out, m_g_cross, m_g_mem, m_w_xq, m_w_xkv, m_g_x_q, m_g_x_k, m_w_xo, m_g_mlp, m_w_1, m_w_2, v_g_mix, v_w_in, v_g_cq, v_w_uq, v_g_ckv, v_w_ukv, v_g_mla_q, v_g_mla_k, v_b_f, v_g_fox_q, v_g_fox_k, v_rel_bias, v_g_ch_q, v_g_ch_k, v_w_br, v_w_out, v_g_cross, v_g_mem, v_w_xq, v_w_xkv, v_g_x_q, v_g_x_k, v_w_xo, v_g_mlp, v_w_1, v_w_2):
    args = locals()
    w = {k: args[k] for k in WEIGHTS}
    m = {k: args['m_' + k] for k in WEIGHTS}
    v = {k: args['v_' + k] for k in WEIGHTS}
    L = w_in.shape[0]

    Ps = [{k: w[k][l] for k in SMALL} for l in range(L)]
    xs, memv = x[0], mem[0]
    cos, ssin = _rope_tables(xs.shape[0])

    gathers = []
    for l in range(L):
        shards = [w[k][l].astype(BF16) for k in BIG]
        gathers.append(copies_start(_gather_copies, shards, [lax.empty((8,) + s.shape, s.dtype) for s in shards],
                                    name=f"l{l}_ag_start"))
    Ps[0]['g_mix'] = Ps[0]['g_mix'] + sum(g[4][0, :1] for g in gathers)

    Ws, saved = [], []
    for l in range(L):
        send_sems, recv_sems, shards, zones, _ = gathers[l]
        shards, zones = copies_wait(_gather_copies, send_sems, recv_sems, shards, zones, xs if l else Ps[0]['g_mix'],
                                    name=f"l{l}_ag_wait")
        zones = [place_mine(z, s, name=f"l{l}_{k}_mine") for k, z, s in zip(BIG, zones, shards)]
        Ws.append({k: w_in_full(z[:, None], 0, name=f"l{l}_w_in_layout") if k == 'w_in' else _full_from_shards(k, z, f"l{l}")
                   for k, z in zip(BIG, zones)})
        xs, sv = _layer_fwd(xs, memv, Ws[l], Ps[l], cos, ssin, f"l{l}")
        saved.append(sv)

    loss, dx = loss_head(xs, loss_target[0], name="loss_head")
    loss = lax.psum(loss[0, 0], ("x", "y", "c"))

    gss, scatters = [None] * L, [None] * L
    for l in reversed(range(L)):
        def send_off(gw, l=l):
            gdst = [w_in_shards(gw[k], name=f"l{l}_dw_in_layout") if k == 'w_in' else _shards_from_full(k, gw[k], f"l{l}")
                    for k in BIG]
            started = copies_start(_scatter_copies, gdst, [lax.empty((N_PEERS,) + g.shape[1:], g.dtype) for g in gdst],
                                   name=f"l{l}_rs_start")
            scatters[l] = started[:4]
            return started[4][0, :1]

        dx, _, gss[l] = _layer_bwd(dx, memv, Ws[l], Ps[l], saved[l], cos, ssin, f"l{l}", send_off)
    grad_x = dx
    for l in range(L):
        scatters[l] = copies_wait(_scatter_copies, *scatters[l], grad_x, name=f"l{l}_rs_wait")

    grads = {}
    small_part = _pack_small({k: jnp.stack([gss[l][k] for l in range(L)]) for k in SMALL})
    small_all = all_gather([small_part], name="ag_small")[0]
    grads.update(_unpack_small(ordered_sum(small_all, name="small_sum"), {k: w[k] for k in SMALL}))

    delta, new_m, new_v = {}, {}, {}
    for t, k in enumerate(BIG):
        grads[k], delta[k], new_m[k], new_v[k] = grad_sum_adamw(
            [scatters[l][0][t] for l in range(L)], [scatters[l][1][t] for l in range(L)], w[k], m[k], v[k], name=f"adamw_{k}")
    sd, sm, sv_ = adamw(_pack_small({k: w[k] for k in SMALL}), _pack_small({k: grads[k] for k in SMALL}),
                        _pack_small({k: m[k] for k in SMALL}), _pack_small({k: v[k] for k in SMALL}), name="adamw_small")
    like = {k: w[k] for k in SMALL}
    delta.update(_unpack_small(sd, like))
    new_m.update(_unpack_small(sm, like))
    new_v.update(_unpack_small(sv_, like))

    return (loss, grad_x[None], *[grads[k] for k in WEIGHTS], *[delta[k] for k in WEIGHTS],
            *[new_m[k] for k in WEIGHTS], *[new_v[k] for k in WEIGHTS])
```

```python
import numpy as np
import jax
import jax.numpy as jnp
from jax import lax
from jax.experimental import pallas as pl
from jax.experimental.pallas import tpu as pltpu

F32, BF16 = jnp.float32, jnp.bfloat16
EPS = 1e-6
NEG = -1e30
LANE = 128
VMEM_LIMIT_BYTES = 56 * 2**20
MESH = pl.DeviceIdType.MESH

D_MODEL = 2048
CHUNK = 64
BAND = 9 * CHUNK
PAD = 8 * CHUNK
REL_CLIP = 128
MLA_HEADS, MLA_NOPE, MLA_ROPE, MLA_QK = 8, 128, 64, 192
N_HEADS = 8
X_HEADS = 4
ROPE_THETA = 10000.0
ADAM_LR, ADAM_B1, ADAM_B2, ADAM_EPS, ADAM_WD, ADAM_STEP = 0.001, 0.9, 0.999, 1e-08, 0.01, 10

Z_MAIN, Z_FOX, Z_CH, Z_GATE, Z_W = 0, 1024, 4096, 7168, 13312
KR_COL, FF_COL = 768, 896

BIG = ('w_in', 'w_uq', 'w_ukv', 'w_br', 'w_out', 'w_xq', 'w_xkv', 'w_xo', 'w_1', 'w_2')
COL_SHARDED = ('w_in', 'w_uq', 'w_ukv', 'w_br', 'w_xo', 'w_1')
RS_GROUPS = (('w_2', 'w_1', 'w_xo', 'w_xq', 'w_xkv', 'w_out', 'w_br'), ('w_uq', 'w_ukv', 'w_in'))
SMALL = ('g_mix', 'g_cq', 'g_ckv', 'g_mla_q', 'g_mla_k', 'b_f', 'g_fox_q', 'g_fox_k', 'rel_bias', 'g_ch_q',
         'g_ch_k', 'g_cross', 'g_mem', 'g_x_q', 'g_x_k', 'g_mlp')
WEIGHTS = ('g_mix', 'w_in', 'g_cq', 'w_uq', 'g_ckv', 'w_ukv', 'g_mla_q', 'g_mla_k', 'b_f', 'g_fox_q', 'g_fox_k',
           'rel_bias', 'g_ch_q', 'g_ch_k', 'w_br', 'w_out', 'g_cross', 'g_mem', 'w_xq', 'w_xkv', 'g_x_q', 'g_x_k',
           'w_xo', 'g_mlp', 'w_1', 'w_2')


def _params(*sem):
    return pltpu.CompilerParams(dimension_semantics=sem, vmem_limit_bytes=VMEM_LIMIT_BYTES)


def _tile(dim, pref):
    if dim <= pref:
        return dim
    for t in range(pref - pref % LANE, 0, -LANE):
        if dim % t == 0:
            return t
    raise ValueError((dim, pref))


def mm(a, b, *, ta=False, tb=False, out_dtype=F32, epi=None, aux=None, name, tm=1024, tn=512, tk=2048):
    M, K = (a.shape[1], a.shape[0]) if ta else a.shape
    N = b.shape[0] if tb else b.shape[1]
    assert (b.shape[1] if tb else b.shape[0]) == K, (a.shape, b.shape, ta, tb)
    tm, tn, tk = _tile(M, tm), _tile(N, tn), _tile(K, tk)
    nk = K // tk
    dn = (((0 if ta else 1,), (1 if tb else 0,)), ((), ()))
    n_aux = 0 if aux is None else 1

    def finish(acc, aux_refs, o_refs):
        if epi is None:
            o_refs[0][...] = acc.astype(o_refs[0].dtype)
        elif epi == 'add':
            o_refs[0][...] = (acc + aux_refs[0][...]).astype(o_refs[0].dtype)
        elif epi == 'relu2':
            o_refs[0][...] = acc
            r = jnp.maximum(acc, 0.0)
            o_refs[1][...] = (r * r).astype(o_refs[1].dtype)
        elif epi == 'mul_drelu2':
            o_refs[0][...] = (acc * (2.0 * jnp.maximum(aux_refs[0][...], 0.0))).astype(o_refs[0].dtype)

    def body(a_ref, b_ref, *rest):
        aux_refs = rest[:n_aux]
        o_refs = rest[n_aux:n_aux + (2 if epi == 'relu2' else 1)]
        part = lax.dot_general(a_ref[...].astype(BF16), b_ref[...].astype(BF16), dn, preferred_element_type=F32)
        if nk == 1:
            finish(part, aux_refs, o_refs)
        else:
            acc_ref = rest[-1]
            k = pl.program_id(2)

            @pl.when(k == 0)
            def _():
                acc_ref[...] = part

            @pl.when(k > 0)
            def _():
                acc_ref[...] += part

            @pl.when(k == nk - 1)
            def _():
                finish(acc_ref[...], aux_refs, o_refs)

    a_spec = pl.BlockSpec((tk, tm), lambda i, j, k: (k, i)) if ta else pl.BlockSpec((tm, tk), lambda i, j, k: (i, k))
    b_spec = pl.BlockSpec((tn, tk), lambda i, j, k: (j, k)) if tb else pl.BlockSpec((tk, tn), lambda i, j, k: (k, j))
    o_spec = pl.BlockSpec((tm, tn), lambda i, j, k: (i, j))
    if epi == 'relu2':
        out_shape = (jax.ShapeDtypeStruct((M, N), F32), jax.ShapeDtypeStruct((M, N), out_dtype))
        out_specs = (o_spec, o_spec)
    else:
        out_shape, out_specs = jax.ShapeDtypeStruct((M, N), out_dtype), o_spec
    return pl.pallas_call(
        body, name=name, out_shape=out_shape, grid=(M // tm, N // tn, nk),
        in_specs=[a_spec, b_spec] + [o_spec] * n_aux, out_specs=out_specs,
        scratch_shapes=[pltpu.VMEM((tm, tn), F32)] if nk > 1 else [],
        compiler_params=_params("parallel", "parallel", "arbitrary"),
    )(a, b, *([aux] if n_aux else []))


def rmsnorm_fwd(x, g, *, col=0, width=None, out_dtype=BF16, name, ts=256):
    S = x.shape[0]
    width = x.shape[1] if width is None else width
    ts, cb = _tile(S, ts), col // width

    def body(x_ref, g_ref, o_ref):
        xf = x_ref[...]
        r = lax.rsqrt(jnp.mean(xf * xf, axis=-1, keepdims=True) + EPS)
        o_ref[...] = (xf * r * g_ref[...]).astype(o_ref.dtype)

    return pl.pallas_call(
        body, name=name, out_shape=jax.ShapeDtypeStruct((S, width), out_dtype), grid=(S // ts,),
        in_specs=[pl.BlockSpec((ts, width), lambda i: (i, cb)), pl.BlockSpec((1, width), lambda i: (0, 0))],
        out_specs=pl.BlockSpec((ts, width), lambda i: (i, 0)), compiler_params=_params("parallel"),
    )(x, g.reshape(1, width))


def rmsnorm_bwd(x, g, dy, *, col=0, width=None, res=None, name, ts=256):
    S = x.shape[0]
    width = x.shape[1] if width is None else width
    ts, cb = _tile(S, ts), col // width
    has_res = res is not None

    def body(x_ref, g_ref, dy_ref, *rest):
        dx_ref, dg_ref = rest[-2:]
        xf = x_ref[...]
        r = lax.rsqrt(jnp.mean(xf * xf, axis=-1, keepdims=True) + EPS)
        dyf = dy_ref[...].astype(F32)
        dyg = dyf * g_ref[...]
        dx = r * dyg - xf * (r * r * r) * jnp.mean(dyg * xf, axis=-1, keepdims=True)
        if has_res:
            dx = dx + rest[0][...]
        dx_ref[...] = dx
        part = jnp.sum(dyf * xf * r, axis=0, keepdims=True)

        @pl.when(pl.program_id(0) == 0)
        def _():
            dg_ref[...] = part

        @pl.when(pl.program_id(0) > 0)
        def _():
            dg_ref[...] += part

    blk = pl.BlockSpec((ts, width), lambda i: (i, 0))
    dx, dg = pl.pallas_call(
        body, name=name,
        out_shape=(jax.ShapeDtypeStruct((S, width), F32), jax.ShapeDtypeStruct((1, width), F32)), grid=(S // ts,),
        in_specs=[pl.BlockSpec((ts, width), lambda i: (i, cb)), pl.BlockSpec((1, width), lambda i: (0, 0)), blk]
        + ([blk] if has_res else []),
        out_specs=(blk, pl.BlockSpec((1, width), lambda i: (0, 0))), compiler_params=_params("arbitrary"),
    )(x, g.reshape(1, width), dy, *([res] if has_res else []))
    return dx, dg.reshape(width)


def headnorm_fwd(x, g, *, col, heads, name, ts=1024):
    S = x.shape[0]
    ts, cb = _tile(S, ts), col // LANE

    def body(x_ref, g_ref, o_ref):
        xf = x_ref[...]
        r = lax.rsqrt(jnp.mean(xf * xf, axis=-1, keepdims=True) + EPS)
        o_ref[0] = (xf * r * g_ref[...]).astype(o_ref.dtype)

    return pl.pallas_call(
        body, name=name, out_shape=jax.ShapeDtypeStruct((heads, S, LANE), BF16), grid=(heads, S // ts),
        in_specs=[pl.BlockSpec((ts, LANE), lambda h, i: (i, cb + h)), pl.BlockSpec((1, LANE), lambda h, i: (0, 0))],
        out_specs=pl.BlockSpec((1, ts, LANE), lambda h, i: (h, i, 0)), compiler_params=_params("parallel", "parallel"),
    )(x, g.reshape(1, LANE))


def headnorm_bwd(x, g, dy, *, col, heads, name, ts=1024):
    S = x.shape[0]
    ts, cb = _tile(S, ts), col // LANE

    def body(x_ref, g_ref, dy_ref, dx_ref, dg_ref):
        xf = x_ref[...]
        r = lax.rsqrt(jnp.mean(xf * xf, axis=-1, keepdims=True) + EPS)
        dyf = dy_ref[0]
        dyg = dyf * g_ref[...]
        dx_ref[...] = r * dyg - xf * (r * r * r) * jnp.mean(dyg * xf, axis=-1, keepdims=True)
        part = jnp.sum(dyf * xf * r, axis=0, keepdims=True)
        first = jnp.logical_and(pl.program_id(0) == 0, pl.program_id(1) == 0)

        @pl.when(first)
        def _():
            dg_ref[...] = part

        @pl.when(jnp.logical_not(first))
        def _():
            dg_ref[...] += part

    dx, dg = pl.pallas_call(
        body, name=name,
        out_shape=(jax.ShapeDtypeStruct((S, heads * LANE), F32), jax.ShapeDtypeStruct((1, LANE), F32)),
        grid=(heads, S // ts),
        in_specs=[pl.BlockSpec((ts, LANE), lambda h, i: (i, cb + h)), pl.BlockSpec((1, LANE), lambda h, i: (0, 0)),
                  pl.BlockSpec((1, ts, LANE), lambda h, i: (h, i, 0))],
        out_specs=(pl.BlockSpec((ts, LANE), lambda h, i: (i, h)), pl.BlockSpec((1, LANE), lambda h, i: (0, 0))),
        compiler_params=_params("arbitrary", "arbitrary"),
    )(x, g.reshape(1, LANE), dy)
    return dx, dg.reshape(LANE)


def _rope_tables(S):
    pos = jnp.arange(S, dtype=F32)
    inv = ROPE_THETA ** (-jnp.arange(0, MLA_ROPE, 2, dtype=F32) / MLA_ROPE)
    ang = pos[:, None] * inv[None, :]
    c, s, z = jnp.cos(ang), jnp.sin(ang), jnp.zeros((S, 64), F32)
    return jnp.concatenate([c, c, z], axis=1), jnp.concatenate([-s, s, z], axis=1)


def _rope(v, cos, ssin, lane):
    partner = jnp.where(lane < 32, pltpu.roll(v, 96, 1), pltpu.roll(v, 32, 1))
    return v * cos + partner * ssin


def mla_prep_fwd(xn, xr, g, cos, ssin, *, n_col, n_stride, r_col, r_stride, heads, name, ts=1024):
    S = xn.shape[0]
    ts = _tile(S, ts)
    nb, ns, rb, rs = n_col // LANE, n_stride // LANE, r_col // LANE, r_stride // LANE
    gn = g[:MLA_NOPE].reshape(1, LANE)
    gr = jnp.concatenate([g[MLA_NOPE:], jnp.zeros((64,), F32)]).reshape(1, LANE)

    def body(n_ref, r_ref, gn_ref, gr_ref, c_ref, s_ref, o_ref):
        n, rr = n_ref[...], r_ref[...]
        ss = jnp.sum(n * n, axis=-1, keepdims=True) + jnp.sum(rr * rr, axis=-1, keepdims=True)
        r = lax.rsqrt(ss * (1.0 / MLA_QK) + EPS)
        lane = lax.broadcasted_iota(jnp.int32, rr.shape, 1)
        o_ref[0, :, :LANE] = (n * r * gn_ref[...]).astype(o_ref.dtype)
        o_ref[0, :, LANE:] = _rope(rr * r * gr_ref[...], c_ref[...], s_ref[...], lane).astype(o_ref.dtype)

    row = lambda h, i: (0, 0)
    return pl.pallas_call(
        body, name=name, out_shape=jax.ShapeDtypeStruct((heads, S, 2 * LANE), BF16), grid=(heads, S // ts),
        in_specs=[pl.BlockSpec((ts, LANE), lambda h, i: (i, nb + ns * h)),
                  pl.BlockSpec((ts, LANE), lambda h, i: (i, rb + rs * h)),
                  pl.BlockSpec((1, LANE), row), pl.BlockSpec((1, LANE), row),
                  pl.BlockSpec((ts, LANE), lambda h, i: (i, 0)), pl.BlockSpec((ts, LANE), lambda h, i: (i, 0))],
        out_specs=pl.BlockSpec((1, ts, 2 * LANE), lambda h, i: (h, i, 0)),
        compiler_params=_params("parallel", "parallel"),
    )(xn, xr, gn, gr, cos, ssin)


def mla_prep_bwd(xn, xr, g, cos, ssin, dy, *, n_col, n_stride, r_col, r_stride, heads, name, ts=1024):
    S = xn.shape[0]
    ts = _tile(S, ts)
    nb, ns, rb, rs = n_col // LANE, n_stride // LANE, r_col // LANE, r_stride // LANE
    shared = r_stride == 0
    gn = g[:MLA_NOPE].reshape(1, LANE)
    gr = jnp.concatenate([g[MLA_NOPE:], jnp.zeros((64,), F32)]).reshape(1, LANE)

    def body(n_ref, r_ref, gn_ref, gr_ref, c_ref, s_ref, dy_ref, dn_ref, dr_ref, dgn_ref, dgr_ref):
        i, h = pl.program_id(0), pl.program_id(1)
        n, rr = n_ref[...], r_ref[...]
        ss = jnp.sum(n * n, axis=-1, keepdims=True) + jnp.sum(rr * rr, axis=-1, keepdims=True)
        r = lax.rsqrt(ss * (1.0 / MLA_QK) + EPS)
        lane = lax.broadcasted_iota(jnp.int32, rr.shape, 1)
        dyn = dy_ref[0, :, :LANE]
        dyr = dy_ref[0, :, LANE:]
        t = dyr * s_ref[...]
        dvr = dyr * c_ref[...] + jnp.where(lane < 32, pltpu.roll(t, 96, 1), pltpu.roll(t, 32, 1))
        dvr = jnp.where(lane < 64, dvr, 0.0)
        dgn_part = jnp.sum(dyn * n * r, axis=0, keepdims=True)
        dgr_part = jnp.sum(dvr * rr * r, axis=0, keepdims=True)
        dyn_g, dvr_g = dyn * gn_ref[...], dvr * gr_ref[...]
        proj = (jnp.sum(dyn_g * n, axis=-1, keepdims=True) + jnp.sum(dvr_g * rr, axis=-1, keepdims=True)) * (1.0 / MLA_QK)
        r3 = r * r * r
        dn_ref[...] = r * dyn_g - n * r3 * proj
        dr = r * dvr_g - rr * r3 * proj
        if shared:
            @pl.when(h == 0)
            def _():
                dr_ref[...] = dr

            @pl.when(h > 0)
            def _():
                dr_ref[...] += dr
        else:
            dr_ref[...] = dr
        first = jnp.logical_and(i == 0, h == 0)

        @pl.when(first)
        def _():
            dgn_ref[...] = dgn_part
            dgr_ref[...] = dgr_part

        @pl.when(jnp.logical_not(first))
        def _():
            dgn_ref[...] += dgn_part
            dgr_ref[...] += dgr_part

    row = lambda i, h: (0, 0)
    dr_cols = LANE if shared else heads * LANE
    dn, dr, dgn, dgr = pl.pallas_call(
        body, name=name,
        out_shape=(jax.ShapeDtypeStruct((S, heads * LANE), F32), jax.ShapeDtypeStruct((S, dr_cols), F32),
                   jax.ShapeDtypeStruct((1, LANE), F32), jax.ShapeDtypeStruct((1, LANE), F32)),
        grid=(S // ts, heads),
        in_specs=[pl.BlockSpec((ts, LANE), lambda i, h: (i, nb + ns * h)),
                  pl.BlockSpec((ts, LANE), lambda i, h: (i, rb + rs * h)),
                  pl.BlockSpec((1, LANE), row), pl.BlockSpec((1, LANE), row),
                  pl.BlockSpec((ts, LANE), lambda i, h: (i, 0)), pl.BlockSpec((ts, LANE), lambda i, h: (i, 0)),
                  pl.BlockSpec((1, ts, 2 * LANE), lambda i, h: (h, i, 0))],
        out_specs=(pl.BlockSpec((ts, LANE), lambda i, h: (i, h)),
                   pl.BlockSpec((ts, LANE), (lambda i, h: (i, 0)) if shared else (lambda i, h: (i, h))),
                   pl.BlockSpec((1, LANE), row), pl.BlockSpec((1, LANE), row)),
        compiler_params=_params("arbitrary", "arbitrary"),
    )(xn, xr, gn, gr, cos, ssin, dy)
    return dn, dr, jnp.concatenate([dgn.reshape(LANE), dgr.reshape(LANE)[:MLA_ROPE]])


_NT = (((1,), (1,)), ((), ()))
_TN = (((0,), (0,)), ((), ()))


def attn_fwd(q, k, v, *, v_col, scale, name, bq=256):
    H, S, dk = q.shape
    Sk = k.shape[1]
    bq, vb = _tile(S, bq), v_col // LANE

    def body(q_ref, k_ref, v_ref, o_ref):
        s = lax.dot_general(q_ref[0], k_ref[0], _NT, preferred_element_type=F32) * scale
        e = jnp.exp(s - jnp.max(s, axis=-1, keepdims=True))
        p = e * (1.0 / jnp.sum(e, axis=-1, keepdims=True))
        o_ref[...] = jnp.dot(p.astype(BF16), v_ref[...].astype(BF16), preferred_element_type=F32).astype(o_ref.dtype)

    return pl.pallas_call(
        body, name=name, out_shape=jax.ShapeDtypeStruct((S, H * LANE), BF16), grid=(H, S // bq),
        in_specs=[pl.BlockSpec((1, bq, dk), lambda h, i: (h, i, 0)), pl.BlockSpec((1, Sk, dk), lambda h, i: (h, 0, 0)),
                  pl.BlockSpec((Sk, LANE), lambda h, i: (0, vb + h))],
        out_specs=pl.BlockSpec((bq, LANE), lambda h, i: (i, h)), compiler_params=_params("parallel", "parallel"),
    )(q, k, v)


def attn_bwd(q, k, v, do, *, v_col, scale, name, bq=256):
    H, S, dk = q.shape
    Sk = k.shape[1]
    bq, vb = _tile(S, bq), v_col // LANE

    def body(q_ref, k_ref, v_ref, do_ref, dq_ref, dk_ref, dv_ref):
        i = pl.program_id(1)
        qb, kb, vv = q_ref[0], k_ref[0], v_ref[...].astype(BF16)
        s = lax.dot_general(qb, kb, _NT, preferred_element_type=F32) * scale
        e = jnp.exp(s - jnp.max(s, axis=-1, keepdims=True))
        p = e * (1.0 / jnp.sum(e, axis=-1, keepdims=True))
        dob = do_ref[...].astype(BF16)
        dv_part = lax.dot_general(p.astype(BF16), dob, _TN, preferred_element_type=F32)
        dp = lax.dot_general(dob, vv, _NT, preferred_element_type=F32)
        ds = p * (dp - jnp.sum(p * dp, axis=-1, keepdims=True))
        dsb = (ds * scale).astype(BF16)
        dq_ref[0] = jnp.dot(dsb, kb, preferred_element_type=F32)
        dk_part = lax.dot_general(dsb, qb, _TN, preferred_element_type=F32)

        @pl.when(i == 0)
        def _():
            dk_ref[0] = dk_part
            dv_ref[...] = dv_part

        @pl.when(i > 0)
        def _():
            dk_ref[0] += dk_part
            dv_ref[...] += dv_part

    return pl.pallas_call(
        body, name=name,
        out_shape=(jax.ShapeDtypeStruct((H, S, dk), F32), jax.ShapeDtypeStruct((H, Sk, dk), F32),
                   jax.ShapeDtypeStruct((Sk, H * LANE), F32)),
        grid=(H, S // bq),
        in_specs=[pl.BlockSpec((1, bq, dk), lambda h, i: (h, i, 0)), pl.BlockSpec((1, Sk, dk), lambda h, i: (h, 0, 0)),
                  pl.BlockSpec((Sk, LANE), lambda h, i: (0, vb + h)), pl.BlockSpec((bq, LANE), lambda h, i: (i, h))],
        out_specs=(pl.BlockSpec((1, bq, dk), lambda h, i: (h, i, 0)), pl.BlockSpec((1, Sk, dk), lambda h, i: (h, 0, 0)),
                   pl.BlockSpec((Sk, LANE), lambda h, i: (0, h))),
        compiler_params=_params("parallel", "arbitrary"),
    )(q, k, v, do)


def _causal_scores(q, kblk, i, start, blk, scale, chunked, cq, ckblk):
    s = lax.dot_general(q, kblk, _NT, preferred_element_type=F32) * scale
    if cq is not None:
        s = s + cq - ckblk
    qpos = i * blk + lax.broadcasted_iota(jnp.int32, s.shape, 0)
    kpos = start + lax.broadcasted_iota(jnp.int32, s.shape, 1)
    ok = (kpos >> 6) <= (qpos >> 6) if chunked else kpos <= qpos
    return jnp.where(ok, s, NEG)


def causal_attn_fwd(q, k, v, *, v_col, chunked, scale, cq=None, ck=None, name, blk=256):
    H, S, dk = q.shape
    blk, vb = _tile(S, blk), v_col // LANE
    fox = cq is not None

    def body(q_ref, k_ref, v_ref, *rest):
        o_ref, lse_ref = rest[-2:]
        s = _causal_scores(q_ref[0], k_ref[0], pl.program_id(1), 0, blk, scale, chunked,
                           rest[0][0] if fox else None, rest[1][0] if fox else None)
        m = jnp.max(s, axis=-1, keepdims=True)
        p = jnp.exp(s - m)
        l = jnp.sum(p, axis=-1, keepdims=True)
        pv = jnp.dot(p.astype(BF16), v_ref[...].astype(BF16), preferred_element_type=F32)
        o_ref[...] = (pv * (1.0 / l)).astype(o_ref.dtype)
        lse_ref[0] = m + jnp.log(l)

    in_specs = [pl.BlockSpec((1, blk, dk), lambda h, i: (h, i, 0)), pl.BlockSpec((1, S, dk), lambda h, i: (h, 0, 0)),
                pl.BlockSpec((S, LANE), lambda h, i: (0, vb + h))]
    args = [q, k, v]
    if fox:
        in_specs += [pl.BlockSpec((1, blk, 1), lambda h, i: (h, i, 0)), pl.BlockSpec((1, 1, S), lambda h, i: (h, 0, 0))]
        args += [cq, ck]
    return pl.pallas_call(
        body, name=name, out_shape=(jax.ShapeDtypeStruct((S, H * LANE), BF16), jax.ShapeDtypeStruct((H, S, 1), F32)),
        grid=(H, S // blk), in_specs=in_specs,
        out_specs=(pl.BlockSpec((blk, LANE), lambda h, i: (i, h)), pl.BlockSpec((1, blk, 1), lambda h, i: (h, i, 0))),
        compiler_params=_params("parallel", "parallel"),
    )(*args)


def causal_attn_bwd(q, k, v, o, do, lse, *, v_col, chunked, scale, cq=None, ck=None, name, blk=256):
    H, S, dk = q.shape
    blk, vb = _tile(S, blk), v_col // LANE
    fox = cq is not None

    def body(q_ref, k_ref, v_ref, o_ref, do_ref, lse_ref, *rest):
        i = pl.program_id(1)
        if fox:
            cq_ref, ck_ref, dq_ref, dk_ref, dv_ref, dcq_ref, dck_ref = rest
        else:
            dq_ref, dk_ref, dv_ref = rest

        qb, kb, dob = q_ref[0], k_ref[0], do_ref[...].astype(BF16)
        delta = jnp.sum(do_ref[...].astype(F32) * o_ref[...].astype(F32), axis=-1, keepdims=True)
        s = _causal_scores(qb, kb, i, 0, blk, scale, chunked, cq_ref[0] if fox else None, ck_ref[0] if fox else None)
        p = jnp.exp(s - lse_ref[0])
        dv_part = lax.dot_general(p.astype(BF16), dob, _TN, preferred_element_type=F32)
        dp = lax.dot_general(dob, v_ref[...].astype(BF16), _NT, preferred_element_type=F32)
        ds = p * (dp - delta)
        dsb = (ds * scale).astype(BF16)
        dq_ref[0] = jnp.dot(dsb, kb, preferred_element_type=F32)
        dk_part = lax.dot_general(dsb, qb, _TN, preferred_element_type=F32)
        if fox:
            dcq_ref[0] = jnp.sum(ds, axis=-1, keepdims=True)
            dck_part = -jnp.sum(ds, axis=0, keepdims=True)

        @pl.when(i == 0)
        def _():
            dk_ref[0] = dk_part
            dv_ref[...] = dv_part
            if fox:
                dck_ref[0] = dck_part

        @pl.when(i > 0)
        def _():
            dk_ref[0] += dk_part
            dv_ref[...] += dv_part
            if fox:
                dck_ref[0] += dck_part

    row = pl.BlockSpec((blk, LANE), lambda h, i: (i, h))
    in_specs = [pl.BlockSpec((1, blk, dk), lambda h, i: (h, i, 0)), pl.BlockSpec((1, S, dk), lambda h, i: (h, 0, 0)),
                pl.BlockSpec((S, LANE), lambda h, i: (0, vb + h)), row, row, pl.BlockSpec((1, blk, 1), lambda h, i: (h, i, 0))]
    args = [q, k, v, o, do, lse]
    out_shape = [jax.ShapeDtypeStruct((H, S, dk), F32), jax.ShapeDtypeStruct((H, S, dk), F32),
                 jax.ShapeDtypeStruct((S, H * LANE), F32)]
    out_specs = [pl.BlockSpec((1, blk, dk), lambda h, i: (h, i, 0)), pl.BlockSpec((1, S, dk), lambda h, i: (h, 0, 0)),
                 pl.BlockSpec((S, LANE), lambda h, i: (0, h))]
    if fox:
        fox_specs = [pl.BlockSpec((1, blk, 1), lambda h, i: (h, i, 0)), pl.BlockSpec((1, 1, S), lambda h, i: (h, 0, 0))]
        in_specs += fox_specs
        args += [cq, ck]
        out_shape += [jax.ShapeDtypeStruct((H, S, 1), F32), jax.ShapeDtypeStruct((H, 1, S), F32)]
        out_specs += fox_specs
    return pl.pallas_call(
        body, name=name, out_shape=tuple(out_shape), grid=(H, S // blk), in_specs=in_specs, out_specs=tuple(out_specs),
        compiler_params=_params("parallel", "arbitrary"),
    )(*args)


CPB = 4
BANDW = BAND + CHUNK


def _band_probs(qc, kb, bias, start, scale):
    s = lax.dot_general(qc, kb, _NT, preferred_element_type=F32) * scale
    col = lax.broadcasted_iota(jnp.int32, s.shape, 1)
    valid = jnp.logical_and(start + col >= PAD, col < BAND)
    s = jnp.where(valid, s + bias, NEG)
    e = jnp.exp(s - jnp.max(s, axis=-1, keepdims=True))
    return e * (1.0 / jnp.sum(e, axis=-1, keepdims=True))


def band_fwd(q, kp, vp, bias, *, scale, name):
    H, S, _ = q.shape
    Sp, rows = S + PAD + CHUNK, CPB * CHUNK

    def body(q_ref, k_ref, v_ref, b_ref, o_ref):
        j = pl.program_id(1)
        for cc in range(CPB):
            start = pl.multiple_of((j * CPB + cc) * CHUNK, CHUNK)
            kb = k_ref[0, pl.ds(start, BANDW), :]
            vb = v_ref[pl.ds(start, BANDW), :].astype(BF16)
            p = _band_probs(q_ref[0, cc * CHUNK:(cc + 1) * CHUNK, :], kb, b_ref[0], start, scale)
            o_ref[cc * CHUNK:(cc + 1) * CHUNK, :] = jnp.dot(p.astype(BF16), vb, preferred_element_type=F32).astype(o_ref.dtype)

    return pl.pallas_call(
        body, name=name, out_shape=jax.ShapeDtypeStruct((S, H * LANE), BF16), grid=(H, S // rows),
        in_specs=[pl.BlockSpec((1, rows, LANE), lambda h, j: (h, j, 0)), pl.BlockSpec((1, Sp, LANE), lambda h, j: (h, 0, 0)),
                  pl.BlockSpec((Sp, LANE), lambda h, j: (0, h)), pl.BlockSpec((1, CHUNK, BANDW), lambda h, j: (h, 0, 0))],
        out_specs=pl.BlockSpec((rows, LANE), lambda h, j: (j, h)), compiler_params=_params("parallel", "parallel"),
    )(q, kp, vp, bias)


def band_bwd(q, kp, vp, bias, do, *, scale, name):
    H, S, _ = q.shape
    Sp, rows = S + PAD + CHUNK, CPB * CHUNK

    def body(q_ref, k_ref, v_ref, b_ref, do_ref, dq_ref, dk_ref, dv_ref, db_ref):
        j = pl.program_id(1)

        @pl.when(j == 0)
        def _():
            dk_ref[...] = jnp.zeros_like(dk_ref)
            dv_ref[...] = jnp.zeros_like(dv_ref)
            db_ref[...] = jnp.zeros_like(db_ref)

        for cc in range(CPB):
            start = pl.multiple_of((j * CPB + cc) * CHUNK, CHUNK)
            sl = slice(cc * CHUNK, (cc + 1) * CHUNK)
            qc = q_ref[0, sl, :]
            kb = k_ref[0, pl.ds(start, BANDW), :]
            vb = v_ref[pl.ds(start, BANDW), :].astype(BF16)
            p = _band_probs(qc, kb, b_ref[0], start, scale)
            dob = do_ref[sl, :].astype(BF16)
            dv_ref[pl.ds(start, BANDW), :] += lax.dot_general(p.astype(BF16), dob, _TN, preferred_element_type=F32)
            dp = lax.dot_general(dob, vb, _NT, preferred_element_type=F32)
            ds = p * (dp - jnp.sum(p * dp, axis=-1, keepdims=True))
            db_ref[0] += ds
            dsb = (ds * scale).astype(BF16)
            dq_ref[0, sl, :] = jnp.dot(dsb, kb, preferred_element_type=F32)
            dk_ref[0, pl.ds(start, BANDW), :] += lax.dot_general(dsb, qc, _TN, preferred_element_type=F32)

    return pl.pallas_call(
        body, name=name,
        out_shape=(jax.ShapeDtypeStruct((H, S, LANE), F32), jax.ShapeDtypeStruct((H, Sp, LANE), F32),
                   jax.ShapeDtypeStruct((Sp, H * LANE), F32), jax.ShapeDtypeStruct((H, CHUNK, BANDW), F32)),
        grid=(H, S // rows),
        in_specs=[pl.BlockSpec((1, rows, LANE), lambda h, j: (h, j, 0)), pl.BlockSpec((1, Sp, LANE), lambda h, j: (h, 0, 0)),
                  pl.BlockSpec((Sp, LANE), lambda h, j: (0, h)), pl.BlockSpec((1, CHUNK, BANDW), lambda h, j: (h, 0, 0)),
                  pl.BlockSpec((rows, LANE), lambda h, j: (j, h))],
        out_specs=(pl.BlockSpec((1, rows, LANE), lambda h, j: (h, j, 0)), pl.BlockSpec((1, Sp, LANE), lambda h, j: (h, 0, 0)),
                   pl.BlockSpec((Sp, LANE), lambda h, j: (0, h)), pl.BlockSpec((1, CHUNK, BANDW), lambda h, j: (h, 0, 0))),
        compiler_params=_params("parallel", "arbitrary"),
    )(q, kp, vp, bias, do)


def band_bias(rel_bias, *, name):
    H = rel_bias.shape[0]
    last = rel_bias[:, 2 * REL_CLIP:]
    row0 = jnp.concatenate([jnp.tile(last, (1, PAD - REL_CLIP)), rel_bias[:, CHUNK + 1:][:, ::-1],
                            jnp.tile(last, (1, CHUNK))], axis=1)

    def body(r_ref, o_ref):
        o_ref[0] = pltpu.roll(jnp.broadcast_to(r_ref[0], (CHUNK, BANDW)), 0, 1, stride=1, stride_axis=0)

    return pl.pallas_call(
        body, name=name, out_shape=jax.ShapeDtypeStruct((H, CHUNK, BANDW), F32), grid=(H,),
        in_specs=[pl.BlockSpec((1, 1, BANDW), lambda h: (h, 0, 0))], out_specs=pl.BlockSpec((1, CHUNK, BANDW), lambda h: (h, 0, 0)),
        compiler_params=_params("parallel"),
    )(row0.reshape(H, 1, BANDW))


def relbias_bwd(dbias, *, name):
    H, W = dbias.shape[0], BANDW
    x = jnp.pad(dbias[:, :, :BAND][:, :, ::-1], ((0, 0), (0, 0), (0, CHUNK)))

    def body(x_ref, o_ref):
        skew = pltpu.roll(x_ref[0], 0, 1, stride=1, stride_axis=0)
        f = jnp.broadcast_to(jnp.sum(skew, axis=0, keepdims=True), (8, W))
        lane = lax.broadcasted_iota(jnp.int32, (8, W), 1)
        direct = jnp.where(jnp.logical_and(lane >= 65, lane <= 255), pltpu.roll(f, 65, 1), 0.0)
        tail = jnp.sum(jnp.where(lane >= 191, f, 0.0), axis=-1, keepdims=True)
        o_ref[0] = direct + jnp.where(lane == 2 * REL_CLIP, tail, 0.0)

    out = pl.pallas_call(
        body, name=name, out_shape=jax.ShapeDtypeStruct((H, 8, W), F32), grid=(H,),
        in_specs=[pl.BlockSpec((1, CHUNK, W), lambda h: (h, 0, 0))], out_specs=pl.BlockSpec((1, 8, W), lambda h: (h, 0, 0)),
        compiler_params=_params("parallel"),
    )(x)
    return out[:, 0, :2 * REL_CLIP + 1]


def _split_dot(x, u, dn):
    hi = x.astype(BF16)
    r1 = x - hi.astype(F32)
    mid = r1.astype(BF16)
    lo = (r1 - mid.astype(F32)).astype(BF16)
    d = lambda t: lax.dot_general(t, u, dn, preferred_element_type=F32)
    return d(hi) + d(mid) + d(lo)


def _upper_ones(S):
    return (np.arange(S)[:, None] <= np.arange(S)[None, :]).astype(np.float32)


def foxgate_fwd(fl, b, *, name):
    H, S = fl.shape
    u = jnp.asarray(_upper_ones(S), BF16)

    def body(f_ref, b_ref, u_ref, o_ref):
        x = f_ref[...] + b_ref[...]
        lf = jnp.minimum(x, 0.0) - jnp.log(1.0 + jnp.exp(-jnp.abs(x)))
        o_ref[...] = _split_dot(lf, u_ref[...], (((1,), (0,)), ((), ())))

    return pl.pallas_call(body, name=name, out_shape=jax.ShapeDtypeStruct((H, S), F32),
                          compiler_params=pltpu.CompilerParams(vmem_limit_bytes=VMEM_LIMIT_BYTES))(fl, b.reshape(H, 1), u)


def foxgate_bwd(fl, b, dcum, *, name):
    H, S = fl.shape
    u = jnp.asarray(_upper_ones(S), BF16)

    def body(f_ref, b_ref, u_ref, dc_ref, df_ref, db_ref):
        x = f_ref[...] + b_ref[...]
        dlf = _split_dot(dc_ref[...], u_ref[...], _NT)
        df = dlf * (1.0 / (1.0 + jnp.exp(x)))
        df_ref[...] = df
        db_ref[...] = jnp.sum(df, axis=-1, keepdims=True)

    df, db = pl.pallas_call(body, name=name,
                            out_shape=(jax.ShapeDtypeStruct((H, S), F32), jax.ShapeDtypeStruct((H, 1), F32)),
                            compiler_params=pltpu.CompilerParams(vmem_limit_bytes=VMEM_LIMIT_BYTES))(fl, b.reshape(H, 1), u, dcum)
    return df, db.reshape(H)


def gate_fwd(z, proj, *, name, ts=256, tc=512):
    S, D = proj[0].shape
    ts, gb, nb = _tile(S, ts), Z_GATE // tc, D // tc

    def body(g0, g1, g2, p0, p1, p2, o_ref):
        acc = None
        for g_ref, p_ref in zip((g0, g1, g2), (p0, p1, p2)):
            t = (1.0 / (1.0 + jnp.exp(-g_ref[...]))) * p_ref[...]
            acc = t if acc is None else acc + t
        o_ref[...] = acc.astype(o_ref.dtype)

    blk = pl.BlockSpec((ts, tc), lambda i, j: (i, j))
    return pl.pallas_call(
        body, name=name, out_shape=jax.ShapeDtypeStruct((S, D), BF16), grid=(S // ts, nb),
        in_specs=[pl.BlockSpec((ts, tc), lambda i, j, n=n: (i, gb + n * nb + j)) for n in range(3)] + [blk] * 3,
        out_specs=blk, compiler_params=_params("parallel", "parallel"),
    )(z, z, z, *proj)


def gate_bwd(z, proj, dm, *, name, ts=256, tc=512):
    S, D = proj[0].shape
    ts, gb, nb = _tile(S, ts), Z_GATE // tc, D // tc

    def body(g0, g1, g2, p0, p1, p2, dm_ref, *outs):
        dmv = dm_ref[...]
        for n, (g_ref, p_ref) in enumerate(zip((g0, g1, g2), (p0, p1, p2))):
            sg = 1.0 / (1.0 + jnp.exp(-g_ref[...]))
            outs[n][...] = (dmv * sg).astype(BF16)
            outs[3 + n][...] = (dmv * p_ref[...] * sg * (1.0 - sg)).astype(BF16)

    blk = pl.BlockSpec((ts, tc), lambda i, j: (i, j))
    outs = pl.pallas_call(
        body, name=name, out_shape=tuple(jax.ShapeDtypeStruct((S, D), BF16) for _ in range(6)), grid=(S // ts, nb),
        in_specs=[pl.BlockSpec((ts, tc), lambda i, j, n=n: (i, gb + n * nb + j)) for n in range(3)] + [blk] * 4,
        out_specs=(blk,) * 6, compiler_params=_params("parallel", "parallel"),
    )(z, z, z, *proj, dm)
    return outs[:3], outs[3:]


def loss_head(y, target, *, name, ts=256):
    S, D = y.shape
    ts = _tile(S, ts)

    def body(y_ref, t_ref, l_ref, dy_ref):
        err = y_ref[...] - t_ref[...]
        dy_ref[...] = err * (1.0 / D)
        part = 0.5 * jnp.sum(jnp.mean(err * err, axis=-1, keepdims=True), axis=0, keepdims=True)

        @pl.when(pl.program_id(0) == 0)
        def _():
            l_ref[...] = part

        @pl.when(pl.program_id(0) > 0)
        def _():
            l_ref[...] += part

    blk = pl.BlockSpec((ts, D), lambda i: (i, 0))
    return pl.pallas_call(
        body, name=name, out_shape=(jax.ShapeDtypeStruct((1, 1), F32), jax.ShapeDtypeStruct((S, D), F32)), grid=(S // ts,),
        in_specs=[blk, blk], out_specs=(pl.BlockSpec((1, 1), lambda i: (0, 0)), blk), compiler_params=_params("arbitrary"),
    )(y, target)


def adamw(w, g, m, v, *, name):
    shape = w.shape
    C = shape[-1]
    R = int(np.prod(shape[:-1]))
    br = R
    while br % 16 == 0 and br * C * 4 > 2**20:
        br //= 2
    w2, g2, m2, v2 = (t.reshape(R, C) for t in (w, g, m, v))

    def body(w_ref, g_ref, m_ref, v_ref, d_ref, nm_ref, nv_ref):
        d_ref[...], nm_ref[...], nv_ref[...] = _adamw_update(w_ref[...], g_ref[...], m_ref[...], v_ref[...])

    blk = pl.BlockSpec((br, C), lambda i: (i, 0))
    outs = pl.pallas_call(
        body, name=name, out_shape=tuple(jax.ShapeDtypeStruct((R, C), F32) for _ in range(3)), grid=(R // br,),
        in_specs=[blk] * 4, out_specs=(blk,) * 3, compiler_params=_params("parallel"),
    )(w2, g2, m2, v2)
    return tuple(o.reshape(shape) for o in outs)


_ANY = pl.BlockSpec(memory_space=pl.ANY)


def _place():
    return lax.axis_index("x"), lax.axis_index("y"), lax.axis_index("c")


def all_gather(xs, after, *, name):
    n = len(xs)

    def body(*refs):
        x_refs, o_refs = refs[:n], refs[n + 1:2 * n + 1]
        send_sems, recv_sems, local_sems = refs[2 * n + 1:]
        px, py, pc = _place()
        me, sibling = (px, py, pc), (px, py, 1 - pc)
        chips = [(1 - px, py), (px, 1 - py), (1 - px, 1 - py)]

        def slot(t, dev):
            return o_refs[t].at[4 * dev[0] + 2 * dev[1] + dev[2]]

        def copy(t, k, block, to, src=None):
            return pltpu.make_async_remote_copy(
                src_ref=slot(t, block) if src is None else src, dst_ref=slot(t, block),
                send_sem=send_sems.at[t, k], recv_sem=recv_sems.at[t, k], device_id=to, device_id_type=MESH)

        mine = [pltpu.make_async_copy(x_refs[t], slot(t, me), local_sems.at[t]) for t in range(n)]
        first = []
        for t in range(n):
            mine[t].start()
            first += [copy(t, 1 + j, me, (*chip, pc), src=x_refs[t]) for j, chip in enumerate(chips)]
            first.append(copy(t, 0, me, sibling, src=x_refs[t]))
        for cp in first:
            cp.start()
        passed = []
        for t in range(n):
            for j, chip in enumerate(chips):
                copy(t, 1 + j, (*chip, pc), me).wait_recv()
                fwd = copy(t, 4 + j, (*chip, pc), sibling)
                fwd.start()
                passed.append(fwd)
        for t in range(n):
            copy(t, 0, sibling, me).wait_recv()
            for j, chip in enumerate(chips):
                copy(t, 4 + j, (*chip, 1 - pc), me).wait_recv()
        for cp in first + passed:
            cp.wait_send()
        for cp in mine:
            cp.wait()

    return pl.pallas_call(
        body, name=name, out_shape=tuple(jax.ShapeDtypeStruct((8,) + x.shape, x.dtype) for x in xs),
        in_specs=[_ANY] * (n + 1), out_specs=(_ANY,) * n,
        scratch_shapes=[pltpu.SemaphoreType.DMA((n, 7)), pltpu.SemaphoreType.DMA((n, 7)), pltpu.SemaphoreType.DMA((n,))],
    )(*xs, after)


def forward_to_sibling(zones, *, name):
    n = len(zones)

    def body(*refs):
        z_refs, (send_sems, recv_sems) = refs[n:2 * n], refs[2 * n:]
        px, py, pc = _place()
        copies = []
        for t in range(n):
            for j, chip in enumerate([(1 - px, py), (px, 1 - py), (1 - px, 1 - py)]):
                slot = z_refs[t].at[4 * chip[0] + 2 * chip[1] + pc]
                copies.append(pltpu.make_async_remote_copy(
                    src_ref=slot, dst_ref=slot, send_sem=send_sems.at[t, j], recv_sem=recv_sems.at[t, j],
                    device_id=(px, py, 1 - pc), device_id_type=MESH))
        for cp in copies:
            cp.start()
        for cp in copies:
            cp.wait()

    return pl.pallas_call(
        body, name=name, out_shape=tuple(jax.ShapeDtypeStruct(z.shape, z.dtype) for z in zones),
        in_specs=[_ANY] * n, out_specs=(_ANY,) * n, input_output_aliases={t: t for t in range(n)},
        scratch_shapes=[pltpu.SemaphoreType.DMA((n, 3)), pltpu.SemaphoreType.DMA((n, 3))],
    )(*zones)


_HBM = pl.BlockSpec(memory_space=pltpu.HBM)
_SEM = pl.BlockSpec(memory_space=pltpu.SEMAPHORE)
_EFFECT = pltpu.SideEffectType.DATAFLOW_SIDE_EFFECTING


N_PEERS = 7
GATHER_FLIPS = (1, 2, 4, 6)


def _peers(flips=range(1, N_PEERS + 1)):
    px, py, pc = _place()
    flip = lambda p, bit: 1 - p if bit else p
    return [(flip(px, m >> 2 & 1), flip(py, m >> 1 & 1), flip(pc, m & 1)) for m in flips]


def _gather_copies(src_refs, zone_refs, send_sems, recv_sems):
    px, py, pc = _place()
    return [pltpu.make_async_remote_copy(src_ref=s, dst_ref=z.at[4 * px + 2 * py + pc], send_sem=send_sems.at[k],
                                         recv_sem=recv_sems.at[k], device_id=peer, device_id_type=MESH)
            for k, peer in enumerate(_peers(GATHER_FLIPS)) for s, z in zip(src_refs, zone_refs)]


def _scatter_copies(part_refs, zone_refs, send_sems, recv_sems):
    return [pltpu.make_async_remote_copy(src_ref=p.at[4 * peer[0] + 2 * peer[1] + peer[2]], dst_ref=z.at[k],
                                         send_sem=send_sems.at[k], recv_sem=recv_sems.at[k], device_id=peer, device_id_type=MESH)
            for k, peer in enumerate(_peers()) for p, z in zip(part_refs, zone_refs)]


def copies_start(make, srcs, zones, *, name):
    n = len(srcs)

    def body(*refs):
        for cp in make(refs[:n], refs[n:2 * n], refs[2 * n], refs[2 * n + 1]):
            cp.start()
        refs[-1][...] = jnp.zeros_like(refs[-1])

    arrays = list(srcs) + list(zones)
    outs = pl.pallas_call(
        body, name=name,
        out_shape=(pltpu.SemaphoreType.DMA((N_PEERS,)), pltpu.SemaphoreType.DMA((N_PEERS,)),
                   *[pltpu.HBM(a.shape, a.dtype) for a in arrays], jax.ShapeDtypeStruct((8, LANE), F32)),
        in_specs=[_HBM] * (2 * n), out_specs=(_SEM, _SEM, *[_HBM] * (2 * n), pl.BlockSpec(memory_space=pltpu.VMEM)),
        input_output_aliases={i: 2 + i for i in range(2 * n)},
        compiler_params=pltpu.CompilerParams(has_side_effects=_EFFECT),
    )(*[pltpu.with_memory_space_constraint(a, pltpu.HBM) for a in arrays])
    return outs[0], outs[1], list(outs[2:2 + n]), list(outs[2 + n:2 + 2 * n]), outs[-1]


def copies_wait(make, send_sems, recv_sems, srcs, zones, after, *, name):
    n = len(srcs)

    def body(*refs):
        for cp in make(refs[:n], refs[n:2 * n], refs[2 * n], refs[2 * n + 1]):
            cp.wait_send()
            cp.wait_recv()

    arrays = list(srcs) + list(zones)
    outs = pl.pallas_call(
        body, name=name, out_shape=tuple(pltpu.HBM(a.shape, a.dtype) for a in arrays),
        in_specs=[_HBM] * (2 * n) + [_SEM, _SEM, _ANY], out_specs=(_HBM,) * (2 * n),
        input_output_aliases={i: i for i in range(2 * n)},
        compiler_params=pltpu.CompilerParams(has_side_effects=_EFFECT),
    )(*arrays, send_sems, recv_sems, after)
    return list(outs[:n]), list(outs[n:])


def place_mine(zone, mine, *, name):
    C = mine.shape[-1]
    R = int(np.prod(mine.shape[:-1]))
    br = _row_block(R, C, mine.dtype.itemsize, budget=2**21)
    me = (4 * lax.axis_index("x") + 2 * lax.axis_index("y") + lax.axis_index("c")).astype(jnp.int32).reshape(1)

    def body(me_ref, m_ref, z_ref, o_ref):
        o_ref[0] = m_ref[...]

    out = pl.pallas_call(
        body, name=name, out_shape=jax.ShapeDtypeStruct((8, R, C), zone.dtype),
        grid_spec=pltpu.PrefetchScalarGridSpec(
            num_scalar_prefetch=1, grid=(R // br,), in_specs=[pl.BlockSpec((br, C), lambda i, me: (i, 0)), _ANY],
            out_specs=pl.BlockSpec((1, br, C), lambda i, me: (me[0], i, 0))),
        input_output_aliases={2: 0}, compiler_params=_params("parallel"),
    )(me, mine.reshape(R, C), zone.reshape(8, R, C))
    return out.reshape(zone.shape)


def _row_block(rows, cols, itemsize, budget=2**20):
    br = rows
    while br % 32 == 0 and br * cols * itemsize > budget:
        br //= 2
    return br


def _adamw_update(w, g, m, v):
    nm = ADAM_B1 * m + (1.0 - ADAM_B1) * g
    nv = ADAM_B2 * v + (1.0 - ADAM_B2) * (g * g)
    m_hat = nm / (1.0 - ADAM_B1 ** ADAM_STEP)
    v_hat = nv / (1.0 - ADAM_B2 ** ADAM_STEP)
    return -ADAM_LR * (m_hat / (jnp.sqrt(v_hat) + ADAM_EPS) + ADAM_WD * w), nm, nv


def grad_sum_adamw(parts, recvs, w, m, v, *, name):
    L, C = len(parts), w.shape[-1]
    R = int(np.prod(w.shape[1:-1]))
    br = _row_block(R, C, 4, budget=2**19)
    chip = (4 * lax.axis_index("x") + 2 * lax.axis_index("y") + lax.axis_index("c")).astype(jnp.int32).reshape(1)

    def body(c_ref, *refs):
        p_refs, r_refs = refs[:L], refs[L:2 * L]
        w_ref, m_ref, v_ref, g_out, d_out, nm_out, nv_out = refs[2 * L:]
        for j in range(L):
            @pl.when(pl.program_id(0) == j)
            def _(j=j):
                g = p_refs[j][0].astype(F32)
                for k in range(N_PEERS):
                    g = g + r_refs[j][k].astype(F32)
                g_out[0] = g
                d_out[0], nm_out[0], nv_out[0] = _adamw_update(w_ref[0], g, m_ref[0], v_ref[0])

    row = lambda j: (lambda l, r, c: jnp.where(l == j, r, 0))
    part_specs = [pl.BlockSpec((1, br, C), lambda l, r, c, f=row(j): (c[0], f(l, r, c), 0)) for j in range(L)]
    recv_specs = [pl.BlockSpec((N_PEERS, br, C), lambda l, r, c, f=row(j): (0, f(l, r, c), 0)) for j in range(L)]
    blk = pl.BlockSpec((1, br, C), lambda l, r, c: (l, r, 0))
    outs = pl.pallas_call(
        body, name=name, out_shape=tuple(jax.ShapeDtypeStruct((L, R, C), F32) for _ in range(4)),
        grid_spec=pltpu.PrefetchScalarGridSpec(
            num_scalar_prefetch=1, grid=(L, R // br), in_specs=part_specs + recv_specs + [blk] * 3, out_specs=(blk,) * 4),
        compiler_params=_params("arbitrary", "arbitrary"),
    )(chip, *[p.reshape(8, R, C) for p in parts], *[r.reshape(N_PEERS, R, C) for r in recvs],
      *[t.reshape(L, R, C) for t in (w, m, v)])
    return tuple(o.reshape(w.shape) for o in outs)


def ordered_sum(parts, *, name):
    _, R, C = parts.shape

    def body(p_ref, o_ref):
        acc = p_ref[0]
        for d in range(1, 8):
            acc = acc + p_ref[d]
        o_ref[...] = acc

    return pl.pallas_call(body, name=name, out_shape=jax.ShapeDtypeStruct((R, C), F32))(parts)


W_IN_COLS, W_IN_SHARD = 13128, 1641
W_IN_SEGMENTS = ((0, 832, 0), (832, 3904, Z_FOX), (3904, 3912, FF_COL), (3912, 6984, Z_CH), (6984, 13128, Z_GATE))


def col_gather(src, table, pieces, out_shape, *, name, tr=1024):
    R, C = src.shape[1:]
    tr = _tile(R, tr)
    width = 2 + 6 * pieces
    nb = table.shape[0] // width
    last_tile, last_valid = C // LANE, C % LANE

    def body(tab, *refs):
        o_ref = refs[-1]
        base = pl.program_id(1) * width
        lane = lax.broadcasted_iota(jnp.int32, (tr, LANE), 1)
        row = lax.broadcasted_iota(jnp.int32, (2 * LANE * pieces, LANE), 0)
        col = lax.broadcasted_iota(jnp.int32, (2 * LANE * pieces, LANE), 1)
        tiles, hit = [], None
        for p in range(pieces):
            e = base + 2 + 6 * p
            for tcol in (1, 2):
                x = refs[2 * p + tcol - 1][0]
                if last_valid:
                    x = jnp.where(jnp.logical_or(tab[e + tcol] < last_tile, lane < last_valid), x, jnp.zeros_like(x))
                tiles.append(x)
            lo, hi = tab[e + 4], tab[e + 5]
            cond = jnp.logical_and(row - 2 * LANE * p == col + tab[e + 3], jnp.logical_and(col >= lo, col < hi))
            hit = cond if hit is None else jnp.logical_or(hit, cond)
        sel = jnp.where(hit, 1.0, 0.0).astype(src.dtype)
        o_ref[0] = jnp.dot(jnp.concatenate(tiles, axis=1), sel, preferred_element_type=F32).astype(o_ref.dtype)

    in_specs = []
    for p in range(pieces):
        for tcol in (1, 2):
            in_specs.append(pl.BlockSpec(
                (1, tr, LANE), lambda i, b, tab, p=p, tcol=tcol: (tab[b * width + 2 + 6 * p], i, tab[b * width + 2 + 6 * p + tcol])))
    return pl.pallas_call(
        body, name=name, out_shape=jax.ShapeDtypeStruct(out_shape, src.dtype),
        grid_spec=pltpu.PrefetchScalarGridSpec(
            num_scalar_prefetch=1, grid=(R // tr, nb), in_specs=in_specs,
            out_specs=pl.BlockSpec((1, tr, LANE), lambda i, b, tab: (tab[b * width], i, tab[b * width + 1]))),
        compiler_params=_params("parallel", "parallel"),
    )(jnp.asarray(table, jnp.int32), *([src] * (2 * pieces)))


def _piece(sd, start, lo, hi, last_tile):
    t0 = start // LANE
    return [sd, t0, min(t0 + 1, last_tile), start % LANE - lo, lo, hi]


def _pad_pieces(rows, pieces):
    out, prev = [], [0, 0, 0, 0, 0, 0] * pieces
    for head, pcs in rows:
        full = list(pcs)
        for p in range(len(pcs) // 6, pieces):
            full += prev[6 * p:6 * p + 3] + [0, 0, 0]
        out.append(head + full)
        prev = full
    return np.asarray(out, np.int32).reshape(-1)


def _w_in_table(layer, L):
    rows = []
    for b in range(Z_W // LANE):
        pcs = []
        for first, last, col in W_IN_SEGMENTS:
            lo, hi = max(LANE * b, col), min(LANE * (b + 1), col + last - first)
            while lo < hi:
                c = first + lo - col
                n = min(hi - lo, W_IN_SHARD - c % W_IN_SHARD)
                pcs += _piece((c // W_IN_SHARD) * L + layer, c % W_IN_SHARD, lo - LANE * b, lo - LANE * b + n, W_IN_SHARD // LANE)
                lo += n
        assert len(pcs) <= 12
        rows.append(([0, b], pcs))
    return _pad_pieces(rows, 2)


def _w_in_grad_table():
    rows = []
    for d in range(8):
        for t in range(-(-W_IN_SHARD // LANE)):
            pcs = []
            c0 = d * W_IN_SHARD + LANE * t
            c1 = min(c0 + LANE, (d + 1) * W_IN_SHARD)
            for first, last, col in W_IN_SEGMENTS:
                lo, hi = max(c0, first), min(c1, last)
                if lo < hi:
                    pcs += _piece(0, col + lo - first, lo - c0, hi - c0, Z_W // LANE - 1)
            assert len(pcs) <= 18
            rows.append(([d, t], pcs))
    return _pad_pieces(rows, 3)


def block_copy(src, out_shape, in_blk, out_blk, grid, in_map, out_map, *, name):
    def body(x_ref, o_ref):
        o_ref[(0,) * (len(out_blk) - 2) + (Ellipsis,)] = x_ref[(0,) * (len(in_blk) - 2) + (Ellipsis,)]

    return pl.pallas_call(
        body, name=name, out_shape=jax.ShapeDtypeStruct(out_shape, src.dtype), grid=grid,
        in_specs=[pl.BlockSpec(in_blk, in_map)], out_specs=pl.BlockSpec(out_blk, out_map),
        compiler_params=_params("parallel", "parallel"),
    )(src)


def _columns_from_owners(z, *, name, lead=()):
    K, c = z.shape[-2:]
    tr, nl = _tile(K, 1024), len(lead)
    return block_copy(z, (K, 8 * c), (1,) * (1 + nl) + (tr, c), (tr, c), (8, K // tr),
                      lambda d, i: (d, *lead, i, 0), lambda d, i: (i, d), name=name)


def _owners_from_columns(g, *, name):
    K, c = g.shape[0], g.shape[1] // 8
    tr = _tile(K, 1024)
    return block_copy(g, (8, K, c), (tr, c), (1, tr, c), (8, K // tr), lambda d, i: (i, d), lambda d, i: (d, i, 0), name=name)


def _full_from_shards(k, sh, tag):
    if k not in COL_SHARDED:
        return sh.reshape((-1, sh.shape[-1]))
    if k == 'w_br':
        return [_columns_from_owners(sh, lead=(n,), name=f"{tag}_w_br{n}_layout") for n in range(3)]
    if k == 'w_uq':
        return _columns_from_owners(jnp.pad(sh, ((0, 0), (0, 0), (0, 64))), name=f"{tag}_w_uq_layout")
    if k == 'w_ukv':
        return block_copy(sh, (256, 2048), (1, 256, LANE), (256, LANE), (2, MLA_HEADS),
                          lambda t, h: (h, 0, t), lambda t, h: (0, t * MLA_HEADS + h), name=f"{tag}_w_ukv_layout")
    return _columns_from_owners(sh, name=f"{tag}_{k}_layout")


def _shards_from_full(k, g, tag):
    if k not in COL_SHARDED:
        return g.reshape((8, g.shape[0] // 8, g.shape[1]))
    if k == 'w_br':
        return jnp.stack([_owners_from_columns(g[n], name=f"{tag}_dw_br{n}_layout") for n in range(3)], axis=1)
    if k == 'w_uq':
        return _owners_from_columns(g, name=f"{tag}_dw_uq_layout")[:, :, :MLA_QK]
    if k == 'w_ukv':
        return block_copy(g, (8, 256, 256), (256, LANE), (1, 256, LANE), (2, MLA_HEADS),
                          lambda t, h: (0, t * MLA_HEADS + h), lambda t, h: (h, 0, t), name=f"{tag}_dw_ukv_layout")
    return _owners_from_columns(g, name=f"{tag}_d{k}_layout")


def w_in_full(gathered, layer, *, name):
    _, L, K, c = gathered.shape
    return col_gather(gathered.reshape(8 * L, K, c), _w_in_table(layer, L), 2, (1, K, Z_W), name=name)[0]


def w_in_shards(g, *, name):
    return col_gather(g[None], _w_in_grad_table(), 3, (8, g.shape[0], W_IN_SHARD), name=name)


def _layer_fwd(x, mem, W, P, cos, ssin, tag):
    S = x.shape[0]
    sv = {'x0': x}
    h = rmsnorm_fwd(x, P['g_mix'], name=f"{tag}_norm_mix")
    z = mm(h, W['w_in'], name=f"{tag}_mm_in")
    sv.update(h=h, z=z)
    cqn = rmsnorm_fwd(z, P['g_cq'], col=0, width=512, name=f"{tag}_norm_cq")
    ckvn = rmsnorm_fwd(z, P['g_ckv'], col=512, width=256, name=f"{tag}_norm_ckv")
    qf = mm(cqn, W['w_uq'], name=f"{tag}_mm_uq")
    kvf = mm(ckvn, W['w_ukv'], name=f"{tag}_mm_ukv")
    qa = mla_prep_fwd(qf, qf, P['g_mla_q'], cos, ssin, n_col=0, n_stride=2 * LANE, r_col=LANE, r_stride=2 * LANE,
                      heads=8, name=f"{tag}_mla_q")
    ka = mla_prep_fwd(kvf, z, P['g_mla_k'], cos, ssin, n_col=0, n_stride=LANE, r_col=KR_COL, r_stride=0,
                      heads=8, name=f"{tag}_mla_k")
    ya, lse_a = causal_attn_fwd(qa, ka, kvf, v_col=1024, chunked=True, scale=MLA_QK ** -0.5, name=f"{tag}_mla_attn")
    sv.update(cqn=cqn, ckvn=ckvn, qf=qf, kvf=kvf, qa=qa, ka=ka, lse_a=lse_a)
    qb = headnorm_fwd(z, P['g_fox_q'], col=Z_FOX, heads=8, name=f"{tag}_fox_qn")
    kb = headnorm_fwd(z, P['g_fox_k'], col=Z_FOX + 1024, heads=8, name=f"{tag}_fox_kn")
    fl = z[:, FF_COL:FF_COL + 8].T
    cum = foxgate_fwd(fl, P['b_f'], name=f"{tag}_fox_gate")
    cq, ck = cum.reshape(8, S, 1), cum.reshape(8, 1, S)
    yb, lse_b = causal_attn_fwd(qb, kb, z, v_col=Z_FOX + 2048, chunked=False, scale=LANE ** -0.5, cq=cq, ck=ck,
                                name=f"{tag}_fox_attn")
    sv.update(qb=qb, kb=kb, fl=fl, cq=cq, ck=ck, lse_b=lse_b)
    qc = headnorm_fwd(z, P['g_ch_q'], col=Z_CH, heads=8, name=f"{tag}_ch_qn")
    kc = headnorm_fwd(z, P['g_ch_k'], col=Z_CH + 1024, heads=8, name=f"{tag}_ch_kn")
    kcp = jnp.pad(kc, ((0, 0), (PAD, CHUNK), (0, 0)))
    vcp = jnp.pad(z[:, Z_CH + 2048:Z_CH + 3072], ((PAD, CHUNK), (0, 0)))
    bias = band_bias(P['rel_bias'], name=f"{tag}_ch_bias")
    yc = band_fwd(qc, kcp, vcp, bias, scale=LANE ** -0.5, name=f"{tag}_ch_attn")
    sv.update(qc=qc, kcp=kcp, vcp=vcp, bias=bias)
    ys = (ya, yb, yc)
    proj = [mm(ys[n], W['w_br'][n], name=f"{tag}_mm_br{n}") for n in range(3)]
    merged = gate_fwd(z, proj, name=f"{tag}_gate")
    x1 = mm(merged, W['w_out'], epi='add', aux=x, name=f"{tag}_mm_out")
    sv.update(ys=ys, proj=proj, merged=merged, x1=x1)
    hc = rmsnorm_fwd(x1, P['g_cross'], name=f"{tag}_norm_cross")
    memn = rmsnorm_fwd(mem, P['g_mem'], name=f"{tag}_norm_mem")
    qx_raw = mm(hc, W['w_xq'], name=f"{tag}_mm_xq")
    memkv = mm(memn, W['w_xkv'], name=f"{tag}_mm_xkv")
    qx = headnorm_fwd(qx_raw, P['g_x_q'], col=0, heads=4, name=f"{tag}_x_qn")
    kx = headnorm_fwd(memkv, P['g_x_k'], col=0, heads=4, name=f"{tag}_x_kn")
    ox = attn_fwd(qx, kx, memkv, v_col=512, scale=LANE ** -0.5, name=f"{tag}_x_attn")
    x2 = mm(ox, W['w_xo'], epi='add', aux=x1, name=f"{tag}_mm_xo")
    sv.update(hc=hc, memn=memn, qx_raw=qx_raw, memkv=memkv, qx=qx, kx=kx, ox=ox, x2=x2)
    hm = rmsnorm_fwd(x2, P['g_mlp'], name=f"{tag}_norm_mlp")
    u, a = mm(hm, W['w_1'], epi='relu2', out_dtype=BF16, name=f"{tag}_mm_w1")
    x3 = mm(a, W['w_2'], epi='add', aux=x2, name=f"{tag}_mm_w2")
    sv.update(hm=hm, u=u, a=a)
    return x3, sv


def _layer_bwd(dx, mem, W, P, sv, cos, ssin, tag, send_off=None):
    S = dx.shape[0]
    z = sv['z']
    gw, gs = {}, {}
    wgrad = lambda a, d, name: mm(a, d, ta=True, out_dtype=BF16, name=name)
    gw['w_2'] = wgrad(sv['a'], dx, f"{tag}_dw2")
    du = mm(dx, W['w_2'], tb=True, epi='mul_drelu2', aux=sv['u'], out_dtype=BF16, name=f"{tag}_du")
    gw['w_1'] = wgrad(sv['hm'], du, f"{tag}_dw1")
    dhm = mm(du, W['w_1'], tb=True, name=f"{tag}_dhm")
    dx, gs['g_mlp'] = rmsnorm_bwd(sv['x2'], P['g_mlp'], dhm, res=dx, name=f"{tag}_dnorm_mlp")
    gw['w_xo'] = wgrad(sv['ox'], dx, f"{tag}_dwxo")
    dox = mm(dx, W['w_xo'], tb=True, out_dtype=BF16, name=f"{tag}_dox")
    dqx, dkx, dvx = attn_bwd(sv['qx'], sv['kx'], sv['memkv'], dox, v_col=512, scale=LANE ** -0.5, name=f"{tag}_x_attn_bwd")
    dqx_raw, gs['g_x_q'] = headnorm_bwd(sv['qx_raw'], P['g_x_q'], dqx, col=0, heads=4, name=f"{tag}_x_qn_bwd")
    dkx_raw, gs['g_x_k'] = headnorm_bwd(sv['memkv'], P['g_x_k'], dkx, col=0, heads=4, name=f"{tag}_x_kn_bwd")
    dqx_b = dqx_raw.astype(BF16)
    gw['w_xq'] = wgrad(sv['hc'], dqx_b, f"{tag}_dwxq")
    dhc = mm(dqx_b, W['w_xq'], tb=True, name=f"{tag}_dhc")
    dx, gs['g_cross'] = rmsnorm_bwd(sv['x1'], P['g_cross'], dhc, res=dx, name=f"{tag}_dnorm_cross")
    dmemkv = jnp.concatenate([dkx_raw, dvx], axis=1).astype(BF16)
    gw['w_xkv'] = wgrad(sv['memn'], dmemkv, f"{tag}_dwxkv")
    dmemn = mm(dmemkv, W['w_xkv'], tb=True, name=f"{tag}_dmemn")
    _, gs['g_mem'] = rmsnorm_bwd(mem, P['g_mem'], dmemn, name=f"{tag}_dnorm_mem")
    gw['w_out'] = wgrad(sv['merged'], dx, f"{tag}_dwout")
    dmerged = mm(dx, W['w_out'], tb=True, name=f"{tag}_dmerged")
    dproj, dgl = gate_bwd(z, sv['proj'], dmerged, name=f"{tag}_gate_bwd")
    gw['w_br'] = [wgrad(sv['ys'][n], dproj[n], f"{tag}_dwbr{n}") for n in range(3)]
    dys = [mm(dproj[n], W['w_br'][n], tb=True, out_dtype=BF16, name=f"{tag}_dys{n}") for n in range(3)]
    dqa, dka, dva = causal_attn_bwd(sv['qa'], sv['ka'], sv['kvf'], sv['ys'][0], dys[0], sv['lse_a'], v_col=1024, chunked=True,
                                    scale=MLA_QK ** -0.5, name=f"{tag}_mla_attn_bwd")
    g_mla_q = P['g_mla_q'] if send_off is None else P['g_mla_q'] + send_off[0](gw)
    dqn, dqr, gs['g_mla_q'] = mla_prep_bwd(sv['qf'], sv['qf'], g_mla_q, cos, ssin, dqa, n_col=0, n_stride=2 * LANE,
                                           r_col=LANE, r_stride=2 * LANE, heads=8, name=f"{tag}_mla_q_bwd")
    dkn, dkr, gs['g_mla_k'] = mla_prep_bwd(sv['kvf'], z, P['g_mla_k'], cos, ssin, dka, n_col=0, n_stride=LANE,
                                           r_col=KR_COL, r_stride=0, heads=8, name=f"{tag}_mla_k_bwd")
    dqf = jnp.stack([dqn.reshape(S, 8, LANE), dqr.reshape(S, 8, LANE)], axis=2).reshape(S, 2048).astype(BF16)
    dkvf = jnp.concatenate([dkn, dva], axis=1).astype(BF16)
    gw['w_uq'] = wgrad(sv['cqn'], dqf, f"{tag}_dwuq")
    gw['w_ukv'] = wgrad(sv['ckvn'], dkvf, f"{tag}_dwukv")
    dcqn = mm(dqf, W['w_uq'], tb=True, name=f"{tag}_dcqn")
    dckvn = mm(dkvf, W['w_ukv'], tb=True, name=f"{tag}_dckvn")
    dcq_raw, gs['g_cq'] = rmsnorm_bwd(z, P['g_cq'], dcqn, col=0, width=512, name=f"{tag}_dnorm_cq")
    dckv_raw, gs['g_ckv'] = rmsnorm_bwd(z, P['g_ckv'], dckvn, col=512, width=256, name=f"{tag}_dnorm_ckv")
    dqb, dkb, dvb, dcq, dck = causal_attn_bwd(sv['qb'], sv['kb'], z, sv['ys'][1], dys[1], sv['lse_b'], v_col=Z_FOX + 2048,
                                              chunked=False, scale=LANE ** -0.5, cq=sv['cq'], ck=sv['ck'],
                                              name=f"{tag}_fox_attn_bwd")
    dqb_raw, gs['g_fox_q'] = headnorm_bwd(z, P['g_fox_q'], dqb, col=Z_FOX, heads=8, name=f"{tag}_fox_qn_bwd")
    dkb_raw, gs['g_fox_k'] = headnorm_bwd(z, P['g_fox_k'], dkb, col=Z_FOX + 1024, heads=8, name=f"{tag}_fox_kn_bwd")
    dfl, gs['b_f'] = foxgate_bwd(sv['fl'], P['b_f'], dcq.reshape(8, S) + dck.reshape(8, S), name=f"{tag}_fox_gate_bwd")
    dqc, dkcp, dvcp, dbias = band_bwd(sv['qc'], sv['kcp'], sv['vcp'], sv['bias'], dys[2], scale=LANE ** -0.5,
                                      name=f"{tag}_ch_attn_bwd")
    dqc_raw, gs['g_ch_q'] = headnorm_bwd(z, P['g_ch_q'], dqc, col=Z_CH, heads=8, name=f"{tag}_ch_qn_bwd")
    dkc_raw, gs['g_ch_k'] = headnorm_bwd(z, P['g_ch_k'], dkcp[:, PAD:PAD + S, :], col=Z_CH + 1024, heads=8,
                                         name=f"{tag}_ch_kn_bwd")
    gs['rel_bias'] = relbias_bwd(dbias, name=f"{tag}_relbias_bwd")
    b16 = lambda t: t.astype(BF16)
    dz = jnp.concatenate([b16(dcq_raw), b16(dckv_raw), b16(dkr), b16(dfl.T), jnp.zeros((S, 120), BF16),
                          b16(dqb_raw), b16(dkb_raw), b16(dvb), b16(dqc_raw), b16(dkc_raw), b16(dvcp[PAD:PAD + S]),
                          dgl[0], dgl[1], dgl[2]], axis=1)
    gw['w_in'] = wgrad(sv['h'], dz, f"{tag}_dwin")
    dh = mm(dz, W['w_in'], tb=True, name=f"{tag}_dh")
    g_mix = P['g_mix'] if send_off is None else P['g_mix'] + send_off[1](gw)
    dx, gs['g_mix'] = rmsnorm_bwd(sv['x0'], g_mix, dh, res=dx, name=f"{tag}_dnorm_mix")
    return dx, gw, gs


def _local_step(x, mem, target, Ws, Ps):
    S = x.shape[0]
    cos, ssin = _rope_tables(S)
    L = len(Ws)
    saved = []
    for l in range(L):
        x, sv = _layer_fwd(x, mem, Ws[l], Ps[l], cos, ssin, f"l{l}")
        saved.append(sv)
    loss, dx = loss_head(x, target, name="loss_head")
    gws, gss = [None] * L, [None] * L
    for l in reversed(range(L)):
        dx, gws[l], gss[l] = _layer_bwd(dx, mem, Ws[l], Ps[l], saved[l], cos, ssin, f"l{l}")
    return loss, dx, gws, gss


def _pack_small(d):
    flat = jnp.concatenate([d[k].reshape(-1) for k in SMALL])
    n = flat.shape[0]
    rows = -(-n // (8 * LANE)) * 8
    return jnp.pad(flat, (0, rows * LANE - n)).reshape(rows, LANE)


def _unpack_small(packed, like):
    flat, out, off = packed.reshape(-1), {}, 0
    for k in SMALL:
        n = int(np.prod(like[k].shape))
        out[k] = flat[off:off + n].reshape(like[k].shape)
        off += n
    return out


def kernel(x, mem, g_mix, w_in, g_cq, w_uq, g_ckv, w_ukv, g_mla_q, g_mla_k, b_f, g_fox_q, g_fox_k, rel_bias, g_ch_q, g_ch_k, w_br, w_out, g_cross, g_mem, w_xq, w_xkv, g_x_q, g_x_k, w_xo, g_mlp, w_1, w_2, loss_target, m_g_mix, m_w_in, m_g_cq, m_w_uq, m_g_ckv, m_w_ukv, m_g_mla_q, m_g_mla_k, m_b_f, m_g_fox_q, m_g_fox_k, m_rel_bias, m_g_ch_q, m_g_ch_k, m_w_br, m_w_out, m_g_cross, m_g_mem, m_w_xq, m_w_xkv, m_g_x_q, m_g_x_k, m_w_xo, m_g_mlp, m_w_1, m_w_2, v_g_mix, v_w_in, v_g_cq, v_w_uq, v_g_ckv, v_w_ukv, v_g_mla_q, v_g_mla_k, v_b_f, v_g_fox_q, v_g_fox_k, v_rel_bias, v_g_ch_q, v_g_ch_k, v_w_br, v_w_out, v_g_cross, v_g_mem, v_w_xq, v_w_xkv, v_g_x_q, v_g_x_k, v_w_xo, v_g_mlp, v_w_1, v_w_2):
    args = locals()
    w = {k: args[k] for k in WEIGHTS}
    m = {k: args['m_' + k] for k in WEIGHTS}
    v = {k: args['v_' + k] for k in WEIGHTS}
    L = w_in.shape[0]

    Ps = [{k: w[k][l] for k in SMALL} for l in range(L)]
    xs, memv = x[0], mem[0]
    cos, ssin = _rope_tables(xs.shape[0])

    gathers = []
    for l in range(L):
        shards = [w[k][l].astype(BF16) for k in BIG]
        gathers.append(copies_start(_gather_copies, shards, [lax.empty((8,) + s.shape, s.dtype) for s in shards],
                                    name=f"l{l}_ag_start"))
    Ps[0]['g_mix'] = Ps[0]['g_mix'] + sum(g[4][0, :1] for g in gathers)

    Ws, saved = [], []
    for l in range(L):
        send_sems, recv_sems, shards, zones, _ = gathers[l]
        shards, zones = copies_wait(_gather_copies, send_sems, recv_sems, shards, zones, xs if l else Ps[0]['g_mix'],
                                    name=f"l{l}_ag_wait")
        zones = forward_to_sibling(zones, name=f"l{l}_ag_forward")
        zones = [place_mine(z, s, name=f"l{l}_{k}_mine") for k, z, s in zip(BIG, zones, shards)]
        Ws.append({k: w_in_full(z[:, None], 0, name=f"l{l}_w_in_layout") if k == 'w_in' else _full_from_shards(k, z, f"l{l}")
                   for k, z in zip(BIG, zones)})
        xs, sv = _layer_fwd(xs, memv, Ws[l], Ps[l], cos, ssin, f"l{l}")
        saved.append(sv)

    loss, dx = loss_head(xs, loss_target[0], name="loss_head")
    loss = lax.psum(loss[0, 0], ("x", "y", "c"))

    gss = [None] * L
    scatters = {group: [None] * L for group in RS_GROUPS}
    for l in reversed(range(L)):
        def send_off(gw, group, l=l):
            gdst = [w_in_shards(gw[k], name=f"l{l}_dw_in_layout") if k == 'w_in' else _shards_from_full(k, gw[k], f"l{l}")
                    for k in group]
            started = copies_start(_scatter_copies, gdst, [lax.empty((N_PEERS,) + g.shape[1:], g.dtype) for g in gdst],
                                   name=f"l{l}_rs_start_{group[0]}")
            scatters[group][l] = started[:4]
            return started[4][0, :1]

        hooks = tuple((lambda gw, group=group: send_off(gw, group)) for group in RS_GROUPS)
        dx, _, gss[l] = _layer_bwd(dx, memv, Ws[l], Ps[l], saved[l], cos, ssin, f"l{l}", hooks)
    grad_x = dx

    grads, delta, new_m, new_v = {}, {}, {}, {}
    after = grad_x
    for group in RS_GROUPS:
        done = [copies_wait(_scatter_copies, *scatters[group][l], after, name=f"l{l}_rs_wait_{group[0]}") for l in range(L)]
        for t, k in enumerate(group):
            grads[k], delta[k], new_m[k], new_v[k] = grad_sum_adamw(
                [done[l][0][t] for l in range(L)], [done[l][1][t] for l in range(L)], w[k], m[k], v[k], name=f"adamw_{k}")
        after = delta[group[-1]]

    small_part = _pack_small({k: jnp.stack([gss[l][k] for l in range(L)]) for k in SMALL})
    small_all = all_gather([small_part], after, name="ag_small")[0]
    grads.update(_unpack_small(ordered_sum(small_all, name="small_sum"), {k: w[k] for k in SMALL}))
    sd, sm, sv_ = adamw(_pack_small({k: w[k] for k in SMALL}), _pack_small({k: grads[k] for k in SMALL}),
                        _pack_small({k: m[k] for k in SMALL}), _pack_small({k: v[k] for k in SMALL}), name="adamw_small")
    like = {k: w[k] for k in SMALL}
    delta.update(_unpack_small(sd, like))
    new_m.update(_unpack_small(sm, like))
    new_v.update(_unpack_small(sv_, like))

    return (loss, grad_x[None], *[grads[k] for k in WEIGHTS], *[delta[k] for k in WEIGHTS],
            *[new_m[k] for k in WEIGHTS], *[new_v[k] for k in WEIGHTS])
```

```python
import numpy as np
import jax
import jax.numpy as jnp
from jax import lax
from jax.experimental import pallas as pl
from jax.experimental.pallas import tpu as pltpu

F32, BF16 = jnp.float32, jnp.bfloat16
EPS = 1e-6
NEG = -1e30
LANE = 128
VMEM_LIMIT_BYTES = 56 * 2**20
MESH = pl.DeviceIdType.MESH

D_MODEL = 2048
CHUNK = 64
BAND = 9 * CHUNK
PAD = 8 * CHUNK
REL_CLIP = 128
MLA_HEADS, MLA_NOPE, MLA_ROPE, MLA_QK = 8, 128, 64, 192
N_HEADS = 8
X_HEADS = 4
ROPE_THETA = 10000.0
ADAM_LR, ADAM_B1, ADAM_B2, ADAM_EPS, ADAM_WD, ADAM_STEP = 0.001, 0.9, 0.999, 1e-08, 0.01, 10

Z_MAIN, Z_FOX, Z_CH, Z_GATE, Z_W = 0, 1024, 4096, 7168, 13312
KR_COL, FF_COL = 768, 896

BIG = ('w_in', 'w_uq', 'w_ukv', 'w_br', 'w_out', 'w_xq', 'w_xkv', 'w_xo', 'w_1', 'w_2')
COL_SHARDED = ('w_in', 'w_uq', 'w_ukv', 'w_br', 'w_xo', 'w_1')
RS_GROUPS = (('w_2', 'w_1', 'w_xo', 'w_xq', 'w_xkv', 'w_out', 'w_br'), ('w_uq', 'w_ukv', 'w_in'))
AG_GROUPS = (('w_in',), ('w_uq', 'w_ukv', 'w_br', 'w_out', 'w_xq', 'w_xkv', 'w_xo', 'w_1', 'w_2'))
SMALL = ('g_mix', 'g_cq', 'g_ckv', 'g_mla_q', 'g_mla_k', 'b_f', 'g_fox_q', 'g_fox_k', 'rel_bias', 'g_ch_q',
         'g_ch_k', 'g_cross', 'g_mem', 'g_x_q', 'g_x_k', 'g_mlp')
WEIGHTS = ('g_mix', 'w_in', 'g_cq', 'w_uq', 'g_ckv', 'w_ukv', 'g_mla_q', 'g_mla_k', 'b_f', 'g_fox_q', 'g_fox_k',
           'rel_bias', 'g_ch_q', 'g_ch_k', 'w_br', 'w_out', 'g_cross', 'g_mem', 'w_xq', 'w_xkv', 'g_x_q', 'g_x_k',
           'w_xo', 'g_mlp', 'w_1', 'w_2')


def _params(*sem):
    return pltpu.CompilerParams(dimension_semantics=sem, vmem_limit_bytes=VMEM_LIMIT_BYTES)


def _tile(dim, pref):
    if dim <= pref:
        return dim
    for t in range(pref - pref % LANE, 0, -LANE):
        if dim % t == 0:
            return t
    raise ValueError((dim, pref))


def mm(a, b, *, ta=False, tb=False, out_dtype=F32, epi=None, aux=None, name, tm=1024, tn=512, tk=2048):
    M, K = (a.shape[1], a.shape[0]) if ta else a.shape
    N = b.shape[0] if tb else b.shape[1]
    assert (b.shape[1] if tb else b.shape[0]) == K, (a.shape, b.shape, ta, tb)
    tm, tn, tk = _tile(M, tm), _tile(N, tn), _tile(K, tk)
    nk = K // tk
    dn = (((0 if ta else 1,), (1 if tb else 0,)), ((), ()))
    n_aux = 0 if aux is None else 1

    def finish(acc, aux_refs, o_refs):
        if epi is None:
            o_refs[0][...] = acc.astype(o_refs[0].dtype)
        elif epi == 'add':
            o_refs[0][...] = (acc + aux_refs[0][...]).astype(o_refs[0].dtype)
        elif epi == 'relu2':
            o_refs[0][...] = acc
            r = jnp.maximum(acc, 0.0)
            o_refs[1][...] = (r * r).astype(o_refs[1].dtype)
        elif epi == 'mul_drelu2':
            o_refs[0][...] = (acc * (2.0 * jnp.maximum(aux_refs[0][...], 0.0))).astype(o_refs[0].dtype)

    def body(a_ref, b_ref, *rest):
        aux_refs = rest[:n_aux]
        o_refs = rest[n_aux:n_aux + (2 if epi == 'relu2' else 1)]
        part = lax.dot_general(a_ref[...].astype(BF16), b_ref[...].astype(BF16), dn, preferred_element_type=F32)
        if nk == 1:
            finish(part, aux_refs, o_refs)
        else:
            acc_ref = rest[-1]
            k = pl.program_id(2)

            @pl.when(k == 0)
            def _():
                acc_ref[...] = part

            @pl.when(k > 0)
            def _():
                acc_ref[...] += part

            @pl.when(k == nk - 1)
            def _():
                finish(acc_ref[...], aux_refs, o_refs)

    a_spec = pl.BlockSpec((tk, tm), lambda i, j, k: (k, i)) if ta else pl.BlockSpec((tm, tk), lambda i, j, k: (i, k))
    b_spec = pl.BlockSpec((tn, tk), lambda i, j, k: (j, k)) if tb else pl.BlockSpec((tk, tn), lambda i, j, k: (k, j))
    o_spec = pl.BlockSpec((tm, tn), lambda i, j, k: (i, j))
    if epi == 'relu2':
        out_shape = (jax.ShapeDtypeStruct((M, N), F32), jax.ShapeDtypeStruct((M, N), out_dtype))
        out_specs = (o_spec, o_spec)
    else:
        out_shape, out_specs = jax.ShapeDtypeStruct((M, N), out_dtype), o_spec
    return pl.pallas_call(
        body, name=name, out_shape=out_shape, grid=(M // tm, N // tn, nk),
        in_specs=[a_spec, b_spec] + [o_spec] * n_aux, out_specs=out_specs,
        scratch_shapes=[pltpu.VMEM((tm, tn), F32)] if nk > 1 else [],
        compiler_params=_params("parallel", "parallel", "arbitrary"),
    )(a, b, *([aux] if n_aux else []))


def rmsnorm_fwd(x, g, *, col=0, width=None, out_dtype=BF16, name, ts=256):
    S = x.shape[0]
    width = x.shape[1] if width is None else width
    ts, cb = _tile(S, ts), col // width

    def body(x_ref, g_ref, o_ref):
        xf = x_ref[...]
        r = lax.rsqrt(jnp.mean(xf * xf, axis=-1, keepdims=True) + EPS)
        o_ref[...] = (xf * r * g_ref[...]).astype(o_ref.dtype)

    return pl.pallas_call(
        body, name=name, out_shape=jax.ShapeDtypeStruct((S, width), out_dtype), grid=(S // ts,),
        in_specs=[pl.BlockSpec((ts, width), lambda i: (i, cb)), pl.BlockSpec((1, width), lambda i: (0, 0))],
        out_specs=pl.BlockSpec((ts, width), lambda i: (i, 0)), compiler_params=_params("parallel"),
    )(x, g.reshape(1, width))


def rmsnorm_bwd(x, g, dy, *, col=0, width=None, res=None, name, ts=256):
    S = x.shape[0]
    width = x.shape[1] if width is None else width
    ts, cb = _tile(S, ts), col // width
    has_res = res is not None

    def body(x_ref, g_ref, dy_ref, *rest):
        dx_ref, dg_ref = rest[-2:]
        xf = x_ref[...]
        r = lax.rsqrt(jnp.mean(xf * xf, axis=-1, keepdims=True) + EPS)
        dyf = dy_ref[...].astype(F32)
        dyg = dyf * g_ref[...]
        dx = r * dyg - xf * (r * r * r) * jnp.mean(dyg * xf, axis=-1, keepdims=True)
        if has_res:
            dx = dx + rest[0][...]
        dx_ref[...] = dx
        part = jnp.sum(dyf * xf * r, axis=0, keepdims=True)

        @pl.when(pl.program_id(0) == 0)
        def _():
            dg_ref[...] = part

        @pl.when(pl.program_id(0) > 0)
        def _():
            dg_ref[...] += part

    blk = pl.BlockSpec((ts, width), lambda i: (i, 0))
    dx, dg = pl.pallas_call(
        body, name=name,
        out_shape=(jax.ShapeDtypeStruct((S, width), F32), jax.ShapeDtypeStruct((1, width), F32)), grid=(S // ts,),
        in_specs=[pl.BlockSpec((ts, width), lambda i: (i, cb)), pl.BlockSpec((1, width), lambda i: (0, 0)), blk]
        + ([blk] if has_res else []),
        out_specs=(blk, pl.BlockSpec((1, width), lambda i: (0, 0))), compiler_params=_params("arbitrary"),
    )(x, g.reshape(1, width), dy, *([res] if has_res else []))
    return dx, dg.reshape(width)


def headnorm_fwd(x, g, *, col, heads, name, ts=1024):
    S = x.shape[0]
    ts, cb = _tile(S, ts), col // LANE

    def body(x_ref, g_ref, o_ref):
        xf = x_ref[...]
        r = lax.rsqrt(jnp.mean(xf * xf, axis=-1, keepdims=True) + EPS)
        o_ref[0] = (xf * r * g_ref[...]).astype(o_ref.dtype)

    return pl.pallas_call(
        body, name=name, out_shape=jax.ShapeDtypeStruct((heads, S, LANE), BF16), grid=(heads, S // ts),
        in_specs=[pl.BlockSpec((ts, LANE), lambda h, i: (i, cb + h)), pl.BlockSpec((1, LANE), lambda h, i: (0, 0))],
        out_specs=pl.BlockSpec((1, ts, LANE), lambda h, i: (h, i, 0)), compiler_params=_params("parallel", "parallel"),
    )(x, g.reshape(1, LANE))


def headnorm_bwd(x, g, dy, *, col, heads, name, ts=1024):
    S = x.shape[0]
    ts, cb = _tile(S, ts), col // LANE

    def body(x_ref, g_ref, dy_ref, dx_ref, dg_ref):
        xf = x_ref[...]
        r = lax.rsqrt(jnp.mean(xf * xf, axis=-1, keepdims=True) + EPS)
        dyf = dy_ref[0]
        dyg = dyf * g_ref[...]
        dx_ref[...] = r * dyg - xf * (r * r * r) * jnp.mean(dyg * xf, axis=-1, keepdims=True)
        part = jnp.sum(dyf * xf * r, axis=0, keepdims=True)
        first = jnp.logical_and(pl.program_id(0) == 0, pl.program_id(1) == 0)

        @pl.when(first)
        def _():
            dg_ref[...] = part

        @pl.when(jnp.logical_not(first))
        def _():
            dg_ref[...] += part

    dx, dg = pl.pallas_call(
        body, name=name,
        out_shape=(jax.ShapeDtypeStruct((S, heads * LANE), F32), jax.ShapeDtypeStruct((1, LANE), F32)),
        grid=(heads, S // ts),
        in_specs=[pl.BlockSpec((ts, LANE), lambda h, i: (i, cb + h)), pl.BlockSpec((1, LANE), lambda h, i: (0, 0)),
                  pl.BlockSpec((1, ts, LANE), lambda h, i: (h, i, 0))],
        out_specs=(pl.BlockSpec((ts, LANE), lambda h, i: (i, h)), pl.BlockSpec((1, LANE), lambda h, i: (0, 0))),
        compiler_params=_params("arbitrary", "arbitrary"),
    )(x, g.reshape(1, LANE), dy)
    return dx, dg.reshape(LANE)


def _rope_tables(S):
    pos = jnp.arange(S, dtype=F32)
    inv = ROPE_THETA ** (-jnp.arange(0, MLA_ROPE, 2, dtype=F32) / MLA_ROPE)
    ang = pos[:, None] * inv[None, :]
    c, s, z = jnp.cos(ang), jnp.sin(ang), jnp.zeros((S, 64), F32)
    return jnp.concatenate([c, c, z], axis=1), jnp.concatenate([-s, s, z], axis=1)


def _rope(v, cos, ssin, lane):
    partner = jnp.where(lane < 32, pltpu.roll(v, 96, 1), pltpu.roll(v, 32, 1))
    return v * cos + partner * ssin


def mla_prep_fwd(xn, xr, g, cos, ssin, *, n_col, n_stride, r_col, r_stride, heads, name, ts=1024):
    S = xn.shape[0]
    ts = _tile(S, ts)
    nb, ns, rb, rs = n_col // LANE, n_stride // LANE, r_col // LANE, r_stride // LANE
    gn = g[:MLA_NOPE].reshape(1, LANE)
    gr = jnp.concatenate([g[MLA_NOPE:], jnp.zeros((64,), F32)]).reshape(1, LANE)

    def body(n_ref, r_ref, gn_ref, gr_ref, c_ref, s_ref, o_ref):
        n, rr = n_ref[...], r_ref[...]
        ss = jnp.sum(n * n, axis=-1, keepdims=True) + jnp.sum(rr * rr, axis=-1, keepdims=True)
        r = lax.rsqrt(ss * (1.0 / MLA_QK) + EPS)
        lane = lax.broadcasted_iota(jnp.int32, rr.shape, 1)
        o_ref[0, :, :LANE] = (n * r * gn_ref[...]).astype(o_ref.dtype)
        o_ref[0, :, LANE:] = _rope(rr * r * gr_ref[...], c_ref[...], s_ref[...], lane).astype(o_ref.dtype)

    row = lambda h, i: (0, 0)
    return pl.pallas_call(
        body, name=name, out_shape=jax.ShapeDtypeStruct((heads, S, 2 * LANE), BF16), grid=(heads, S // ts),
        in_specs=[pl.BlockSpec((ts, LANE), lambda h, i: (i, nb + ns * h)),
                  pl.BlockSpec((ts, LANE), lambda h, i: (i, rb + rs * h)),
                  pl.BlockSpec((1, LANE), row), pl.BlockSpec((1, LANE), row),
                  pl.BlockSpec((ts, LANE), lambda h, i: (i, 0)), pl.BlockSpec((ts, LANE), lambda h, i: (i, 0))],
        out_specs=pl.BlockSpec((1, ts, 2 * LANE), lambda h, i: (h, i, 0)),
        compiler_params=_params("parallel", "parallel"),
    )(xn, xr, gn, gr, cos, ssin)


def mla_prep_bwd(xn, xr, g, cos, ssin, dy, *, n_col, n_stride, r_col, r_stride, heads, name, ts=1024):
    S = xn.shape[0]
    ts = _tile(S, ts)
    nb, ns, rb, rs = n_col // LANE, n_stride // LANE, r_col // LANE, r_stride // LANE
    shared = r_stride == 0
    gn = g[:MLA_NOPE].reshape(1, LANE)
    gr = jnp.concatenate([g[MLA_NOPE:], jnp.zeros((64,), F32)]).reshape(1, LANE)

    def body(n_ref, r_ref, gn_ref, gr_ref, c_ref, s_ref, dy_ref, dn_ref, dr_ref, dgn_ref, dgr_ref):
        i, h = pl.program_id(0), pl.program_id(1)
        n, rr = n_ref[...], r_ref[...]
        ss = jnp.sum(n * n, axis=-1, keepdims=True) + jnp.sum(rr * rr, axis=-1, keepdims=True)
        r = lax.rsqrt(ss * (1.0 / MLA_QK) + EPS)
        lane = lax.broadcasted_iota(jnp.int32, rr.shape, 1)
        dyn = dy_ref[0, :, :LANE]
        dyr = dy_ref[0, :, LANE:]
        t = dyr * s_ref[...]
        dvr = dyr * c_ref[...] + jnp.where(lane < 32, pltpu.roll(t, 96, 1), pltpu.roll(t, 32, 1))
        dvr = jnp.where(lane < 64, dvr, 0.0)
        dgn_part = jnp.sum(dyn * n * r, axis=0, keepdims=True)
        dgr_part = jnp.sum(dvr * rr * r, axis=0, keepdims=True)
        dyn_g, dvr_g = dyn * gn_ref[...], dvr * gr_ref[...]
        proj = (jnp.sum(dyn_g * n, axis=-1, keepdims=True) + jnp.sum(dvr_g * rr, axis=-1, keepdims=True)) * (1.0 / MLA_QK)
        r3 = r * r * r
        dn_ref[...] = r * dyn_g - n * r3 * proj
        dr = r * dvr_g - rr * r3 * proj
        if shared:
            @pl.when(h == 0)
            def _():
                dr_ref[...] = dr

            @pl.when(h > 0)
            def _():
                dr_ref[...] += dr
        else:
            dr_ref[...] = dr
        first = jnp.logical_and(i == 0, h == 0)

        @pl.when(first)
        def _():
            dgn_ref[...] = dgn_part
            dgr_ref[...] = dgr_part

        @pl.when(jnp.logical_not(first))
        def _():
            dgn_ref[...] += dgn_part
            dgr_ref[...] += dgr_part

    row = lambda i, h: (0, 0)
    dr_cols = LANE if shared else heads * LANE
    dn, dr, dgn, dgr = pl.pallas_call(
        body, name=name,
        out_shape=(jax.ShapeDtypeStruct((S, heads * LANE), F32), jax.ShapeDtypeStruct((S, dr_cols), F32),
                   jax.ShapeDtypeStruct((1, LANE), F32), jax.ShapeDtypeStruct((1, LANE), F32)),
        grid=(S // ts, heads),
        in_specs=[pl.BlockSpec((ts, LANE), lambda i, h: (i, nb + ns * h)),
                  pl.BlockSpec((ts, LANE), lambda i, h: (i, rb + rs * h)),
                  pl.BlockSpec((1, LANE), row), pl.BlockSpec((1, LANE), row),
                  pl.BlockSpec((ts, LANE), lambda i, h: (i, 0)), pl.BlockSpec((ts, LANE), lambda i, h: (i, 0)),
                  pl.BlockSpec((1, ts, 2 * LANE), lambda i, h: (h, i, 0))],
        out_specs=(pl.BlockSpec((ts, LANE), lambda i, h: (i, h)),
                   pl.BlockSpec((ts, LANE), (lambda i, h: (i, 0)) if shared else (lambda i, h: (i, h))),
                   pl.BlockSpec((1, LANE), row), pl.BlockSpec((1, LANE), row)),
        compiler_params=_params("arbitrary", "arbitrary"),
    )(xn, xr, gn, gr, cos, ssin, dy)
    return dn, dr, jnp.concatenate([dgn.reshape(LANE), dgr.reshape(LANE)[:MLA_ROPE]])


_NT = (((1,), (1,)), ((), ()))
_TN = (((0,), (0,)), ((), ()))


def attn_fwd(q, k, v, *, v_col, scale, name, bq=256):
    H, S, dk = q.shape
    Sk = k.shape[1]
    bq, vb = _tile(S, bq), v_col // LANE

    def body(q_ref, k_ref, v_ref, o_ref):
        s = lax.dot_general(q_ref[0], k_ref[0], _NT, preferred_element_type=F32) * scale
        e = jnp.exp(s - jnp.max(s, axis=-1, keepdims=True))
        p = e * (1.0 / jnp.sum(e, axis=-1, keepdims=True))
        o_ref[...] = jnp.dot(p.astype(BF16), v_ref[...].astype(BF16), preferred_element_type=F32).astype(o_ref.dtype)

    return pl.pallas_call(
        body, name=name, out_shape=jax.ShapeDtypeStruct((S, H * LANE), BF16), grid=(H, S // bq),
        in_specs=[pl.BlockSpec((1, bq, dk), lambda h, i: (h, i, 0)), pl.BlockSpec((1, Sk, dk), lambda h, i: (h, 0, 0)),
                  pl.BlockSpec((Sk, LANE), lambda h, i: (0, vb + h))],
        out_specs=pl.BlockSpec((bq, LANE), lambda h, i: (i, h)), compiler_params=_params("parallel", "parallel"),
    )(q, k, v)


def attn_bwd(q, k, v, do, *, v_col, scale, name, bq=256):
    H, S, dk = q.shape
    Sk = k.shape[1]
    bq, vb = _tile(S, bq), v_col // LANE

    def body(q_ref, k_ref, v_ref, do_ref, dq_ref, dk_ref, dv_ref):
        i = pl.program_id(1)
        qb, kb, vv = q_ref[0], k_ref[0], v_ref[...].astype(BF16)
        s = lax.dot_general(qb, kb, _NT, preferred_element_type=F32) * scale
        e = jnp.exp(s - jnp.max(s, axis=-1, keepdims=True))
        p = e * (1.0 / jnp.sum(e, axis=-1, keepdims=True))
        dob = do_ref[...].astype(BF16)
        dv_part = lax.dot_general(p.astype(BF16), dob, _TN, preferred_element_type=F32)
        dp = lax.dot_general(dob, vv, _NT, preferred_element_type=F32)
        ds = p * (dp - jnp.sum(p * dp, axis=-1, keepdims=True))
        dsb = (ds * scale).astype(BF16)
        dq_ref[0] = jnp.dot(dsb, kb, preferred_element_type=F32)
        dk_part = lax.dot_general(dsb, qb, _TN, preferred_element_type=F32)

        @pl.when(i == 0)
        def _():
            dk_ref[0] = dk_part
            dv_ref[...] = dv_part

        @pl.when(i > 0)
        def _():
            dk_ref[0] += dk_part
            dv_ref[...] += dv_part

    return pl.pallas_call(
        body, name=name,
        out_shape=(jax.ShapeDtypeStruct((H, S, dk), F32), jax.ShapeDtypeStruct((H, Sk, dk), F32),
                   jax.ShapeDtypeStruct((Sk, H * LANE), F32)),
        grid=(H, S // bq),
        in_specs=[pl.BlockSpec((1, bq, dk), lambda h, i: (h, i, 0)), pl.BlockSpec((1, Sk, dk), lambda h, i: (h, 0, 0)),
                  pl.BlockSpec((Sk, LANE), lambda h, i: (0, vb + h)), pl.BlockSpec((bq, LANE), lambda h, i: (i, h))],
        out_specs=(pl.BlockSpec((1, bq, dk), lambda h, i: (h, i, 0)), pl.BlockSpec((1, Sk, dk), lambda h, i: (h, 0, 0)),
                   pl.BlockSpec((Sk, LANE), lambda h, i: (0, h))),
        compiler_params=_params("parallel", "arbitrary"),
    )(q, k, v, do)


def _causal_scores(q, kblk, i, start, blk, scale, chunked, cq, ckblk):
    s = lax.dot_general(q, kblk, _NT, preferred_element_type=F32) * scale
    if cq is not None:
        s = s + cq - ckblk
    qpos = i * blk + lax.broadcasted_iota(jnp.int32, s.shape, 0)
    kpos = start + lax.broadcasted_iota(jnp.int32, s.shape, 1)
    ok = (kpos >> 6) <= (qpos >> 6) if chunked else kpos <= qpos
    return jnp.where(ok, s, NEG)


def causal_attn_fwd(q, k, v, *, v_col, chunked, scale, cq=None, ck=None, name, blk=256):
    H, S, dk = q.shape
    blk, vb = _tile(S, blk), v_col // LANE
    fox = cq is not None

    def body(q_ref, k_ref, v_ref, *rest):
        o_ref, lse_ref = rest[-2:]
        s = _causal_scores(q_ref[0], k_ref[0], pl.program_id(1), 0, blk, scale, chunked,
                           rest[0][0] if fox else None, rest[1][0] if fox else None)
        m = jnp.max(s, axis=-1, keepdims=True)
        p = jnp.exp(s - m)
        l = jnp.sum(p, axis=-1, keepdims=True)
        pv = jnp.dot(p.astype(BF16), v_ref[...].astype(BF16), preferred_element_type=F32)
        o_ref[...] = (pv * (1.0 / l)).astype(o_ref.dtype)
        lse_ref[0] = m + jnp.log(l)

    in_specs = [pl.BlockSpec((1, blk, dk), lambda h, i: (h, i, 0)), pl.BlockSpec((1, S, dk), lambda h, i: (h, 0, 0)),
                pl.BlockSpec((S, LANE), lambda h, i: (0, vb + h))]
    args = [q, k, v]
    if fox:
        in_specs += [pl.BlockSpec((1, blk, 1), lambda h, i: (h, i, 0)), pl.BlockSpec((1, 1, S), lambda h, i: (h, 0, 0))]
        args += [cq, ck]
    return pl.pallas_call(
        body, name=name, out_shape=(jax.ShapeDtypeStruct((S, H * LANE), BF16), jax.ShapeDtypeStruct((H, S, 1), F32)),
        grid=(H, S // blk), in_specs=in_specs,
        out_specs=(pl.BlockSpec((blk, LANE), lambda h, i: (i, h)), pl.BlockSpec((1, blk, 1), lambda h, i: (h, i, 0))),
        compiler_params=_params("parallel", "parallel"),
    )(*args)


def causal_attn_bwd(q, k, v, o, do, lse, *, v_col, chunked, scale, cq=None, ck=None, name, blk=256):
    H, S, dk = q.shape
    blk, vb = _tile(S, blk), v_col // LANE
    fox = cq is not None

    def body(q_ref, k_ref, v_ref, o_ref, do_ref, lse_ref, *rest):
        i = pl.program_id(1)
        if fox:
            cq_ref, ck_ref, dq_ref, dk_ref, dv_ref, dcq_ref, dck_ref = rest
        else:
            dq_ref, dk_ref, dv_ref = rest

        qb, kb, dob = q_ref[0], k_ref[0], do_ref[...].astype(BF16)
        delta = jnp.sum(do_ref[...].astype(F32) * o_ref[...].astype(F32), axis=-1, keepdims=True)
        s = _causal_scores(qb, kb, i, 0, blk, scale, chunked, cq_ref[0] if fox else None, ck_ref[0] if fox else None)
        p = jnp.exp(s - lse_ref[0])
        dv_part = lax.dot_general(p.astype(BF16), dob, _TN, preferred_element_type=F32)
        dp = lax.dot_general(dob, v_ref[...].astype(BF16), _NT, preferred_element_type=F32)
        ds = p * (dp - delta)
        dsb = (ds * scale).astype(BF16)
        dq_ref[0] = jnp.dot(dsb, kb, preferred_element_type=F32)
        dk_part = lax.dot_general(dsb, qb, _TN, preferred_element_type=F32)
        if fox:
            dcq_ref[0] = jnp.sum(ds, axis=-1, keepdims=True)
            dck_part = -jnp.sum(ds, axis=0, keepdims=True)

        @pl.when(i == 0)
        def _():
            dk_ref[0] = dk_part
            dv_ref[...] = dv_part
            if fox:
                dck_ref[0] = dck_part

        @pl.when(i > 0)
        def _():
            dk_ref[0] += dk_part
            dv_ref[...] += dv_part
            if fox:
                dck_ref[0] += dck_part

    row = pl.BlockSpec((blk, LANE), lambda h, i: (i, h))
    in_specs = [pl.BlockSpec((1, blk, dk), lambda h, i: (h, i, 0)), pl.BlockSpec((1, S, dk), lambda h, i: (h, 0, 0)),
                pl.BlockSpec((S, LANE), lambda h, i: (0, vb + h)), row, row, pl.BlockSpec((1, blk, 1), lambda h, i: (h, i, 0))]
    args = [q, k, v, o, do, lse]
    out_shape = [jax.ShapeDtypeStruct((H, S, dk), F32), jax.ShapeDtypeStruct((H, S, dk), F32),
                 jax.ShapeDtypeStruct((S, H * LANE), F32)]
    out_specs = [pl.BlockSpec((1, blk, dk), lambda h, i: (h, i, 0)), pl.BlockSpec((1, S, dk), lambda h, i: (h, 0, 0)),
                 pl.BlockSpec((S, LANE), lambda h, i: (0, h))]
    if fox:
        fox_specs = [pl.BlockSpec((1, blk, 1), lambda h, i: (h, i, 0)), pl.BlockSpec((1, 1, S), lambda h, i: (h, 0, 0))]
        in_specs += fox_specs
        args += [cq, ck]
        out_shape += [jax.ShapeDtypeStruct((H, S, 1), F32), jax.ShapeDtypeStruct((H, 1, S), F32)]
        out_specs += fox_specs
    return pl.pallas_call(
        body, name=name, out_shape=tuple(out_shape), grid=(H, S // blk), in_specs=in_specs, out_specs=tuple(out_specs),
        compiler_params=_params("parallel", "arbitrary"),
    )(*args)


CPB = 4
BANDW = BAND + CHUNK


def _band_probs(qc, kb, bias, start, scale):
    s = lax.dot_general(qc, kb, _NT, preferred_element_type=F32) * scale
    col = lax.broadcasted_iota(jnp.int32, s.shape, 1)
    valid = jnp.logical_and(start + col >= PAD, col < BAND)
    s = jnp.where(valid, s + bias, NEG)
    e = jnp.exp(s - jnp.max(s, axis=-1, keepdims=True))
    return e * (1.0 / jnp.sum(e, axis=-1, keepdims=True))


def band_fwd(q, kp, vp, bias, *, scale, name):
    H, S, _ = q.shape
    Sp, rows = S + PAD + CHUNK, CPB * CHUNK

    def body(q_ref, k_ref, v_ref, b_ref, o_ref):
        j = pl.program_id(1)
        for cc in range(CPB):
            start = pl.multiple_of((j * CPB + cc) * CHUNK, CHUNK)
            kb = k_ref[0, pl.ds(start, BANDW), :]
            vb = v_ref[pl.ds(start, BANDW), :].astype(BF16)
            p = _band_probs(q_ref[0, cc * CHUNK:(cc + 1) * CHUNK, :], kb, b_ref[0], start, scale)
            o_ref[cc * CHUNK:(cc + 1) * CHUNK, :] = jnp.dot(p.astype(BF16), vb, preferred_element_type=F32).astype(o_ref.dtype)

    return pl.pallas_call(
        body, name=name, out_shape=jax.ShapeDtypeStruct((S, H * LANE), BF16), grid=(H, S // rows),
        in_specs=[pl.BlockSpec((1, rows, LANE), lambda h, j: (h, j, 0)), pl.BlockSpec((1, Sp, LANE), lambda h, j: (h, 0, 0)),
                  pl.BlockSpec((Sp, LANE), lambda h, j: (0, h)), pl.BlockSpec((1, CHUNK, BANDW), lambda h, j: (h, 0, 0))],
        out_specs=pl.BlockSpec((rows, LANE), lambda h, j: (j, h)), compiler_params=_params("parallel", "parallel"),
    )(q, kp, vp, bias)


def band_bwd(q, kp, vp, bias, do, *, scale, name):
    H, S, _ = q.shape
    Sp, rows = S + PAD + CHUNK, CPB * CHUNK

    def body(q_ref, k_ref, v_ref, b_ref, do_ref, dq_ref, dk_ref, dv_ref, db_ref):
        j = pl.program_id(1)

        @pl.when(j == 0)
        def _():
            dk_ref[...] = jnp.zeros_like(dk_ref)
            dv_ref[...] = jnp.zeros_like(dv_ref)
            db_ref[...] = jnp.zeros_like(db_ref)

        for cc in range(CPB):
            start = pl.multiple_of((j * CPB + cc) * CHUNK, CHUNK)
            sl = slice(cc * CHUNK, (cc + 1) * CHUNK)
            qc = q_ref[0, sl, :]
            kb = k_ref[0, pl.ds(start, BANDW), :]
            vb = v_ref[pl.ds(start, BANDW), :].astype(BF16)
            p = _band_probs(qc, kb, b_ref[0], start, scale)
            dob = do_ref[sl, :].astype(BF16)
            dv_ref[pl.ds(start, BANDW), :] += lax.dot_general(p.astype(BF16), dob, _TN, preferred_element_type=F32)
            dp = lax.dot_general(dob, vb, _NT, preferred_element_type=F32)
            ds = p * (dp - jnp.sum(p * dp, axis=-1, keepdims=True))
            db_ref[0] += ds
            dsb = (ds * scale).astype(BF16)
            dq_ref[0, sl, :] = jnp.dot(dsb, kb, preferred_element_type=F32)
            dk_ref[0, pl.ds(start, BANDW), :] += lax.dot_general(dsb, qc, _TN, preferred_element_type=F32)

    return pl.pallas_call(
        body, name=name,
        out_shape=(jax.ShapeDtypeStruct((H, S, LANE), F32), jax.ShapeDtypeStruct((H, Sp, LANE), F32),
                   jax.ShapeDtypeStruct((Sp, H * LANE), F32), jax.ShapeDtypeStruct((H, CHUNK, BANDW), F32)),
        grid=(H, S // rows),
        in_specs=[pl.BlockSpec((1, rows, LANE), lambda h, j: (h, j, 0)), pl.BlockSpec((1, Sp, LANE), lambda h, j: (h, 0, 0)),
                  pl.BlockSpec((Sp, LANE), lambda h, j: (0, h)), pl.BlockSpec((1, CHUNK, BANDW), lambda h, j: (h, 0, 0)),
                  pl.BlockSpec((rows, LANE), lambda h, j: (j, h))],
        out_specs=(pl.BlockSpec((1, rows, LANE), lambda h, j: (h, j, 0)), pl.BlockSpec((1, Sp, LANE), lambda h, j: (h, 0, 0)),
                   pl.BlockSpec((Sp, LANE), lambda h, j: (0, h)), pl.BlockSpec((1, CHUNK, BANDW), lambda h, j: (h, 0, 0))),
        compiler_params=_params("parallel", "arbitrary"),
    )(q, kp, vp, bias, do)


def band_bias(rel_bias, *, name):
    H = rel_bias.shape[0]
    last = rel_bias[:, 2 * REL_CLIP:]
    row0 = jnp.concatenate([jnp.tile(last, (1, PAD - REL_CLIP)), rel_bias[:, CHUNK + 1:][:, ::-1],
                            jnp.tile(last, (1, CHUNK))], axis=1)

    def body(r_ref, o_ref):
        o_ref[0] = pltpu.roll(jnp.broadcast_to(r_ref[0], (CHUNK, BANDW)), 0, 1, stride=1, stride_axis=0)

    return pl.pallas_call(
        body, name=name, out_shape=jax.ShapeDtypeStruct((H, CHUNK, BANDW), F32), grid=(H,),
        in_specs=[pl.BlockSpec((1, 1, BANDW), lambda h: (h, 0, 0))], out_specs=pl.BlockSpec((1, CHUNK, BANDW), lambda h: (h, 0, 0)),
        compiler_params=_params("parallel"),
    )(row0.reshape(H, 1, BANDW))


def relbias_bwd(dbias, *, name):
    H, W = dbias.shape[0], BANDW
    x = jnp.pad(dbias[:, :, :BAND][:, :, ::-1], ((0, 0), (0, 0), (0, CHUNK)))

    def body(x_ref, o_ref):
        skew = pltpu.roll(x_ref[0], 0, 1, stride=1, stride_axis=0)
        f = jnp.broadcast_to(jnp.sum(skew, axis=0, keepdims=True), (8, W))
        lane = lax.broadcasted_iota(jnp.int32, (8, W), 1)
        direct = jnp.where(jnp.logical_and(lane >= 65, lane <= 255), pltpu.roll(f, 65, 1), 0.0)
        tail = jnp.sum(jnp.where(lane >= 191, f, 0.0), axis=-1, keepdims=True)
        o_ref[0] = direct + jnp.where(lane == 2 * REL_CLIP, tail, 0.0)

    out = pl.pallas_call(
        body, name=name, out_shape=jax.ShapeDtypeStruct((H, 8, W), F32), grid=(H,),
        in_specs=[pl.BlockSpec((1, CHUNK, W), lambda h: (h, 0, 0))], out_specs=pl.BlockSpec((1, 8, W), lambda h: (h, 0, 0)),
        compiler_params=_params("parallel"),
    )(x)
    return out[:, 0, :2 * REL_CLIP + 1]


def _split_dot(x, u, dn):
    hi = x.astype(BF16)
    r1 = x - hi.astype(F32)
    mid = r1.astype(BF16)
    lo = (r1 - mid.astype(F32)).astype(BF16)
    d = lambda t: lax.dot_general(t, u, dn, preferred_element_type=F32)
    return d(hi) + d(mid) + d(lo)


def _upper_ones(S):
    return (np.arange(S)[:, None] <= np.arange(S)[None, :]).astype(np.float32)


def foxgate_fwd(fl, b, *, name):
    H, S = fl.shape
    u = jnp.asarray(_upper_ones(S), BF16)

    def body(f_ref, b_ref, u_ref, o_ref):
        x = f_ref[...] + b_ref[...]
        lf = jnp.minimum(x, 0.0) - jnp.log(1.0 + jnp.exp(-jnp.abs(x)))
        o_ref[...] = _split_dot(lf, u_ref[...], (((1,), (0,)), ((), ())))

    return pl.pallas_call(body, name=name, out_shape=jax.ShapeDtypeStruct((H, S), F32),
                          compiler_params=pltpu.CompilerParams(vmem_limit_bytes=VMEM_LIMIT_BYTES))(fl, b.reshape(H, 1), u)


def foxgate_bwd(fl, b, dcum, *, name):
    H, S = fl.shape
    u = jnp.asarray(_upper_ones(S), BF16)

    def body(f_ref, b_ref, u_ref, dc_ref, df_ref, db_ref):
        x = f_ref[...] + b_ref[...]
        dlf = _split_dot(dc_ref[...], u_ref[...], _NT)
        df = dlf * (1.0 / (1.0 + jnp.exp(x)))
        df_ref[...] = df
        db_ref[...] = jnp.sum(df, axis=-1, keepdims=True)

    df, db = pl.pallas_call(body, name=name,
                            out_shape=(jax.ShapeDtypeStruct((H, S), F32), jax.ShapeDtypeStruct((H, 1), F32)),
                            compiler_params=pltpu.CompilerParams(vmem_limit_bytes=VMEM_LIMIT_BYTES))(fl, b.reshape(H, 1), u, dcum)
    return df, db.reshape(H)


def gate_fwd(z, proj, *, name, ts=256, tc=512):
    S, D = proj[0].shape
    ts, gb, nb = _tile(S, ts), Z_GATE // tc, D // tc

    def body(g0, g1, g2, p0, p1, p2, o_ref):
        acc = None
        for g_ref, p_ref in zip((g0, g1, g2), (p0, p1, p2)):
            t = (1.0 / (1.0 + jnp.exp(-g_ref[...]))) * p_ref[...]
            acc = t if acc is None else acc + t
        o_ref[...] = acc.astype(o_ref.dtype)

    blk = pl.BlockSpec((ts, tc), lambda i, j: (i, j))
    return pl.pallas_call(
        body, name=name, out_shape=jax.ShapeDtypeStruct((S, D), BF16), grid=(S // ts, nb),
        in_specs=[pl.BlockSpec((ts, tc), lambda i, j, n=n: (i, gb + n * nb + j)) for n in range(3)] + [blk] * 3,
        out_specs=blk, compiler_params=_params("parallel", "parallel"),
    )(z, z, z, *proj)


def gate_bwd(z, proj, dm, *, name, ts=256, tc=512):
    S, D = proj[0].shape
    ts, gb, nb = _tile(S, ts), Z_GATE // tc, D // tc

    def body(g0, g1, g2, p0, p1, p2, dm_ref, *outs):
        dmv = dm_ref[...]
        for n, (g_ref, p_ref) in enumerate(zip((g0, g1, g2), (p0, p1, p2))):
            sg = 1.0 / (1.0 + jnp.exp(-g_ref[...]))
            outs[n][...] = (dmv * sg).astype(BF16)
            outs[3 + n][...] = (dmv * p_ref[...] * sg * (1.0 - sg)).astype(BF16)

    blk = pl.BlockSpec((ts, tc), lambda i, j: (i, j))
    outs = pl.pallas_call(
        body, name=name, out_shape=tuple(jax.ShapeDtypeStruct((S, D), BF16) for _ in range(6)), grid=(S // ts, nb),
        in_specs=[pl.BlockSpec((ts, tc), lambda i, j, n=n: (i, gb + n * nb + j)) for n in range(3)] + [blk] * 4,
        out_specs=(blk,) * 6, compiler_params=_params("parallel", "parallel"),
    )(z, z, z, *proj, dm)
    return outs[:3], outs[3:]


def loss_head(y, target, *, name, ts=256):
    S, D = y.shape
    ts = _tile(S, ts)

    def body(y_ref, t_ref, l_ref, dy_ref):
        err = y_ref[...] - t_ref[...]
        dy_ref[...] = err * (1.0 / D)
        part = 0.5 * jnp.sum(jnp.mean(err * err, axis=-1, keepdims=True), axis=0, keepdims=True)

        @pl.when(pl.program_id(0) == 0)
        def _():
            l_ref[...] = part

        @pl.when(pl.program_id(0) > 0)
        def _():
            l_ref[...] += part

    blk = pl.BlockSpec((ts, D), lambda i: (i, 0))
    return pl.pallas_call(
        body, name=name, out_shape=(jax.ShapeDtypeStruct((1, 1), F32), jax.ShapeDtypeStruct((S, D), F32)), grid=(S // ts,),
        in_specs=[blk, blk], out_specs=(pl.BlockSpec((1, 1), lambda i: (0, 0)), blk), compiler_params=_params("arbitrary"),
    )(y, target)


def adamw(w, g, m, v, *, name):
    shape = w.shape
    C = shape[-1]
    R = int(np.prod(shape[:-1]))
    br = R
    while br % 16 == 0 and br * C * 4 > 2**20:
        br //= 2
    w2, g2, m2, v2 = (t.reshape(R, C) for t in (w, g, m, v))

    def body(w_ref, g_ref, m_ref, v_ref, d_ref, nm_ref, nv_ref):
        d_ref[...], nm_ref[...], nv_ref[...] = _adamw_update(w_ref[...], g_ref[...], m_ref[...], v_ref[...])

    blk = pl.BlockSpec((br, C), lambda i: (i, 0))
    outs = pl.pallas_call(
        body, name=name, out_shape=tuple(jax.ShapeDtypeStruct((R, C), F32) for _ in range(3)), grid=(R // br,),
        in_specs=[blk] * 4, out_specs=(blk,) * 3, compiler_params=_params("parallel"),
    )(w2, g2, m2, v2)
    return tuple(o.reshape(shape) for o in outs)


_ANY = pl.BlockSpec(memory_space=pl.ANY)


def _place():
    return lax.axis_index("x"), lax.axis_index("y"), lax.axis_index("c")


def all_gather(xs, after, *, name):
    n = len(xs)

    def body(*refs):
        x_refs, o_refs = refs[:n], refs[n + 1:2 * n + 1]
        send_sems, recv_sems, local_sems = refs[2 * n + 1:]
        px, py, pc = _place()
        me, sibling = (px, py, pc), (px, py, 1 - pc)
        chips = [(1 - px, py), (px, 1 - py), (1 - px, 1 - py)]

        def slot(t, dev):
            return o_refs[t].at[4 * dev[0] + 2 * dev[1] + dev[2]]

        def copy(t, k, block, to, src=None):
            return pltpu.make_async_remote_copy(
                src_ref=slot(t, block) if src is None else src, dst_ref=slot(t, block),
                send_sem=send_sems.at[t, k], recv_sem=recv_sems.at[t, k], device_id=to, device_id_type=MESH)

        mine = [pltpu.make_async_copy(x_refs[t], slot(t, me), local_sems.at[t]) for t in range(n)]
        first = []
        for t in range(n):
            mine[t].start()
            first += [copy(t, 1 + j, me, (*chip, pc), src=x_refs[t]) for j, chip in enumerate(chips)]
            first.append(copy(t, 0, me, sibling, src=x_refs[t]))
        for cp in first:
            cp.start()
        passed = []
        for t in range(n):
            for j, chip in enumerate(chips):
                copy(t, 1 + j, (*chip, pc), me).wait_recv()
                fwd = copy(t, 4 + j, (*chip, pc), sibling)
                fwd.start()
                passed.append(fwd)
        for t in range(n):
            copy(t, 0, sibling, me).wait_recv()
            for j, chip in enumerate(chips):
                copy(t, 4 + j, (*chip, 1 - pc), me).wait_recv()
        for cp in first + passed:
            cp.wait_send()
        for cp in mine:
            cp.wait()

    return pl.pallas_call(
        body, name=name, out_shape=tuple(jax.ShapeDtypeStruct((8,) + x.shape, x.dtype) for x in xs),
        in_specs=[_ANY] * (n + 1), out_specs=(_ANY,) * n,
        scratch_shapes=[pltpu.SemaphoreType.DMA((n, 7)), pltpu.SemaphoreType.DMA((n, 7)), pltpu.SemaphoreType.DMA((n,))],
    )(*xs, after)


def forward_to_sibling(zones, *, name):
    n = len(zones)

    def body(*refs):
        z_refs, (send_sems, recv_sems) = refs[n:2 * n], refs[2 * n:]
        px, py, pc = _place()
        copies = []
        for t in range(n):
            for j, chip in enumerate([(1 - px, py), (px, 1 - py), (1 - px, 1 - py)]):
                slot = z_refs[t].at[4 * chip[0] + 2 * chip[1] + pc]
                copies.append(pltpu.make_async_remote_copy(
                    src_ref=slot, dst_ref=slot, send_sem=send_sems.at[t, j], recv_sem=recv_sems.at[t, j],
                    device_id=(px, py, 1 - pc), device_id_type=MESH))
        for cp in copies:
            cp.start()
        for cp in copies:
            cp.wait()

    return pl.pallas_call(
        body, name=name, out_shape=tuple(jax.ShapeDtypeStruct(z.shape, z.dtype) for z in zones),
        in_specs=[_ANY] * n, out_specs=(_ANY,) * n, input_output_aliases={t: t for t in range(n)},
        scratch_shapes=[pltpu.SemaphoreType.DMA((n, 3)), pltpu.SemaphoreType.DMA((n, 3))],
    )(*zones)


_HBM = pl.BlockSpec(memory_space=pltpu.HBM)
_SEM = pl.BlockSpec(memory_space=pltpu.SEMAPHORE)
_EFFECT = pltpu.SideEffectType.DATAFLOW_SIDE_EFFECTING


N_PEERS = 7
GATHER_FLIPS = (1, 2, 4, 6)


def _peers(flips=range(1, N_PEERS + 1)):
    px, py, pc = _place()
    flip = lambda p, bit: 1 - p if bit else p
    return [(flip(px, m >> 2 & 1), flip(py, m >> 1 & 1), flip(pc, m & 1)) for m in flips]


def _gather_copies(src_refs, zone_refs, send_sems, recv_sems):
    px, py, pc = _place()
    return [pltpu.make_async_remote_copy(src_ref=s, dst_ref=z.at[4 * px + 2 * py + pc], send_sem=send_sems.at[k],
                                         recv_sem=recv_sems.at[k], device_id=peer, device_id_type=MESH)
            for k, peer in enumerate(_peers(GATHER_FLIPS)) for s, z in zip(src_refs, zone_refs)]


def _scatter_copies(part_refs, zone_refs, send_sems, recv_sems):
    return [pltpu.make_async_remote_copy(src_ref=p.at[4 * peer[0] + 2 * peer[1] + peer[2]], dst_ref=z.at[k],
                                         send_sem=send_sems.at[k], recv_sem=recv_sems.at[k], device_id=peer, device_id_type=MESH)
            for k, peer in enumerate(_peers()) for p, z in zip(part_refs, zone_refs)]


def copies_start(make, srcs, zones, *, name):
    n = len(srcs)

    def body(*refs):
        for cp in make(refs[:n], refs[n:2 * n], refs[2 * n], refs[2 * n + 1]):
            cp.start()
        refs[-1][...] = jnp.zeros_like(refs[-1])

    arrays = list(srcs) + list(zones)
    outs = pl.pallas_call(
        body, name=name,
        out_shape=(pltpu.SemaphoreType.DMA((N_PEERS,)), pltpu.SemaphoreType.DMA((N_PEERS,)),
                   *[pltpu.HBM(a.shape, a.dtype) for a in arrays], jax.ShapeDtypeStruct((8, LANE), F32)),
        in_specs=[_HBM] * (2 * n), out_specs=(_SEM, _SEM, *[_HBM] * (2 * n), pl.BlockSpec(memory_space=pltpu.VMEM)),
        input_output_aliases={i: 2 + i for i in range(2 * n)},
        compiler_params=pltpu.CompilerParams(has_side_effects=_EFFECT),
    )(*[pltpu.with_memory_space_constraint(a, pltpu.HBM) for a in arrays])
    return outs[0], outs[1], list(outs[2:2 + n]), list(outs[2 + n:2 + 2 * n]), outs[-1]


def copies_wait(make, send_sems, recv_sems, srcs, zones, after, *, name):
    n = len(srcs)

    def body(*refs):
        for cp in make(refs[:n], refs[n:2 * n], refs[2 * n], refs[2 * n + 1]):
            cp.wait_send()
            cp.wait_recv()

    arrays = list(srcs) + list(zones)
    outs = pl.pallas_call(
        body, name=name, out_shape=tuple(pltpu.HBM(a.shape, a.dtype) for a in arrays),
        in_specs=[_HBM] * (2 * n) + [_SEM, _SEM, _ANY], out_specs=(_HBM,) * (2 * n),
        input_output_aliases={i: i for i in range(2 * n)},
        compiler_params=pltpu.CompilerParams(has_side_effects=_EFFECT),
    )(*arrays, send_sems, recv_sems, after)
    return list(outs[:n]), list(outs[n:])


def place_mine(zone, mine, *, name):
    C = mine.shape[-1]
    R = int(np.prod(mine.shape[:-1]))
    br = _row_block(R, C, mine.dtype.itemsize, budget=2**21)
    me = (4 * lax.axis_index("x") + 2 * lax.axis_index("y") + lax.axis_index("c")).astype(jnp.int32).reshape(1)

    def body(me_ref, m_ref, z_ref, o_ref):
        o_ref[0] = m_ref[...]

    out = pl.pallas_call(
        body, name=name, out_shape=jax.ShapeDtypeStruct((8, R, C), zone.dtype),
        grid_spec=pltpu.PrefetchScalarGridSpec(
            num_scalar_prefetch=1, grid=(R // br,), in_specs=[pl.BlockSpec((br, C), lambda i, me: (i, 0)), _ANY],
            out_specs=pl.BlockSpec((1, br, C), lambda i, me: (me[0], i, 0))),
        input_output_aliases={2: 0}, compiler_params=_params("parallel"),
    )(me, mine.reshape(R, C), zone.reshape(8, R, C))
    return out.reshape(zone.shape)


def _row_block(rows, cols, itemsize, budget=2**20):
    br = rows
    while br % 32 == 0 and br * cols * itemsize > budget:
        br //= 2
    return br


def _adamw_update(w, g, m, v):
    nm = ADAM_B1 * m + (1.0 - ADAM_B1) * g
    nv = ADAM_B2 * v + (1.0 - ADAM_B2) * (g * g)
    m_hat = nm / (1.0 - ADAM_B1 ** ADAM_STEP)
    v_hat = nv / (1.0 - ADAM_B2 ** ADAM_STEP)
    return -ADAM_LR * (m_hat / (jnp.sqrt(v_hat) + ADAM_EPS) + ADAM_WD * w), nm, nv


def grad_sum_adamw(parts, recvs, w, m, v, *, name):
    L, C = len(parts), w.shape[-1]
    R = int(np.prod(w.shape[1:-1]))
    br = _row_block(R, C, 4, budget=2**19)
    chip = (4 * lax.axis_index("x") + 2 * lax.axis_index("y") + lax.axis_index("c")).astype(jnp.int32).reshape(1)

    def body(c_ref, *refs):
        p_refs, r_refs = refs[:L], refs[L:2 * L]
        w_ref, m_ref, v_ref, g_out, d_out, nm_out, nv_out = refs[2 * L:]
        for j in range(L):
            @pl.when(pl.program_id(0) == j)
            def _(j=j):
                g = p_refs[j][0].astype(F32)
                for k in range(N_PEERS):
                    g = g + r_refs[j][k].astype(F32)
                g_out[0] = g
                d_out[0], nm_out[0], nv_out[0] = _adamw_update(w_ref[0], g, m_ref[0], v_ref[0])

    row = lambda j: (lambda l, r, c: jnp.where(l == j, r, 0))
    part_specs = [pl.BlockSpec((1, br, C), lambda l, r, c, f=row(j): (c[0], f(l, r, c), 0)) for j in range(L)]
    recv_specs = [pl.BlockSpec((N_PEERS, br, C), lambda l, r, c, f=row(j): (0, f(l, r, c), 0)) for j in range(L)]
    blk = pl.BlockSpec((1, br, C), lambda l, r, c: (l, r, 0))
    outs = pl.pallas_call(
        body, name=name, out_shape=tuple(jax.ShapeDtypeStruct((L, R, C), F32) for _ in range(4)),
        grid_spec=pltpu.PrefetchScalarGridSpec(
            num_scalar_prefetch=1, grid=(L, R // br), in_specs=part_specs + recv_specs + [blk] * 3, out_specs=(blk,) * 4),
        compiler_params=_params("arbitrary", "arbitrary"),
    )(chip, *[p.reshape(8, R, C) for p in parts], *[r.reshape(N_PEERS, R, C) for r in recvs],
      *[t.reshape(L, R, C) for t in (w, m, v)])
    return tuple(o.reshape(w.shape) for o in outs)


def ordered_sum(parts, *, name):
    _, R, C = parts.shape

    def body(p_ref, o_ref):
        acc = p_ref[0]
        for d in range(1, 8):
            acc = acc + p_ref[d]
        o_ref[...] = acc

    return pl.pallas_call(body, name=name, out_shape=jax.ShapeDtypeStruct((R, C), F32))(parts)


W_IN_COLS, W_IN_SHARD = 13128, 1641
W_IN_SEGMENTS = ((0, 832, 0), (832, 3904, Z_FOX), (3904, 3912, FF_COL), (3912, 6984, Z_CH), (6984, 13128, Z_GATE))


def col_gather(src, table, pieces, out_shape, *, name, tr=1024):
    R, C = src.shape[1:]
    tr = _tile(R, tr)
    width = 2 + 6 * pieces
    nb = table.shape[0] // width
    last_tile, last_valid = C // LANE, C % LANE

    def body(tab, *refs):
        o_ref = refs[-1]
        base = pl.program_id(1) * width
        lane = lax.broadcasted_iota(jnp.int32, (tr, LANE), 1)
        row = lax.broadcasted_iota(jnp.int32, (2 * LANE * pieces, LANE), 0)
        col = lax.broadcasted_iota(jnp.int32, (2 * LANE * pieces, LANE), 1)
        tiles, hit = [], None
        for p in range(pieces):
            e = base + 2 + 6 * p
            for tcol in (1, 2):
                x = refs[2 * p + tcol - 1][0]
                if last_valid:
                    x = jnp.where(jnp.logical_or(tab[e + tcol] < last_tile, lane < last_valid), x, jnp.zeros_like(x))
                tiles.append(x)
            lo, hi = tab[e + 4], tab[e + 5]
            cond = jnp.logical_and(row - 2 * LANE * p == col + tab[e + 3], jnp.logical_and(col >= lo, col < hi))
            hit = cond if hit is None else jnp.logical_or(hit, cond)
        sel = jnp.where(hit, 1.0, 0.0).astype(src.dtype)
        o_ref[0] = jnp.dot(jnp.concatenate(tiles, axis=1), sel, preferred_element_type=F32).astype(o_ref.dtype)

    in_specs = []
    for p in range(pieces):
        for tcol in (1, 2):
            in_specs.append(pl.BlockSpec(
                (1, tr, LANE), lambda i, b, tab, p=p, tcol=tcol: (tab[b * width + 2 + 6 * p], i, tab[b * width + 2 + 6 * p + tcol])))
    return pl.pallas_call(
        body, name=name, out_shape=jax.ShapeDtypeStruct(out_shape, src.dtype),
        grid_spec=pltpu.PrefetchScalarGridSpec(
            num_scalar_prefetch=1, grid=(R // tr, nb), in_specs=in_specs,
            out_specs=pl.BlockSpec((1, tr, LANE), lambda i, b, tab: (tab[b * width], i, tab[b * width + 1]))),
        compiler_params=_params("parallel", "parallel"),
    )(jnp.asarray(table, jnp.int32), *([src] * (2 * pieces)))


def _piece(sd, start, lo, hi, last_tile):
    t0 = start // LANE
    return [sd, t0, min(t0 + 1, last_tile), start % LANE - lo, lo, hi]


def _pad_pieces(rows, pieces):
    out, prev = [], [0, 0, 0, 0, 0, 0] * pieces
    for head, pcs in rows:
        full = list(pcs)
        for p in range(len(pcs) // 6, pieces):
            full += prev[6 * p:6 * p + 3] + [0, 0, 0]
        out.append(head + full)
        prev = full
    return np.asarray(out, np.int32).reshape(-1)


def _w_in_table(layer, L):
    rows = []
    for b in range(Z_W // LANE):
        pcs = []
        for first, last, col in W_IN_SEGMENTS:
            lo, hi = max(LANE * b, col), min(LANE * (b + 1), col + last - first)
            while lo < hi:
                c = first + lo - col
                n = min(hi - lo, W_IN_SHARD - c % W_IN_SHARD)
                pcs += _piece((c // W_IN_SHARD) * L + layer, c % W_IN_SHARD, lo - LANE * b, lo - LANE * b + n, W_IN_SHARD // LANE)
                lo += n
        assert len(pcs) <= 12
        rows.append(([0, b], pcs))
    return _pad_pieces(rows, 2)


def _w_in_grad_table():
    rows = []
    for d in range(8):
        for t in range(-(-W_IN_SHARD // LANE)):
            pcs = []
            c0 = d * W_IN_SHARD + LANE * t
            c1 = min(c0 + LANE, (d + 1) * W_IN_SHARD)
            for first, last, col in W_IN_SEGMENTS:
                lo, hi = max(c0, first), min(c1, last)
                if lo < hi:
                    pcs += _piece(0, col + lo - first, lo - c0, hi - c0, Z_W // LANE - 1)
            assert len(pcs) <= 18
            rows.append(([d, t], pcs))
    return _pad_pieces(rows, 3)


def block_copy(src, out_shape, in_blk, out_blk, grid, in_map, out_map, *, name):
    def body(x_ref, o_ref):
        o_ref[(0,) * (len(out_blk) - 2) + (Ellipsis,)] = x_ref[(0,) * (len(in_blk) - 2) + (Ellipsis,)]

    return pl.pallas_call(
        body, name=name, out_shape=jax.ShapeDtypeStruct(out_shape, src.dtype), grid=grid,
        in_specs=[pl.BlockSpec(in_blk, in_map)], out_specs=pl.BlockSpec(out_blk, out_map),
        compiler_params=_params("parallel", "parallel"),
    )(src)


def _columns_from_owners(z, *, name, lead=()):
    K, c = z.shape[-2:]
    tr, nl = _tile(K, 1024), len(lead)
    return block_copy(z, (K, 8 * c), (1,) * (1 + nl) + (tr, c), (tr, c), (8, K // tr),
                      lambda d, i: (d, *lead, i, 0), lambda d, i: (i, d), name=name)


def _owners_from_columns(g, *, name):
    K, c = g.shape[0], g.shape[1] // 8
    tr = _tile(K, 1024)
    return block_copy(g, (8, K, c), (tr, c), (1, tr, c), (8, K // tr), lambda d, i: (i, d), lambda d, i: (d, i, 0), name=name)


def _full_from_shards(k, sh, tag):
    if k not in COL_SHARDED:
        return sh.reshape((-1, sh.shape[-1]))
    if k == 'w_br':
        return [_columns_from_owners(sh, lead=(n,), name=f"{tag}_w_br{n}_layout") for n in range(3)]
    if k == 'w_uq':
        return _columns_from_owners(jnp.pad(sh, ((0, 0), (0, 0), (0, 64))), name=f"{tag}_w_uq_layout")
    if k == 'w_ukv':
        return block_copy(sh, (256, 2048), (1, 256, LANE), (256, LANE), (2, MLA_HEADS),
                          lambda t, h: (h, 0, t), lambda t, h: (0, t * MLA_HEADS + h), name=f"{tag}_w_ukv_layout")
    return _columns_from_owners(sh, name=f"{tag}_{k}_layout")


def _shards_from_full(k, g, tag):
    if k not in COL_SHARDED:
        return g.reshape((8, g.shape[0] // 8, g.shape[1]))
    if k == 'w_br':
        return jnp.stack([_owners_from_columns(g[n], name=f"{tag}_dw_br{n}_layout") for n in range(3)], axis=1)
    if k == 'w_uq':
        return _owners_from_columns(g, name=f"{tag}_dw_uq_layout")[:, :, :MLA_QK]
    if k == 'w_ukv':
        return block_copy(g, (8, 256, 256), (256, LANE), (1, 256, LANE), (2, MLA_HEADS),
                          lambda t, h: (0, t * MLA_HEADS + h), lambda t, h: (h, 0, t), name=f"{tag}_dw_ukv_layout")
    return _owners_from_columns(g, name=f"{tag}_d{k}_layout")


def w_in_full(gathered, layer, *, name):
    _, L, K, c = gathered.shape
    return col_gather(gathered.reshape(8 * L, K, c), _w_in_table(layer, L), 2, (1, K, Z_W), name=name)[0]


def w_in_shards(g, *, name):
    return col_gather(g[None], _w_in_grad_table(), 3, (8, g.shape[0], W_IN_SHARD), name=name)


def _layer_fwd(x, mem, W, P, cos, ssin, tag, later=None):
    S = x.shape[0]
    sv = {'x0': x}
    h = rmsnorm_fwd(x, P['g_mix'], name=f"{tag}_norm_mix")
    z = mm(h, W['w_in'], name=f"{tag}_mm_in")
    if later is not None:
        W = later(z)
    sv.update(h=h, z=z)
    cqn = rmsnorm_fwd(z, P['g_cq'], col=0, width=512, name=f"{tag}_norm_cq")
    ckvn = rmsnorm_fwd(z, P['g_ckv'], col=512, width=256, name=f"{tag}_norm_ckv")
    qf = mm(cqn, W['w_uq'], name=f"{tag}_mm_uq")
    kvf = mm(ckvn, W['w_ukv'], name=f"{tag}_mm_ukv")
    qa = mla_prep_fwd(qf, qf, P['g_mla_q'], cos, ssin, n_col=0, n_stride=2 * LANE, r_col=LANE, r_stride=2 * LANE,
                      heads=8, name=f"{tag}_mla_q")
    ka = mla_prep_fwd(kvf, z, P['g_mla_k'], cos, ssin, n_col=0, n_stride=LANE, r_col=KR_COL, r_stride=0,
                      heads=8, name=f"{tag}_mla_k")
    ya, lse_a = causal_attn_fwd(qa, ka, kvf, v_col=1024, chunked=True, scale=MLA_QK ** -0.5, name=f"{tag}_mla_attn")
    sv.update(cqn=cqn, ckvn=ckvn, qf=qf, kvf=kvf, qa=qa, ka=ka, lse_a=lse_a)
    qb = headnorm_fwd(z, P['g_fox_q'], col=Z_FOX, heads=8, name=f"{tag}_fox_qn")
    kb = headnorm_fwd(z, P['g_fox_k'], col=Z_FOX + 1024, heads=8, name=f"{tag}_fox_kn")
    fl = z[:, FF_COL:FF_COL + 8].T
    cum = foxgate_fwd(fl, P['b_f'], name=f"{tag}_fox_gate")
    cq, ck = cum.reshape(8, S, 1), cum.reshape(8, 1, S)
    yb, lse_b = causal_attn_fwd(qb, kb, z, v_col=Z_FOX + 2048, chunked=False, scale=LANE ** -0.5, cq=cq, ck=ck,
                                name=f"{tag}_fox_attn")
    sv.update(qb=qb, kb=kb, fl=fl, cq=cq, ck=ck, lse_b=lse_b)
    qc = headnorm_fwd(z, P['g_ch_q'], col=Z_CH, heads=8, name=f"{tag}_ch_qn")
    kc = headnorm_fwd(z, P['g_ch_k'], col=Z_CH + 1024, heads=8, name=f"{tag}_ch_kn")
    kcp = jnp.pad(kc, ((0, 0), (PAD, CHUNK), (0, 0)))
    vcp = jnp.pad(z[:, Z_CH + 2048:Z_CH + 3072], ((PAD, CHUNK), (0, 0)))
    bias = band_bias(P['rel_bias'], name=f"{tag}_ch_bias")
    yc = band_fwd(qc, kcp, vcp, bias, scale=LANE ** -0.5, name=f"{tag}_ch_attn")
    sv.update(qc=qc, kcp=kcp, vcp=vcp, bias=bias)
    ys = (ya, yb, yc)
    proj = [mm(ys[n], W['w_br'][n], name=f"{tag}_mm_br{n}") for n in range(3)]
    merged = gate_fwd(z, proj, name=f"{tag}_gate")
    x1 = mm(merged, W['w_out'], epi='add', aux=x, name=f"{tag}_mm_out")
    sv.update(ys=ys, proj=proj, merged=merged, x1=x1)
    hc = rmsnorm_fwd(x1, P['g_cross'], name=f"{tag}_norm_cross")
    memn = rmsnorm_fwd(mem, P['g_mem'], name=f"{tag}_norm_mem")
    qx_raw = mm(hc, W['w_xq'], name=f"{tag}_mm_xq")
    memkv = mm(memn, W['w_xkv'], name=f"{tag}_mm_xkv")
    qx = headnorm_fwd(qx_raw, P['g_x_q'], col=0, heads=4, name=f"{tag}_x_qn")
    kx = headnorm_fwd(memkv, P['g_x_k'], col=0, heads=4, name=f"{tag}_x_kn")
    ox = attn_fwd(qx, kx, memkv, v_col=512, scale=LANE ** -0.5, name=f"{tag}_x_attn")
    x2 = mm(ox, W['w_xo'], epi='add', aux=x1, name=f"{tag}_mm_xo")
    sv.update(hc=hc, memn=memn, qx_raw=qx_raw, memkv=memkv, qx=qx, kx=kx, ox=ox, x2=x2)
    hm = rmsnorm_fwd(x2, P['g_mlp'], name=f"{tag}_norm_mlp")
    u, a = mm(hm, W['w_1'], epi='relu2', out_dtype=BF16, name=f"{tag}_mm_w1")
    x3 = mm(a, W['w_2'], epi='add', aux=x2, name=f"{tag}_mm_w2")
    sv.update(hm=hm, u=u, a=a)
    return x3, sv


def _layer_bwd(dx, mem, W, P, sv, cos, ssin, tag, send_off=None):
    S = dx.shape[0]
    z = sv['z']
    gw, gs = {}, {}
    wgrad = lambda a, d, name: mm(a, d, ta=True, out_dtype=BF16, name=name)
    gw['w_2'] = wgrad(sv['a'], dx, f"{tag}_dw2")
    du = mm(dx, W['w_2'], tb=True, epi='mul_drelu2', aux=sv['u'], out_dtype=BF16, name=f"{tag}_du")
    gw['w_1'] = wgrad(sv['hm'], du, f"{tag}_dw1")
    dhm = mm(du, W['w_1'], tb=True, name=f"{tag}_dhm")
    dx, gs['g_mlp'] = rmsnorm_bwd(sv['x2'], P['g_mlp'], dhm, res=dx, name=f"{tag}_dnorm_mlp")
    gw['w_xo'] = wgrad(sv['ox'], dx, f"{tag}_dwxo")
    dox = mm(dx, W['w_xo'], tb=True, out_dtype=BF16, name=f"{tag}_dox")
    dqx, dkx, dvx = attn_bwd(sv['qx'], sv['kx'], sv['memkv'], dox, v_col=512, scale=LANE ** -0.5, name=f"{tag}_x_attn_bwd")
    dqx_raw, gs['g_x_q'] = headnorm_bwd(sv['qx_raw'], P['g_x_q'], dqx, col=0, heads=4, name=f"{tag}_x_qn_bwd")
    dkx_raw, gs['g_x_k'] = headnorm_bwd(sv['memkv'], P['g_x_k'], dkx, col=0, heads=4, name=f"{tag}_x_kn_bwd")
    dqx_b = dqx_raw.astype(BF16)
    gw['w_xq'] = wgrad(sv['hc'], dqx_b, f"{tag}_dwxq")
    dhc = mm(dqx_b, W['w_xq'], tb=True, name=f"{tag}_dhc")
    dx, gs['g_cross'] = rmsnorm_bwd(sv['x1'], P['g_cross'], dhc, res=dx, name=f"{tag}_dnorm_cross")
    dmemkv = jnp.concatenate([dkx_raw, dvx], axis=1).astype(BF16)
    gw['w_xkv'] = wgrad(sv['memn'], dmemkv, f"{tag}_dwxkv")
    dmemn = mm(dmemkv, W['w_xkv'], tb=True, name=f"{tag}_dmemn")
    _, gs['g_mem'] = rmsnorm_bwd(mem, P['g_mem'], dmemn, name=f"{tag}_dnorm_mem")
    gw['w_out'] = wgrad(sv['merged'], dx, f"{tag}_dwout")
    dmerged = mm(dx, W['w_out'], tb=True, name=f"{tag}_dmerged")
    dproj, dgl = gate_bwd(z, sv['proj'], dmerged, name=f"{tag}_gate_bwd")
    gw['w_br'] = [wgrad(sv['ys'][n], dproj[n], f"{tag}_dwbr{n}") for n in range(3)]
    dys = [mm(dproj[n], W['w_br'][n], tb=True, out_dtype=BF16, name=f"{tag}_dys{n}") for n in range(3)]
    dqa, dka, dva = causal_attn_bwd(sv['qa'], sv['ka'], sv['kvf'], sv['ys'][0], dys[0], sv['lse_a'], v_col=1024, chunked=True,
                                    scale=MLA_QK ** -0.5, name=f"{tag}_mla_attn_bwd")
    g_mla_q = P['g_mla_q'] if send_off is None else P['g_mla_q'] + send_off[0](gw)
    dqn, dqr, gs['g_mla_q'] = mla_prep_bwd(sv['qf'], sv['qf'], g_mla_q, cos, ssin, dqa, n_col=0, n_stride=2 * LANE,
                                           r_col=LANE, r_stride=2 * LANE, heads=8, name=f"{tag}_mla_q_bwd")
    dkn, dkr, gs['g_mla_k'] = mla_prep_bwd(sv['kvf'], z, P['g_mla_k'], cos, ssin, dka, n_col=0, n_stride=LANE,
                                           r_col=KR_COL, r_stride=0, heads=8, name=f"{tag}_mla_k_bwd")
    dqf = jnp.stack([dqn.reshape(S, 8, LANE), dqr.reshape(S, 8, LANE)], axis=2).reshape(S, 2048).astype(BF16)
    dkvf = jnp.concatenate([dkn, dva], axis=1).astype(BF16)
    gw['w_uq'] = wgrad(sv['cqn'], dqf, f"{tag}_dwuq")
    gw['w_ukv'] = wgrad(sv['ckvn'], dkvf, f"{tag}_dwukv")
    dcqn = mm(dqf, W['w_uq'], tb=True, name=f"{tag}_dcqn")
    dckvn = mm(dkvf, W['w_ukv'], tb=True, name=f"{tag}_dckvn")
    dcq_raw, gs['g_cq'] = rmsnorm_bwd(z, P['g_cq'], dcqn, col=0, width=512, name=f"{tag}_dnorm_cq")
    dckv_raw, gs['g_ckv'] = rmsnorm_bwd(z, P['g_ckv'], dckvn, col=512, width=256, name=f"{tag}_dnorm_ckv")
    dqb, dkb, dvb, dcq, dck = causal_attn_bwd(sv['qb'], sv['kb'], z, sv['ys'][1], dys[1], sv['lse_b'], v_col=Z_FOX + 2048,
                                              chunked=False, scale=LANE ** -0.5, cq=sv['cq'], ck=sv['ck'],
                                              name=f"{tag}_fox_attn_bwd")
    dqb_raw, gs['g_fox_q'] = headnorm_bwd(z, P['g_fox_q'], dqb, col=Z_FOX, heads=8, name=f"{tag}_fox_qn_bwd")
    dkb_raw, gs['g_fox_k'] = headnorm_bwd(z, P['g_fox_k'], dkb, col=Z_FOX + 1024, heads=8, name=f"{tag}_fox_kn_bwd")
    dfl, gs['b_f'] = foxgate_bwd(sv['fl'], P['b_f'], dcq.reshape(8, S) + dck.reshape(8, S), name=f"{tag}_fox_gate_bwd")
    dqc, dkcp, dvcp, dbias = band_bwd(sv['qc'], sv['kcp'], sv['vcp'], sv['bias'], dys[2], scale=LANE ** -0.5,
                                      name=f"{tag}_ch_attn_bwd")
    dqc_raw, gs['g_ch_q'] = headnorm_bwd(z, P['g_ch_q'], dqc, col=Z_CH, heads=8, name=f"{tag}_ch_qn_bwd")
    dkc_raw, gs['g_ch_k'] = headnorm_bwd(z, P['g_ch_k'], dkcp[:, PAD:PAD + S, :], col=Z_CH + 1024, heads=8,
                                         name=f"{tag}_ch_kn_bwd")
    gs['rel_bias'] = relbias_bwd(dbias, name=f"{tag}_relbias_bwd")
    b16 = lambda t: t.astype(BF16)
    dz = jnp.concatenate([b16(dcq_raw), b16(dckv_raw), b16(dkr), b16(dfl.T), jnp.zeros((S, 120), BF16),
                          b16(dqb_raw), b16(dkb_raw), b16(dvb), b16(dqc_raw), b16(dkc_raw), b16(dvcp[PAD:PAD + S]),
                          dgl[0], dgl[1], dgl[2]], axis=1)
    gw['w_in'] = wgrad(sv['h'], dz, f"{tag}_dwin")
    dh = mm(dz, W['w_in'], tb=True, name=f"{tag}_dh")
    g_mix = P['g_mix'] if send_off is None else P['g_mix'] + send_off[1](gw)
    dx, gs['g_mix'] = rmsnorm_bwd(sv['x0'], g_mix, dh, res=dx, name=f"{tag}_dnorm_mix")
    return dx, gw, gs


def _local_step(x, mem, target, Ws, Ps):
    S = x.shape[0]
    cos, ssin = _rope_tables(S)
    L = len(Ws)
    saved = []
    for l in range(L):
        x, sv = _layer_fwd(x, mem, Ws[l], Ps[l], cos, ssin, f"l{l}")
        saved.append(sv)
    loss, dx = loss_head(x, target, name="loss_head")
    gws, gss = [None] * L, [None] * L
    for l in reversed(range(L)):
        dx, gws[l], gss[l] = _layer_bwd(dx, mem, Ws[l], Ps[l], saved[l], cos, ssin, f"l{l}")
    return loss, dx, gws, gss


def _pack_small(d):
    flat = jnp.concatenate([d[k].reshape(-1) for k in SMALL])
    n = flat.shape[0]
    rows = -(-n // (8 * LANE)) * 8
    return jnp.pad(flat, (0, rows * LANE - n)).reshape(rows, LANE)


def _unpack_small(packed, like):
    flat, out, off = packed.reshape(-1), {}, 0
    for k in SMALL:
        n = int(np.prod(like[k].shape))
        out[k] = flat[off:off + n].reshape(like[k].shape)
        off += n
    return out


def kernel(x, mem, g_mix, w_in, g_cq, w_uq, g_ckv, w_ukv, g_mla_q, g_mla_k, b_f, g_fox_q, g_fox_k, rel_bias, g_ch_q, g_ch_k, w_br, w_out, g_cross, g_mem, w_xq, w_xkv, g_x_q, g_x_k, w_xo, g_mlp, w_1, w_2, loss_target, m_g_mix, m_w_in, m_g_cq, m_w_uq, m_g_ckv, m_w_ukv, m_g_mla_q, m_g_mla_k, m_b_f, m_g_fox_q, m_g_fox_k, m_rel_bias, m_g_ch_q, m_g_ch_k, m_w_br, m_w_out, m_g_cross, m_g_mem, m_w_xq, m_w_xkv, m_g_x_q, m_g_x_k, m_w_xo, m_g_mlp, m_w_1, m_w_2, v_g_mix, v_w_in, v_g_cq, v_w_uq, v_g_ckv, v_w_ukv, v_g_mla_q, v_g_mla_k, v_b_f, v_g_fox_q, v_g_fox_k, v_rel_bias, v_g_ch_q, v_g_ch_k, v_w_br, v_w_out, v_g_cross, v_g_mem, v_w_xq, v_w_xkv, v_g_x_q, v_g_x_k, v_w_xo, v_g_mlp, v_w_1, v_w_2):
    args = locals()
    w = {k: args[k] for k in WEIGHTS}
    m = {k: args['m_' + k] for k in WEIGHTS}
    v = {k: args['v_' + k] for k in WEIGHTS}
    L = w_in.shape[0]

    Ps = [{k: w[k][l] for k in SMALL} for l in range(L)]
    xs, memv = x[0], mem[0]
    cos, ssin = _rope_tables(xs.shape[0])

    gathers = {}
    for l in range(L):
        for group in AG_GROUPS:
            shards = [w[k][l].astype(BF16) for k in group]
            gathers[l, group] = copies_start(_gather_copies, shards, [lax.empty((8,) + s.shape, s.dtype) for s in shards],
                                             name=f"l{l}_ag_start_{group[0]}")
    Ps[0]['g_mix'] = Ps[0]['g_mix'] + sum(g[4][0, :1] for g in gathers.values())

    def arrived(l, group, after):
        send_sems, recv_sems, shards, zones, _ = gathers[l, group]
        shards, zones = copies_wait(_gather_copies, send_sems, recv_sems, shards, zones, after, name=f"l{l}_ag_wait_{group[0]}")
        zones = forward_to_sibling(zones, name=f"l{l}_ag_forward_{group[0]}")
        zones = [place_mine(z, s, name=f"l{l}_{k}_mine") for k, z, s in zip(group, zones, shards)]
        return {k: w_in_full(z[:, None], 0, name=f"l{l}_w_in_layout") if k == 'w_in' else _full_from_shards(k, z, f"l{l}")
                for k, z in zip(group, zones)}

    Ws, saved = [], []
    for l in range(L):
        Ws.append(arrived(l, AG_GROUPS[0], xs if l else Ps[0]['g_mix']))

        def later(anchor, l=l):
            Ws[l].update(arrived(l, AG_GROUPS[1], anchor))
            return Ws[l]

        xs, sv = _layer_fwd(xs, memv, Ws[l], Ps[l], cos, ssin, f"l{l}", later)
        saved.append(sv)

    loss, dx = loss_head(xs, loss_target[0], name="loss_head")
    loss = lax.psum(loss[0, 0], ("x", "y", "c"))

    gss = [None] * L
    scatters = {group: [None] * L for group in RS_GROUPS}
    for l in reversed(range(L)):
        def send_off(gw, group, l=l):
            gdst = [w_in_shards(gw[k], name=f"l{l}_dw_in_layout") if k == 'w_in' else _shards_from_full(k, gw[k], f"l{l}")
                    for k in group]
            started = copies_start(_scatter_copies, gdst, [lax.empty((N_PEERS,) + g.shape[1:], g.dtype) for g in gdst],
                                   name=f"l{l}_rs_start_{group[0]}")
            scatters[group][l] = started[:4]
            return started[4][0, :1]

        hooks = tuple((lambda gw, group=group: send_off(gw, group)) for group in RS_GROUPS)
        dx, _, gss[l] = _layer_bwd(dx, memv, Ws[l], Ps[l], saved[l], cos, ssin, f"l{l}", hooks)
    grad_x = dx

    grads, delta, new_m, new_v = {}, {}, {}, {}
    after = grad_x
    for group in RS_GROUPS:
        done = [copies_wait(_scatter_copies, *scatters[group][l], after, name=f"l{l}_rs_wait_{group[0]}") for l in range(L)]
        for t, k in enumerate(group):
            grads[k], delta[k], new_m[k], new_v[k] = grad_sum_adamw(
                [done[l][0][t] for l in range(L)], [done[l][1][t] for l in range(L)], w[k], m[k], v[k], name=f"adamw_{k}")
        after = sum(delta[k][(0,) * (delta[k].ndim - 1)][:1] for k in group)

    small_part = _pack_small({k: jnp.stack([gss[l][k] for l in range(L)]) for k in SMALL})
    small_all = all_gather([small_part], after, name="ag_small")[0]
    grads.update(_unpack_small(ordered_sum(small_all, name="small_sum"), {k: w[k] for k in SMALL}))
    sd, sm, sv_ = adamw(_pack_small({k: w[k] for k in SMALL}), _pack_small({k: grads[k] for k in SMALL}),
                        _pack_small({k: m[k] for k in SMALL}), _pack_small({k: v[k] for k in SMALL}), name="adamw_small")
    like = {k: w[k] for k in SMALL}
    delta.update(_unpack_small(sd, like))
    new_m.update(_unpack_small(sm, like))
    new_v.update(_unpack_small(sv_, like))

    return (loss, grad_x[None], *[grads[k] for k in WEIGHTS], *[delta[k] for k in WEIGHTS],
            *[new_m[k] for k in WEIGHTS], *[new_v[k] for k in WEIGHTS])
```

```python
import numpy as np
import jax
import jax.numpy as jnp
from jax import lax
from jax.experimental import pallas as pl
from jax.experimental.pallas import tpu as pltpu

F32, BF16 = jnp.float32, jnp.bfloat16
EPS = 1e-6
NEG = -1e30
LANE = 128
VMEM_LIMIT_BYTES = 56 * 2**20
MESH = pl.DeviceIdType.MESH

D_MODEL = 2048
CHUNK = 64
BAND = 9 * CHUNK
PAD = 8 * CHUNK
REL_CLIP = 128
MLA_HEADS, MLA_NOPE, MLA_ROPE, MLA_QK = 8, 128, 64, 192
N_HEADS = 8
X_HEADS = 4
ROPE_THETA = 10000.0
ADAM_LR, ADAM_B1, ADAM_B2, ADAM_EPS, ADAM_WD, ADAM_STEP = 0.001, 0.9, 0.999, 1e-08, 0.01, 10

Z_MAIN, Z_FOX, Z_CH, Z_GATE, Z_W = 0, 1024, 4096, 7168, 13312
KR_COL, FF_COL = 768, 896

BIG = ('w_in', 'w_uq', 'w_ukv', 'w_br', 'w_out', 'w_xq', 'w_xkv', 'w_xo', 'w_1', 'w_2')
COL_SHARDED = ('w_in', 'w_uq', 'w_ukv', 'w_br', 'w_xo', 'w_1')
RS_GROUPS = (('w_2', 'w_1', 'w_xo', 'w_xq', 'w_xkv', 'w_out', 'w_br'), ('w_uq', 'w_ukv', 'w_in'))
AG_GROUPS = (('w_in',), ('w_uq', 'w_ukv', 'w_br', 'w_out', 'w_xq', 'w_xkv', 'w_xo', 'w_1', 'w_2'))
SMALL = ('g_mix', 'g_cq', 'g_ckv', 'g_mla_q', 'g_mla_k', 'b_f', 'g_fox_q', 'g_fox_k', 'rel_bias', 'g_ch_q',
         'g_ch_k', 'g_cross', 'g_mem', 'g_x_q', 'g_x_k', 'g_mlp')
WEIGHTS = ('g_mix', 'w_in', 'g_cq', 'w_uq', 'g_ckv', 'w_ukv', 'g_mla_q', 'g_mla_k', 'b_f', 'g_fox_q', 'g_fox_k',
           'rel_bias', 'g_ch_q', 'g_ch_k', 'w_br', 'w_out', 'g_cross', 'g_mem', 'w_xq', 'w_xkv', 'g_x_q', 'g_x_k',
           'w_xo', 'g_mlp', 'w_1', 'w_2')


def _params(*sem):
    return pltpu.CompilerParams(dimension_semantics=sem, vmem_limit_bytes=VMEM_LIMIT_BYTES)


def _tile(dim, pref):
    if dim <= pref:
        return dim
    for t in range(pref - pref % LANE, 0, -LANE):
        if dim % t == 0:
            return t
    raise ValueError((dim, pref))


def mm(a, b, *, ta=False, tb=False, out_dtype=F32, epi=None, aux=None, name, tm=1024, tn=512, tk=2048):
    M, K = (a.shape[1], a.shape[0]) if ta else a.shape
    N = b.shape[0] if tb else b.shape[1]
    assert (b.shape[1] if tb else b.shape[0]) == K, (a.shape, b.shape, ta, tb)
    tm, tn, tk = _tile(M, tm), _tile(N, tn), _tile(K, tk)
    nk = K // tk
    dn = (((0 if ta else 1,), (1 if tb else 0,)), ((), ()))
    n_aux = 0 if aux is None else 1

    def finish(acc, aux_refs, o_refs):
        if epi is None:
            o_refs[0][...] = acc.astype(o_refs[0].dtype)
        elif epi == 'add':
            o_refs[0][...] = (acc + aux_refs[0][...]).astype(o_refs[0].dtype)
        elif epi == 'relu2':
            o_refs[0][...] = acc
            r = jnp.maximum(acc, 0.0)
            o_refs[1][...] = (r * r).astype(o_refs[1].dtype)
        elif epi == 'mul_drelu2':
            o_refs[0][...] = (acc * (2.0 * jnp.maximum(aux_refs[0][...], 0.0))).astype(o_refs[0].dtype)

    def body(a_ref, b_ref, *rest):
        aux_refs = rest[:n_aux]
        o_refs = rest[n_aux:n_aux + (2 if epi == 'relu2' else 1)]
        part = lax.dot_general(a_ref[...].astype(BF16), b_ref[...].astype(BF16), dn, preferred_element_type=F32)
        if nk == 1:
            finish(part, aux_refs, o_refs)
        else:
            acc_ref = rest[-1]
            k = pl.program_id(2)

            @pl.when(k == 0)
            def _():
                acc_ref[...] = part

            @pl.when(k > 0)
            def _():
                acc_ref[...] += part

            @pl.when(k == nk - 1)
            def _():
                finish(acc_ref[...], aux_refs, o_refs)

    a_spec = pl.BlockSpec((tk, tm), lambda i, j, k: (k, i)) if ta else pl.BlockSpec((tm, tk), lambda i, j, k: (i, k))
    b_spec = pl.BlockSpec((tn, tk), lambda i, j, k: (j, k)) if tb else pl.BlockSpec((tk, tn), lambda i, j, k: (k, j))
    o_spec = pl.BlockSpec((tm, tn), lambda i, j, k: (i, j))
    if epi == 'relu2':
        out_shape = (jax.ShapeDtypeStruct((M, N), F32), jax.ShapeDtypeStruct((M, N), out_dtype))
        out_specs = (o_spec, o_spec)
    else:
        out_shape, out_specs = jax.ShapeDtypeStruct((M, N), out_dtype), o_spec
    return pl.pallas_call(
        body, name=name, out_shape=out_shape, grid=(M // tm, N // tn, nk),
        in_specs=[a_spec, b_spec] + [o_spec] * n_aux, out_specs=out_specs,
        scratch_shapes=[pltpu.VMEM((tm, tn), F32)] if nk > 1 else [],
        compiler_params=_params("parallel", "parallel", "arbitrary"),
    )(a, b, *([aux] if n_aux else []))


def rmsnorm_fwd(x, g, *, col=0, width=None, out_dtype=BF16, name, ts=256):
    S = x.shape[0]
    width = x.shape[1] if width is None else width
    ts, cb = _tile(S, ts), col // width

    def body(x_ref, g_ref, o_ref):
        xf = x_ref[...]
        r = lax.rsqrt(jnp.mean(xf * xf, axis=-1, keepdims=True) + EPS)
        o_ref[...] = (xf * r * g_ref[...]).astype(o_ref.dtype)

    return pl.pallas_call(
        body, name=name, out_shape=jax.ShapeDtypeStruct((S, width), out_dtype), grid=(S // ts,),
        in_specs=[pl.BlockSpec((ts, width), lambda i: (i, cb)), pl.BlockSpec((1, width), lambda i: (0, 0))],
        out_specs=pl.BlockSpec((ts, width), lambda i: (i, 0)), compiler_params=_params("parallel"),
    )(x, g.reshape(1, width))


def rmsnorm_bwd(x, g, dy, *, col=0, width=None, res=None, name, ts=256):
    S = x.shape[0]
    width = x.shape[1] if width is None else width
    ts, cb = _tile(S, ts), col // width
    has_res = res is not None

    def body(x_ref, g_ref, dy_ref, *rest):
        dx_ref, dg_ref = rest[-2:]
        xf = x_ref[...]
        r = lax.rsqrt(jnp.mean(xf * xf, axis=-1, keepdims=True) + EPS)
        dyf = dy_ref[...].astype(F32)
        dyg = dyf * g_ref[...]
        dx = r * dyg - xf * (r * r * r) * jnp.mean(dyg * xf, axis=-1, keepdims=True)
        if has_res:
            dx = dx + rest[0][...]
        dx_ref[...] = dx
        part = jnp.sum(dyf * xf * r, axis=0, keepdims=True)

        @pl.when(pl.program_id(0) == 0)
        def _():
            dg_ref[...] = part

        @pl.when(pl.program_id(0) > 0)
        def _():
            dg_ref[...] += part

    blk = pl.BlockSpec((ts, width), lambda i: (i, 0))
    dx, dg = pl.pallas_call(
        body, name=name,
        out_shape=(jax.ShapeDtypeStruct((S, width), F32), jax.ShapeDtypeStruct((1, width), F32)), grid=(S // ts,),
        in_specs=[pl.BlockSpec((ts, width), lambda i: (i, cb)), pl.BlockSpec((1, width), lambda i: (0, 0)), blk]
        + ([blk] if has_res else []),
        out_specs=(blk, pl.BlockSpec((1, width), lambda i: (0, 0))), compiler_params=_params("arbitrary"),
    )(x, g.reshape(1, width), dy, *([res] if has_res else []))
    return dx, dg.reshape(width)


def headnorm_fwd(x, g, *, col, heads, name, ts=1024):
    S = x.shape[0]
    ts, cb = _tile(S, ts), col // LANE

    def body(x_ref, g_ref, o_ref):
        xf = x_ref[...]
        r = lax.rsqrt(jnp.mean(xf * xf, axis=-1, keepdims=True) + EPS)
        o_ref[0] = (xf * r * g_ref[...]).astype(o_ref.dtype)

    return pl.pallas_call(
        body, name=name, out_shape=jax.ShapeDtypeStruct((heads, S, LANE), BF16), grid=(heads, S // ts),
        in_specs=[pl.BlockSpec((ts, LANE), lambda h, i: (i, cb + h)), pl.BlockSpec((1, LANE), lambda h, i: (0, 0))],
        out_specs=pl.BlockSpec((1, ts, LANE), lambda h, i: (h, i, 0)), compiler_params=_params("parallel", "parallel"),
    )(x, g.reshape(1, LANE))


def headnorm_bwd(x, g, dy, *, col, heads, name, ts=1024):
    S = x.shape[0]
    ts, cb = _tile(S, ts), col // LANE

    def body(x_ref, g_ref, dy_ref, dx_ref, dg_ref):
        xf = x_ref[...]
        r = lax.rsqrt(jnp.mean(xf * xf, axis=-1, keepdims=True) + EPS)
        dyf = dy_ref[0]
        dyg = dyf * g_ref[...]
        dx_ref[...] = r * dyg - xf * (r * r * r) * jnp.mean(dyg * xf, axis=-1, keepdims=True)
        part = jnp.sum(dyf * xf * r, axis=0, keepdims=True)
        first = jnp.logical_and(pl.program_id(0) == 0, pl.program_id(1) == 0)

        @pl.when(first)
        def _():
            dg_ref[...] = part

        @pl.when(jnp.logical_not(first))
        def _():
            dg_ref[...] += part

    dx, dg = pl.pallas_call(
        body, name=name,
        out_shape=(jax.ShapeDtypeStruct((S, heads * LANE), F32), jax.ShapeDtypeStruct((1, LANE), F32)),
        grid=(heads, S // ts),
        in_specs=[pl.BlockSpec((ts, LANE), lambda h, i: (i, cb + h)), pl.BlockSpec((1, LANE), lambda h, i: (0, 0)),
                  pl.BlockSpec((1, ts, LANE), lambda h, i: (h, i, 0))],
        out_specs=(pl.BlockSpec((ts, LANE), lambda h, i: (i, h)), pl.BlockSpec((1, LANE), lambda h, i: (0, 0))),
        compiler_params=_params("arbitrary", "arbitrary"),
    )(x, g.reshape(1, LANE), dy)
    return dx, dg.reshape(LANE)


def _rope_tables(S):
    pos = jnp.arange(S, dtype=F32)
    inv = ROPE_THETA ** (-jnp.arange(0, MLA_ROPE, 2, dtype=F32) / MLA_ROPE)
    ang = pos[:, None] * inv[None, :]
    c, s, z = jnp.cos(ang), jnp.sin(ang), jnp.zeros((S, 64), F32)
    return jnp.concatenate([c, c, z], axis=1), jnp.concatenate([-s, s, z], axis=1)


def _rope(v, cos, ssin, lane):
    partner = jnp.where(lane < 32, pltpu.roll(v, 96, 1), pltpu.roll(v, 32, 1))
    return v * cos + partner * ssin


def mla_prep_fwd(xn, xr, g, cos, ssin, *, n_col, n_stride, r_col, r_stride, heads, name, ts=1024):
    S = xn.shape[0]
    ts = _tile(S, ts)
    nb, ns, rb, rs = n_col // LANE, n_stride // LANE, r_col // LANE, r_stride // LANE
    gn = g[:MLA_NOPE].reshape(1, LANE)
    gr = jnp.concatenate([g[MLA_NOPE:], jnp.zeros((64,), F32)]).reshape(1, LANE)

    def body(n_ref, r_ref, gn_ref, gr_ref, c_ref, s_ref, o_ref):
        n, rr = n_ref[...], r_ref[...]
        ss = jnp.sum(n * n, axis=-1, keepdims=True) + jnp.sum(rr * rr, axis=-1, keepdims=True)
        r = lax.rsqrt(ss * (1.0 / MLA_QK) + EPS)
        lane = lax.broadcasted_iota(jnp.int32, rr.shape, 1)
        o_ref[0, :, :LANE] = (n * r * gn_ref[...]).astype(o_ref.dtype)
        o_ref[0, :, LANE:] = _rope(rr * r * gr_ref[...], c_ref[...], s_ref[...], lane).astype(o_ref.dtype)

    row = lambda h, i: (0, 0)
    return pl.pallas_call(
        body, name=name, out_shape=jax.ShapeDtypeStruct((heads, S, 2 * LANE), BF16), grid=(heads, S // ts),
        in_specs=[pl.BlockSpec((ts, LANE), lambda h, i: (i, nb + ns * h)),
                  pl.BlockSpec((ts, LANE), lambda h, i: (i, rb + rs * h)),
                  pl.BlockSpec((1, LANE), row), pl.BlockSpec((1, LANE), row),
                  pl.BlockSpec((ts, LANE), lambda h, i: (i, 0)), pl.BlockSpec((ts, LANE), lambda h, i: (i, 0))],
        out_specs=pl.BlockSpec((1, ts, 2 * LANE), lambda h, i: (h, i, 0)),
        compiler_params=_params("parallel", "parallel"),
    )(xn, xr, gn, gr, cos, ssin)


def mla_prep_bwd(xn, xr, g, cos, ssin, dy, *, n_col, n_stride, r_col, r_stride, heads, name, ts=1024):
    S = xn.shape[0]
    ts = _tile(S, ts)
    nb, ns, rb, rs = n_col // LANE, n_stride // LANE, r_col // LANE, r_stride // LANE
    shared = r_stride == 0
    gn = g[:MLA_NOPE].reshape(1, LANE)
    gr = jnp.concatenate([g[MLA_NOPE:], jnp.zeros((64,), F32)]).reshape(1, LANE)

    def body(n_ref, r_ref, gn_ref, gr_ref, c_ref, s_ref, dy_ref, dn_ref, dr_ref, dgn_ref, dgr_ref):
        i, h = pl.program_id(0), pl.program_id(1)
        n, rr = n_ref[...], r_ref[...]
        ss = jnp.sum(n * n, axis=-1, keepdims=True) + jnp.sum(rr * rr, axis=-1, keepdims=True)
        r = lax.rsqrt(ss * (1.0 / MLA_QK) + EPS)
        lane = lax.broadcasted_iota(jnp.int32, rr.shape, 1)
        dyn = dy_ref[0, :, :LANE]
        dyr = dy_ref[0, :, LANE:]
        t = dyr * s_ref[...]
        dvr = dyr * c_ref[...] + jnp.where(lane < 32, pltpu.roll(t, 96, 1), pltpu.roll(t, 32, 1))
        dvr = jnp.where(lane < 64, dvr, 0.0)
        dgn_part = jnp.sum(dyn * n * r, axis=0, keepdims=True)
        dgr_part = jnp.sum(dvr * rr * r, axis=0, keepdims=True)
        dyn_g, dvr_g = dyn * gn_ref[...], dvr * gr_ref[...]
        proj = (jnp.sum(dyn_g * n, axis=-1, keepdims=True) + jnp.sum(dvr_g * rr, axis=-1, keepdims=True)) * (1.0 / MLA_QK)
        r3 = r * r * r
        dn_ref[...] = r * dyn_g - n * r3 * proj
        dr = r * dvr_g - rr * r3 * proj
        if shared:
            @pl.when(h == 0)
            def _():
                dr_ref[...] = dr

            @pl.when(h > 0)
            def _():
                dr_ref[...] += dr
        else:
            dr_ref[...] = dr
        first = jnp.logical_and(i == 0, h == 0)

        @pl.when(first)
        def _():
            dgn_ref[...] = dgn_part
            dgr_ref[...] = dgr_part

        @pl.when(jnp.logical_not(first))
        def _():
            dgn_ref[...] += dgn_part
            dgr_ref[...] += dgr_part

    row = lambda i, h: (0, 0)
    dr_cols = LANE if shared else heads * LANE
    dn, dr, dgn, dgr = pl.pallas_call(
        body, name=name,
        out_shape=(jax.ShapeDtypeStruct((S, heads * LANE), F32), jax.ShapeDtypeStruct((S, dr_cols), F32),
                   jax.ShapeDtypeStruct((1, LANE), F32), jax.ShapeDtypeStruct((1, LANE), F32)),
        grid=(S // ts, heads),
        in_specs=[pl.BlockSpec((ts, LANE), lambda i, h: (i, nb + ns * h)),
                  pl.BlockSpec((ts, LANE), lambda i, h: (i, rb + rs * h)),
                  pl.BlockSpec((1, LANE), row), pl.BlockSpec((1, LANE), row),
                  pl.BlockSpec((ts, LANE), lambda i, h: (i, 0)), pl.BlockSpec((ts, LANE), lambda i, h: (i, 0)),
                  pl.BlockSpec((1, ts, 2 * LANE), lambda i, h: (h, i, 0))],
        out_specs=(pl.BlockSpec((ts, LANE), lambda i, h: (i, h)),
                   pl.BlockSpec((ts, LANE), (lambda i, h: (i, 0)) if shared else (lambda i, h: (i, h))),
                   pl.BlockSpec((1, LANE), row), pl.BlockSpec((1, LANE), row)),
        compiler_params=_params("arbitrary", "arbitrary"),
    )(xn, xr, gn, gr, cos, ssin, dy)
    return dn, dr, jnp.concatenate([dgn.reshape(LANE), dgr.reshape(LANE)[:MLA_ROPE]])


_NT = (((1,), (1,)), ((), ()))
_TN = (((0,), (0,)), ((), ()))


def attn_fwd(q, k, v, *, v_col, scale, name, bq=256):
    H, S, dk = q.shape
    Sk = k.shape[1]
    bq, vb = _tile(S, bq), v_col // LANE

    def body(q_ref, k_ref, v_ref, o_ref):
        s = lax.dot_general(q_ref[0], k_ref[0], _NT, preferred_element_type=F32) * scale
        e = jnp.exp(s - jnp.max(s, axis=-1, keepdims=True))
        p = e * (1.0 / jnp.sum(e, axis=-1, keepdims=True))
        o_ref[...] = jnp.dot(p.astype(BF16), v_ref[...].astype(BF16), preferred_element_type=F32).astype(o_ref.dtype)

    return pl.pallas_call(
        body, name=name, out_shape=jax.ShapeDtypeStruct((S, H * LANE), BF16), grid=(H, S // bq),
        in_specs=[pl.BlockSpec((1, bq, dk), lambda h, i: (h, i, 0)), pl.BlockSpec((1, Sk, dk), lambda h, i: (h, 0, 0)),
                  pl.BlockSpec((Sk, LANE), lambda h, i: (0, vb + h))],
        out_specs=pl.BlockSpec((bq, LANE), lambda h, i: (i, h)), compiler_params=_params("parallel", "parallel"),
    )(q, k, v)


def attn_bwd(q, k, v, do, *, v_col, scale, name, bq=256):
    H, S, dk = q.shape
    Sk = k.shape[1]
    bq, vb = _tile(S, bq), v_col // LANE

    def body(q_ref, k_ref, v_ref, do_ref, dq_ref, dk_ref, dv_ref):
        i = pl.program_id(1)
        qb, kb, vv = q_ref[0], k_ref[0], v_ref[...].astype(BF16)
        s = lax.dot_general(qb, kb, _NT, preferred_element_type=F32) * scale
        e = jnp.exp(s - jnp.max(s, axis=-1, keepdims=True))
        p = e * (1.0 / jnp.sum(e, axis=-1, keepdims=True))
        dob = do_ref[...].astype(BF16)
        dv_part = lax.dot_general(p.astype(BF16), dob, _TN, preferred_element_type=F32)
        dp = lax.dot_general(dob, vv, _NT, preferred_element_type=F32)
        ds = p * (dp - jnp.sum(p * dp, axis=-1, keepdims=True))
        dsb = (ds * scale).astype(BF16)
        dq_ref[0] = jnp.dot(dsb, kb, preferred_element_type=F32)
        dk_part = lax.dot_general(dsb, qb, _TN, preferred_element_type=F32)

        @pl.when(i == 0)
        def _():
            dk_ref[0] = dk_part
            dv_ref[...] = dv_part

        @pl.when(i > 0)
        def _():
            dk_ref[0] += dk_part
            dv_ref[...] += dv_part

    return pl.pallas_call(
        body, name=name,
        out_shape=(jax.ShapeDtypeStruct((H, S, dk), F32), jax.ShapeDtypeStruct((H, Sk, dk), F32),
                   jax.ShapeDtypeStruct((Sk, H * LANE), F32)),
        grid=(H, S // bq),
        in_specs=[pl.BlockSpec((1, bq, dk), lambda h, i: (h, i, 0)), pl.BlockSpec((1, Sk, dk), lambda h, i: (h, 0, 0)),
                  pl.BlockSpec((Sk, LANE), lambda h, i: (0, vb + h)), pl.BlockSpec((bq, LANE), lambda h, i: (i, h))],
        out_specs=(pl.BlockSpec((1, bq, dk), lambda h, i: (h, i, 0)), pl.BlockSpec((1, Sk, dk), lambda h, i: (h, 0, 0)),
                   pl.BlockSpec((Sk, LANE), lambda h, i: (0, h))),
        compiler_params=_params("parallel", "arbitrary"),
    )(q, k, v, do)


def _causal_scores(q, kblk, i, start, blk, scale, chunked, cq, ckblk):
    s = lax.dot_general(q, kblk, _NT, preferred_element_type=F32) * scale
    if cq is not None:
        s = s + cq - ckblk
    qpos = i * blk + lax.broadcasted_iota(jnp.int32, s.shape, 0)
    kpos = start + lax.broadcasted_iota(jnp.int32, s.shape, 1)
    ok = (kpos >> 6) <= (qpos >> 6) if chunked else kpos <= qpos
    return jnp.where(ok, s, NEG)


def causal_attn_fwd(q, k, v, *, v_col, chunked, scale, cq=None, ck=None, name, blk=256):
    H, S, dk = q.shape
    blk, vb = _tile(S, blk), v_col // LANE
    fox = cq is not None

    def body(q_ref, k_ref, v_ref, *rest):
        o_ref, lse_ref = rest[-2:]
        s = _causal_scores(q_ref[0], k_ref[0], pl.program_id(1), 0, blk, scale, chunked,
                           rest[0][0] if fox else None, rest[1][0] if fox else None)
        m = jnp.max(s, axis=-1, keepdims=True)
        p = jnp.exp(s - m)
        l = jnp.sum(p, axis=-1, keepdims=True)
        pv = jnp.dot(p.astype(BF16), v_ref[...].astype(BF16), preferred_element_type=F32)
        o_ref[...] = (pv * (1.0 / l)).astype(o_ref.dtype)
        lse_ref[0] = m + jnp.log(l)

    in_specs = [pl.BlockSpec((1, blk, dk), lambda h, i: (h, i, 0)), pl.BlockSpec((1, S, dk), lambda h, i: (h, 0, 0)),
                pl.BlockSpec((S, LANE), lambda h, i: (0, vb + h))]
    args = [q, k, v]
    if fox:
        in_specs += [pl.BlockSpec((1, blk, 1), lambda h, i: (h, i, 0)), pl.BlockSpec((1, 1, S), lambda h, i: (h, 0, 0))]
        args += [cq, ck]
    return pl.pallas_call(
        body, name=name, out_shape=(jax.ShapeDtypeStruct((S, H * LANE), BF16), jax.ShapeDtypeStruct((H, S, 1), F32)),
        grid=(H, S // blk), in_specs=in_specs,
        out_specs=(pl.BlockSpec((blk, LANE), lambda h, i: (i, h)), pl.BlockSpec((1, blk, 1), lambda h, i: (h, i, 0))),
        compiler_params=_params("parallel", "parallel"),
    )(*args)


def causal_attn_bwd(q, k, v, o, do, lse, *, v_col, chunked, scale, cq=None, ck=None, name, blk=256):
    H, S, dk = q.shape
    blk, vb = _tile(S, blk), v_col // LANE
    kc = 2 * blk if S % (2 * blk) == 0 else blk
    fox = cq is not None

    def body(q_ref, k_ref, v_ref, o_ref, do_ref, lse_ref, *rest):
        i = pl.program_id(1)
        if fox:
            cq_ref, ck_ref, dq_ref, dk_ref, dv_ref, dcq_ref, dck_ref = rest
        else:
            dq_ref, dk_ref, dv_ref = rest

        @pl.when(i == 0)
        def _():
            dk_ref[...] = jnp.zeros_like(dk_ref)
            dv_ref[...] = jnp.zeros_like(dv_ref)
            if fox:
                dck_ref[...] = jnp.zeros_like(dck_ref)

        qb, dob, lse_b = q_ref[0], do_ref[...].astype(BF16), lse_ref[0]
        delta = jnp.sum(do_ref[...].astype(F32) * o_ref[...].astype(F32), axis=-1, keepdims=True)
        dq_ref[...] = jnp.zeros_like(dq_ref)
        if fox:
            dcq_ref[...] = jnp.zeros_like(dcq_ref)
        for c in range(S // kc):
            @pl.when(c * kc < (i + 1) * blk)
            def _(ks=slice(c * kc, (c + 1) * kc), start=c * kc):
                kblk = k_ref[0, ks, :]
                s = _causal_scores(qb, kblk, i, start, blk, scale, chunked,
                                   cq_ref[0] if fox else None, ck_ref[0, :, ks] if fox else None)
                p = jnp.exp(s - lse_b)
                dv_ref[ks, :] += lax.dot_general(p.astype(BF16), dob, _TN, preferred_element_type=F32)
                dp = lax.dot_general(dob, v_ref[ks, :].astype(BF16), _NT, preferred_element_type=F32)
                ds = p * (dp - delta)
                dsb = (ds * scale).astype(BF16)
                dq_ref[0] += jnp.dot(dsb, kblk, preferred_element_type=F32)
                dk_ref[0, ks, :] += lax.dot_general(dsb, qb, _TN, preferred_element_type=F32)
                if fox:
                    dcq_ref[0] += jnp.sum(ds, axis=-1, keepdims=True)
                    dck_ref[0, :, ks] += -jnp.sum(ds, axis=0, keepdims=True)

    row = pl.BlockSpec((blk, LANE), lambda h, i: (i, h))
    in_specs = [pl.BlockSpec((1, blk, dk), lambda h, i: (h, i, 0)), pl.BlockSpec((1, S, dk), lambda h, i: (h, 0, 0)),
                pl.BlockSpec((S, LANE), lambda h, i: (0, vb + h)), row, row, pl.BlockSpec((1, blk, 1), lambda h, i: (h, i, 0))]
    args = [q, k, v, o, do, lse]
    out_shape = [jax.ShapeDtypeStruct((H, S, dk), F32), jax.ShapeDtypeStruct((H, S, dk), F32),
                 jax.ShapeDtypeStruct((S, H * LANE), F32)]
    out_specs = [pl.BlockSpec((1, blk, dk), lambda h, i: (h, i, 0)), pl.BlockSpec((1, S, dk), lambda h, i: (h, 0, 0)),
                 pl.BlockSpec((S, LANE), lambda h, i: (0, h))]
    if fox:
        fox_specs = [pl.BlockSpec((1, blk, 1), lambda h, i: (h, i, 0)), pl.BlockSpec((1, 1, S), lambda h, i: (h, 0, 0))]
        in_specs += fox_specs
        args += [cq, ck]
        out_shape += [jax.ShapeDtypeStruct((H, S, 1), F32), jax.ShapeDtypeStruct((H, 1, S), F32)]
        out_specs += fox_specs
    return pl.pallas_call(
        body, name=name, out_shape=tuple(out_shape), grid=(H, S // blk), in_specs=in_specs, out_specs=tuple(out_specs),
        compiler_params=_params("parallel", "arbitrary"),
    )(*args)


CPB = 4
BANDW = BAND + CHUNK
WIN = BAND + (CPB - 1) * CHUNK


def chunk_band_dbias(db):
    H = db.shape[0]
    d4 = db.reshape(H, CPB, CHUNK, WIN)
    return sum(d4[:, c, :, CHUNK * c:CHUNK * c + BAND] for c in range(CPB))


def _band_probs(qb, kb, bias, start, scale):
    s = lax.dot_general(qb, kb, _NT, preferred_element_type=F32) * scale
    real = start + lax.broadcasted_iota(jnp.int32, s.shape, 1) >= PAD
    s = jnp.where(real, s + bias, NEG)
    e = jnp.exp(s - jnp.max(s, axis=-1, keepdims=True))
    return e * (1.0 / jnp.sum(e, axis=-1, keepdims=True))


def band_fwd(q, kp, vp, bias, *, scale, name):
    H, S, _ = q.shape
    Sp, rows = S + PAD + CHUNK, CPB * CHUNK

    def body(q_ref, k_ref, v_ref, b_ref, o_ref):
        start = pl.multiple_of(pl.program_id(1) * rows, rows)
        p = _band_probs(q_ref[0], k_ref[0, pl.ds(start, WIN), :], b_ref[0], start, scale)
        vb = v_ref[pl.ds(start, WIN), :].astype(BF16)
        o_ref[...] = jnp.dot(p.astype(BF16), vb, preferred_element_type=F32).astype(o_ref.dtype)

    return pl.pallas_call(
        body, name=name, out_shape=jax.ShapeDtypeStruct((S, H * LANE), BF16), grid=(H, S // rows),
        in_specs=[pl.BlockSpec((1, rows, LANE), lambda h, j: (h, j, 0)), pl.BlockSpec((1, Sp, LANE), lambda h, j: (h, 0, 0)),
                  pl.BlockSpec((Sp, LANE), lambda h, j: (0, h)), pl.BlockSpec((1, rows, WIN), lambda h, j: (h, 0, 0))],
        out_specs=pl.BlockSpec((rows, LANE), lambda h, j: (j, h)), compiler_params=_params("parallel", "parallel"),
    )(q, kp, vp, bias)


def band_bwd(q, kp, vp, bias, do, *, scale, name):
    H, S, _ = q.shape
    Sp, rows = S + PAD + CHUNK, CPB * CHUNK

    def body(q_ref, k_ref, v_ref, b_ref, do_ref, dq_ref, dk_ref, dv_ref, db_ref):
        j = pl.program_id(1)

        @pl.when(j == 0)
        def _():
            dk_ref[...] = jnp.zeros_like(dk_ref)
            dv_ref[...] = jnp.zeros_like(dv_ref)
            db_ref[...] = jnp.zeros_like(db_ref)

        start = pl.multiple_of(j * rows, rows)
        qb = q_ref[0]
        kb = k_ref[0, pl.ds(start, WIN), :]
        vb = v_ref[pl.ds(start, WIN), :].astype(BF16)
        p = _band_probs(qb, kb, b_ref[0], start, scale)
        dob = do_ref[...].astype(BF16)
        dv_ref[pl.ds(start, WIN), :] += lax.dot_general(p.astype(BF16), dob, _TN, preferred_element_type=F32)
        dp = lax.dot_general(dob, vb, _NT, preferred_element_type=F32)
        ds = p * (dp - jnp.sum(p * dp, axis=-1, keepdims=True))
        db_ref[0] += ds
        dsb = (ds * scale).astype(BF16)
        dq_ref[0] = jnp.dot(dsb, kb, preferred_element_type=F32)
        dk_ref[0, pl.ds(start, WIN), :] += lax.dot_general(dsb, qb, _TN, preferred_element_type=F32)

    blk_b = pl.BlockSpec((1, rows, WIN), lambda h, j: (h, 0, 0))
    return pl.pallas_call(
        body, name=name,
        out_shape=(jax.ShapeDtypeStruct((H, S, LANE), F32), jax.ShapeDtypeStruct((H, Sp, LANE), F32),
                   jax.ShapeDtypeStruct((Sp, H * LANE), F32), jax.ShapeDtypeStruct((H, rows, WIN), F32)),
        grid=(H, S // rows),
        in_specs=[pl.BlockSpec((1, rows, LANE), lambda h, j: (h, j, 0)), pl.BlockSpec((1, Sp, LANE), lambda h, j: (h, 0, 0)),
                  pl.BlockSpec((Sp, LANE), lambda h, j: (0, h)), blk_b, pl.BlockSpec((rows, LANE), lambda h, j: (j, h))],
        out_specs=(pl.BlockSpec((1, rows, LANE), lambda h, j: (h, j, 0)), pl.BlockSpec((1, Sp, LANE), lambda h, j: (h, 0, 0)),
                   pl.BlockSpec((Sp, LANE), lambda h, j: (0, h)), blk_b),
        compiler_params=_params("parallel", "arbitrary"),
    )(q, kp, vp, bias, do)


def band_bias(rel_bias, *, name):
    H, rows, wide = rel_bias.shape[0], CPB * CHUNK, 1024
    last = rel_bias[:, 2 * REL_CLIP:]
    row0 = jnp.concatenate([jnp.tile(last, (1, PAD - REL_CLIP)), rel_bias[:, CHUNK + 1:][:, ::-1],
                            jnp.tile(last, (1, wide - BAND))], axis=1)

    def body(r_ref, o_ref):
        skew = pltpu.roll(jnp.broadcast_to(r_ref[0], (rows, wide)), 0, 1, stride=1, stride_axis=0)[:, :WIN]
        first = (lax.broadcasted_iota(jnp.int32, (rows, WIN), 0) >> 6) * CHUNK
        col = lax.broadcasted_iota(jnp.int32, (rows, WIN), 1)
        o_ref[0] = jnp.where(jnp.logical_and(col >= first, col < first + BAND), skew, NEG)

    return pl.pallas_call(
        body, name=name, out_shape=jax.ShapeDtypeStruct((H, rows, WIN), F32), grid=(H,),
        in_specs=[pl.BlockSpec((1, 1, wide), lambda h: (h, 0, 0))], out_specs=pl.BlockSpec((1, rows, WIN), lambda h: (h, 0, 0)),
        compiler_params=_params("parallel"),
    )(row0.reshape(H, 1, wide))


def relbias_bwd(dbias, *, name):
    H, W = dbias.shape[0], BANDW
    x = jnp.pad(dbias[:, :, ::-1], ((0, 0), (0, 0), (0, CHUNK)))

    def body(x_ref, o_ref):
        skew = pltpu.roll(x_ref[0], 0, 1, stride=1, stride_axis=0)
        f = jnp.broadcast_to(jnp.sum(skew, axis=0, keepdims=True), (8, W))
        lane = lax.broadcasted_iota(jnp.int32, (8, W), 1)
        direct = jnp.where(jnp.logical_and(lane >= 65, lane <= 255), pltpu.roll(f, 65, 1), 0.0)
        tail = jnp.sum(jnp.where(lane >= 191, f, 0.0), axis=-1, keepdims=True)
        o_ref[0] = direct + jnp.where(lane == 2 * REL_CLIP, tail, 0.0)

    out = pl.pallas_call(
        body, name=name, out_shape=jax.ShapeDtypeStruct((H, 8, W), F32), grid=(H,),
        in_specs=[pl.BlockSpec((1, CHUNK, W), lambda h: (h, 0, 0))], out_specs=pl.BlockSpec((1, 8, W), lambda h: (h, 0, 0)),
        compiler_params=_params("parallel"),
    )(x)
    return out[:, 0, :2 * REL_CLIP + 1]


def _split_dot(x, u, dn):
    hi = x.astype(BF16)
    r1 = x - hi.astype(F32)
    mid = r1.astype(BF16)
    lo = (r1 - mid.astype(F32)).astype(BF16)
    d = lambda t: lax.dot_general(t, u, dn, preferred_element_type=F32)
    return d(hi) + d(mid) + d(lo)


def _upper_ones(S):
    return (np.arange(S)[:, None] <= np.arange(S)[None, :]).astype(np.float32)


def foxgate_fwd(fl, b, *, name):
    H, S = fl.shape
    u = jnp.asarray(_upper_ones(S), BF16)

    def body(f_ref, b_ref, u_ref, o_ref):
        x = f_ref[...] + b_ref[...]
        lf = jnp.minimum(x, 0.0) - jnp.log(1.0 + jnp.exp(-jnp.abs(x)))
        o_ref[...] = _split_dot(lf, u_ref[...], (((1,), (0,)), ((), ())))

    return pl.pallas_call(body, name=name, out_shape=jax.ShapeDtypeStruct((H, S), F32),
                          compiler_params=pltpu.CompilerParams(vmem_limit_bytes=VMEM_LIMIT_BYTES))(fl, b.reshape(H, 1), u)


def foxgate_bwd(fl, b, dcum, *, name):
    H, S = fl.shape
    u = jnp.asarray(_upper_ones(S), BF16)

    def body(f_ref, b_ref, u_ref, dc_ref, df_ref, db_ref):
        x = f_ref[...] + b_ref[...]
        dlf = _split_dot(dc_ref[...], u_ref[...], _NT)
        df = dlf * (1.0 / (1.0 + jnp.exp(x)))
        df_ref[...] = df
        db_ref[...] = jnp.sum(df, axis=-1, keepdims=True)

    df, db = pl.pallas_call(body, name=name,
                            out_shape=(jax.ShapeDtypeStruct((H, S), F32), jax.ShapeDtypeStruct((H, 1), F32)),
                            compiler_params=pltpu.CompilerParams(vmem_limit_bytes=VMEM_LIMIT_BYTES))(fl, b.reshape(H, 1), u, dcum)
    return df, db.reshape(H)


def gate_fwd(z, proj, *, name, ts=256, tc=512):
    S, D = proj[0].shape
    ts, gb, nb = _tile(S, ts), Z_GATE // tc, D // tc

    def body(g0, g1, g2, p0, p1, p2, o_ref):
        acc = None
        for g_ref, p_ref in zip((g0, g1, g2), (p0, p1, p2)):
            t = (1.0 / (1.0 + jnp.exp(-g_ref[...]))) * p_ref[...]
            acc = t if acc is None else acc + t
        o_ref[...] = acc.astype(o_ref.dtype)

    blk = pl.BlockSpec((ts, tc), lambda i, j: (i, j))
    return pl.pallas_call(
        body, name=name, out_shape=jax.ShapeDtypeStruct((S, D), BF16), grid=(S // ts, nb),
        in_specs=[pl.BlockSpec((ts, tc), lambda i, j, n=n: (i, gb + n * nb + j)) for n in range(3)] + [blk] * 3,
        out_specs=blk, compiler_params=_params("parallel", "parallel"),
    )(z, z, z, *proj)


def gate_bwd(z, proj, dm, *, name, ts=256, tc=512):
    S, D = proj[0].shape
    ts, gb, nb = _tile(S, ts), Z_GATE // tc, D // tc

    def body(g0, g1, g2, p0, p1, p2, dm_ref, *outs):
        dmv = dm_ref[...]
        for n, (g_ref, p_ref) in enumerate(zip((g0, g1, g2), (p0, p1, p2))):
            sg = 1.0 / (1.0 + jnp.exp(-g_ref[...]))
            outs[n][...] = (dmv * sg).astype(BF16)
            outs[3 + n][...] = (dmv * p_ref[...] * sg * (1.0 - sg)).astype(BF16)

    blk = pl.BlockSpec((ts, tc), lambda i, j: (i, j))
    outs = pl.pallas_call(
        body, name=name, out_shape=tuple(jax.ShapeDtypeStruct((S, D), BF16) for _ in range(6)), grid=(S // ts, nb),
        in_specs=[pl.BlockSpec((ts, tc), lambda i, j, n=n: (i, gb + n * nb + j)) for n in range(3)] + [blk] * 4,
        out_specs=(blk,) * 6, compiler_params=_params("parallel", "parallel"),
    )(z, z, z, *proj, dm)
    return outs[:3], outs[3:]


def loss_head(y, target, *, name, ts=256):
    S, D = y.shape
    ts = _tile(S, ts)

    def body(y_ref, t_ref, l_ref, dy_ref):
        err = y_ref[...] - t_ref[...]
        dy_ref[...] = err * (1.0 / D)
        part = 0.5 * jnp.sum(jnp.mean(err * err, axis=-1, keepdims=True), axis=0, keepdims=True)

        @pl.when(pl.program_id(0) == 0)
        def _():
            l_ref[...] = part

        @pl.when(pl.program_id(0) > 0)
        def _():
            l_ref[...] += part

    blk = pl.BlockSpec((ts, D), lambda i: (i, 0))
    return pl.pallas_call(
        body, name=name, out_shape=(jax.ShapeDtypeStruct((1, 1), F32), jax.ShapeDtypeStruct((S, D), F32)), grid=(S // ts,),
        in_specs=[blk, blk], out_specs=(pl.BlockSpec((1, 1), lambda i: (0, 0)), blk), compiler_params=_params("arbitrary"),
    )(y, target)


def adamw(w, g, m, v, *, name):
    shape = w.shape
    C = shape[-1]
    R = int(np.prod(shape[:-1]))
    br = R
    while br % 16 == 0 and br * C * 4 > 2**20:
        br //= 2
    w2, g2, m2, v2 = (t.reshape(R, C) for t in (w, g, m, v))

    def body(w_ref, g_ref, m_ref, v_ref, d_ref, nm_ref, nv_ref):
        d_ref[...], nm_ref[...], nv_ref[...] = _adamw_update(w_ref[...], g_ref[...], m_ref[...], v_ref[...])

    blk = pl.BlockSpec((br, C), lambda i: (i, 0))
    outs = pl.pallas_call(
        body, name=name, out_shape=tuple(jax.ShapeDtypeStruct((R, C), F32) for _ in range(3)), grid=(R // br,),
        in_specs=[blk] * 4, out_specs=(blk,) * 3, compiler_params=_params("parallel"),
    )(w2, g2, m2, v2)
    return tuple(o.reshape(shape) for o in outs)


_ANY = pl.BlockSpec(memory_space=pl.ANY)


def _place():
    return lax.axis_index("x"), lax.axis_index("y"), lax.axis_index("c")


def all_gather(xs, after, *, name):
    n = len(xs)

    def body(*refs):
        x_refs, o_refs = refs[:n], refs[n + 1:2 * n + 1]
        send_sems, recv_sems, local_sems = refs[2 * n + 1:]
        px, py, pc = _place()
        me, sibling = (px, py, pc), (px, py, 1 - pc)
        chips = [(1 - px, py), (px, 1 - py), (1 - px, 1 - py)]

        def slot(t, dev):
            return o_refs[t].at[4 * dev[0] + 2 * dev[1] + dev[2]]

        def copy(t, k, block, to, src=None):
            return pltpu.make_async_remote_copy(
                src_ref=slot(t, block) if src is None else src, dst_ref=slot(t, block),
                send_sem=send_sems.at[t, k], recv_sem=recv_sems.at[t, k], device_id=to, device_id_type=MESH)

        mine = [pltpu.make_async_copy(x_refs[t], slot(t, me), local_sems.at[t]) for t in range(n)]
        first = []
        for t in range(n):
            mine[t].start()
            first += [copy(t, 1 + j, me, (*chip, pc), src=x_refs[t]) for j, chip in enumerate(chips)]
            first.append(copy(t, 0, me, sibling, src=x_refs[t]))
        for cp in first:
            cp.start()
        passed = []
        for t in range(n):
            for j, chip in enumerate(chips):
                copy(t, 1 + j, (*chip, pc), me).wait_recv()
                fwd = copy(t, 4 + j, (*chip, pc), sibling)
                fwd.start()
                passed.append(fwd)
        for t in range(n):
            copy(t, 0, sibling, me).wait_recv()
            for j, chip in enumerate(chips):
                copy(t, 4 + j, (*chip, 1 - pc), me).wait_recv()
        for cp in first + passed:
            cp.wait_send()
        for cp in mine:
            cp.wait()

    return pl.pallas_call(
        body, name=name, out_shape=tuple(jax.ShapeDtypeStruct((8,) + x.shape, x.dtype) for x in xs),
        in_specs=[_ANY] * (n + 1), out_specs=(_ANY,) * n,
        scratch_shapes=[pltpu.SemaphoreType.DMA((n, 7)), pltpu.SemaphoreType.DMA((n, 7)), pltpu.SemaphoreType.DMA((n,))],
    )(*xs, after)


def forward_to_sibling(zones, *, name):
    n = len(zones)

    def body(*refs):
        z_refs, (send_sems, recv_sems) = refs[n:2 * n], refs[2 * n:]
        px, py, pc = _place()
        copies = []
        for t in range(n):
            for j, chip in enumerate([(1 - px, py), (px, 1 - py), (1 - px, 1 - py)]):
                slot = z_refs[t].at[4 * chip[0] + 2 * chip[1] + pc]
                copies.append(pltpu.make_async_remote_copy(
                    src_ref=slot, dst_ref=slot, send_sem=send_sems.at[t, j], recv_sem=recv_sems.at[t, j],
                    device_id=(px, py, 1 - pc), device_id_type=MESH))
        for cp in copies:
            cp.start()
        for cp in copies:
            cp.wait()

    return pl.pallas_call(
        body, name=name, out_shape=tuple(jax.ShapeDtypeStruct(z.shape, z.dtype) for z in zones),
        in_specs=[_ANY] * n, out_specs=(_ANY,) * n, input_output_aliases={t: t for t in range(n)},
        scratch_shapes=[pltpu.SemaphoreType.DMA((n, 3)), pltpu.SemaphoreType.DMA((n, 3))],
    )(*zones)


_HBM = pl.BlockSpec(memory_space=pltpu.HBM)
_SEM = pl.BlockSpec(memory_space=pltpu.SEMAPHORE)
_EFFECT = pltpu.SideEffectType.DATAFLOW_SIDE_EFFECTING


N_PEERS = 7
GATHER_FLIPS = (1, 2, 4, 6)


def _peers(flips=range(1, N_PEERS + 1)):
    px, py, pc = _place()
    flip = lambda p, bit: 1 - p if bit else p
    return [(flip(px, m >> 2 & 1), flip(py, m >> 1 & 1), flip(pc, m & 1)) for m in flips]


def _gather_copies(src_refs, zone_refs, send_sems, recv_sems):
    px, py, pc = _place()
    return [pltpu.make_async_remote_copy(src_ref=s, dst_ref=z.at[4 * px + 2 * py + pc], send_sem=send_sems.at[k],
                                         recv_sem=recv_sems.at[k], device_id=peer, device_id_type=MESH)
            for k, peer in enumerate(_peers(GATHER_FLIPS)) for s, z in zip(src_refs, zone_refs)]


def _scatter_copies(part_refs, zone_refs, send_sems, recv_sems):
    return [pltpu.make_async_remote_copy(src_ref=p.at[4 * peer[0] + 2 * peer[1] + peer[2]], dst_ref=z.at[k],
                                         send_sem=send_sems.at[k], recv_sem=recv_sems.at[k], device_id=peer, device_id_type=MESH)
            for k, peer in enumerate(_peers()) for p, z in zip(part_refs, zone_refs)]


def copies_start(make, srcs, zones, *, name):
    n = len(srcs)

    def body(*refs):
        for cp in make(refs[:n], refs[n:2 * n], refs[2 * n], refs[2 * n + 1]):
            cp.start()
        refs[-1][...] = jnp.zeros_like(refs[-1])

    arrays = list(srcs) + list(zones)
    outs = pl.pallas_call(
        body, name=name,
        out_shape=(pltpu.SemaphoreType.DMA((N_PEERS,)), pltpu.SemaphoreType.DMA((N_PEERS,)),
                   *[pltpu.HBM(a.shape, a.dtype) for a in arrays], jax.ShapeDtypeStruct((8, LANE), F32)),
        in_specs=[_HBM] * (2 * n), out_specs=(_SEM, _SEM, *[_HBM] * (2 * n), pl.BlockSpec(memory_space=pltpu.VMEM)),
        input_output_aliases={i: 2 + i for i in range(2 * n)},
        compiler_params=pltpu.CompilerParams(has_side_effects=_EFFECT),
    )(*[pltpu.with_memory_space_constraint(a, pltpu.HBM) for a in arrays])
    return outs[0], outs[1], list(outs[2:2 + n]), list(outs[2 + n:2 + 2 * n]), outs[-1]


def copies_wait(make, send_sems, recv_sems, srcs, zones, after, *, name):
    n = len(srcs)

    def body(*refs):
        for cp in make(refs[:n], refs[n:2 * n], refs[2 * n], refs[2 * n + 1]):
            cp.wait_send()
            cp.wait_recv()

    arrays = list(srcs) + list(zones)
    outs = pl.pallas_call(
        body, name=name, out_shape=tuple(pltpu.HBM(a.shape, a.dtype) for a in arrays),
        in_specs=[_HBM] * (2 * n) + [_SEM, _SEM, _ANY], out_specs=(_HBM,) * (2 * n),
        input_output_aliases={i: i for i in range(2 * n)},
        compiler_params=pltpu.CompilerParams(has_side_effects=_EFFECT),
    )(*arrays, send_sems, recv_sems, after)
    return list(outs[:n]), list(outs[n:])


def place_mine(zone, mine, *, name):
    C = mine.shape[-1]
    R = int(np.prod(mine.shape[:-1]))
    br = _row_block(R, C, mine.dtype.itemsize, budget=2**21)
    me = (4 * lax.axis_index("x") + 2 * lax.axis_index("y") + lax.axis_index("c")).astype(jnp.int32).reshape(1)

    def body(me_ref, m_ref, z_ref, o_ref):
        o_ref[0] = m_ref[...]

    out = pl.pallas_call(
        body, name=name, out_shape=jax.ShapeDtypeStruct((8, R, C), zone.dtype),
        grid_spec=pltpu.PrefetchScalarGridSpec(
            num_scalar_prefetch=1, grid=(R // br,), in_specs=[pl.BlockSpec((br, C), lambda i, me: (i, 0)), _ANY],
            out_specs=pl.BlockSpec((1, br, C), lambda i, me: (me[0], i, 0))),
        input_output_aliases={2: 0}, compiler_params=_params("parallel"),
    )(me, mine.reshape(R, C), zone.reshape(8, R, C))
    return out.reshape(zone.shape)


def _row_block(rows, cols, itemsize, budget=2**20):
    br = rows
    while br % 32 == 0 and br * cols * itemsize > budget:
        br //= 2
    return br


def _adamw_update(w, g, m, v):
    nm = ADAM_B1 * m + (1.0 - ADAM_B1) * g
    nv = ADAM_B2 * v + (1.0 - ADAM_B2) * (g * g)
    m_hat = nm / (1.0 - ADAM_B1 ** ADAM_STEP)
    v_hat = nv / (1.0 - ADAM_B2 ** ADAM_STEP)
    return -ADAM_LR * (m_hat / (jnp.sqrt(v_hat) + ADAM_EPS) + ADAM_WD * w), nm, nv


def grad_sum_adamw(parts, recvs, w, m, v, *, name):
    L, C = len(parts), w.shape[-1]
    R = int(np.prod(w.shape[1:-1]))
    br = _row_block(R, C, 4, budget=2**19)
    chip = (4 * lax.axis_index("x") + 2 * lax.axis_index("y") + lax.axis_index("c")).astype(jnp.int32).reshape(1)

    def body(c_ref, *refs):
        p_refs, r_refs = refs[:L], refs[L:2 * L]
        w_ref, m_ref, v_ref, g_out, d_out, nm_out, nv_out = refs[2 * L:]
        for j in range(L):
            @pl.when(pl.program_id(0) == j)
            def _(j=j):
                g = p_refs[j][0].astype(F32)
                for k in range(N_PEERS):
                    g = g + r_refs[j][k].astype(F32)
                g_out[0] = g
                d_out[0], nm_out[0], nv_out[0] = _adamw_update(w_ref[0], g, m_ref[0], v_ref[0])

    row = lambda j: (lambda l, r, c: jnp.where(l == j, r, 0))
    part_specs = [pl.BlockSpec((1, br, C), lambda l, r, c, f=row(j): (c[0], f(l, r, c), 0)) for j in range(L)]
    recv_specs = [pl.BlockSpec((N_PEERS, br, C), lambda l, r, c, f=row(j): (0, f(l, r, c), 0)) for j in range(L)]
    blk = pl.BlockSpec((1, br, C), lambda l, r, c: (l, r, 0))
    outs = pl.pallas_call(
        body, name=name, out_shape=tuple(jax.ShapeDtypeStruct((L, R, C), F32) for _ in range(4)),
        grid_spec=pltpu.PrefetchScalarGridSpec(
            num_scalar_prefetch=1, grid=(L, R // br), in_specs=part_specs + recv_specs + [blk] * 3, out_specs=(blk,) * 4),
        compiler_params=_params("arbitrary", "arbitrary"),
    )(chip, *[p.reshape(8, R, C) for p in parts], *[r.reshape(N_PEERS, R, C) for r in recvs],
      *[t.reshape(L, R, C) for t in (w, m, v)])
    return tuple(o.reshape(w.shape) for o in outs)


def ordered_sum(parts, *, name):
    _, R, C = parts.shape

    def body(p_ref, o_ref):
        acc = p_ref[0]
        for d in range(1, 8):
            acc = acc + p_ref[d]
        o_ref[...] = acc

    return pl.pallas_call(body, name=name, out_shape=jax.ShapeDtypeStruct((R, C), F32))(parts)


W_IN_COLS, W_IN_SHARD = 13128, 1641
W_IN_SEGMENTS = ((0, 832, 0), (832, 3904, Z_FOX), (3904, 3912, FF_COL), (3912, 6984, Z_CH), (6984, 13128, Z_GATE))


def col_gather(src, table, pieces, out_shape, *, name, tr=1024):
    R, C = src.shape[1:]
    tr = _tile(R, tr)
    width = 2 + 6 * pieces
    nb = table.shape[0] // width
    last_tile, last_valid = C // LANE, C % LANE

    def body(tab, *refs):
        o_ref = refs[-1]
        base = pl.program_id(1) * width
        lane = lax.broadcasted_iota(jnp.int32, (tr, LANE), 1)
        row = lax.broadcasted_iota(jnp.int32, (2 * LANE * pieces, LANE), 0)
        col = lax.broadcasted_iota(jnp.int32, (2 * LANE * pieces, LANE), 1)
        tiles, hit = [], None
        for p in range(pieces):
            e = base + 2 + 6 * p
            for tcol in (1, 2):
                x = refs[2 * p + tcol - 1][0]
                if last_valid:
                    x = jnp.where(jnp.logical_or(tab[e + tcol] < last_tile, lane < last_valid), x, jnp.zeros_like(x))
                tiles.append(x)
            lo, hi = tab[e + 4], tab[e + 5]
            cond = jnp.logical_and(row - 2 * LANE * p == col + tab[e + 3], jnp.logical_and(col >= lo, col < hi))
            hit = cond if hit is None else jnp.logical_or(hit, cond)
        sel = jnp.where(hit, 1.0, 0.0).astype(src.dtype)
        o_ref[0] = jnp.dot(jnp.concatenate(tiles, axis=1), sel, preferred_element_type=F32).astype(o_ref.dtype)

    in_specs = []
    for p in range(pieces):
        for tcol in (1, 2):
            in_specs.append(pl.BlockSpec(
                (1, tr, LANE), lambda i, b, tab, p=p, tcol=tcol: (tab[b * width + 2 + 6 * p], i, tab[b * width + 2 + 6 * p + tcol])))
    return pl.pallas_call(
        body, name=name, out_shape=jax.ShapeDtypeStruct(out_shape, src.dtype),
        grid_spec=pltpu.PrefetchScalarGridSpec(
            num_scalar_prefetch=1, grid=(R // tr, nb), in_specs=in_specs,
            out_specs=pl.BlockSpec((1, tr, LANE), lambda i, b, tab: (tab[b * width], i, tab[b * width + 1]))),
        compiler_params=_params("parallel", "parallel"),
    )(jnp.asarray(table, jnp.int32), *([src] * (2 * pieces)))


def _piece(sd, start, lo, hi, last_tile):
    t0 = start // LANE
    return [sd, t0, min(t0 + 1, last_tile), start % LANE - lo, lo, hi]


def _pad_pieces(rows, pieces):
    out, prev = [], [0, 0, 0, 0, 0, 0] * pieces
    for head, pcs in rows:
        full = list(pcs)
        for p in range(len(pcs) // 6, pieces):
            full += prev[6 * p:6 * p + 3] + [0, 0, 0]
        out.append(head + full)
        prev = full
    return np.asarray(out, np.int32).reshape(-1)


def _w_in_table(layer, L):
    rows = []
    for b in range(Z_W // LANE):
        pcs = []
        for first, last, col in W_IN_SEGMENTS:
            lo, hi = max(LANE * b, col), min(LANE * (b + 1), col + last - first)
            while lo < hi:
                c = first + lo - col
                n = min(hi - lo, W_IN_SHARD - c % W_IN_SHARD)
                pcs += _piece((c // W_IN_SHARD) * L + layer, c % W_IN_SHARD, lo - LANE * b, lo - LANE * b + n, W_IN_SHARD // LANE)
                lo += n
        assert len(pcs) <= 12
        rows.append(([0, b], pcs))
    return _pad_pieces(rows, 2)


def _w_in_grad_table():
    rows = []
    for d in range(8):
        for t in range(-(-W_IN_SHARD // LANE)):
            pcs = []
            c0 = d * W_IN_SHARD + LANE * t
            c1 = min(c0 + LANE, (d + 1) * W_IN_SHARD)
            for first, last, col in W_IN_SEGMENTS:
                lo, hi = max(c0, first), min(c1, last)
                if lo < hi:
                    pcs += _piece(0, col + lo - first, lo - c0, hi - c0, Z_W // LANE - 1)
            assert len(pcs) <= 18
            rows.append(([d, t], pcs))
    return _pad_pieces(rows, 3)


def block_copy(src, out_shape, in_blk, out_blk, grid, in_map, out_map, *, name):
    def body(x_ref, o_ref):
        o_ref[(0,) * (len(out_blk) - 2) + (Ellipsis,)] = x_ref[(0,) * (len(in_blk) - 2) + (Ellipsis,)]

    return pl.pallas_call(
        body, name=name, out_shape=jax.ShapeDtypeStruct(out_shape, src.dtype), grid=grid,
        in_specs=[pl.BlockSpec(in_blk, in_map)], out_specs=pl.BlockSpec(out_blk, out_map),
        compiler_params=_params("parallel", "parallel"),
    )(src)


def _columns_from_owners(z, *, name, lead=()):
    K, c = z.shape[-2:]
    tr, nl = _tile(K, 1024), len(lead)
    return block_copy(z, (K, 8 * c), (1,) * (1 + nl) + (tr, c), (tr, c), (8, K // tr),
                      lambda d, i: (d, *lead, i, 0), lambda d, i: (i, d), name=name)


def _owners_from_columns(g, *, name):
    K, c = g.shape[0], g.shape[1] // 8
    tr = _tile(K, 1024)
    return block_copy(g, (8, K, c), (tr, c), (1, tr, c), (8, K // tr), lambda d, i: (i, d), lambda d, i: (d, i, 0), name=name)


def _full_from_shards(k, sh, tag):
    if k not in COL_SHARDED:
        return sh.reshape((-1, sh.shape[-1]))
    if k == 'w_br':
        return [_columns_from_owners(sh, lead=(n,), name=f"{tag}_w_br{n}_layout") for n in range(3)]
    if k == 'w_uq':
        return _columns_from_owners(jnp.pad(sh, ((0, 0), (0, 0), (0, 64))), name=f"{tag}_w_uq_layout")
    if k == 'w_ukv':
        return block_copy(sh, (256, 2048), (1, 256, LANE), (256, LANE), (2, MLA_HEADS),
                          lambda t, h: (h, 0, t), lambda t, h: (0, t * MLA_HEADS + h), name=f"{tag}_w_ukv_layout")
    return _columns_from_owners(sh, name=f"{tag}_{k}_layout")


def _shards_from_full(k, g, tag):
    if k not in COL_SHARDED:
        return g.reshape((8, g.shape[0] // 8, g.shape[1]))
    if k == 'w_br':
        return jnp.stack([_owners_from_columns(g[n], name=f"{tag}_dw_br{n}_layout") for n in range(3)], axis=1)
    if k == 'w_uq':
        return _owners_from_columns(g, name=f"{tag}_dw_uq_layout")[:, :, :MLA_QK]
    if k == 'w_ukv':
        return block_copy(g, (8, 256, 256), (256, LANE), (1, 256, LANE), (2, MLA_HEADS),
                          lambda t, h: (0, t * MLA_HEADS + h), lambda t, h: (h, 0, t), name=f"{tag}_dw_ukv_layout")
    return _owners_from_columns(g, name=f"{tag}_d{k}_layout")


def w_in_full(gathered, layer, *, name):
    _, L, K, c = gathered.shape
    return col_gather(gathered.reshape(8 * L, K, c), _w_in_table(layer, L), 2, (1, K, Z_W), name=name)[0]


def w_in_shards(g, *, name):
    return col_gather(g[None], _w_in_grad_table(), 3, (8, g.shape[0], W_IN_SHARD), name=name)


def _layer_fwd(x, mem, W, P, cos, ssin, tag, later=None):
    S = x.shape[0]
    sv = {'x0': x}
    h = rmsnorm_fwd(x, P['g_mix'], name=f"{tag}_norm_mix")
    z = mm(h, W['w_in'], name=f"{tag}_mm_in")
    if later is not None:
        W = later(z)
    sv.update(h=h, z=z)
    cqn = rmsnorm_fwd(z, P['g_cq'], col=0, width=512, name=f"{tag}_norm_cq")
    ckvn = rmsnorm_fwd(z, P['g_ckv'], col=512, width=256, name=f"{tag}_norm_ckv")
    qf = mm(cqn, W['w_uq'], name=f"{tag}_mm_uq")
    kvf = mm(ckvn, W['w_ukv'], name=f"{tag}_mm_ukv")
    qa = mla_prep_fwd(qf, qf, P['g_mla_q'], cos, ssin, n_col=0, n_stride=2 * LANE, r_col=LANE, r_stride=2 * LANE,
                      heads=8, name=f"{tag}_mla_q")
    ka = mla_prep_fwd(kvf, z, P['g_mla_k'], cos, ssin, n_col=0, n_stride=LANE, r_col=KR_COL, r_stride=0,
                      heads=8, name=f"{tag}_mla_k")
    ya, lse_a = causal_attn_fwd(qa, ka, kvf, v_col=1024, chunked=True, scale=MLA_QK ** -0.5, name=f"{tag}_mla_attn")
    sv.update(cqn=cqn, ckvn=ckvn, qf=qf, kvf=kvf, qa=qa, ka=ka, lse_a=lse_a)
    qb = headnorm_fwd(z, P['g_fox_q'], col=Z_FOX, heads=8, name=f"{tag}_fox_qn")
    kb = headnorm_fwd(z, P['g_fox_k'], col=Z_FOX + 1024, heads=8, name=f"{tag}_fox_kn")
    fl = z[:, FF_COL:FF_COL + 8].T
    cum = foxgate_fwd(fl, P['b_f'], name=f"{tag}_fox_gate")
    cq, ck = cum.reshape(8, S, 1), cum.reshape(8, 1, S)
    yb, lse_b = causal_attn_fwd(qb, kb, z, v_col=Z_FOX + 2048, chunked=False, scale=LANE ** -0.5, cq=cq, ck=ck,
                                name=f"{tag}_fox_attn")
    sv.update(qb=qb, kb=kb, fl=fl, cq=cq, ck=ck, lse_b=lse_b)
    qc = headnorm_fwd(z, P['g_ch_q'], col=Z_CH, heads=8, name=f"{tag}_ch_qn")
    kc = headnorm_fwd(z, P['g_ch_k'], col=Z_CH + 1024, heads=8, name=f"{tag}_ch_kn")
    kcp = jnp.pad(kc, ((0, 0), (PAD, CHUNK), (0, 0)))
    vcp = jnp.pad(z[:, Z_CH + 2048:Z_CH + 3072], ((PAD, CHUNK), (0, 0)))
    bias = band_bias(P['rel_bias'], name=f"{tag}_ch_bias")
    yc = band_fwd(qc, kcp, vcp, bias, scale=LANE ** -0.5, name=f"{tag}_ch_attn")
    sv.update(qc=qc, kcp=kcp, vcp=vcp, bias=bias)
    ys = (ya, yb, yc)
    proj = [mm(ys[n], W['w_br'][n], name=f"{tag}_mm_br{n}") for n in range(3)]
    merged = gate_fwd(z, proj, name=f"{tag}_gate")
    x1 = mm(merged, W['w_out'], epi='add', aux=x, name=f"{tag}_mm_out")
    sv.update(ys=ys, proj=proj, merged=merged, x1=x1)
    hc = rmsnorm_fwd(x1, P['g_cross'], name=f"{tag}_norm_cross")
    memn = rmsnorm_fwd(mem, P['g_mem'], name=f"{tag}_norm_mem")
    qx_raw = mm(hc, W['w_xq'], name=f"{tag}_mm_xq")
    memkv = mm(memn, W['w_xkv'], name=f"{tag}_mm_xkv")
    qx = headnorm_fwd(qx_raw, P['g_x_q'], col=0, heads=4, name=f"{tag}_x_qn")
    kx = headnorm_fwd(memkv, P['g_x_k'], col=0, heads=4, name=f"{tag}_x_kn")
    ox = attn_fwd(qx, kx, memkv, v_col=512, scale=LANE ** -0.5, name=f"{tag}_x_attn")
    x2 = mm(ox, W['w_xo'], epi='add', aux=x1, name=f"{tag}_mm_xo")
    sv.update(hc=hc, memn=memn, qx_raw=qx_raw, memkv=memkv, qx=qx, kx=kx, ox=ox, x2=x2)
    hm = rmsnorm_fwd(x2, P['g_mlp'], name=f"{tag}_norm_mlp")
    u, a = mm(hm, W['w_1'], epi='relu2', out_dtype=BF16, name=f"{tag}_mm_w1")
    x3 = mm(a, W['w_2'], epi='add', aux=x2, name=f"{tag}_mm_w2")
    sv.update(hm=hm, u=u, a=a)
    return x3, sv


def _layer_bwd(dx, mem, W, P, sv, cos, ssin, tag, send_off=None):
    S = dx.shape[0]
    z = sv['z']
    gw, gs = {}, {}
    wgrad = lambda a, d, name: mm(a, d, ta=True, out_dtype=BF16, name=name)
    gw['w_2'] = wgrad(sv['a'], dx, f"{tag}_dw2")
    du = mm(dx, W['w_2'], tb=True, epi='mul_drelu2', aux=sv['u'], out_dtype=BF16, name=f"{tag}_du")
    gw['w_1'] = wgrad(sv['hm'], du, f"{tag}_dw1")
    dhm = mm(du, W['w_1'], tb=True, name=f"{tag}_dhm")
    dx, gs['g_mlp'] = rmsnorm_bwd(sv['x2'], P['g_mlp'], dhm, res=dx, name=f"{tag}_dnorm_mlp")
    gw['w_xo'] = wgrad(sv['ox'], dx, f"{tag}_dwxo")
    dox = mm(dx, W['w_xo'], tb=True, out_dtype=BF16, name=f"{tag}_dox")
    dqx, dkx, dvx = attn_bwd(sv['qx'], sv['kx'], sv['memkv'], dox, v_col=512, scale=LANE ** -0.5, name=f"{tag}_x_attn_bwd")
    dqx_raw, gs['g_x_q'] = headnorm_bwd(sv['qx_raw'], P['g_x_q'], dqx, col=0, heads=4, name=f"{tag}_x_qn_bwd")
    dkx_raw, gs['g_x_k'] = headnorm_bwd(sv['memkv'], P['g_x_k'], dkx, col=0, heads=4, name=f"{tag}_x_kn_bwd")
    dqx_b = dqx_raw.astype(BF16)
    gw['w_xq'] = wgrad(sv['hc'], dqx_b, f"{tag}_dwxq")
    dhc = mm(dqx_b, W['w_xq'], tb=True, name=f"{tag}_dhc")
    dx, gs['g_cross'] = rmsnorm_bwd(sv['x1'], P['g_cross'], dhc, res=dx, name=f"{tag}_dnorm_cross")
    dmemkv = jnp.concatenate([dkx_raw, dvx], axis=1).astype(BF16)
    gw['w_xkv'] = wgrad(sv['memn'], dmemkv, f"{tag}_dwxkv")
    dmemn = mm(dmemkv, W['w_xkv'], tb=True, name=f"{tag}_dmemn")
    _, gs['g_mem'] = rmsnorm_bwd(mem, P['g_mem'], dmemn, name=f"{tag}_dnorm_mem")
    gw['w_out'] = wgrad(sv['merged'], dx, f"{tag}_dwout")
    dmerged = mm(dx, W['w_out'], tb=True, name=f"{tag}_dmerged")
    dproj, dgl = gate_bwd(z, sv['proj'], dmerged, name=f"{tag}_gate_bwd")
    gw['w_br'] = [wgrad(sv['ys'][n], dproj[n], f"{tag}_dwbr{n}") for n in range(3)]
    dys = [mm(dproj[n], W['w_br'][n], tb=True, out_dtype=BF16, name=f"{tag}_dys{n}") for n in range(3)]
    dqa, dka, dva = causal_attn_bwd(sv['qa'], sv['ka'], sv['kvf'], sv['ys'][0], dys[0], sv['lse_a'], v_col=1024, chunked=True,
                                    scale=MLA_QK ** -0.5, name=f"{tag}_mla_attn_bwd")
    g_mla_q = P['g_mla_q'] if send_off is None else P['g_mla_q'] + send_off[0](gw)
    dqn, dqr, gs['g_mla_q'] = mla_prep_bwd(sv['qf'], sv['qf'], g_mla_q, cos, ssin, dqa, n_col=0, n_stride=2 * LANE,
                                           r_col=LANE, r_stride=2 * LANE, heads=8, name=f"{tag}_mla_q_bwd")
    dkn, dkr, gs['g_mla_k'] = mla_prep_bwd(sv['kvf'], z, P['g_mla_k'], cos, ssin, dka, n_col=0, n_stride=LANE,
                                           r_col=KR_COL, r_stride=0, heads=8, name=f"{tag}_mla_k_bwd")
    dqf = jnp.stack([dqn.reshape(S, 8, LANE), dqr.reshape(S, 8, LANE)], axis=2).reshape(S, 2048).astype(BF16)
    dkvf = jnp.concatenate([dkn, dva], axis=1).astype(BF16)
    gw['w_uq'] = wgrad(sv['cqn'], dqf, f"{tag}_dwuq")
    gw['w_ukv'] = wgrad(sv['ckvn'], dkvf, f"{tag}_dwukv")
    dcqn = mm(dqf, W['w_uq'], tb=True, name=f"{tag}_dcqn")
    dckvn = mm(dkvf, W['w_ukv'], tb=True, name=f"{tag}_dckvn")
    dcq_raw, gs['g_cq'] = rmsnorm_bwd(z, P['g_cq'], dcqn, col=0, width=512, name=f"{tag}_dnorm_cq")
    dckv_raw, gs['g_ckv'] = rmsnorm_bwd(z, P['g_ckv'], dckvn, col=512, width=256, name=f"{tag}_dnorm_ckv")
    dqb, dkb, dvb, dcq, dck = causal_attn_bwd(sv['qb'], sv['kb'], z, sv['ys'][1], dys[1], sv['lse_b'], v_col=Z_FOX + 2048,
                                              chunked=False, scale=LANE ** -0.5, cq=sv['cq'], ck=sv['ck'],
                                              name=f"{tag}_fox_attn_bwd")
    dqb_raw, gs['g_fox_q'] = headnorm_bwd(z, P['g_fox_q'], dqb, col=Z_FOX, heads=8, name=f"{tag}_fox_qn_bwd")
    dkb_raw, gs['g_fox_k'] = headnorm_bwd(z, P['g_fox_k'], dkb, col=Z_FOX + 1024, heads=8, name=f"{tag}_fox_kn_bwd")
    dfl, gs['b_f'] = foxgate_bwd(sv['fl'], P['b_f'], dcq.reshape(8, S) + dck.reshape(8, S), name=f"{tag}_fox_gate_bwd")
    dqc, dkcp, dvcp, dbias = band_bwd(sv['qc'], sv['kcp'], sv['vcp'], sv['bias'], dys[2], scale=LANE ** -0.5,
                                      name=f"{tag}_ch_attn_bwd")
    dqc_raw, gs['g_ch_q'] = headnorm_bwd(z, P['g_ch_q'], dqc, col=Z_CH, heads=8, name=f"{tag}_ch_qn_bwd")
    dkc_raw, gs['g_ch_k'] = headnorm_bwd(z, P['g_ch_k'], dkcp[:, PAD:PAD + S, :], col=Z_CH + 1024, heads=8,
                                         name=f"{tag}_ch_kn_bwd")
    gs['rel_bias'] = relbias_bwd(chunk_band_dbias(dbias), name=f"{tag}_relbias_bwd")
    b16 = lambda t: t.astype(BF16)
    dz = jnp.concatenate([b16(dcq_raw), b16(dckv_raw), b16(dkr), b16(dfl.T), jnp.zeros((S, 120), BF16),
                          b16(dqb_raw), b16(dkb_raw), b16(dvb), b16(dqc_raw), b16(dkc_raw), b16(dvcp[PAD:PAD + S]),
                          dgl[0], dgl[1], dgl[2]], axis=1)
    gw['w_in'] = wgrad(sv['h'], dz, f"{tag}_dwin")
    dh = mm(dz, W['w_in'], tb=True, name=f"{tag}_dh")
    g_mix = P['g_mix'] if send_off is None else P['g_mix'] + send_off[1](gw)
    dx, gs['g_mix'] = rmsnorm_bwd(sv['x0'], g_mix, dh, res=dx, name=f"{tag}_dnorm_mix")
    return dx, gw, gs


def _local_step(x, mem, target, Ws, Ps):
    S = x.shape[0]
    cos, ssin = _rope_tables(S)
    L = len(Ws)
    saved = []
    for l in range(L):
        x, sv = _layer_fwd(x, mem, Ws[l], Ps[l], cos, ssin, f"l{l}")
        saved.append(sv)
    loss, dx = loss_head(x, target, name="loss_head")
    gws, gss = [None] * L, [None] * L
    for l in reversed(range(L)):
        dx, gws[l], gss[l] = _layer_bwd(dx, mem, Ws[l], Ps[l], saved[l], cos, ssin, f"l{l}")
    return loss, dx, gws, gss


def _pack_small(d):
    flat = jnp.concatenate([d[k].reshape(-1) for k in SMALL])
    n = flat.shape[0]
    rows = -(-n // (8 * LANE)) * 8
    return jnp.pad(flat, (0, rows * LANE - n)).reshape(rows, LANE)


def _unpack_small(packed, like):
    flat, out, off = packed.reshape(-1), {}, 0
    for k in SMALL:
        n = int(np.prod(like[k].shape))
        out[k] = flat[off:off + n].reshape(like[k].shape)
        off += n
    return out


def kernel(x, mem, g_mix, w_in, g_cq, w_uq, g_ckv, w_ukv, g_mla_q, g_mla_k, b_f, g_fox_q, g_fox_k, rel_bias, g_ch_q, g_ch_k, w_br, w_out, g_cross, g_mem, w_xq, w_xkv, g_x_q, g_x_k, w_xo, g_mlp, w_1, w_2, loss_target, m_g_mix, m_w_in, m_g_cq, m_w_uq, m_g_ckv, m_w_ukv, m_g_mla_q, m_g_mla_k, m_b_f, m_g_fox_q, m_g_fox_k, m_rel_bias, m_g_ch_q, m_g_ch_k, m_w_br, m_w_out, m_g_cross, m_g_mem, m_w_xq, m_w_xkv, m_g_x_q, m_g_x_k, m_w_xo, m_g_mlp, m_w_1, m_w_2, v_g_mix, v_w_in, v_g_cq, v_w_uq, v_g_ckv, v_w_ukv, v_g_mla_q, v_g_mla_k, v_b_f, v_g_fox_q, v_g_fox_k, v_rel_bias, v_g_ch_q, v_g_ch_k, v_w_br, v_w_out, v_g_cross, v_g_mem, v_w_xq, v_w_xkv, v_g_x_q, v_g_x_k, v_w_xo, v_g_mlp, v_w_1, v_w_2):
    args = locals()
    w = {k: args[k] for k in WEIGHTS}
    m = {k: args['m_' + k] for k in WEIGHTS}
    v = {k: args['v_' + k] for k in WEIGHTS}
    L = w_in.shape[0]

    Ps = [{k: w[k][l] for k in SMALL} for l in range(L)]
    xs, memv = x[0], mem[0]
    cos, ssin = _rope_tables(xs.shape[0])

    gathers = {}
    for l in range(L):
        for group in AG_GROUPS:
            shards = [w[k][l].astype(BF16) for k in group]
            gathers[l, group] = copies_start(_gather_copies, shards, [lax.empty((8,) + s.shape, s.dtype) for s in shards],
                                             name=f"l{l}_ag_start_{group[0]}")
    Ps[0]['g_mix'] = Ps[0]['g_mix'] + sum(g[4][0, :1] for g in gathers.values())

    def arrived(l, group, after):
        send_sems, recv_sems, shards, zones, _ = gathers[l, group]
        shards, zones = copies_wait(_gather_copies, send_sems, recv_sems, shards, zones, after, name=f"l{l}_ag_wait_{group[0]}")
        zones = forward_to_sibling(zones, name=f"l{l}_ag_forward_{group[0]}")
        zones = [place_mine(z, s, name=f"l{l}_{k}_mine") for k, z, s in zip(group, zones, shards)]
        return {k: w_in_full(z[:, None], 0, name=f"l{l}_w_in_layout") if k == 'w_in' else _full_from_shards(k, z, f"l{l}")
                for k, z in zip(group, zones)}

    Ws, saved = [], []
    for l in range(L):
        Ws.append(arrived(l, AG_GROUPS[0], xs if l else Ps[0]['g_mix']))

        def later(anchor, l=l):
            Ws[l].update(arrived(l, AG_GROUPS[1], anchor))
            return Ws[l]

        xs, sv = _layer_fwd(xs, memv, Ws[l], Ps[l], cos, ssin, f"l{l}", later)
        saved.append(sv)

    loss, dx = loss_head(xs, loss_target[0], name="loss_head")
    loss = lax.psum(loss[0, 0], ("x", "y", "c"))

    gss = [None] * L
    scatters = {group: [None] * L for group in RS_GROUPS}
    for l in reversed(range(L)):
        def send_off(gw, group, l=l):
            gdst = [w_in_shards(gw[k], name=f"l{l}_dw_in_layout") if k == 'w_in' else _shards_from_full(k, gw[k], f"l{l}")
                    for k in group]
            started = copies_start(_scatter_copies, gdst, [lax.empty((N_PEERS,) + g.shape[1:], g.dtype) for g in gdst],
                                   name=f"l{l}_rs_start_{group[0]}")
            scatters[group][l] = started[:4]
            return started[4][0, :1]

        hooks = tuple((lambda gw, group=group: send_off(gw, group)) for group in RS_GROUPS)
        dx, _, gss[l] = _layer_bwd(dx, memv, Ws[l], Ps[l], saved[l], cos, ssin, f"l{l}", hooks)
    grad_x = dx

    grads, delta, new_m, new_v = {}, {}, {}, {}
    after = grad_x
    for group in RS_GROUPS:
        done = [copies_wait(_scatter_copies, *scatters[group][l], after, name=f"l{l}_rs_wait_{group[0]}") for l in range(L)]
        for t, k in enumerate(group):
            grads[k], delta[k], new_m[k], new_v[k] = grad_sum_adamw(
                [done[l][0][t] for l in range(L)], [done[l][1][t] for l in range(L)], w[k], m[k], v[k], name=f"adamw_{k}")
        after = sum(delta[k][(0,) * (delta[k].ndim - 1)][:1] for k in group)

    small_part = _pack_small({k: jnp.stack([gss[l][k] for l in range(L)]) for k in SMALL})
    small_all = all_gather([small_part], after, name="ag_small")[0]
    grads.update(_unpack_small(ordered_sum(small_all, name="small_sum"), {k: w[k] for k in SMALL}))
    sd, sm, sv_ = adamw(_pack_small({k: w[k] for k in SMALL}), _pack_small({k: grads[k] for k in SMALL}),
                        _pack_small({k: m[k] for k in SMALL}), _pack_small({k: v[k] for k in SMALL}), name="adamw_small")
    like = {k: w[k] for k in SMALL}
    delta.update(_unpack_small(sd, like))
    new_m.update(_unpack_small(sm, like))
    new_v.update(_unpack_small(sv_, like))

    return (loss, grad_x[None], *[grads[k] for k in WEIGHTS], *[delta[k] for k in WEIGHTS],
            *[new_m[k] for k in WEIGHTS], *[new_v[k] for k in WEIGHTS])
```

```python
import numpy as np
import jax
import jax.numpy as jnp
from jax import lax
from jax.experimental import pallas as pl
from jax.experimental.pallas import tpu as pltpu

F32, BF16 = jnp.float32, jnp.bfloat16
EPS = 1e-6
NEG = -1e30
LANE = 128
VMEM_LIMIT_BYTES = 56 * 2**20
MESH = pl.DeviceIdType.MESH

D_MODEL = 2048
CHUNK = 64
BAND = 9 * CHUNK
PAD = 8 * CHUNK
REL_CLIP = 128
MLA_HEADS, MLA_NOPE, MLA_ROPE, MLA_QK = 8, 128, 64, 192
N_HEADS = 8
X_HEADS = 4
ROPE_THETA = 10000.0
ADAM_LR, ADAM_B1, ADAM_B2, ADAM_EPS, ADAM_WD, ADAM_STEP = 0.001, 0.9, 0.999, 1e-08, 0.01, 10

Z_MAIN, Z_FOX, Z_CH, Z_GATE, Z_W = 0, 1024, 4096, 7168, 13312
KR_COL, FF_COL = 768, 896

BIG = ('w_in', 'w_uq', 'w_ukv', 'w_br', 'w_out', 'w_xq', 'w_xkv', 'w_xo', 'w_1', 'w_2')
COL_SHARDED = ('w_in', 'w_uq', 'w_ukv', 'w_br', 'w_xo', 'w_1')
RS_GROUPS = (('w_2', 'w_1', 'w_xo', 'w_xq', 'w_xkv', 'w_out', 'w_br'), ('w_uq', 'w_ukv', 'w_in'))
AG_GROUPS = (('w_in',), ('w_uq', 'w_ukv', 'w_br', 'w_out', 'w_xq', 'w_xkv', 'w_xo', 'w_1', 'w_2'))
SMALL = ('g_mix', 'g_cq', 'g_ckv', 'g_mla_q', 'g_mla_k', 'b_f', 'g_fox_q', 'g_fox_k', 'rel_bias', 'g_ch_q',
         'g_ch_k', 'g_cross', 'g_mem', 'g_x_q', 'g_x_k', 'g_mlp')
WEIGHTS = ('g_mix', 'w_in', 'g_cq', 'w_uq', 'g_ckv', 'w_ukv', 'g_mla_q', 'g_mla_k', 'b_f', 'g_fox_q', 'g_fox_k',
           'rel_bias', 'g_ch_q', 'g_ch_k', 'w_br', 'w_out', 'g_cross', 'g_mem', 'w_xq', 'w_xkv', 'g_x_q', 'g_x_k',
           'w_xo', 'g_mlp', 'w_1', 'w_2')


def _params(*sem):
    return pltpu.CompilerParams(dimension_semantics=sem, vmem_limit_bytes=VMEM_LIMIT_BYTES)


def _tile(dim, pref):
    if dim <= pref:
        return dim
    for t in range(pref - pref % LANE, 0, -LANE):
        if dim % t == 0:
            return t
    raise ValueError((dim, pref))


def mm(a, b, *, ta=False, tb=False, out_dtype=F32, epi=None, aux=None, name, tm=1024, tn=512, tk=2048):
    M, K = (a.shape[1], a.shape[0]) if ta else a.shape
    N = b.shape[0] if tb else b.shape[1]
    assert (b.shape[1] if tb else b.shape[0]) == K, (a.shape, b.shape, ta, tb)
    tm, tn, tk = _tile(M, tm), _tile(N, tn), _tile(K, tk)
    nk = K // tk
    dn = (((0 if ta else 1,), (1 if tb else 0,)), ((), ()))
    n_aux = 0 if aux is None else 1

    def finish(acc, aux_refs, o_refs):
        if epi is None:
            o_refs[0][...] = acc.astype(o_refs[0].dtype)
        elif epi == 'add':
            o_refs[0][...] = (acc + aux_refs[0][...]).astype(o_refs[0].dtype)
        elif epi == 'relu2':
            o_refs[0][...] = acc
            r = jnp.maximum(acc, 0.0)
            o_refs[1][...] = (r * r).astype(o_refs[1].dtype)
        elif epi == 'mul_drelu2':
            o_refs[0][...] = (acc * (2.0 * jnp.maximum(aux_refs[0][...], 0.0))).astype(o_refs[0].dtype)

    def body(a_ref, b_ref, *rest):
        aux_refs = rest[:n_aux]
        o_refs = rest[n_aux:n_aux + (2 if epi == 'relu2' else 1)]
        part = lax.dot_general(a_ref[...].astype(BF16), b_ref[...].astype(BF16), dn, preferred_element_type=F32)
        if nk == 1:
            finish(part, aux_refs, o_refs)
        else:
            acc_ref = rest[-1]
            k = pl.program_id(2)

            @pl.when(k == 0)
            def _():
                acc_ref[...] = part

            @pl.when(k > 0)
            def _():
                acc_ref[...] += part

            @pl.when(k == nk - 1)
            def _():
                finish(acc_ref[...], aux_refs, o_refs)

    a_spec = pl.BlockSpec((tk, tm), lambda i, j, k: (k, i)) if ta else pl.BlockSpec((tm, tk), lambda i, j, k: (i, k))
    b_spec = pl.BlockSpec((tn, tk), lambda i, j, k: (j, k)) if tb else pl.BlockSpec((tk, tn), lambda i, j, k: (k, j))
    o_spec = pl.BlockSpec((tm, tn), lambda i, j, k: (i, j))
    if epi == 'relu2':
        out_shape = (jax.ShapeDtypeStruct((M, N), F32), jax.ShapeDtypeStruct((M, N), out_dtype))
        out_specs = (o_spec, o_spec)
    else:
        out_shape, out_specs = jax.ShapeDtypeStruct((M, N), out_dtype), o_spec
    return pl.pallas_call(
        body, name=name, out_shape=out_shape, grid=(M // tm, N // tn, nk),
        in_specs=[a_spec, b_spec] + [o_spec] * n_aux, out_specs=out_specs,
        scratch_shapes=[pltpu.VMEM((tm, tn), F32)] if nk > 1 else [],
        compiler_params=_params("parallel", "parallel", "arbitrary"),
    )(a, b, *([aux] if n_aux else []))


def rmsnorm_fwd(x, g, *, col=0, width=None, out_dtype=BF16, name, ts=256):
    S = x.shape[0]
    width = x.shape[1] if width is None else width
    ts, cb = _tile(S, ts), col // width

    def body(x_ref, g_ref, o_ref):
        xf = x_ref[...]
        r = lax.rsqrt(jnp.mean(xf * xf, axis=-1, keepdims=True) + EPS)
        o_ref[...] = (xf * r * g_ref[...]).astype(o_ref.dtype)

    return pl.pallas_call(
        body, name=name, out_shape=jax.ShapeDtypeStruct((S, width), out_dtype), grid=(S // ts,),
        in_specs=[pl.BlockSpec((ts, width), lambda i: (i, cb)), pl.BlockSpec((1, width), lambda i: (0, 0))],
        out_specs=pl.BlockSpec((ts, width), lambda i: (i, 0)), compiler_params=_params("parallel"),
    )(x, g.reshape(1, width))


def rmsnorm_bwd(x, g, dy, *, col=0, width=None, res=None, name, ts=256):
    S = x.shape[0]
    width = x.shape[1] if width is None else width
    ts, cb = _tile(S, ts), col // width
    has_res = res is not None

    def body(x_ref, g_ref, dy_ref, *rest):
        dx_ref, dg_ref = rest[-2:]
        xf = x_ref[...]
        r = lax.rsqrt(jnp.mean(xf * xf, axis=-1, keepdims=True) + EPS)
        dyf = dy_ref[...].astype(F32)
        dyg = dyf * g_ref[...]
        dx = r * dyg - xf * (r * r * r) * jnp.mean(dyg * xf, axis=-1, keepdims=True)
        if has_res:
            dx = dx + rest[0][...]
        dx_ref[...] = dx
        part = jnp.sum(dyf * xf * r, axis=0, keepdims=True)

        @pl.when(pl.program_id(0) == 0)
        def _():
            dg_ref[...] = part

        @pl.when(pl.program_id(0) > 0)
        def _():
            dg_ref[...] += part

    blk = pl.BlockSpec((ts, width), lambda i: (i, 0))
    dx, dg = pl.pallas_call(
        body, name=name,
        out_shape=(jax.ShapeDtypeStruct((S, width), F32), jax.ShapeDtypeStruct((1, width), F32)), grid=(S // ts,),
        in_specs=[pl.BlockSpec((ts, width), lambda i: (i, cb)), pl.BlockSpec((1, width), lambda i: (0, 0)), blk]
        + ([blk] if has_res else []),
        out_specs=(blk, pl.BlockSpec((1, width), lambda i: (0, 0))), compiler_params=_params("arbitrary"),
    )(x, g.reshape(1, width), dy, *([res] if has_res else []))
    return dx, dg.reshape(width)


def headnorm_fwd(x, g, *, col, heads, name, ts=1024):
    S = x.shape[0]
    ts, cb = _tile(S, ts), col // LANE

    def body(x_ref, g_ref, o_ref):
        xf = x_ref[...]
        r = lax.rsqrt(jnp.mean(xf * xf, axis=-1, keepdims=True) + EPS)
        o_ref[0] = (xf * r * g_ref[...]).astype(o_ref.dtype)

    return pl.pallas_call(
        body, name=name, out_shape=jax.ShapeDtypeStruct((heads, S, LANE), BF16), grid=(heads, S // ts),
        in_specs=[pl.BlockSpec((ts, LANE), lambda h, i: (i, cb + h)), pl.BlockSpec((1, LANE), lambda h, i: (0, 0))],
        out_specs=pl.BlockSpec((1, ts, LANE), lambda h, i: (h, i, 0)), compiler_params=_params("parallel", "parallel"),
    )(x, g.reshape(1, LANE))


def headnorm_bwd(x, g, dy, *, col, heads, name, ts=1024):
    S = x.shape[0]
    ts, cb = _tile(S, ts), col // LANE

    def body(x_ref, g_ref, dy_ref, dx_ref, dg_ref):
        xf = x_ref[...]
        r = lax.rsqrt(jnp.mean(xf * xf, axis=-1, keepdims=True) + EPS)
        dyf = dy_ref[0]
        dyg = dyf * g_ref[...]
        dx_ref[...] = r * dyg - xf * (r * r * r) * jnp.mean(dyg * xf, axis=-1, keepdims=True)
        part = jnp.sum(dyf * xf * r, axis=0, keepdims=True)
        first = jnp.logical_and(pl.program_id(0) == 0, pl.program_id(1) == 0)

        @pl.when(first)
        def _():
            dg_ref[...] = part

        @pl.when(jnp.logical_not(first))
        def _():
            dg_ref[...] += part

    dx, dg = pl.pallas_call(
        body, name=name,
        out_shape=(jax.ShapeDtypeStruct((S, heads * LANE), F32), jax.ShapeDtypeStruct((1, LANE), F32)),
        grid=(heads, S // ts),
        in_specs=[pl.BlockSpec((ts, LANE), lambda h, i: (i, cb + h)), pl.BlockSpec((1, LANE), lambda h, i: (0, 0)),
                  pl.BlockSpec((1, ts, LANE), lambda h, i: (h, i, 0))],
        out_specs=(pl.BlockSpec((ts, LANE), lambda h, i: (i, h)), pl.BlockSpec((1, LANE), lambda h, i: (0, 0))),
        compiler_params=_params("arbitrary", "arbitrary"),
    )(x, g.reshape(1, LANE), dy)
    return dx, dg.reshape(LANE)


def _rope_tables(S):
    pos = jnp.arange(S, dtype=F32)
    inv = ROPE_THETA ** (-jnp.arange(0, MLA_ROPE, 2, dtype=F32) / MLA_ROPE)
    ang = pos[:, None] * inv[None, :]
    c, s, z = jnp.cos(ang), jnp.sin(ang), jnp.zeros((S, 64), F32)
    return jnp.concatenate([c, c, z], axis=1), jnp.concatenate([-s, s, z], axis=1)


def _rope(v, cos, ssin, lane):
    partner = jnp.where(lane < 32, pltpu.roll(v, 96, 1), pltpu.roll(v, 32, 1))
    return v * cos + partner * ssin


def mla_prep_fwd(xn, xr, g, cos, ssin, *, n_col, n_stride, r_col, r_stride, heads, name, ts=1024):
    S = xn.shape[0]
    ts = _tile(S, ts)
    nb, ns, rb, rs = n_col // LANE, n_stride // LANE, r_col // LANE, r_stride // LANE
    gn = g[:MLA_NOPE].reshape(1, LANE)
    gr = jnp.concatenate([g[MLA_NOPE:], jnp.zeros((64,), F32)]).reshape(1, LANE)

    def body(n_ref, r_ref, gn_ref, gr_ref, c_ref, s_ref, o_ref):
        n, rr = n_ref[...], r_ref[...]
        ss = jnp.sum(n * n, axis=-1, keepdims=True) + jnp.sum(rr * rr, axis=-1, keepdims=True)
        r = lax.rsqrt(ss * (1.0 / MLA_QK) + EPS)
        lane = lax.broadcasted_iota(jnp.int32, rr.shape, 1)
        o_ref[0, :, :LANE] = (n * r * gn_ref[...]).astype(o_ref.dtype)
        o_ref[0, :, LANE:] = _rope(rr * r * gr_ref[...], c_ref[...], s_ref[...], lane).astype(o_ref.dtype)

    row = lambda h, i: (0, 0)
    return pl.pallas_call(
        body, name=name, out_shape=jax.ShapeDtypeStruct((heads, S, 2 * LANE), BF16), grid=(heads, S // ts),
        in_specs=[pl.BlockSpec((ts, LANE), lambda h, i: (i, nb + ns * h)),
                  pl.BlockSpec((ts, LANE), lambda h, i: (i, rb + rs * h)),
                  pl.BlockSpec((1, LANE), row), pl.BlockSpec((1, LANE), row),
                  pl.BlockSpec((ts, LANE), lambda h, i: (i, 0)), pl.BlockSpec((ts, LANE), lambda h, i: (i, 0))],
        out_specs=pl.BlockSpec((1, ts, 2 * LANE), lambda h, i: (h, i, 0)),
        compiler_params=_params("parallel", "parallel"),
    )(xn, xr, gn, gr, cos, ssin)


def mla_prep_bwd(xn, xr, g, cos, ssin, dy, *, n_col, n_stride, r_col, r_stride, heads, name, ts=1024):
    S = xn.shape[0]
    ts = _tile(S, ts)
    nb, ns, rb, rs = n_col // LANE, n_stride // LANE, r_col // LANE, r_stride // LANE
    shared = r_stride == 0
    gn = g[:MLA_NOPE].reshape(1, LANE)
    gr = jnp.concatenate([g[MLA_NOPE:], jnp.zeros((64,), F32)]).reshape(1, LANE)

    def body(n_ref, r_ref, gn_ref, gr_ref, c_ref, s_ref, dy_ref, dn_ref, dr_ref, dgn_ref, dgr_ref):
        i, h = pl.program_id(0), pl.program_id(1)
        n, rr = n_ref[...], r_ref[...]
        ss = jnp.sum(n * n, axis=-1, keepdims=True) + jnp.sum(rr * rr, axis=-1, keepdims=True)
        r = lax.rsqrt(ss * (1.0 / MLA_QK) + EPS)
        lane = lax.broadcasted_iota(jnp.int32, rr.shape, 1)
        dyn = dy_ref[0, :, :LANE]
        dyr = dy_ref[0, :, LANE:]
        t = dyr * s_ref[...]
        dvr = dyr * c_ref[...] + jnp.where(lane < 32, pltpu.roll(t, 96, 1), pltpu.roll(t, 32, 1))
        dvr = jnp.where(lane < 64, dvr, 0.0)
        dgn_part = jnp.sum(dyn * n * r, axis=0, keepdims=True)
        dgr_part = jnp.sum(dvr * rr * r, axis=0, keepdims=True)
        dyn_g, dvr_g = dyn * gn_ref[...], dvr * gr_ref[...]
        proj = (jnp.sum(dyn_g * n, axis=-1, keepdims=True) + jnp.sum(dvr_g * rr, axis=-1, keepdims=True)) * (1.0 / MLA_QK)
        r3 = r * r * r
        dn_ref[...] = r * dyn_g - n * r3 * proj
        dr = r * dvr_g - rr * r3 * proj
        if shared:
            @pl.when(h == 0)
            def _():
                dr_ref[...] = dr

            @pl.when(h > 0)
            def _():
                dr_ref[...] += dr
        else:
            dr_ref[...] = dr
        first = jnp.logical_and(i == 0, h == 0)

        @pl.when(first)
        def _():
            dgn_ref[...] = dgn_part
            dgr_ref[...] = dgr_part

        @pl.when(jnp.logical_not(first))
        def _():
            dgn_ref[...] += dgn_part
            dgr_ref[...] += dgr_part

    row = lambda i, h: (0, 0)
    dr_cols = LANE if shared else heads * LANE
    dn, dr, dgn, dgr = pl.pallas_call(
        body, name=name,
        out_shape=(jax.ShapeDtypeStruct((S, heads * LANE), F32), jax.ShapeDtypeStruct((S, dr_cols), F32),
                   jax.ShapeDtypeStruct((1, LANE), F32), jax.ShapeDtypeStruct((1, LANE), F32)),
        grid=(S // ts, heads),
        in_specs=[pl.BlockSpec((ts, LANE), lambda i, h: (i, nb + ns * h)),
                  pl.BlockSpec((ts, LANE), lambda i, h: (i, rb + rs * h)),
                  pl.BlockSpec((1, LANE), row), pl.BlockSpec((1, LANE), row),
                  pl.BlockSpec((ts, LANE), lambda i, h: (i, 0)), pl.BlockSpec((ts, LANE), lambda i, h: (i, 0)),
                  pl.BlockSpec((1, ts, 2 * LANE), lambda i, h: (h, i, 0))],
        out_specs=(pl.BlockSpec((ts, LANE), lambda i, h: (i, h)),
                   pl.BlockSpec((ts, LANE), (lambda i, h: (i, 0)) if shared else (lambda i, h: (i, h))),
                   pl.BlockSpec((1, LANE), row), pl.BlockSpec((1, LANE), row)),
        compiler_params=_params("arbitrary", "arbitrary"),
    )(xn, xr, gn, gr, cos, ssin, dy)
    return dn, dr, jnp.concatenate([dgn.reshape(LANE), dgr.reshape(LANE)[:MLA_ROPE]])


_NT = (((1,), (1,)), ((), ()))
_TN = (((0,), (0,)), ((), ()))


def attn_fwd(q, k, v, *, v_col, scale, name, bq=256):
    H, S, dk = q.shape
    Sk = k.shape[1]
    bq, vb = _tile(S, bq), v_col // LANE

    def body(q_ref, k_ref, v_ref, o_ref):
        s = lax.dot_general(q_ref[0], k_ref[0], _NT, preferred_element_type=F32) * scale
        e = jnp.exp(s - jnp.max(s, axis=-1, keepdims=True))
        p = e * (1.0 / jnp.sum(e, axis=-1, keepdims=True))
        o_ref[...] = jnp.dot(p.astype(BF16), v_ref[...].astype(BF16), preferred_element_type=F32).astype(o_ref.dtype)

    return pl.pallas_call(
        body, name=name, out_shape=jax.ShapeDtypeStruct((S, H * LANE), BF16), grid=(H, S // bq),
        in_specs=[pl.BlockSpec((1, bq, dk), lambda h, i: (h, i, 0)), pl.BlockSpec((1, Sk, dk), lambda h, i: (h, 0, 0)),
                  pl.BlockSpec((Sk, LANE), lambda h, i: (0, vb + h))],
        out_specs=pl.BlockSpec((bq, LANE), lambda h, i: (i, h)), compiler_params=_params("parallel", "parallel"),
    )(q, k, v)


def attn_bwd(q, k, v, do, *, v_col, scale, name, bq=256):
    H, S, dk = q.shape
    Sk = k.shape[1]
    bq, vb = _tile(S, bq), v_col // LANE

    def body(q_ref, k_ref, v_ref, do_ref, dq_ref, dk_ref, dv_ref):
        i = pl.program_id(1)
        qb, kb, vv = q_ref[0], k_ref[0], v_ref[...].astype(BF16)
        s = lax.dot_general(qb, kb, _NT, preferred_element_type=F32) * scale
        e = jnp.exp(s - jnp.max(s, axis=-1, keepdims=True))
        p = e * (1.0 / jnp.sum(e, axis=-1, keepdims=True))
        dob = do_ref[...].astype(BF16)
        dv_part = lax.dot_general(p.astype(BF16), dob, _TN, preferred_element_type=F32)
        dp = lax.dot_general(dob, vv, _NT, preferred_element_type=F32)
        ds = p * (dp - jnp.sum(p * dp, axis=-1, keepdims=True))
        dsb = (ds * scale).astype(BF16)
        dq_ref[0] = jnp.dot(dsb, kb, preferred_element_type=F32)
        dk_part = lax.dot_general(dsb, qb, _TN, preferred_element_type=F32)

        @pl.when(i == 0)
        def _():
            dk_ref[0] = dk_part
            dv_ref[...] = dv_part

        @pl.when(i > 0)
        def _():
            dk_ref[0] += dk_part
            dv_ref[...] += dv_part

    return pl.pallas_call(
        body, name=name,
        out_shape=(jax.ShapeDtypeStruct((H, S, dk), F32), jax.ShapeDtypeStruct((H, Sk, dk), F32),
                   jax.ShapeDtypeStruct((Sk, H * LANE), F32)),
        grid=(H, S // bq),
        in_specs=[pl.BlockSpec((1, bq, dk), lambda h, i: (h, i, 0)), pl.BlockSpec((1, Sk, dk), lambda h, i: (h, 0, 0)),
                  pl.BlockSpec((Sk, LANE), lambda h, i: (0, vb + h)), pl.BlockSpec((bq, LANE), lambda h, i: (i, h))],
        out_specs=(pl.BlockSpec((1, bq, dk), lambda h, i: (h, i, 0)), pl.BlockSpec((1, Sk, dk), lambda h, i: (h, 0, 0)),
                   pl.BlockSpec((Sk, LANE), lambda h, i: (0, h))),
        compiler_params=_params("parallel", "arbitrary"),
    )(q, k, v, do)


def _causal_scores(q, kblk, i, start, blk, scale, chunked, cq, ckblk):
    s = lax.dot_general(q, kblk, _NT, preferred_element_type=F32) * scale
    if cq is not None:
        s = s + cq - ckblk
    qpos = i * blk + lax.broadcasted_iota(jnp.int32, s.shape, 0)
    kpos = start + lax.broadcasted_iota(jnp.int32, s.shape, 1)
    ok = (kpos >> 6) <= (qpos >> 6) if chunked else kpos <= qpos
    return jnp.where(ok, s, NEG)


def causal_attn_fwd(q, k, v, *, v_col, chunked, scale, cq=None, ck=None, name, blk=256):
    H, S, dk = q.shape
    blk, vb = _tile(S, blk), v_col // LANE
    fox = cq is not None

    def body(q_ref, k_ref, v_ref, *rest):
        o_ref, lse_ref = rest[-2:]
        s = _causal_scores(q_ref[0], k_ref[0], pl.program_id(1), 0, blk, scale, chunked,
                           rest[0][0] if fox else None, rest[1][0] if fox else None)
        m = jnp.max(s, axis=-1, keepdims=True)
        p = jnp.exp(s - m)
        l = jnp.sum(p, axis=-1, keepdims=True)
        pv = jnp.dot(p.astype(BF16), v_ref[...].astype(BF16), preferred_element_type=F32)
        o_ref[...] = (pv * (1.0 / l)).astype(o_ref.dtype)
        lse_ref[0] = m + jnp.log(l)

    in_specs = [pl.BlockSpec((1, blk, dk), lambda h, i: (h, i, 0)), pl.BlockSpec((1, S, dk), lambda h, i: (h, 0, 0)),
                pl.BlockSpec((S, LANE), lambda h, i: (0, vb + h))]
    args = [q, k, v]
    if fox:
        in_specs += [pl.BlockSpec((1, blk, 1), lambda h, i: (h, i, 0)), pl.BlockSpec((1, 1, S), lambda h, i: (h, 0, 0))]
        args += [cq, ck]
    return pl.pallas_call(
        body, name=name, out_shape=(jax.ShapeDtypeStruct((S, H * LANE), BF16), jax.ShapeDtypeStruct((H, S, 1), F32)),
        grid=(H, S // blk), in_specs=in_specs,
        out_specs=(pl.BlockSpec((blk, LANE), lambda h, i: (i, h)), pl.BlockSpec((1, blk, 1), lambda h, i: (h, i, 0))),
        compiler_params=_params("parallel", "parallel"),
    )(*args)


def causal_attn_bwd(q, k, v, o, do, lse, *, v_col, chunked, scale, cq=None, ck=None, name, blk=256):
    H, S, dk = q.shape
    blk, vb = _tile(S, blk), v_col // LANE
    kc = 2 * blk if S % (2 * blk) == 0 else blk
    fox = cq is not None

    def body(q_ref, k_ref, v_ref, o_ref, do_ref, lse_ref, *rest):
        i = pl.program_id(1)
        if fox:
            cq_ref, ck_ref, dq_ref, dk_ref, dv_ref, dcq_ref, dck_ref = rest
        else:
            dq_ref, dk_ref, dv_ref = rest

        @pl.when(i == 0)
        def _():
            dk_ref[...] = jnp.zeros_like(dk_ref)
            dv_ref[...] = jnp.zeros_like(dv_ref)
            if fox:
                dck_ref[...] = jnp.zeros_like(dck_ref)

        qb, dob, lse_b = q_ref[0], do_ref[...].astype(BF16), lse_ref[0]
        delta = jnp.sum(do_ref[...].astype(F32) * o_ref[...].astype(F32), axis=-1, keepdims=True)
        dq_ref[...] = jnp.zeros_like(dq_ref)
        if fox:
            dcq_ref[...] = jnp.zeros_like(dcq_ref)
        for c in range(S // kc):
            @pl.when(c * kc < (i + 1) * blk)
            def _(ks=slice(c * kc, (c + 1) * kc), start=c * kc):
                kblk = k_ref[0, ks, :]
                s = _causal_scores(qb, kblk, i, start, blk, scale, chunked,
                                   cq_ref[0] if fox else None, ck_ref[0, :, ks] if fox else None)
                p = jnp.exp(s - lse_b)
                dv_ref[ks, :] += lax.dot_general(p.astype(BF16), dob, _TN, preferred_element_type=F32)
                dp = lax.dot_general(dob, v_ref[ks, :].astype(BF16), _NT, preferred_element_type=F32)
                ds = p * (dp - delta)
                dsb = (ds * scale).astype(BF16)
                dq_ref[0] += jnp.dot(dsb, kblk, preferred_element_type=F32)
                dk_ref[0, ks, :] += lax.dot_general(dsb, qb, _TN, preferred_element_type=F32)
                if fox:
                    dcq_ref[0] += jnp.sum(ds, axis=-1, keepdims=True)
                    dck_ref[0, :, ks] += -jnp.sum(ds, axis=0, keepdims=True)

    row = pl.BlockSpec((blk, LANE), lambda h, i: (i, h))
    in_specs = [pl.BlockSpec((1, blk, dk), lambda h, i: (h, i, 0)), pl.BlockSpec((1, S, dk), lambda h, i: (h, 0, 0)),
                pl.BlockSpec((S, LANE), lambda h, i: (0, vb + h)), row, row, pl.BlockSpec((1, blk, 1), lambda h, i: (h, i, 0))]
    args = [q, k, v, o, do, lse]
    out_shape = [jax.ShapeDtypeStruct((H, S, dk), F32), jax.ShapeDtypeStruct((H, S, dk), F32),
                 jax.ShapeDtypeStruct((S, H * LANE), F32)]
    out_specs = [pl.BlockSpec((1, blk, dk), lambda h, i: (h, i, 0)), pl.BlockSpec((1, S, dk), lambda h, i: (h, 0, 0)),
                 pl.BlockSpec((S, LANE), lambda h, i: (0, h))]
    if fox:
        fox_specs = [pl.BlockSpec((1, blk, 1), lambda h, i: (h, i, 0)), pl.BlockSpec((1, 1, S), lambda h, i: (h, 0, 0))]
        in_specs += fox_specs
        args += [cq, ck]
        out_shape += [jax.ShapeDtypeStruct((H, S, 1), F32), jax.ShapeDtypeStruct((H, 1, S), F32)]
        out_specs += fox_specs
    return pl.pallas_call(
        body, name=name, out_shape=tuple(out_shape), grid=(H, S // blk), in_specs=in_specs, out_specs=tuple(out_specs),
        compiler_params=_params("parallel", "arbitrary"),
    )(*args)


CPB = 4
BANDW = BAND + CHUNK
WIN = BAND + (CPB - 1) * CHUNK


def chunk_band_dbias(db):
    H = db.shape[0]
    d4 = db.reshape(H, CPB, CHUNK, WIN)
    return sum(d4[:, c, :, CHUNK * c:CHUNK * c + BAND] for c in range(CPB))


def _band_probs(qb, kb, bias, start, scale):
    s = lax.dot_general(qb, kb, _NT, preferred_element_type=F32) * scale
    real = start + lax.broadcasted_iota(jnp.int32, s.shape, 1) >= PAD
    s = jnp.where(real, s + bias, NEG)
    e = jnp.exp(s - jnp.max(s, axis=-1, keepdims=True))
    return e * (1.0 / jnp.sum(e, axis=-1, keepdims=True))


def band_fwd(q, kp, vp, bias, *, scale, name):
    H, S, _ = q.shape
    Sp, rows = S + PAD + CHUNK, CPB * CHUNK

    def body(q_ref, k_ref, v_ref, b_ref, o_ref):
        start = pl.multiple_of(pl.program_id(1) * rows, rows)
        p = _band_probs(q_ref[0], k_ref[0, pl.ds(start, WIN), :], b_ref[0], start, scale)
        vb = v_ref[pl.ds(start, WIN), :].astype(BF16)
        o_ref[...] = jnp.dot(p.astype(BF16), vb, preferred_element_type=F32).astype(o_ref.dtype)

    return pl.pallas_call(
        body, name=name, out_shape=jax.ShapeDtypeStruct((S, H * LANE), BF16), grid=(H, S // rows),
        in_specs=[pl.BlockSpec((1, rows, LANE), lambda h, j: (h, j, 0)), pl.BlockSpec((1, Sp, LANE), lambda h, j: (h, 0, 0)),
                  pl.BlockSpec((Sp, LANE), lambda h, j: (0, h)), pl.BlockSpec((1, rows, WIN), lambda h, j: (h, 0, 0))],
        out_specs=pl.BlockSpec((rows, LANE), lambda h, j: (j, h)), compiler_params=_params("parallel", "parallel"),
    )(q, kp, vp, bias)


def band_bwd(q, kp, vp, bias, do, *, scale, name):
    H, S, _ = q.shape
    Sp, rows = S + PAD + CHUNK, CPB * CHUNK

    def body(q_ref, k_ref, v_ref, b_ref, do_ref, dq_ref, dk_ref, dv_ref, db_ref):
        j = pl.program_id(1)

        @pl.when(j == 0)
        def _():
            dk_ref[...] = jnp.zeros_like(dk_ref)
            dv_ref[...] = jnp.zeros_like(dv_ref)
            db_ref[...] = jnp.zeros_like(db_ref)

        start = pl.multiple_of(j * rows, rows)
        qb = q_ref[0]
        kb = k_ref[0, pl.ds(start, WIN), :]
        vb = v_ref[pl.ds(start, WIN), :].astype(BF16)
        p = _band_probs(qb, kb, b_ref[0], start, scale)
        dob = do_ref[...].astype(BF16)
        dv_ref[pl.ds(start, WIN), :] += lax.dot_general(p.astype(BF16), dob, _TN, preferred_element_type=F32)
        dp = lax.dot_general(dob, vb, _NT, preferred_element_type=F32)
        ds = p * (dp - jnp.sum(p * dp, axis=-1, keepdims=True))
        db_ref[0] += ds
        dsb = (ds * scale).astype(BF16)
        dq_ref[0] = jnp.dot(dsb, kb, preferred_element_type=F32)
        dk_ref[0, pl.ds(start, WIN), :] += lax.dot_general(dsb, qb, _TN, preferred_element_type=F32)

    blk_b = pl.BlockSpec((1, rows, WIN), lambda h, j: (h, 0, 0))
    return pl.pallas_call(
        body, name=name,
        out_shape=(jax.ShapeDtypeStruct((H, S, LANE), F32), jax.ShapeDtypeStruct((H, Sp, LANE), F32),
                   jax.ShapeDtypeStruct((Sp, H * LANE), F32), jax.ShapeDtypeStruct((H, rows, WIN), F32)),
        grid=(H, S // rows),
        in_specs=[pl.BlockSpec((1, rows, LANE), lambda h, j: (h, j, 0)), pl.BlockSpec((1, Sp, LANE), lambda h, j: (h, 0, 0)),
                  pl.BlockSpec((Sp, LANE), lambda h, j: (0, h)), blk_b, pl.BlockSpec((rows, LANE), lambda h, j: (j, h))],
        out_specs=(pl.BlockSpec((1, rows, LANE), lambda h, j: (h, j, 0)), pl.BlockSpec((1, Sp, LANE), lambda h, j: (h, 0, 0)),
                   pl.BlockSpec((Sp, LANE), lambda h, j: (0, h)), blk_b),
        compiler_params=_params("parallel", "arbitrary"),
    )(q, kp, vp, bias, do)


def band_bias(rel_bias, *, name):
    H, rows, wide = rel_bias.shape[0], CPB * CHUNK, 1024
    last = rel_bias[:, 2 * REL_CLIP:]
    row0 = jnp.concatenate([jnp.tile(last, (1, PAD - REL_CLIP)), rel_bias[:, CHUNK + 1:][:, ::-1],
                            jnp.tile(last, (1, wide - BAND))], axis=1)

    def body(r_ref, o_ref):
        skew = pltpu.roll(jnp.broadcast_to(r_ref[0], (rows, wide)), 0, 1, stride=1, stride_axis=0)[:, :WIN]
        first = (lax.broadcasted_iota(jnp.int32, (rows, WIN), 0) >> 6) * CHUNK
        col = lax.broadcasted_iota(jnp.int32, (rows, WIN), 1)
        o_ref[0] = jnp.where(jnp.logical_and(col >= first, col < first + BAND), skew, NEG)

    return pl.pallas_call(
        body, name=name, out_shape=jax.ShapeDtypeStruct((H, rows, WIN), F32), grid=(H,),
        in_specs=[pl.BlockSpec((1, 1, wide), lambda h: (h, 0, 0))], out_specs=pl.BlockSpec((1, rows, WIN), lambda h: (h, 0, 0)),
        compiler_params=_params("parallel"),
    )(row0.reshape(H, 1, wide))


def relbias_bwd(dbias, *, name):
    H, W = dbias.shape[0], BANDW
    x = jnp.pad(dbias[:, :, ::-1], ((0, 0), (0, 0), (0, CHUNK)))

    def body(x_ref, o_ref):
        skew = pltpu.roll(x_ref[0], 0, 1, stride=1, stride_axis=0)
        f = jnp.broadcast_to(jnp.sum(skew, axis=0, keepdims=True), (8, W))
        lane = lax.broadcasted_iota(jnp.int32, (8, W), 1)
        direct = jnp.where(jnp.logical_and(lane >= 65, lane <= 255), pltpu.roll(f, 65, 1), 0.0)
        tail = jnp.sum(jnp.where(lane >= 191, f, 0.0), axis=-1, keepdims=True)
        o_ref[0] = direct + jnp.where(lane == 2 * REL_CLIP, tail, 0.0)

    out = pl.pallas_call(
        body, name=name, out_shape=jax.ShapeDtypeStruct((H, 8, W), F32), grid=(H,),
        in_specs=[pl.BlockSpec((1, CHUNK, W), lambda h: (h, 0, 0))], out_specs=pl.BlockSpec((1, 8, W), lambda h: (h, 0, 0)),
        compiler_params=_params("parallel"),
    )(x)
    return out[:, 0, :2 * REL_CLIP + 1]


def _split_dot(x, u, dn):
    hi = x.astype(BF16)
    r1 = x - hi.astype(F32)
    mid = r1.astype(BF16)
    lo = (r1 - mid.astype(F32)).astype(BF16)
    d = lambda t: lax.dot_general(t, u, dn, preferred_element_type=F32)
    return d(hi) + d(mid) + d(lo)


def _upper_ones(S):
    return (np.arange(S)[:, None] <= np.arange(S)[None, :]).astype(np.float32)


def foxgate_fwd(fl, b, *, name):
    H, S = fl.shape
    u = jnp.asarray(_upper_ones(S), BF16)

    def body(f_ref, b_ref, u_ref, o_ref):
        x = f_ref[...] + b_ref[...]
        lf = jnp.minimum(x, 0.0) - jnp.log(1.0 + jnp.exp(-jnp.abs(x)))
        o_ref[...] = _split_dot(lf, u_ref[...], (((1,), (0,)), ((), ())))

    return pl.pallas_call(body, name=name, out_shape=jax.ShapeDtypeStruct((H, S), F32),
                          compiler_params=pltpu.CompilerParams(vmem_limit_bytes=VMEM_LIMIT_BYTES))(fl, b.reshape(H, 1), u)


def foxgate_bwd(fl, b, dcum, *, name):
    H, S = fl.shape
    u = jnp.asarray(_upper_ones(S), BF16)

    def body(f_ref, b_ref, u_ref, dc_ref, df_ref, db_ref):
        x = f_ref[...] + b_ref[...]
        dlf = _split_dot(dc_ref[...], u_ref[...], _NT)
        df = dlf * (1.0 / (1.0 + jnp.exp(x)))
        df_ref[...] = df
        db_ref[...] = jnp.sum(df, axis=-1, keepdims=True)

    df, db = pl.pallas_call(body, name=name,
                            out_shape=(jax.ShapeDtypeStruct((H, S), F32), jax.ShapeDtypeStruct((H, 1), F32)),
                            compiler_params=pltpu.CompilerParams(vmem_limit_bytes=VMEM_LIMIT_BYTES))(fl, b.reshape(H, 1), u, dcum)
    return df, db.reshape(H)


def gate_fwd(z, proj, *, name, ts=256, tc=512):
    S, D = proj[0].shape
    ts, gb, nb = _tile(S, ts), Z_GATE // tc, D // tc

    def body(g0, g1, g2, p0, p1, p2, o_ref):
        acc = None
        for g_ref, p_ref in zip((g0, g1, g2), (p0, p1, p2)):
            t = (1.0 / (1.0 + jnp.exp(-g_ref[...]))) * p_ref[...]
            acc = t if acc is None else acc + t
        o_ref[...] = acc.astype(o_ref.dtype)

    blk = pl.BlockSpec((ts, tc), lambda i, j: (i, j))
    return pl.pallas_call(
        body, name=name, out_shape=jax.ShapeDtypeStruct((S, D), BF16), grid=(S // ts, nb),
        in_specs=[pl.BlockSpec((ts, tc), lambda i, j, n=n: (i, gb + n * nb + j)) for n in range(3)] + [blk] * 3,
        out_specs=blk, compiler_params=_params("parallel", "parallel"),
    )(z, z, z, *proj)


def gate_bwd(z, proj, dm, *, name, ts=256, tc=512):
    S, D = proj[0].shape
    ts, gb, nb = _tile(S, ts), Z_GATE // tc, D // tc

    def body(g0, g1, g2, p0, p1, p2, dm_ref, *outs):
        dmv = dm_ref[...]
        for n, (g_ref, p_ref) in enumerate(zip((g0, g1, g2), (p0, p1, p2))):
            sg = 1.0 / (1.0 + jnp.exp(-g_ref[...]))
            outs[n][...] = (dmv * sg).astype(BF16)
            outs[3 + n][...] = (dmv * p_ref[...] * sg * (1.0 - sg)).astype(BF16)

    blk = pl.BlockSpec((ts, tc), lambda i, j: (i, j))
    outs = pl.pallas_call(
        body, name=name, out_shape=tuple(jax.ShapeDtypeStruct((S, D), BF16) for _ in range(6)), grid=(S // ts, nb),
        in_specs=[pl.BlockSpec((ts, tc), lambda i, j, n=n: (i, gb + n * nb + j)) for n in range(3)] + [blk] * 4,
        out_specs=(blk,) * 6, compiler_params=_params("parallel", "parallel"),
    )(z, z, z, *proj, dm)
    return outs[:3], outs[3:]


def loss_head(y, target, *, name, ts=256):
    S, D = y.shape
    ts = _tile(S, ts)

    def body(y_ref, t_ref, l_ref, dy_ref):
        err = y_ref[...] - t_ref[...]
        dy_ref[...] = err * (1.0 / D)
        part = 0.5 * jnp.sum(jnp.mean(err * err, axis=-1, keepdims=True), axis=0, keepdims=True)

        @pl.when(pl.program_id(0) == 0)
        def _():
            l_ref[...] = part

        @pl.when(pl.program_id(0) > 0)
        def _():
            l_ref[...] += part

    blk = pl.BlockSpec((ts, D), lambda i: (i, 0))
    return pl.pallas_call(
        body, name=name, out_shape=(jax.ShapeDtypeStruct((1, 1), F32), jax.ShapeDtypeStruct((S, D), F32)), grid=(S // ts,),
        in_specs=[blk, blk], out_specs=(pl.BlockSpec((1, 1), lambda i: (0, 0)), blk), compiler_params=_params("arbitrary"),
    )(y, target)


def adamw(w, g, m, v, *, name):
    shape = w.shape
    C = shape[-1]
    R = int(np.prod(shape[:-1]))
    br = R
    while br % 16 == 0 and br * C * 4 > 2**20:
        br //= 2
    w2, g2, m2, v2 = (t.reshape(R, C) for t in (w, g, m, v))

    def body(w_ref, g_ref, m_ref, v_ref, d_ref, nm_ref, nv_ref):
        d_ref[...], nm_ref[...], nv_ref[...] = _adamw_update(w_ref[...], g_ref[...], m_ref[...], v_ref[...])

    blk = pl.BlockSpec((br, C), lambda i: (i, 0))
    outs = pl.pallas_call(
        body, name=name, out_shape=tuple(jax.ShapeDtypeStruct((R, C), F32) for _ in range(3)), grid=(R // br,),
        in_specs=[blk] * 4, out_specs=(blk,) * 3, compiler_params=_params("parallel"),
    )(w2, g2, m2, v2)
    return tuple(o.reshape(shape) for o in outs)


_ANY = pl.BlockSpec(memory_space=pl.ANY)


def _place():
    return lax.axis_index("x"), lax.axis_index("y"), lax.axis_index("c")


def all_gather(xs, after, *, name):
    n = len(xs)

    def body(*refs):
        x_refs, o_refs = refs[:n], refs[n + 1:2 * n + 1]
        send_sems, recv_sems, local_sems = refs[2 * n + 1:]
        px, py, pc = _place()
        me, sibling = (px, py, pc), (px, py, 1 - pc)
        chips = [(1 - px, py), (px, 1 - py), (1 - px, 1 - py)]

        def slot(t, dev):
            return o_refs[t].at[4 * dev[0] + 2 * dev[1] + dev[2]]

        def copy(t, k, block, to, src=None):
            return pltpu.make_async_remote_copy(
                src_ref=slot(t, block) if src is None else src, dst_ref=slot(t, block),
                send_sem=send_sems.at[t, k], recv_sem=recv_sems.at[t, k], device_id=to, device_id_type=MESH)

        mine = [pltpu.make_async_copy(x_refs[t], slot(t, me), local_sems.at[t]) for t in range(n)]
        first = []
        for t in range(n):
            mine[t].start()
            first += [copy(t, 1 + j, me, (*chip, pc), src=x_refs[t]) for j, chip in enumerate(chips)]
            first.append(copy(t, 0, me, sibling, src=x_refs[t]))
        for cp in first:
            cp.start()
        passed = []
        for t in range(n):
            for j, chip in enumerate(chips):
                copy(t, 1 + j, (*chip, pc), me).wait_recv()
                fwd = copy(t, 4 + j, (*chip, pc), sibling)
                fwd.start()
                passed.append(fwd)
        for t in range(n):
            copy(t, 0, sibling, me).wait_recv()
            for j, chip in enumerate(chips):
                copy(t, 4 + j, (*chip, 1 - pc), me).wait_recv()
        for cp in first + passed:
            cp.wait_send()
        for cp in mine:
            cp.wait()

    return pl.pallas_call(
        body, name=name, out_shape=tuple(jax.ShapeDtypeStruct((8,) + x.shape, x.dtype) for x in xs),
        in_specs=[_ANY] * (n + 1), out_specs=(_ANY,) * n,
        scratch_shapes=[pltpu.SemaphoreType.DMA((n, 7)), pltpu.SemaphoreType.DMA((n, 7)), pltpu.SemaphoreType.DMA((n,))],
    )(*xs, after)


def forward_to_sibling(zones, *, name):
    n = len(zones)

    def body(*refs):
        z_refs, (send_sems, recv_sems) = refs[n:2 * n], refs[2 * n:]
        px, py, pc = _place()
        copies = []
        for t in range(n):
            for j, chip in enumerate([(1 - px, py), (px, 1 - py), (1 - px, 1 - py)]):
                slot = z_refs[t].at[4 * chip[0] + 2 * chip[1] + pc]
                copies.append(pltpu.make_async_remote_copy(
                    src_ref=slot, dst_ref=slot, send_sem=send_sems.at[t, j], recv_sem=recv_sems.at[t, j],
                    device_id=(px, py, 1 - pc), device_id_type=MESH))
        for cp in copies:
            cp.start()
        for cp in copies:
            cp.wait()

    return pl.pallas_call(
        body, name=name, out_shape=tuple(jax.ShapeDtypeStruct(z.shape, z.dtype) for z in zones),
        in_specs=[_ANY] * n, out_specs=(_ANY,) * n, input_output_aliases={t: t for t in range(n)},
        scratch_shapes=[pltpu.SemaphoreType.DMA((n, 3)), pltpu.SemaphoreType.DMA((n, 3))],
    )(*zones)


_HBM = pl.BlockSpec(memory_space=pltpu.HBM)
_SEM = pl.BlockSpec(memory_space=pltpu.SEMAPHORE)
_EFFECT = pltpu.SideEffectType.DATAFLOW_SIDE_EFFECTING


N_PEERS = 7
GATHER_FLIPS = (1, 2, 4, 6)


def _peers(flips=range(1, N_PEERS + 1)):
    px, py, pc = _place()
    flip = lambda p, bit: 1 - p if bit else p
    return [(flip(px, m >> 2 & 1), flip(py, m >> 1 & 1), flip(pc, m & 1)) for m in flips]


def _gather_copies(src_refs, zone_refs, send_sems, recv_sems):
    px, py, pc = _place()
    return [pltpu.make_async_remote_copy(src_ref=s, dst_ref=z.at[4 * px + 2 * py + pc], send_sem=send_sems.at[k],
                                         recv_sem=recv_sems.at[k], device_id=peer, device_id_type=MESH)
            for k, peer in enumerate(_peers(GATHER_FLIPS)) for s, z in zip(src_refs, zone_refs)]


def _scatter_copies(part_refs, zone_refs, send_sems, recv_sems):
    return [pltpu.make_async_remote_copy(src_ref=p.at[4 * peer[0] + 2 * peer[1] + peer[2]], dst_ref=z.at[k],
                                         send_sem=send_sems.at[k], recv_sem=recv_sems.at[k], device_id=peer, device_id_type=MESH)
            for k, peer in enumerate(_peers()) for p, z in zip(part_refs, zone_refs)]


def copies_start(make, srcs, zones, *, name, after=None):
    n = len(srcs)
    extra = [] if after is None else [after]

    def body(*refs):
        k = 2 * n + len(extra)
        for cp in make(refs[:n], refs[n:2 * n], refs[k], refs[k + 1]):
            cp.start()
        refs[-1][...] = jnp.zeros_like(refs[-1])

    arrays = list(srcs) + list(zones)
    outs = pl.pallas_call(
        body, name=name,
        out_shape=(pltpu.SemaphoreType.DMA((N_PEERS,)), pltpu.SemaphoreType.DMA((N_PEERS,)),
                   *[pltpu.HBM(a.shape, a.dtype) for a in arrays], jax.ShapeDtypeStruct((8, LANE), F32)),
        in_specs=[_HBM] * (2 * n) + [_ANY] * len(extra),
        out_specs=(_SEM, _SEM, *[_HBM] * (2 * n), pl.BlockSpec(memory_space=pltpu.VMEM)),
        input_output_aliases={i: 2 + i for i in range(2 * n)},
        compiler_params=pltpu.CompilerParams(has_side_effects=_EFFECT),
    )(*[pltpu.with_memory_space_constraint(a, pltpu.HBM) for a in arrays], *extra)
    return outs[0], outs[1], list(outs[2:2 + n]), list(outs[2 + n:2 + 2 * n]), outs[-1]


def copies_wait(make, send_sems, recv_sems, srcs, zones, after, *, name):
    n = len(srcs)

    def body(*refs):
        for cp in make(refs[:n], refs[n:2 * n], refs[2 * n], refs[2 * n + 1]):
            cp.wait_send()
            cp.wait_recv()

    arrays = list(srcs) + list(zones)
    outs = pl.pallas_call(
        body, name=name, out_shape=tuple(pltpu.HBM(a.shape, a.dtype) for a in arrays),
        in_specs=[_HBM] * (2 * n) + [_SEM, _SEM, _ANY], out_specs=(_HBM,) * (2 * n),
        input_output_aliases={i: i for i in range(2 * n)},
        compiler_params=pltpu.CompilerParams(has_side_effects=_EFFECT),
    )(*arrays, send_sems, recv_sems, after)
    return list(outs[:n]), list(outs[n:])


def place_mine(zone, mine, *, name):
    C = mine.shape[-1]
    R = int(np.prod(mine.shape[:-1]))
    br = _row_block(R, C, mine.dtype.itemsize, budget=2**21)
    me = (4 * lax.axis_index("x") + 2 * lax.axis_index("y") + lax.axis_index("c")).astype(jnp.int32).reshape(1)

    def body(me_ref, m_ref, z_ref, o_ref):
        o_ref[0] = m_ref[...]

    out = pl.pallas_call(
        body, name=name, out_shape=jax.ShapeDtypeStruct((8, R, C), zone.dtype),
        grid_spec=pltpu.PrefetchScalarGridSpec(
            num_scalar_prefetch=1, grid=(R // br,), in_specs=[pl.BlockSpec((br, C), lambda i, me: (i, 0)), _ANY],
            out_specs=pl.BlockSpec((1, br, C), lambda i, me: (me[0], i, 0))),
        input_output_aliases={2: 0}, compiler_params=_params("parallel"),
    )(me, mine.reshape(R, C), zone.reshape(8, R, C))
    return out.reshape(zone.shape)


def _row_block(rows, cols, itemsize, budget=2**20):
    br = rows
    while br % 32 == 0 and br * cols * itemsize > budget:
        br //= 2
    return br


def _adamw_update(w, g, m, v):
    nm = ADAM_B1 * m + (1.0 - ADAM_B1) * g
    nv = ADAM_B2 * v + (1.0 - ADAM_B2) * (g * g)
    m_hat = nm / (1.0 - ADAM_B1 ** ADAM_STEP)
    v_hat = nv / (1.0 - ADAM_B2 ** ADAM_STEP)
    return -ADAM_LR * (m_hat / (jnp.sqrt(v_hat) + ADAM_EPS) + ADAM_WD * w), nm, nv


def grad_sum_adamw(parts, recvs, w, m, v, *, name):
    L, C = len(parts), w.shape[-1]
    R = int(np.prod(w.shape[1:-1]))
    br = _row_block(R, C, 4, budget=2**19)
    chip = (4 * lax.axis_index("x") + 2 * lax.axis_index("y") + lax.axis_index("c")).astype(jnp.int32).reshape(1)

    def body(c_ref, *refs):
        p_refs, r_refs = refs[:L], refs[L:2 * L]
        w_ref, m_ref, v_ref, g_out, d_out, nm_out, nv_out = refs[2 * L:]
        for j in range(L):
            @pl.when(pl.program_id(0) == j)
            def _(j=j):
                g = p_refs[j][0].astype(F32)
                for k in range(N_PEERS):
                    g = g + r_refs[j][k].astype(F32)
                g_out[0] = g
                d_out[0], nm_out[0], nv_out[0] = _adamw_update(w_ref[0], g, m_ref[0], v_ref[0])

    row = lambda j: (lambda l, r, c: jnp.where(l == j, r, 0))
    part_specs = [pl.BlockSpec((1, br, C), lambda l, r, c, f=row(j): (c[0], f(l, r, c), 0)) for j in range(L)]
    recv_specs = [pl.BlockSpec((N_PEERS, br, C), lambda l, r, c, f=row(j): (0, f(l, r, c), 0)) for j in range(L)]
    blk = pl.BlockSpec((1, br, C), lambda l, r, c: (l, r, 0))
    outs = pl.pallas_call(
        body, name=name, out_shape=tuple(jax.ShapeDtypeStruct((L, R, C), F32) for _ in range(4)),
        grid_spec=pltpu.PrefetchScalarGridSpec(
            num_scalar_prefetch=1, grid=(L, R // br), in_specs=part_specs + recv_specs + [blk] * 3, out_specs=(blk,) * 4),
        compiler_params=_params("arbitrary", "arbitrary"),
    )(chip, *[p.reshape(8, R, C) for p in parts], *[r.reshape(N_PEERS, R, C) for r in recvs],
      *[t.reshape(L, R, C) for t in (w, m, v)])
    return tuple(o.reshape(w.shape) for o in outs)


def ordered_sum(parts, *, name):
    _, R, C = parts.shape

    def body(p_ref, o_ref):
        acc = p_ref[0]
        for d in range(1, 8):
            acc = acc + p_ref[d]
        o_ref[...] = acc

    return pl.pallas_call(body, name=name, out_shape=jax.ShapeDtypeStruct((R, C), F32))(parts)


W_IN_COLS, W_IN_SHARD = 13128, 1641
W_IN_SEGMENTS = ((0, 832, 0), (832, 3904, Z_FOX), (3904, 3912, FF_COL), (3912, 6984, Z_CH), (6984, 13128, Z_GATE))


def col_gather(src, table, pieces, out_shape, *, name, tr=1024, after=None):
    R, C = src.shape[1:]
    tr = _tile(R, tr)
    width = 2 + 6 * pieces
    nb = table.shape[0] // width
    last_tile, last_valid = C // LANE, C % LANE

    extra = [] if after is None else [after]

    def body(tab, *refs):
        o_ref, acc_ref = refs[-2:]
        base = pl.program_id(1) * width
        lane = lax.broadcasted_iota(jnp.int32, (tr, LANE), 1)
        row = lax.broadcasted_iota(jnp.int32, (2 * LANE, LANE), 0)
        col = lax.broadcasted_iota(jnp.int32, (2 * LANE, LANE), 1)
        acc_ref[...] = jnp.zeros_like(acc_ref)
        for p in range(pieces):
            e = base + 2 + 6 * p
            lo, hi = tab[e + 4], tab[e + 5]

            @pl.when(hi > lo)
            def _(p=p, e=e, lo=lo, hi=hi):
                tiles = []
                for tcol in (1, 2):
                    x = refs[2 * p + tcol - 1][0]
                    if last_valid:
                        x = jnp.where(jnp.logical_or(tab[e + tcol] < last_tile, lane < last_valid), x, jnp.zeros_like(x))
                    tiles.append(x)
                hit = jnp.logical_and(row == col + tab[e + 3], jnp.logical_and(col >= lo, col < hi))
                sel = jnp.where(hit, 1.0, 0.0).astype(src.dtype)
                acc_ref[...] += jnp.dot(jnp.concatenate(tiles, axis=1), sel, preferred_element_type=F32)
        o_ref[0] = acc_ref[...].astype(o_ref.dtype)

    in_specs = []
    for p in range(pieces):
        for tcol in (1, 2):
            in_specs.append(pl.BlockSpec(
                (1, tr, LANE), lambda i, b, tab, p=p, tcol=tcol: (tab[b * width + 2 + 6 * p], i, tab[b * width + 2 + 6 * p + tcol])))
    return pl.pallas_call(
        body, name=name, out_shape=jax.ShapeDtypeStruct(out_shape, src.dtype),
        grid_spec=pltpu.PrefetchScalarGridSpec(
            num_scalar_prefetch=1, grid=(R // tr, nb), in_specs=in_specs + [_ANY] * len(extra),
            out_specs=pl.BlockSpec((1, tr, LANE), lambda i, b, tab: (tab[b * width], i, tab[b * width + 1])),
            scratch_shapes=[pltpu.VMEM((tr, LANE), F32)]),
        compiler_params=_params("parallel", "parallel"),
    )(jnp.asarray(table, jnp.int32), *([src] * (2 * pieces)), *extra)


def _piece(sd, start, lo, hi, last_tile):
    t0 = start // LANE
    return [sd, t0, min(t0 + 1, last_tile), start % LANE - lo, lo, hi]


def _pad_pieces(rows, pieces):
    out, prev = [], [0, 0, 0, 0, 0, 0] * pieces
    for head, pcs in rows:
        full = list(pcs)
        for p in range(len(pcs) // 6, pieces):
            full += prev[6 * p:6 * p + 3] + [0, 0, 0]
        out.append(head + full)
        prev = full
    return np.asarray(out, np.int32).reshape(-1)


def _w_in_table(layer, L):
    rows = []
    for b in range(Z_W // LANE):
        pcs = []
        for first, last, col in W_IN_SEGMENTS:
            lo, hi = max(LANE * b, col), min(LANE * (b + 1), col + last - first)
            while lo < hi:
                c = first + lo - col
                n = min(hi - lo, W_IN_SHARD - c % W_IN_SHARD)
                pcs += _piece((c // W_IN_SHARD) * L + layer, c % W_IN_SHARD, lo - LANE * b, lo - LANE * b + n, W_IN_SHARD // LANE)
                lo += n
        assert len(pcs) <= 12
        rows.append(([0, b], pcs))
    return _pad_pieces(rows, 2)


def _w_in_grad_table():
    rows = []
    for d in range(8):
        for t in range(-(-W_IN_SHARD // LANE)):
            pcs = []
            c0 = d * W_IN_SHARD + LANE * t
            c1 = min(c0 + LANE, (d + 1) * W_IN_SHARD)
            for first, last, col in W_IN_SEGMENTS:
                lo, hi = max(c0, first), min(c1, last)
                if lo < hi:
                    pcs += _piece(0, col + lo - first, lo - c0, hi - c0, Z_W // LANE - 1)
            assert len(pcs) <= 18
            rows.append(([d, t], pcs))
    return _pad_pieces(rows, 3)


def block_copy(src, out_shape, in_blk, out_blk, grid, in_map, out_map, *, name):
    def body(x_ref, o_ref):
        o_ref[(0,) * (len(out_blk) - 2) + (Ellipsis,)] = x_ref[(0,) * (len(in_blk) - 2) + (Ellipsis,)]

    return pl.pallas_call(
        body, name=name, out_shape=jax.ShapeDtypeStruct(out_shape, src.dtype), grid=grid,
        in_specs=[pl.BlockSpec(in_blk, in_map)], out_specs=pl.BlockSpec(out_blk, out_map),
        compiler_params=_params("parallel", "parallel"),
    )(src)


def _columns_from_owners(z, *, name, lead=()):
    K, c = z.shape[-2:]
    tr, nl = _tile(K, 1024), len(lead)
    return block_copy(z, (K, 8 * c), (1,) * (1 + nl) + (tr, c), (tr, c), (8, K // tr),
                      lambda d, i: (d, *lead, i, 0), lambda d, i: (i, d), name=name)


def _owners_from_columns(g, *, name):
    K, c = g.shape[0], g.shape[1] // 8
    tr = _tile(K, 1024)
    return block_copy(g, (8, K, c), (tr, c), (1, tr, c), (8, K // tr), lambda d, i: (i, d), lambda d, i: (d, i, 0), name=name)


def _full_from_shards(k, sh, tag):
    if k not in COL_SHARDED:
        return sh.reshape((-1, sh.shape[-1]))
    if k == 'w_br':
        return [_columns_from_owners(sh, lead=(n,), name=f"{tag}_w_br{n}_layout") for n in range(3)]
    if k == 'w_uq':
        return _columns_from_owners(jnp.pad(sh, ((0, 0), (0, 0), (0, 64))), name=f"{tag}_w_uq_layout")
    if k == 'w_ukv':
        return block_copy(sh, (256, 2048), (1, 256, LANE), (256, LANE), (2, MLA_HEADS),
                          lambda t, h: (h, 0, t), lambda t, h: (0, t * MLA_HEADS + h), name=f"{tag}_w_ukv_layout")
    return _columns_from_owners(sh, name=f"{tag}_{k}_layout")


def _shards_from_full(k, g, tag):
    if k not in COL_SHARDED:
        return g.reshape((8, g.shape[0] // 8, g.shape[1]))
    if k == 'w_br':
        return jnp.stack([_owners_from_columns(g[n], name=f"{tag}_dw_br{n}_layout") for n in range(3)], axis=1)
    if k == 'w_uq':
        return _owners_from_columns(g, name=f"{tag}_dw_uq_layout")[:, :, :MLA_QK]
    if k == 'w_ukv':
        return block_copy(g, (8, 256, 256), (256, LANE), (1, 256, LANE), (2, MLA_HEADS),
                          lambda t, h: (0, t * MLA_HEADS + h), lambda t, h: (h, 0, t), name=f"{tag}_dw_ukv_layout")
    return _owners_from_columns(g, name=f"{tag}_d{k}_layout")


def w_in_full(gathered, layer, *, name, after=None):
    _, L, K, c = gathered.shape
    return col_gather(gathered.reshape(8 * L, K, c), _w_in_table(layer, L), 2, (1, K, Z_W), name=name, after=after)[0]


def w_in_shards(g, *, name):
    return col_gather(g[None], _w_in_grad_table(), 3, (8, g.shape[0], W_IN_SHARD), name=name)


def _layer_fwd(x, mem, W, P, cos, ssin, tag, later=None):
    S = x.shape[0]
    sv = {'x0': x}
    h = rmsnorm_fwd(x, P['g_mix'], name=f"{tag}_norm_mix")
    z = mm(h, W['w_in'], name=f"{tag}_mm_in")
    if later is not None:
        W = later(z)
    sv.update(h=h, z=z)
    cqn = rmsnorm_fwd(z, P['g_cq'], col=0, width=512, name=f"{tag}_norm_cq")
    ckvn = rmsnorm_fwd(z, P['g_ckv'], col=512, width=256, name=f"{tag}_norm_ckv")
    qf = mm(cqn, W['w_uq'], name=f"{tag}_mm_uq")
    kvf = mm(ckvn, W['w_ukv'], name=f"{tag}_mm_ukv")
    qa = mla_prep_fwd(qf, qf, P['g_mla_q'], cos, ssin, n_col=0, n_stride=2 * LANE, r_col=LANE, r_stride=2 * LANE,
                      heads=8, name=f"{tag}_mla_q")
    ka = mla_prep_fwd(kvf, z, P['g_mla_k'], cos, ssin, n_col=0, n_stride=LANE, r_col=KR_COL, r_stride=0,
                      heads=8, name=f"{tag}_mla_k")
    ya, lse_a = causal_attn_fwd(qa, ka, kvf, v_col=1024, chunked=True, scale=MLA_QK ** -0.5, name=f"{tag}_mla_attn")
    sv.update(cqn=cqn, ckvn=ckvn, qf=qf, kvf=kvf, qa=qa, ka=ka, lse_a=lse_a)
    qb = headnorm_fwd(z, P['g_fox_q'], col=Z_FOX, heads=8, name=f"{tag}_fox_qn")
    kb = headnorm_fwd(z, P['g_fox_k'], col=Z_FOX + 1024, heads=8, name=f"{tag}_fox_kn")
    fl = z[:, FF_COL:FF_COL + 8].T
    cum = foxgate_fwd(fl, P['b_f'], name=f"{tag}_fox_gate")
    cq, ck = cum.reshape(8, S, 1), cum.reshape(8, 1, S)
    yb, lse_b = causal_attn_fwd(qb, kb, z, v_col=Z_FOX + 2048, chunked=False, scale=LANE ** -0.5, cq=cq, ck=ck,
                                name=f"{tag}_fox_attn")
    sv.update(qb=qb, kb=kb, fl=fl, cq=cq, ck=ck, lse_b=lse_b)
    qc = headnorm_fwd(z, P['g_ch_q'], col=Z_CH, heads=8, name=f"{tag}_ch_qn")
    kc = headnorm_fwd(z, P['g_ch_k'], col=Z_CH + 1024, heads=8, name=f"{tag}_ch_kn")
    kcp = jnp.pad(kc, ((0, 0), (PAD, CHUNK), (0, 0)))
    vcp = jnp.pad(z[:, Z_CH + 2048:Z_CH + 3072], ((PAD, CHUNK), (0, 0)))
    bias = band_bias(P['rel_bias'], name=f"{tag}_ch_bias")
    yc = band_fwd(qc, kcp, vcp, bias, scale=LANE ** -0.5, name=f"{tag}_ch_attn")
    sv.update(qc=qc, kcp=kcp, vcp=vcp, bias=bias)
    ys = (ya, yb, yc)
    proj = [mm(ys[n], W['w_br'][n], name=f"{tag}_mm_br{n}") for n in range(3)]
    merged = gate_fwd(z, proj, name=f"{tag}_gate")
    x1 = mm(merged, W['w_out'], epi='add', aux=x, name=f"{tag}_mm_out")
    sv.update(ys=ys, proj=proj, merged=merged, x1=x1)
    hc = rmsnorm_fwd(x1, P['g_cross'], name=f"{tag}_norm_cross")
    memn = rmsnorm_fwd(mem, P['g_mem'], name=f"{tag}_norm_mem")
    qx_raw = mm(hc, W['w_xq'], name=f"{tag}_mm_xq")
    memkv = mm(memn, W['w_xkv'], name=f"{tag}_mm_xkv")
    qx = headnorm_fwd(qx_raw, P['g_x_q'], col=0, heads=4, name=f"{tag}_x_qn")
    kx = headnorm_fwd(memkv, P['g_x_k'], col=0, heads=4, name=f"{tag}_x_kn")
    ox = attn_fwd(qx, kx, memkv, v_col=512, scale=LANE ** -0.5, name=f"{tag}_x_attn")
    x2 = mm(ox, W['w_xo'], epi='add', aux=x1, name=f"{tag}_mm_xo")
    sv.update(hc=hc, memn=memn, qx_raw=qx_raw, memkv=memkv, qx=qx, kx=kx, ox=ox, x2=x2)
    hm = rmsnorm_fwd(x2, P['g_mlp'], name=f"{tag}_norm_mlp")
    u, a = mm(hm, W['w_1'], epi='relu2', out_dtype=BF16, name=f"{tag}_mm_w1")
    x3 = mm(a, W['w_2'], epi='add', aux=x2, name=f"{tag}_mm_w2")
    sv.update(hm=hm, u=u, a=a)
    return x3, sv


def _layer_bwd(dx, mem, W, P, sv, cos, ssin, tag, send_off=None):
    S = dx.shape[0]
    z = sv['z']
    gw, gs = {}, {}
    wgrad = lambda a, d, name: mm(a, d, ta=True, out_dtype=BF16, name=name)
    gw['w_2'] = wgrad(sv['a'], dx, f"{tag}_dw2")
    du = mm(dx, W['w_2'], tb=True, epi='mul_drelu2', aux=sv['u'], out_dtype=BF16, name=f"{tag}_du")
    gw['w_1'] = wgrad(sv['hm'], du, f"{tag}_dw1")
    dhm = mm(du, W['w_1'], tb=True, name=f"{tag}_dhm")
    dx, gs['g_mlp'] = rmsnorm_bwd(sv['x2'], P['g_mlp'], dhm, res=dx, name=f"{tag}_dnorm_mlp")
    gw['w_xo'] = wgrad(sv['ox'], dx, f"{tag}_dwxo")
    dox = mm(dx, W['w_xo'], tb=True, out_dtype=BF16, name=f"{tag}_dox")
    dqx, dkx, dvx = attn_bwd(sv['qx'], sv['kx'], sv['memkv'], dox, v_col=512, scale=LANE ** -0.5, name=f"{tag}_x_attn_bwd")
    dqx_raw, gs['g_x_q'] = headnorm_bwd(sv['qx_raw'], P['g_x_q'], dqx, col=0, heads=4, name=f"{tag}_x_qn_bwd")
    dkx_raw, gs['g_x_k'] = headnorm_bwd(sv['memkv'], P['g_x_k'], dkx, col=0, heads=4, name=f"{tag}_x_kn_bwd")
    dqx_b = dqx_raw.astype(BF16)
    gw['w_xq'] = wgrad(sv['hc'], dqx_b, f"{tag}_dwxq")
    dhc = mm(dqx_b, W['w_xq'], tb=True, name=f"{tag}_dhc")
    dx, gs['g_cross'] = rmsnorm_bwd(sv['x1'], P['g_cross'], dhc, res=dx, name=f"{tag}_dnorm_cross")
    dmemkv = jnp.concatenate([dkx_raw, dvx], axis=1).astype(BF16)
    gw['w_xkv'] = wgrad(sv['memn'], dmemkv, f"{tag}_dwxkv")
    dmemn = mm(dmemkv, W['w_xkv'], tb=True, name=f"{tag}_dmemn")
    _, gs['g_mem'] = rmsnorm_bwd(mem, P['g_mem'], dmemn, name=f"{tag}_dnorm_mem")
    gw['w_out'] = wgrad(sv['merged'], dx, f"{tag}_dwout")
    dmerged = mm(dx, W['w_out'], tb=True, name=f"{tag}_dmerged")
    dproj, dgl = gate_bwd(z, sv['proj'], dmerged, name=f"{tag}_gate_bwd")
    gw['w_br'] = [wgrad(sv['ys'][n], dproj[n], f"{tag}_dwbr{n}") for n in range(3)]
    dys = [mm(dproj[n], W['w_br'][n], tb=True, out_dtype=BF16, name=f"{tag}_dys{n}") for n in range(3)]
    dqa, dka, dva = causal_attn_bwd(sv['qa'], sv['ka'], sv['kvf'], sv['ys'][0], dys[0], sv['lse_a'], v_col=1024, chunked=True,
                                    scale=MLA_QK ** -0.5, name=f"{tag}_mla_attn_bwd")
    g_mla_q = P['g_mla_q'] if send_off is None else P['g_mla_q'] + send_off[0](gw)
    dqn, dqr, gs['g_mla_q'] = mla_prep_bwd(sv['qf'], sv['qf'], g_mla_q, cos, ssin, dqa, n_col=0, n_stride=2 * LANE,
                                           r_col=LANE, r_stride=2 * LANE, heads=8, name=f"{tag}_mla_q_bwd")
    dkn, dkr, gs['g_mla_k'] = mla_prep_bwd(sv['kvf'], z, P['g_mla_k'], cos, ssin, dka, n_col=0, n_stride=LANE,
                                           r_col=KR_COL, r_stride=0, heads=8, name=f"{tag}_mla_k_bwd")
    dqf = jnp.stack([dqn.reshape(S, 8, LANE), dqr.reshape(S, 8, LANE)], axis=2).reshape(S, 2048).astype(BF16)
    dkvf = jnp.concatenate([dkn, dva], axis=1).astype(BF16)
    gw['w_uq'] = wgrad(sv['cqn'], dqf, f"{tag}_dwuq")
    gw['w_ukv'] = wgrad(sv['ckvn'], dkvf, f"{tag}_dwukv")
    dcqn = mm(dqf, W['w_uq'], tb=True, name=f"{tag}_dcqn")
    dckvn = mm(dkvf, W['w_ukv'], tb=True, name=f"{tag}_dckvn")
    dcq_raw, gs['g_cq'] = rmsnorm_bwd(z, P['g_cq'], dcqn, col=0, width=512, name=f"{tag}_dnorm_cq")
    dckv_raw, gs['g_ckv'] = rmsnorm_bwd(z, P['g_ckv'], dckvn, col=512, width=256, name=f"{tag}_dnorm_ckv")
    dqb, dkb, dvb, dcq, dck = causal_attn_bwd(sv['qb'], sv['kb'], z, sv['ys'][1], dys[1], sv['lse_b'], v_col=Z_FOX + 2048,
                                              chunked=False, scale=LANE ** -0.5, cq=sv['cq'], ck=sv['ck'],
                                              name=f"{tag}_fox_attn_bwd")
    dqb_raw, gs['g_fox_q'] = headnorm_bwd(z, P['g_fox_q'], dqb, col=Z_FOX, heads=8, name=f"{tag}_fox_qn_bwd")
    dkb_raw, gs['g_fox_k'] = headnorm_bwd(z, P['g_fox_k'], dkb, col=Z_FOX + 1024, heads=8, name=f"{tag}_fox_kn_bwd")
    dfl, gs['b_f'] = foxgate_bwd(sv['fl'], P['b_f'], dcq.reshape(8, S) + dck.reshape(8, S), name=f"{tag}_fox_gate_bwd")
    dqc, dkcp, dvcp, dbias = band_bwd(sv['qc'], sv['kcp'], sv['vcp'], sv['bias'], dys[2], scale=LANE ** -0.5,
                                      name=f"{tag}_ch_attn_bwd")
    dqc_raw, gs['g_ch_q'] = headnorm_bwd(z, P['g_ch_q'], dqc, col=Z_CH, heads=8, name=f"{tag}_ch_qn_bwd")
    dkc_raw, gs['g_ch_k'] = headnorm_bwd(z, P['g_ch_k'], dkcp[:, PAD:PAD + S, :], col=Z_CH + 1024, heads=8,
                                         name=f"{tag}_ch_kn_bwd")
    gs['rel_bias'] = relbias_bwd(chunk_band_dbias(dbias), name=f"{tag}_relbias_bwd")
    b16 = lambda t: t.astype(BF16)
    dz = jnp.concatenate([b16(dcq_raw), b16(dckv_raw), b16(dkr), b16(dfl.T), jnp.zeros((S, 120), BF16),
                          b16(dqb_raw), b16(dkb_raw), b16(dvb), b16(dqc_raw), b16(dkc_raw), b16(dvcp[PAD:PAD + S]),
                          dgl[0], dgl[1], dgl[2]], axis=1)
    gw['w_in'] = wgrad(sv['h'], dz, f"{tag}_dwin")
    dh = mm(dz, W['w_in'], tb=True, name=f"{tag}_dh")
    g_mix = P['g_mix'] if send_off is None else P['g_mix'] + send_off[1](gw)
    dx, gs['g_mix'] = rmsnorm_bwd(sv['x0'], g_mix, dh, res=dx, name=f"{tag}_dnorm_mix")
    return dx, gw, gs


def _local_step(x, mem, target, Ws, Ps):
    S = x.shape[0]
    cos, ssin = _rope_tables(S)
    L = len(Ws)
    saved = []
    for l in range(L):
        x, sv = _layer_fwd(x, mem, Ws[l], Ps[l], cos, ssin, f"l{l}")
        saved.append(sv)
    loss, dx = loss_head(x, target, name="loss_head")
    gws, gss = [None] * L, [None] * L
    for l in reversed(range(L)):
        dx, gws[l], gss[l] = _layer_bwd(dx, mem, Ws[l], Ps[l], saved[l], cos, ssin, f"l{l}")
    return loss, dx, gws, gss


def _pack_small(d):
    flat = jnp.concatenate([d[k].reshape(-1) for k in SMALL])
    n = flat.shape[0]
    rows = -(-n // (8 * LANE)) * 8
    return jnp.pad(flat, (0, rows * LANE - n)).reshape(rows, LANE)


def _unpack_small(packed, like):
    flat, out, off = packed.reshape(-1), {}, 0
    for k in SMALL:
        n = int(np.prod(like[k].shape))
        out[k] = flat[off:off + n].reshape(like[k].shape)
        off += n
    return out


def kernel(x, mem, g_mix, w_in, g_cq, w_uq, g_ckv, w_ukv, g_mla_q, g_mla_k, b_f, g_fox_q, g_fox_k, rel_bias, g_ch_q, g_ch_k, w_br, w_out, g_cross, g_mem, w_xq, w_xkv, g_x_q, g_x_k, w_xo, g_mlp, w_1, w_2, loss_target, m_g_mix, m_w_in, m_g_cq, m_w_uq, m_g_ckv, m_w_ukv, m_g_mla_q, m_g_mla_k, m_b_f, m_g_fox_q, m_g_fox_k, m_rel_bias, m_g_ch_q, m_g_ch_k, m_w_br, m_w_out, m_g_cross, m_g_mem, m_w_xq, m_w_xkv, m_g_x_q, m_g_x_k, m_w_xo, m_g_mlp, m_w_1, m_w_2, v_g_mix, v_w_in, v_g_cq, v_w_uq, v_g_ckv, v_w_ukv, v_g_mla_q, v_g_mla_k, v_b_f, v_g_fox_q, v_g_fox_k, v_rel_bias, v_g_ch_q, v_g_ch_k, v_w_br, v_w_out, v_g_cross, v_g_mem, v_w_xq, v_w_xkv, v_g_x_q, v_g_x_k, v_w_xo, v_g_mlp, v_w_1, v_w_2):
    args = locals()
    w = {k: args[k] for k in WEIGHTS}
    m = {k: args['m_' + k] for k in WEIGHTS}
    v = {k: args['v_' + k] for k in WEIGHTS}
    L = w_in.shape[0]

    Ps = [{k: w[k][l] for k in SMALL} for l in range(L)]
    xs, memv = x[0], mem[0]
    cos, ssin = _rope_tables(xs.shape[0])

    gathers = {}

    def start_gather(l, group, after=None):
        shards = [w[k][l].astype(BF16) for k in group]
        gathers[l, group] = copies_start(_gather_copies, shards, [lax.empty((8,) + s.shape, s.dtype) for s in shards],
                                         name=f"l{l}_ag_start_{group[0]}", after=after)
        return gathers[l, group][4][0, :1]

    Ps[0]['g_mix'] = Ps[0]['g_mix'] + start_gather(0, AG_GROUPS[0])

    def arrived(l, group, after, on_landed=None):
        send_sems, recv_sems, shards, zones, _ = gathers[l, group]
        shards, zones = copies_wait(_gather_copies, send_sems, recv_sems, shards, zones, after, name=f"l{l}_ag_wait_{group[0]}")
        then = None if on_landed is None else on_landed(shards[0])
        zones = forward_to_sibling(zones, name=f"l{l}_ag_forward_{group[0]}")
        zones = [place_mine(z, s, name=f"l{l}_{k}_mine") for k, z, s in zip(group, zones, shards)]
        return {k: w_in_full(z[:, None], 0, name=f"l{l}_w_in_layout", after=then) if k == 'w_in'
                else _full_from_shards(k, z, f"l{l}") for k, z in zip(group, zones)}

    Ws, saved = [], []
    for l in range(L):
        if l == 0:
            def start_the_rest(anchor):
                rest = [(j, group) for j in range(L) for group in AG_GROUPS if (j, group) != (0, AG_GROUPS[0])]
                return sum(start_gather(j, group, after=anchor) for j, group in rest)

            Ws.append(arrived(0, AG_GROUPS[0], Ps[0]['g_mix'], start_the_rest))
        else:
            Ws.append(arrived(l, AG_GROUPS[0], xs))

        def later(anchor, l=l):
            Ws[l].update(arrived(l, AG_GROUPS[1], anchor))
            return Ws[l]

        xs, sv = _layer_fwd(xs, memv, Ws[l], Ps[l], cos, ssin, f"l{l}", later)
        saved.append(sv)

    loss, dx = loss_head(xs, loss_target[0], name="loss_head")
    loss = lax.psum(loss[0, 0], ("x", "y", "c"))

    gss = [None] * L
    scatters = {group: [None] * L for group in RS_GROUPS}
    for l in reversed(range(L)):
        def send_off(gw, group, l=l):
            gdst = [w_in_shards(gw[k], name=f"l{l}_dw_in_layout") if k == 'w_in' else _shards_from_full(k, gw[k], f"l{l}")
                    for k in group]
            started = copies_start(_scatter_copies, gdst, [lax.empty((N_PEERS,) + g.shape[1:], g.dtype) for g in gdst],
                                   name=f"l{l}_rs_start_{group[0]}")
            scatters[group][l] = started[:4]
            return started[4][0, :1]

        hooks = tuple((lambda gw, group=group: send_off(gw, group)) for group in RS_GROUPS)
        dx, _, gss[l] = _layer_bwd(dx, memv, Ws[l], Ps[l], saved[l], cos, ssin, f"l{l}", hooks)
    grad_x = dx

    grads, delta, new_m, new_v = {}, {}, {}, {}
    after = grad_x
    for group in RS_GROUPS:
        done = [copies_wait(_scatter_copies, *scatters[group][l], after, name=f"l{l}_rs_wait_{group[0]}") for l in range(L)]
        for t, k in enumerate(group):
            grads[k], delta[k], new_m[k], new_v[k] = grad_sum_adamw(
                [done[l][0][t] for l in range(L)], [done[l][1][t] for l in range(L)], w[k], m[k], v[k], name=f"adamw_{k}")
        after = sum(delta[k][(0,) * (delta[k].ndim - 1)][:1] for k in group)

    small_part = _pack_small({k: jnp.stack([gss[l][k] for l in range(L)]) for k in SMALL})
    small_all = all_gather([small_part], after, name="ag_small")[0]
    grads.update(_unpack_small(ordered_sum(small_all, name="small_sum"), {k: w[k] for k in SMALL}))
    sd, sm, sv_ = adamw(_pack_small({k: w[k] for k in SMALL}), _pack_small({k: grads[k] for k in SMALL}),
                        _pack_small({k: m[k] for k in SMALL}), _pack_small({k: v[k] for k in SMALL}), name="adamw_small")
    like = {k: w[k] for k in SMALL}
    delta.update(_unpack_small(sd, like))
    new_m.update(_unpack_small(sm, like))
    new_v.update(_unpack_small(sv_, like))

    return (loss, grad_x[None], *[grads[k] for k in WEIGHTS], *[delta[k] for k in WEIGHTS],
            *[new_m[k] for k in WEIGHTS], *[new_v[k] for k in WEIGHTS])
```

```python
import numpy as np
import jax
import jax.numpy as jnp
from jax import lax
from jax.experimental import pallas as pl
from jax.experimental.pallas import tpu as pltpu

F32, BF16 = jnp.float32, jnp.bfloat16
EPS = 1e-6
NEG = -1e30
LANE = 128
VMEM_LIMIT_BYTES = 56 * 2**20
MESH = pl.DeviceIdType.MESH

D_MODEL = 2048
CHUNK = 64
BAND = 9 * CHUNK
PAD = 8 * CHUNK
REL_CLIP = 128
MLA_HEADS, MLA_NOPE, MLA_ROPE, MLA_QK = 8, 128, 64, 192
N_HEADS = 8
X_HEADS = 4
ROPE_THETA = 10000.0
ADAM_LR, ADAM_B1, ADAM_B2, ADAM_EPS, ADAM_WD, ADAM_STEP = 0.001, 0.9, 0.999, 1e-08, 0.01, 10

Z_MAIN, Z_FOX, Z_CH, Z_GATE, Z_W = 0, 1024, 4096, 7168, 13312
KR_COL, FF_COL = 768, 896

BIG = ('w_in', 'w_uq', 'w_ukv', 'w_br', 'w_out', 'w_xq', 'w_xkv', 'w_xo', 'w_1', 'w_2')
COL_SHARDED = ('w_in', 'w_uq', 'w_ukv', 'w_br', 'w_xo', 'w_1')
RS_GROUPS = (('w_2', 'w_1', 'w_xo', 'w_xq', 'w_xkv', 'w_out', 'w_br'), ('w_uq', 'w_ukv', 'w_in'))
AG_GROUPS = (('w_in',), ('w_uq', 'w_ukv', 'w_br', 'w_out', 'w_xq', 'w_xkv', 'w_xo', 'w_1', 'w_2'))
SMALL = ('g_mix', 'g_cq', 'g_ckv', 'g_mla_q', 'g_mla_k', 'b_f', 'g_fox_q', 'g_fox_k', 'rel_bias', 'g_ch_q',
         'g_ch_k', 'g_cross', 'g_mem', 'g_x_q', 'g_x_k', 'g_mlp')
WEIGHTS = ('g_mix', 'w_in', 'g_cq', 'w_uq', 'g_ckv', 'w_ukv', 'g_mla_q', 'g_mla_k', 'b_f', 'g_fox_q', 'g_fox_k',
           'rel_bias', 'g_ch_q', 'g_ch_k', 'w_br', 'w_out', 'g_cross', 'g_mem', 'w_xq', 'w_xkv', 'g_x_q', 'g_x_k',
           'w_xo', 'g_mlp', 'w_1', 'w_2')


def _params(*sem):
    return pltpu.CompilerParams(dimension_semantics=sem, vmem_limit_bytes=VMEM_LIMIT_BYTES)


def _tile(dim, pref):
    if dim <= pref:
        return dim
    for t in range(pref - pref % LANE, 0, -LANE):
        if dim % t == 0:
            return t
    raise ValueError((dim, pref))


def mm(a, b, *, ta=False, tb=False, out_dtype=F32, epi=None, aux=None, name, tm=1024, tn=512, tk=2048):
    M, K = (a.shape[1], a.shape[0]) if ta else a.shape
    N = b.shape[0] if tb else b.shape[1]
    assert (b.shape[1] if tb else b.shape[0]) == K, (a.shape, b.shape, ta, tb)
    tm, tn, tk = _tile(M, tm), _tile(N, tn), _tile(K, tk)
    nk = K // tk
    dn = (((0 if ta else 1,), (1 if tb else 0,)), ((), ()))
    n_aux = 0 if aux is None else 1

    def finish(acc, aux_refs, o_refs):
        if epi is None:
            o_refs[0][...] = acc.astype(o_refs[0].dtype)
        elif epi == 'add':
            o_refs[0][...] = (acc + aux_refs[0][...]).astype(o_refs[0].dtype)
        elif epi == 'relu2':
            o_refs[0][...] = acc
            r = jnp.maximum(acc, 0.0)
            o_refs[1][...] = (r * r).astype(o_refs[1].dtype)
        elif epi == 'mul_drelu2':
            o_refs[0][...] = (acc * (2.0 * jnp.maximum(aux_refs[0][...], 0.0))).astype(o_refs[0].dtype)

    def body(a_ref, b_ref, *rest):
        aux_refs = rest[:n_aux]
        o_refs = rest[n_aux:n_aux + (2 if epi == 'relu2' else 1)]
        part = lax.dot_general(a_ref[...].astype(BF16), b_ref[...].astype(BF16), dn, preferred_element_type=F32)
        if nk == 1:
            finish(part, aux_refs, o_refs)
        else:
            acc_ref = rest[-1]
            k = pl.program_id(2)

            @pl.when(k == 0)
            def _():
                acc_ref[...] = part

            @pl.when(k > 0)
            def _():
                acc_ref[...] += part

            @pl.when(k == nk - 1)
            def _():
                finish(acc_ref[...], aux_refs, o_refs)

    a_spec = pl.BlockSpec((tk, tm), lambda i, j, k: (k, i)) if ta else pl.BlockSpec((tm, tk), lambda i, j, k: (i, k))
    b_spec = pl.BlockSpec((tn, tk), lambda i, j, k: (j, k)) if tb else pl.BlockSpec((tk, tn), lambda i, j, k: (k, j))
    o_spec = pl.BlockSpec((tm, tn), lambda i, j, k: (i, j))
    if epi == 'relu2':
        out_shape = (jax.ShapeDtypeStruct((M, N), F32), jax.ShapeDtypeStruct((M, N), out_dtype))
        out_specs = (o_spec, o_spec)
    else:
        out_shape, out_specs = jax.ShapeDtypeStruct((M, N), out_dtype), o_spec
    return pl.pallas_call(
        body, name=name, out_shape=out_shape, grid=(M // tm, N // tn, nk),
        in_specs=[a_spec, b_spec] + [o_spec] * n_aux, out_specs=out_specs,
        scratch_shapes=[pltpu.VMEM((tm, tn), F32)] if nk > 1 else [],
        compiler_params=_params("parallel", "parallel", "arbitrary"),
    )(a, b, *([aux] if n_aux else []))


def rmsnorm_fwd(x, g, *, col=0, width=None, out_dtype=BF16, name, ts=256):
    S = x.shape[0]
    width = x.shape[1] if width is None else width
    ts, cb = _tile(S, ts), col // width

    def body(x_ref, g_ref, o_ref):
        xf = x_ref[...]
        r = lax.rsqrt(jnp.mean(xf * xf, axis=-1, keepdims=True) + EPS)
        o_ref[...] = (xf * r * g_ref[...]).astype(o_ref.dtype)

    return pl.pallas_call(
        body, name=name, out_shape=jax.ShapeDtypeStruct((S, width), out_dtype), grid=(S // ts,),
        in_specs=[pl.BlockSpec((ts, width), lambda i: (i, cb)), pl.BlockSpec((1, width), lambda i: (0, 0))],
        out_specs=pl.BlockSpec((ts, width), lambda i: (i, 0)), compiler_params=_params("parallel"),
    )(x, g.reshape(1, width))


def rmsnorm_bwd(x, g, dy, *, col=0, width=None, res=None, name, ts=256):
    S = x.shape[0]
    width = x.shape[1] if width is None else width
    ts, cb = _tile(S, ts), col // width
    has_res = res is not None

    def body(x_ref, g_ref, dy_ref, *rest):
        dx_ref, dg_ref = rest[-2:]
        xf = x_ref[...]
        r = lax.rsqrt(jnp.mean(xf * xf, axis=-1, keepdims=True) + EPS)
        dyf = dy_ref[...].astype(F32)
        dyg = dyf * g_ref[...]
        dx = r * dyg - xf * (r * r * r) * jnp.mean(dyg * xf, axis=-1, keepdims=True)
        if has_res:
            dx = dx + rest[0][...]
        dx_ref[...] = dx
        part = jnp.sum(dyf * xf * r, axis=0, keepdims=True)

        @pl.when(pl.program_id(0) == 0)
        def _():
            dg_ref[...] = part

        @pl.when(pl.program_id(0) > 0)
        def _():
            dg_ref[...] += part

    blk = pl.BlockSpec((ts, width), lambda i: (i, 0))
    dx, dg = pl.pallas_call(
        body, name=name,
        out_shape=(jax.ShapeDtypeStruct((S, width), F32), jax.ShapeDtypeStruct((1, width), F32)), grid=(S // ts,),
        in_specs=[pl.BlockSpec((ts, width), lambda i: (i, cb)), pl.BlockSpec((1, width), lambda i: (0, 0)), blk]
        + ([blk] if has_res else []),
        out_specs=(blk, pl.BlockSpec((1, width), lambda i: (0, 0))), compiler_params=_params("arbitrary"),
    )(x, g.reshape(1, width), dy, *([res] if has_res else []))
    return dx, dg.reshape(width)


def headnorm_fwd(x, g, *, col, heads, name, ts=1024):
    S = x.shape[0]
    ts, cb = _tile(S, ts), col // LANE

    def body(x_ref, g_ref, o_ref):
        xf = x_ref[...]
        r = lax.rsqrt(jnp.mean(xf * xf, axis=-1, keepdims=True) + EPS)
        o_ref[0] = (xf * r * g_ref[...]).astype(o_ref.dtype)

    return pl.pallas_call(
        body, name=name, out_shape=jax.ShapeDtypeStruct((heads, S, LANE), BF16), grid=(heads, S // ts),
        in_specs=[pl.BlockSpec((ts, LANE), lambda h, i: (i, cb + h)), pl.BlockSpec((1, LANE), lambda h, i: (0, 0))],
        out_specs=pl.BlockSpec((1, ts, LANE), lambda h, i: (h, i, 0)), compiler_params=_params("parallel", "parallel"),
    )(x, g.reshape(1, LANE))


def headnorm_bwd(x, g, dy, *, col, heads, name, ts=1024):
    S = x.shape[0]
    ts, cb = _tile(S, ts), col // LANE

    def body(x_ref, g_ref, dy_ref, dx_ref, dg_ref):
        xf = x_ref[...]
        r = lax.rsqrt(jnp.mean(xf * xf, axis=-1, keepdims=True) + EPS)
        dyf = dy_ref[0]
        dyg = dyf * g_ref[...]
        dx_ref[...] = r * dyg - xf * (r * r * r) * jnp.mean(dyg * xf, axis=-1, keepdims=True)
        part = jnp.sum(dyf * xf * r, axis=0, keepdims=True)
        first = jnp.logical_and(pl.program_id(0) == 0, pl.program_id(1) == 0)

        @pl.when(first)
        def _():
            dg_ref[...] = part

        @pl.when(jnp.logical_not(first))
        def _():
            dg_ref[...] += part

    dx, dg = pl.pallas_call(
        body, name=name,
        out_shape=(jax.ShapeDtypeStruct((S, heads * LANE), F32), jax.ShapeDtypeStruct((1, LANE), F32)),
        grid=(heads, S // ts),
        in_specs=[pl.BlockSpec((ts, LANE), lambda h, i: (i, cb + h)), pl.BlockSpec((1, LANE), lambda h, i: (0, 0)),
                  pl.BlockSpec((1, ts, LANE), lambda h, i: (h, i, 0))],
        out_specs=(pl.BlockSpec((ts, LANE), lambda h, i: (i, h)), pl.BlockSpec((1, LANE), lambda h, i: (0, 0))),
        compiler_params=_params("arbitrary", "arbitrary"),
    )(x, g.reshape(1, LANE), dy)
    return dx, dg.reshape(LANE)


def _rope_tables(S):
    pos = jnp.arange(S, dtype=F32)
    inv = ROPE_THETA ** (-jnp.arange(0, MLA_ROPE, 2, dtype=F32) / MLA_ROPE)
    ang = pos[:, None] * inv[None, :]
    c, s, z = jnp.cos(ang), jnp.sin(ang), jnp.zeros((S, 64), F32)
    return jnp.concatenate([c, c, z], axis=1), jnp.concatenate([-s, s, z], axis=1)


def _rope(v, cos, ssin, lane):
    partner = jnp.where(lane < 32, pltpu.roll(v, 96, 1), pltpu.roll(v, 32, 1))
    return v * cos + partner * ssin


def mla_prep_fwd(xn, xr, g, cos, ssin, *, n_col, n_stride, r_col, r_stride, heads, name, ts=1024):
    S = xn.shape[0]
    ts = _tile(S, ts)
    nb, ns, rb, rs = n_col // LANE, n_stride // LANE, r_col // LANE, r_stride // LANE
    gn = g[:MLA_NOPE].reshape(1, LANE)
    gr = jnp.concatenate([g[MLA_NOPE:], jnp.zeros((64,), F32)]).reshape(1, LANE)

    def body(n_ref, r_ref, gn_ref, gr_ref, c_ref, s_ref, o_ref):
        n, rr = n_ref[...], r_ref[...]
        ss = jnp.sum(n * n, axis=-1, keepdims=True) + jnp.sum(rr * rr, axis=-1, keepdims=True)
        r = lax.rsqrt(ss * (1.0 / MLA_QK) + EPS)
        lane = lax.broadcasted_iota(jnp.int32, rr.shape, 1)
        o_ref[0, :, :LANE] = (n * r * gn_ref[...]).astype(o_ref.dtype)
        o_ref[0, :, LANE:] = _rope(rr * r * gr_ref[...], c_ref[...], s_ref[...], lane).astype(o_ref.dtype)

    row = lambda h, i: (0, 0)
    return pl.pallas_call(
        body, name=name, out_shape=jax.ShapeDtypeStruct((heads, S, 2 * LANE), BF16), grid=(heads, S // ts),
        in_specs=[pl.BlockSpec((ts, LANE), lambda h, i: (i, nb + ns * h)),
                  pl.BlockSpec((ts, LANE), lambda h, i: (i, rb + rs * h)),
                  pl.BlockSpec((1, LANE), row), pl.BlockSpec((1, LANE), row),
                  pl.BlockSpec((ts, LANE), lambda h, i: (i, 0)), pl.BlockSpec((ts, LANE), lambda h, i: (i, 0))],
        out_specs=pl.BlockSpec((1, ts, 2 * LANE), lambda h, i: (h, i, 0)),
        compiler_params=_params("parallel", "parallel"),
    )(xn, xr, gn, gr, cos, ssin)


def mla_prep_bwd(xn, xr, g, cos, ssin, dy, *, n_col, n_stride, r_col, r_stride, heads, name, ts=1024):
    S = xn.shape[0]
    ts = _tile(S, ts)
    nb, ns, rb, rs = n_col // LANE, n_stride // LANE, r_col // LANE, r_stride // LANE
    shared = r_stride == 0
    gn = g[:MLA_NOPE].reshape(1, LANE)
    gr = jnp.concatenate([g[MLA_NOPE:], jnp.zeros((64,), F32)]).reshape(1, LANE)

    def body(n_ref, r_ref, gn_ref, gr_ref, c_ref, s_ref, dy_ref, dn_ref, dr_ref, dgn_ref, dgr_ref):
        i, h = pl.program_id(0), pl.program_id(1)
        n, rr = n_ref[...], r_ref[...]
        ss = jnp.sum(n * n, axis=-1, keepdims=True) + jnp.sum(rr * rr, axis=-1, keepdims=True)
        r = lax.rsqrt(ss * (1.0 / MLA_QK) + EPS)
        lane = lax.broadcasted_iota(jnp.int32, rr.shape, 1)
        dyn = dy_ref[0, :, :LANE]
        dyr = dy_ref[0, :, LANE:]
        t = dyr * s_ref[...]
        dvr = dyr * c_ref[...] + jnp.where(lane < 32, pltpu.roll(t, 96, 1), pltpu.roll(t, 32, 1))
        dvr = jnp.where(lane < 64, dvr, 0.0)
        dgn_part = jnp.sum(dyn * n * r, axis=0, keepdims=True)
        dgr_part = jnp.sum(dvr * rr * r, axis=0, keepdims=True)
        dyn_g, dvr_g = dyn * gn_ref[...], dvr * gr_ref[...]
        proj = (jnp.sum(dyn_g * n, axis=-1, keepdims=True) + jnp.sum(dvr_g * rr, axis=-1, keepdims=True)) * (1.0 / MLA_QK)
        r3 = r * r * r
        dn_ref[...] = r * dyn_g - n * r3 * proj
        dr = r * dvr_g - rr * r3 * proj
        if shared:
            @pl.when(h == 0)
            def _():
                dr_ref[...] = dr

            @pl.when(h > 0)
            def _():
                dr_ref[...] += dr
        else:
            dr_ref[...] = dr
        first = jnp.logical_and(i == 0, h == 0)

        @pl.when(first)
        def _():
            dgn_ref[...] = dgn_part
            dgr_ref[...] = dgr_part

        @pl.when(jnp.logical_not(first))
        def _():
            dgn_ref[...] += dgn_part
            dgr_ref[...] += dgr_part

    row = lambda i, h: (0, 0)
    dr_cols = LANE if shared else heads * LANE
    dn, dr, dgn, dgr = pl.pallas_call(
        body, name=name,
        out_shape=(jax.ShapeDtypeStruct((S, heads * LANE), F32), jax.ShapeDtypeStruct((S, dr_cols), F32),
                   jax.ShapeDtypeStruct((1, LANE), F32), jax.ShapeDtypeStruct((1, LANE), F32)),
        grid=(S // ts, heads),
        in_specs=[pl.BlockSpec((ts, LANE), lambda i, h: (i, nb + ns * h)),
                  pl.BlockSpec((ts, LANE), lambda i, h: (i, rb + rs * h)),
                  pl.BlockSpec((1, LANE), row), pl.BlockSpec((1, LANE), row),
                  pl.BlockSpec((ts, LANE), lambda i, h: (i, 0)), pl.BlockSpec((ts, LANE), lambda i, h: (i, 0)),
                  pl.BlockSpec((1, ts, 2 * LANE), lambda i, h: (h, i, 0))],
        out_specs=(pl.BlockSpec((ts, LANE), lambda i, h: (i, h)),
                   pl.BlockSpec((ts, LANE), (lambda i, h: (i, 0)) if shared else (lambda i, h: (i, h))),
                   pl.BlockSpec((1, LANE), row), pl.BlockSpec((1, LANE), row)),
        compiler_params=_params("arbitrary", "arbitrary"),
    )(xn, xr, gn, gr, cos, ssin, dy)
    return dn, dr, jnp.concatenate([dgn.reshape(LANE), dgr.reshape(LANE)[:MLA_ROPE]])


_NT = (((1,), (1,)), ((), ()))
_TN = (((0,), (0,)), ((), ()))


def attn_fwd(q, k, v, *, v_col, scale, name, bq=256):
    H, S, dk = q.shape
    Sk = k.shape[1]
    bq, vb = _tile(S, bq), v_col // LANE

    def body(q_ref, k_ref, v_ref, o_ref):
        s = lax.dot_general(q_ref[0], k_ref[0], _NT, preferred_element_type=F32) * scale
        e = jnp.exp(s - jnp.max(s, axis=-1, keepdims=True))
        p = e * (1.0 / jnp.sum(e, axis=-1, keepdims=True))
        o_ref[...] = jnp.dot(p.astype(BF16), v_ref[...].astype(BF16), preferred_element_type=F32).astype(o_ref.dtype)

    return pl.pallas_call(
        body, name=name, out_shape=jax.ShapeDtypeStruct((S, H * LANE), BF16), grid=(H, S // bq),
        in_specs=[pl.BlockSpec((1, bq, dk), lambda h, i: (h, i, 0)), pl.BlockSpec((1, Sk, dk), lambda h, i: (h, 0, 0)),
                  pl.BlockSpec((Sk, LANE), lambda h, i: (0, vb + h))],
        out_specs=pl.BlockSpec((bq, LANE), lambda h, i: (i, h)), compiler_params=_params("parallel", "parallel"),
    )(q, k, v)


def attn_bwd(q, k, v, do, *, v_col, scale, name, bq=256):
    H, S, dk = q.shape
    Sk = k.shape[1]
    bq, vb = _tile(S, bq), v_col // LANE

    def body(q_ref, k_ref, v_ref, do_ref, dq_ref, dk_ref, dv_ref):
        i = pl.program_id(1)
        qb, kb, vv = q_ref[0], k_ref[0], v_ref[...].astype(BF16)
        s = lax.dot_general(qb, kb, _NT, preferred_element_type=F32) * scale
        e = jnp.exp(s - jnp.max(s, axis=-1, keepdims=True))
        p = e * (1.0 / jnp.sum(e, axis=-1, keepdims=True))
        dob = do_ref[...].astype(BF16)
        dv_part = lax.dot_general(p.astype(BF16), dob, _TN, preferred_element_type=F32)
        dp = lax.dot_general(dob, vv, _NT, preferred_element_type=F32)
        ds = p * (dp - jnp.sum(p * dp, axis=-1, keepdims=True))
        dsb = (ds * scale).astype(BF16)
        dq_ref[0] = jnp.dot(dsb, kb, preferred_element_type=F32)
        dk_part = lax.dot_general(dsb, qb, _TN, preferred_element_type=F32)

        @pl.when(i == 0)
        def _():
            dk_ref[0] = dk_part
            dv_ref[...] = dv_part

        @pl.when(i > 0)
        def _():
            dk_ref[0] += dk_part
            dv_ref[...] += dv_part

    return pl.pallas_call(
        body, name=name,
        out_shape=(jax.ShapeDtypeStruct((H, S, dk), F32), jax.ShapeDtypeStruct((H, Sk, dk), F32),
                   jax.ShapeDtypeStruct((Sk, H * LANE), F32)),
        grid=(H, S // bq),
        in_specs=[pl.BlockSpec((1, bq, dk), lambda h, i: (h, i, 0)), pl.BlockSpec((1, Sk, dk), lambda h, i: (h, 0, 0)),
                  pl.BlockSpec((Sk, LANE), lambda h, i: (0, vb + h)), pl.BlockSpec((bq, LANE), lambda h, i: (i, h))],
        out_specs=(pl.BlockSpec((1, bq, dk), lambda h, i: (h, i, 0)), pl.BlockSpec((1, Sk, dk), lambda h, i: (h, 0, 0)),
                   pl.BlockSpec((Sk, LANE), lambda h, i: (0, h))),
        compiler_params=_params("parallel", "arbitrary"),
    )(q, k, v, do)


def _causal_scores(q, kblk, i, start, blk, scale, chunked, cq, ckblk):
    s = lax.dot_general(q, kblk, _NT, preferred_element_type=F32) * scale
    if cq is not None:
        s = s + cq - ckblk
    qpos = i * blk + lax.broadcasted_iota(jnp.int32, s.shape, 0)
    kpos = start + lax.broadcasted_iota(jnp.int32, s.shape, 1)
    ok = (kpos >> 6) <= (qpos >> 6) if chunked else kpos <= qpos
    return jnp.where(ok, s, NEG)


def causal_attn_fwd(q, k, v, *, v_col, chunked, scale, cq=None, ck=None, name, blk=256):
    H, S, dk = q.shape
    blk, vb = _tile(S, blk), v_col // LANE
    fox = cq is not None

    def body(q_ref, k_ref, v_ref, *rest):
        o_ref, lse_ref = rest[-2:]
        s = _causal_scores(q_ref[0], k_ref[0], pl.program_id(1), 0, blk, scale, chunked,
                           rest[0][0] if fox else None, rest[1][0] if fox else None)
        m = jnp.max(s, axis=-1, keepdims=True)
        p = jnp.exp(s - m)
        l = jnp.sum(p, axis=-1, keepdims=True)
        pv = jnp.dot(p.astype(BF16), v_ref[...].astype(BF16), preferred_element_type=F32)
        o_ref[...] = (pv * (1.0 / l)).astype(o_ref.dtype)
        lse_ref[0] = m + jnp.log(l)

    in_specs = [pl.BlockSpec((1, blk, dk), lambda h, i: (h, i, 0)), pl.BlockSpec((1, S, dk), lambda h, i: (h, 0, 0)),
                pl.BlockSpec((S, LANE), lambda h, i: (0, vb + h))]
    args = [q, k, v]
    if fox:
        in_specs += [pl.BlockSpec((1, blk, 1), lambda h, i: (h, i, 0)), pl.BlockSpec((1, 1, S), lambda h, i: (h, 0, 0))]
        args += [cq, ck]
    return pl.pallas_call(
        body, name=name, out_shape=(jax.ShapeDtypeStruct((S, H * LANE), BF16), jax.ShapeDtypeStruct((H, S, 1), F32)),
        grid=(H, S // blk), in_specs=in_specs,
        out_specs=(pl.BlockSpec((blk, LANE), lambda h, i: (i, h)), pl.BlockSpec((1, blk, 1), lambda h, i: (h, i, 0))),
        compiler_params=_params("parallel", "parallel"),
    )(*args)


def causal_attn_bwd(q, k, v, o, do, lse, *, v_col, chunked, scale, cq=None, ck=None, name, blk=256):
    H, S, dk = q.shape
    blk, vb = _tile(S, blk), v_col // LANE
    kc = 2 * blk if S % (2 * blk) == 0 else blk
    fox = cq is not None

    def body(q_ref, k_ref, v_ref, o_ref, do_ref, lse_ref, *rest):
        i = pl.program_id(1)
        if fox:
            cq_ref, ck_ref, dq_ref, dk_ref, dv_ref, dcq_ref, dck_ref = rest
        else:
            dq_ref, dk_ref, dv_ref = rest

        @pl.when(i == 0)
        def _():
            dk_ref[...] = jnp.zeros_like(dk_ref)
            dv_ref[...] = jnp.zeros_like(dv_ref)
            if fox:
                dck_ref[...] = jnp.zeros_like(dck_ref)

        qb, dob, lse_b = q_ref[0], do_ref[...].astype(BF16), lse_ref[0]
        delta = jnp.sum(do_ref[...].astype(F32) * o_ref[...].astype(F32), axis=-1, keepdims=True)
        dq_ref[...] = jnp.zeros_like(dq_ref)
        if fox:
            dcq_ref[...] = jnp.zeros_like(dcq_ref)
        for c in range(S // kc):
            @pl.when(c * kc < (i + 1) * blk)
            def _(ks=slice(c * kc, (c + 1) * kc), start=c * kc):
                kblk = k_ref[0, ks, :]
                s = _causal_scores(qb, kblk, i, start, blk, scale, chunked,
                                   cq_ref[0] if fox else None, ck_ref[0, :, ks] if fox else None)
                p = jnp.exp(s - lse_b)
                dv_ref[ks, :] += lax.dot_general(p.astype(BF16), dob, _TN, preferred_element_type=F32)
                dp = lax.dot_general(dob, v_ref[ks, :].astype(BF16), _NT, preferred_element_type=F32)
                ds = p * (dp - delta)
                dsb = (ds * scale).astype(BF16)
                dq_ref[0] += jnp.dot(dsb, kblk, preferred_element_type=F32)
                dk_ref[0, ks, :] += lax.dot_general(dsb, qb, _TN, preferred_element_type=F32)
                if fox:
                    dcq_ref[0] += jnp.sum(ds, axis=-1, keepdims=True)
                    dck_ref[0, :, ks] += -jnp.sum(ds, axis=0, keepdims=True)

    row = pl.BlockSpec((blk, LANE), lambda h, i: (i, h))
    in_specs = [pl.BlockSpec((1, blk, dk), lambda h, i: (h, i, 0)), pl.BlockSpec((1, S, dk), lambda h, i: (h, 0, 0)),
                pl.BlockSpec((S, LANE), lambda h, i: (0, vb + h)), row, row, pl.BlockSpec((1, blk, 1), lambda h, i: (h, i, 0))]
    args = [q, k, v, o, do, lse]
    out_shape = [jax.ShapeDtypeStruct((H, S, dk), F32), jax.ShapeDtypeStruct((H, S, dk), F32),
                 jax.ShapeDtypeStruct((S, H * LANE), F32)]
    out_specs = [pl.BlockSpec((1, blk, dk), lambda h, i: (h, i, 0)), pl.BlockSpec((1, S, dk), lambda h, i: (h, 0, 0)),
                 pl.BlockSpec((S, LANE), lambda h, i: (0, h))]
    if fox:
        fox_specs = [pl.BlockSpec((1, blk, 1), lambda h, i: (h, i, 0)), pl.BlockSpec((1, 1, S), lambda h, i: (h, 0, 0))]
        in_specs += fox_specs
        args += [cq, ck]
        out_shape += [jax.ShapeDtypeStruct((H, S, 1), F32), jax.ShapeDtypeStruct((H, 1, S), F32)]
        out_specs += fox_specs
    return pl.pallas_call(
        body, name=name, out_shape=tuple(out_shape), grid=(H, S // blk), in_specs=in_specs, out_specs=tuple(out_specs),
        compiler_params=_params("parallel", "arbitrary"),
    )(*args)


CPB = 4
BANDW = BAND + CHUNK
WIN = BAND + (CPB - 1) * CHUNK


def chunk_band_dbias(db):
    H = db.shape[0]
    d4 = db.reshape(H, CPB, CHUNK, WIN)
    return sum(d4[:, c, :, CHUNK * c:CHUNK * c + BAND] for c in range(CPB))


def _band_probs(qb, kb, bias, start, scale):
    s = lax.dot_general(qb, kb, _NT, preferred_element_type=F32) * scale
    real = start + lax.broadcasted_iota(jnp.int32, s.shape, 1) >= PAD
    s = jnp.where(real, s + bias, NEG)
    e = jnp.exp(s - jnp.max(s, axis=-1, keepdims=True))
    return e * (1.0 / jnp.sum(e, axis=-1, keepdims=True))


def band_fwd(q, kp, vp, bias, *, scale, name):
    H, S, _ = q.shape
    Sp, rows = S + PAD + CHUNK, CPB * CHUNK

    def body(q_ref, k_ref, v_ref, b_ref, o_ref):
        start = pl.multiple_of(pl.program_id(1) * rows, rows)
        p = _band_probs(q_ref[0], k_ref[0, pl.ds(start, WIN), :], b_ref[0], start, scale)
        vb = v_ref[pl.ds(start, WIN), :].astype(BF16)
        o_ref[...] = jnp.dot(p.astype(BF16), vb, preferred_element_type=F32).astype(o_ref.dtype)

    return pl.pallas_call(
        body, name=name, out_shape=jax.ShapeDtypeStruct((S, H * LANE), BF16), grid=(H, S // rows),
        in_specs=[pl.BlockSpec((1, rows, LANE), lambda h, j: (h, j, 0)), pl.BlockSpec((1, Sp, LANE), lambda h, j: (h, 0, 0)),
                  pl.BlockSpec((Sp, LANE), lambda h, j: (0, h)), pl.BlockSpec((1, rows, WIN), lambda h, j: (h, 0, 0))],
        out_specs=pl.BlockSpec((rows, LANE), lambda h, j: (j, h)), compiler_params=_params("parallel", "parallel"),
    )(q, kp, vp, bias)


def band_bwd(q, kp, vp, bias, do, *, scale, name):
    H, S, _ = q.shape
    Sp, rows = S + PAD + CHUNK, CPB * CHUNK

    def body(q_ref, k_ref, v_ref, b_ref, do_ref, dq_ref, dk_ref, dv_ref, db_ref):
        j = pl.program_id(1)

        @pl.when(j == 0)
        def _():
            dk_ref[...] = jnp.zeros_like(dk_ref)
            dv_ref[...] = jnp.zeros_like(dv_ref)
            db_ref[...] = jnp.zeros_like(db_ref)

        start = pl.multiple_of(j * rows, rows)
        qb = q_ref[0]
        kb = k_ref[0, pl.ds(start, WIN), :]
        vb = v_ref[pl.ds(start, WIN), :].astype(BF16)
        p = _band_probs(qb, kb, b_ref[0], start, scale)
        dob = do_ref[...].astype(BF16)
        dv_ref[pl.ds(start, WIN), :] += lax.dot_general(p.astype(BF16), dob, _TN, preferred_element_type=F32)
        dp = lax.dot_general(dob, vb, _NT, preferred_element_type=F32)
        ds = p * (dp - jnp.sum(p * dp, axis=-1, keepdims=True))
        db_ref[0] += ds
        dsb = (ds * scale).astype(BF16)
        dq_ref[0] = jnp.dot(dsb, kb, preferred_element_type=F32)
        dk_ref[0, pl.ds(start, WIN), :] += lax.dot_general(dsb, qb, _TN, preferred_element_type=F32)

    blk_b = pl.BlockSpec((1, rows, WIN), lambda h, j: (h, 0, 0))
    return pl.pallas_call(
        body, name=name,
        out_shape=(jax.ShapeDtypeStruct((H, S, LANE), F32), jax.ShapeDtypeStruct((H, Sp, LANE), F32),
                   jax.ShapeDtypeStruct((Sp, H * LANE), F32), jax.ShapeDtypeStruct((H, rows, WIN), F32)),
        grid=(H, S // rows),
        in_specs=[pl.BlockSpec((1, rows, LANE), lambda h, j: (h, j, 0)), pl.BlockSpec((1, Sp, LANE), lambda h, j: (h, 0, 0)),
                  pl.BlockSpec((Sp, LANE), lambda h, j: (0, h)), blk_b, pl.BlockSpec((rows, LANE), lambda h, j: (j, h))],
        out_specs=(pl.BlockSpec((1, rows, LANE), lambda h, j: (h, j, 0)), pl.BlockSpec((1, Sp, LANE), lambda h, j: (h, 0, 0)),
                   pl.BlockSpec((Sp, LANE), lambda h, j: (0, h)), blk_b),
        compiler_params=_params("parallel", "arbitrary"),
    )(q, kp, vp, bias, do)


def band_bias(rel_bias, *, name):
    H, rows, wide = rel_bias.shape[0], CPB * CHUNK, 1024
    last = rel_bias[:, 2 * REL_CLIP:]
    row0 = jnp.concatenate([jnp.tile(last, (1, PAD - REL_CLIP)), rel_bias[:, CHUNK + 1:][:, ::-1],
                            jnp.tile(last, (1, wide - BAND))], axis=1)

    def body(r_ref, o_ref):
        skew = pltpu.roll(jnp.broadcast_to(r_ref[0], (rows, wide)), 0, 1, stride=1, stride_axis=0)[:, :WIN]
        first = (lax.broadcasted_iota(jnp.int32, (rows, WIN), 0) >> 6) * CHUNK
        col = lax.broadcasted_iota(jnp.int32, (rows, WIN), 1)
        o_ref[0] = jnp.where(jnp.logical_and(col >= first, col < first + BAND), skew, NEG)

    return pl.pallas_call(
        body, name=name, out_shape=jax.ShapeDtypeStruct((H, rows, WIN), F32), grid=(H,),
        in_specs=[pl.BlockSpec((1, 1, wide), lambda h: (h, 0, 0))], out_specs=pl.BlockSpec((1, rows, WIN), lambda h: (h, 0, 0)),
        compiler_params=_params("parallel"),
    )(row0.reshape(H, 1, wide))


def relbias_bwd(dbias, *, name):
    H, W = dbias.shape[0], BANDW
    x = jnp.pad(dbias[:, :, ::-1], ((0, 0), (0, 0), (0, CHUNK)))

    def body(x_ref, o_ref):
        skew = pltpu.roll(x_ref[0], 0, 1, stride=1, stride_axis=0)
        f = jnp.broadcast_to(jnp.sum(skew, axis=0, keepdims=True), (8, W))
        lane = lax.broadcasted_iota(jnp.int32, (8, W), 1)
        direct = jnp.where(jnp.logical_and(lane >= 65, lane <= 255), pltpu.roll(f, 65, 1), 0.0)
        tail = jnp.sum(jnp.where(lane >= 191, f, 0.0), axis=-1, keepdims=True)
        o_ref[0] = direct + jnp.where(lane == 2 * REL_CLIP, tail, 0.0)

    out = pl.pallas_call(
        body, name=name, out_shape=jax.ShapeDtypeStruct((H, 8, W), F32), grid=(H,),
        in_specs=[pl.BlockSpec((1, CHUNK, W), lambda h: (h, 0, 0))], out_specs=pl.BlockSpec((1, 8, W), lambda h: (h, 0, 0)),
        compiler_params=_params("parallel"),
    )(x)
    return out[:, 0, :2 * REL_CLIP + 1]


def _split_dot(x, u, dn):
    hi = x.astype(BF16)
    r1 = x - hi.astype(F32)
    mid = r1.astype(BF16)
    lo = (r1 - mid.astype(F32)).astype(BF16)
    d = lambda t: lax.dot_general(t, u, dn, preferred_element_type=F32)
    return d(hi) + d(mid) + d(lo)


def _upper_ones(S):
    return (np.arange(S)[:, None] <= np.arange(S)[None, :]).astype(np.float32)


def foxgate_fwd(fl, b, *, name):
    H, S = fl.shape
    u = jnp.asarray(_upper_ones(S), BF16)

    def body(f_ref, b_ref, u_ref, o_ref):
        x = f_ref[...] + b_ref[...]
        lf = jnp.minimum(x, 0.0) - jnp.log(1.0 + jnp.exp(-jnp.abs(x)))
        o_ref[...] = _split_dot(lf, u_ref[...], (((1,), (0,)), ((), ())))

    return pl.pallas_call(body, name=name, out_shape=jax.ShapeDtypeStruct((H, S), F32),
                          compiler_params=pltpu.CompilerParams(vmem_limit_bytes=VMEM_LIMIT_BYTES))(fl, b.reshape(H, 1), u)


def foxgate_bwd(fl, b, dcum, *, name):
    H, S = fl.shape
    u = jnp.asarray(_upper_ones(S), BF16)

    def body(f_ref, b_ref, u_ref, dc_ref, df_ref, db_ref):
        x = f_ref[...] + b_ref[...]
        dlf = _split_dot(dc_ref[...], u_ref[...], _NT)
        df = dlf * (1.0 / (1.0 + jnp.exp(x)))
        df_ref[...] = df
        db_ref[...] = jnp.sum(df, axis=-1, keepdims=True)

    df, db = pl.pallas_call(body, name=name,
                            out_shape=(jax.ShapeDtypeStruct((H, S), F32), jax.ShapeDtypeStruct((H, 1), F32)),
                            compiler_params=pltpu.CompilerParams(vmem_limit_bytes=VMEM_LIMIT_BYTES))(fl, b.reshape(H, 1), u, dcum)
    return df, db.reshape(H)


def gate_fwd(z, proj, *, name, ts=256, tc=512):
    S, D = proj[0].shape
    ts, gb, nb = _tile(S, ts), Z_GATE // tc, D // tc

    def body(g0, g1, g2, p0, p1, p2, o_ref):
        acc = None
        for g_ref, p_ref in zip((g0, g1, g2), (p0, p1, p2)):
            t = (1.0 / (1.0 + jnp.exp(-g_ref[...]))) * p_ref[...]
            acc = t if acc is None else acc + t
        o_ref[...] = acc.astype(o_ref.dtype)

    blk = pl.BlockSpec((ts, tc), lambda i, j: (i, j))
    return pl.pallas_call(
        body, name=name, out_shape=jax.ShapeDtypeStruct((S, D), BF16), grid=(S // ts, nb),
        in_specs=[pl.BlockSpec((ts, tc), lambda i, j, n=n: (i, gb + n * nb + j)) for n in range(3)] + [blk] * 3,
        out_specs=blk, compiler_params=_params("parallel", "parallel"),
    )(z, z, z, *proj)


def gate_bwd(z, proj, dm, *, name, ts=256, tc=512):
    S, D = proj[0].shape
    ts, gb, nb = _tile(S, ts), Z_GATE // tc, D // tc

    def body(g0, g1, g2, p0, p1, p2, dm_ref, *outs):
        dmv = dm_ref[...]
        for n, (g_ref, p_ref) in enumerate(zip((g0, g1, g2), (p0, p1, p2))):
            sg = 1.0 / (1.0 + jnp.exp(-g_ref[...]))
            outs[n][...] = (dmv * sg).astype(BF16)
            outs[3 + n][...] = (dmv * p_ref[...] * sg * (1.0 - sg)).astype(BF16)

    blk = pl.BlockSpec((ts, tc), lambda i, j: (i, j))
    outs = pl.pallas_call(
        body, name=name, out_shape=tuple(jax.ShapeDtypeStruct((S, D), BF16) for _ in range(6)), grid=(S // ts, nb),
        in_specs=[pl.BlockSpec((ts, tc), lambda i, j, n=n: (i, gb + n * nb + j)) for n in range(3)] + [blk] * 4,
        out_specs=(blk,) * 6, compiler_params=_params("parallel", "parallel"),
    )(z, z, z, *proj, dm)
    return outs[:3], outs[3:]


def loss_head(y, target, *, name, ts=256):
    S, D = y.shape
    ts = _tile(S, ts)

    def body(y_ref, t_ref, l_ref, dy_ref):
        err = y_ref[...] - t_ref[...]
        dy_ref[...] = err * (1.0 / D)
        part = 0.5 * jnp.sum(jnp.mean(err * err, axis=-1, keepdims=True), axis=0, keepdims=True)

        @pl.when(pl.program_id(0) == 0)
        def _():
            l_ref[...] = part

        @pl.when(pl.program_id(0) > 0)
        def _():
            l_ref[...] += part

    blk = pl.BlockSpec((ts, D), lambda i: (i, 0))
    return pl.pallas_call(
        body, name=name, out_shape=(jax.ShapeDtypeStruct((1, 1), F32), jax.ShapeDtypeStruct((S, D), F32)), grid=(S // ts,),
        in_specs=[blk, blk], out_specs=(pl.BlockSpec((1, 1), lambda i: (0, 0)), blk), compiler_params=_params("arbitrary"),
    )(y, target)


def adamw(w, g, m, v, *, name):
    shape = w.shape
    C = shape[-1]
    R = int(np.prod(shape[:-1]))
    br = R
    while br % 16 == 0 and br * C * 4 > 2**20:
        br //= 2
    w2, g2, m2, v2 = (t.reshape(R, C) for t in (w, g, m, v))

    def body(w_ref, g_ref, m_ref, v_ref, d_ref, nm_ref, nv_ref):
        d_ref[...], nm_ref[...], nv_ref[...] = _adamw_update(w_ref[...], g_ref[...], m_ref[...], v_ref[...])

    blk = pl.BlockSpec((br, C), lambda i: (i, 0))
    outs = pl.pallas_call(
        body, name=name, out_shape=tuple(jax.ShapeDtypeStruct((R, C), F32) for _ in range(3)), grid=(R // br,),
        in_specs=[blk] * 4, out_specs=(blk,) * 3, compiler_params=_params("parallel"),
    )(w2, g2, m2, v2)
    return tuple(o.reshape(shape) for o in outs)


_ANY = pl.BlockSpec(memory_space=pl.ANY)


def _place():
    return lax.axis_index("x"), lax.axis_index("y"), lax.axis_index("c")


def all_gather(xs, after, *, name):
    n = len(xs)

    def body(*refs):
        x_refs, o_refs = refs[:n], refs[n + 1:2 * n + 1]
        send_sems, recv_sems, local_sems = refs[2 * n + 1:]
        px, py, pc = _place()
        me, sibling = (px, py, pc), (px, py, 1 - pc)
        chips = [(1 - px, py), (px, 1 - py), (1 - px, 1 - py)]

        def slot(t, dev):
            return o_refs[t].at[4 * dev[0] + 2 * dev[1] + dev[2]]

        def copy(t, k, block, to, src=None):
            return pltpu.make_async_remote_copy(
                src_ref=slot(t, block) if src is None else src, dst_ref=slot(t, block),
                send_sem=send_sems.at[t, k], recv_sem=recv_sems.at[t, k], device_id=to, device_id_type=MESH)

        mine = [pltpu.make_async_copy(x_refs[t], slot(t, me), local_sems.at[t]) for t in range(n)]
        first = []
        for t in range(n):
            mine[t].start()
            first += [copy(t, 1 + j, me, (*chip, pc), src=x_refs[t]) for j, chip in enumerate(chips)]
            first.append(copy(t, 0, me, sibling, src=x_refs[t]))
        for cp in first:
            cp.start()
        passed = []
        for t in range(n):
            for j, chip in enumerate(chips):
                copy(t, 1 + j, (*chip, pc), me).wait_recv()
                fwd = copy(t, 4 + j, (*chip, pc), sibling)
                fwd.start()
                passed.append(fwd)
        for t in range(n):
            copy(t, 0, sibling, me).wait_recv()
            for j, chip in enumerate(chips):
                copy(t, 4 + j, (*chip, 1 - pc), me).wait_recv()
        for cp in first + passed:
            cp.wait_send()
        for cp in mine:
            cp.wait()

    return pl.pallas_call(
        body, name=name, out_shape=tuple(jax.ShapeDtypeStruct((8,) + x.shape, x.dtype) for x in xs),
        in_specs=[_ANY] * (n + 1), out_specs=(_ANY,) * n,
        scratch_shapes=[pltpu.SemaphoreType.DMA((n, 7)), pltpu.SemaphoreType.DMA((n, 7)), pltpu.SemaphoreType.DMA((n,))],
    )(*xs, after)


def forward_to_sibling(zones, *, name):
    n = len(zones)

    def body(*refs):
        z_refs, (send_sems, recv_sems) = refs[n:2 * n], refs[2 * n:]
        px, py, pc = _place()
        copies = []
        for t in range(n):
            for j, chip in enumerate([(1 - px, py), (px, 1 - py), (1 - px, 1 - py)]):
                slot = z_refs[t].at[4 * chip[0] + 2 * chip[1] + pc]
                copies.append(pltpu.make_async_remote_copy(
                    src_ref=slot, dst_ref=slot, send_sem=send_sems.at[t, j], recv_sem=recv_sems.at[t, j],
                    device_id=(px, py, 1 - pc), device_id_type=MESH))
        for cp in copies:
            cp.start()
        for cp in copies:
            cp.wait()

    return pl.pallas_call(
        body, name=name, out_shape=tuple(jax.ShapeDtypeStruct(z.shape, z.dtype) for z in zones),
        in_specs=[_ANY] * n, out_specs=(_ANY,) * n, input_output_aliases={t: t for t in range(n)},
        scratch_shapes=[pltpu.SemaphoreType.DMA((n, 3)), pltpu.SemaphoreType.DMA((n, 3))],
    )(*zones)


_HBM = pl.BlockSpec(memory_space=pltpu.HBM)
_SEM = pl.BlockSpec(memory_space=pltpu.SEMAPHORE)
_EFFECT = pltpu.SideEffectType.DATAFLOW_SIDE_EFFECTING


N_PEERS = 7
GATHER_FLIPS = (1, 2, 4, 6)


def _peers(flips=range(1, N_PEERS + 1)):
    px, py, pc = _place()
    flip = lambda p, bit: 1 - p if bit else p
    return [(flip(px, m >> 2 & 1), flip(py, m >> 1 & 1), flip(pc, m & 1)) for m in flips]


def _gather_copies(src_refs, zone_refs, send_sems, recv_sems):
    px, py, pc = _place()
    return [pltpu.make_async_remote_copy(src_ref=s, dst_ref=z.at[4 * px + 2 * py + pc], send_sem=send_sems.at[k],
                                         recv_sem=recv_sems.at[k], device_id=peer, device_id_type=MESH)
            for k, peer in enumerate(_peers(GATHER_FLIPS)) for s, z in zip(src_refs, zone_refs)]


def _scatter_copies(part_refs, zone_refs, send_sems, recv_sems):
    return [pltpu.make_async_remote_copy(src_ref=p.at[4 * peer[0] + 2 * peer[1] + peer[2]], dst_ref=z.at[k],
                                         send_sem=send_sems.at[k], recv_sem=recv_sems.at[k], device_id=peer, device_id_type=MESH)
            for k, peer in enumerate(_peers()) for p, z in zip(part_refs, zone_refs)]


def copies_start(make, srcs, zones, *, name, after=None):
    n = len(srcs)
    extra = [] if after is None else [after]

    def body(*refs):
        k = 2 * n + len(extra)
        for cp in make(refs[:n], refs[n:2 * n], refs[k], refs[k + 1]):
            cp.start()
        refs[-1][...] = jnp.zeros_like(refs[-1])

    arrays = list(srcs) + list(zones)
    outs = pl.pallas_call(
        body, name=name,
        out_shape=(pltpu.SemaphoreType.DMA((N_PEERS,)), pltpu.SemaphoreType.DMA((N_PEERS,)),
                   *[pltpu.HBM(a.shape, a.dtype) for a in arrays], jax.ShapeDtypeStruct((8, LANE), F32)),
        in_specs=[_HBM] * (2 * n) + [_ANY] * len(extra),
        out_specs=(_SEM, _SEM, *[_HBM] * (2 * n), pl.BlockSpec(memory_space=pltpu.VMEM)),
        input_output_aliases={i: 2 + i for i in range(2 * n)},
        compiler_params=pltpu.CompilerParams(has_side_effects=_EFFECT),
    )(*[pltpu.with_memory_space_constraint(a, pltpu.HBM) for a in arrays], *extra)
    return outs[0], outs[1], list(outs[2:2 + n]), list(outs[2 + n:2 + 2 * n]), outs[-1]


def copies_wait(make, send_sems, recv_sems, srcs, zones, after, *, name):
    n = len(srcs)

    def body(*refs):
        for cp in make(refs[:n], refs[n:2 * n], refs[2 * n], refs[2 * n + 1]):
            cp.wait_send()
            cp.wait_recv()

    arrays = list(srcs) + list(zones)
    outs = pl.pallas_call(
        body, name=name, out_shape=tuple(pltpu.HBM(a.shape, a.dtype) for a in arrays),
        in_specs=[_HBM] * (2 * n) + [_SEM, _SEM, _ANY], out_specs=(_HBM,) * (2 * n),
        input_output_aliases={i: i for i in range(2 * n)},
        compiler_params=pltpu.CompilerParams(has_side_effects=_EFFECT),
    )(*arrays, send_sems, recv_sems, after)
    return list(outs[:n]), list(outs[n:])


def place_mine(zone, mine, *, name):
    C = mine.shape[-1]
    R = int(np.prod(mine.shape[:-1]))
    br = _row_block(R, C, mine.dtype.itemsize, budget=2**21)
    me = (4 * lax.axis_index("x") + 2 * lax.axis_index("y") + lax.axis_index("c")).astype(jnp.int32).reshape(1)

    def body(me_ref, m_ref, z_ref, o_ref):
        o_ref[0] = m_ref[...]

    out = pl.pallas_call(
        body, name=name, out_shape=jax.ShapeDtypeStruct((8, R, C), zone.dtype),
        grid_spec=pltpu.PrefetchScalarGridSpec(
            num_scalar_prefetch=1, grid=(R // br,), in_specs=[pl.BlockSpec((br, C), lambda i, me: (i, 0)), _ANY],
            out_specs=pl.BlockSpec((1, br, C), lambda i, me: (me[0], i, 0))),
        input_output_aliases={2: 0}, compiler_params=_params("parallel"),
    )(me, mine.reshape(R, C), zone.reshape(8, R, C))
    return out.reshape(zone.shape)


def _row_block(rows, cols, itemsize, budget=2**20):
    br = rows
    while br % 32 == 0 and br * cols * itemsize > budget:
        br //= 2
    return br


def _adamw_update(w, g, m, v):
    nm = ADAM_B1 * m + (1.0 - ADAM_B1) * g
    nv = ADAM_B2 * v + (1.0 - ADAM_B2) * (g * g)
    m_hat = nm / (1.0 - ADAM_B1 ** ADAM_STEP)
    v_hat = nv / (1.0 - ADAM_B2 ** ADAM_STEP)
    return -ADAM_LR * (m_hat / (jnp.sqrt(v_hat) + ADAM_EPS) + ADAM_WD * w), nm, nv


def grad_sum_adamw(parts, recvs, w, m, v, *, name):
    L, C = len(parts), w.shape[-1]
    R = int(np.prod(w.shape[1:-1]))
    br = _row_block(R, C, 4, budget=2**19)
    chip = (4 * lax.axis_index("x") + 2 * lax.axis_index("y") + lax.axis_index("c")).astype(jnp.int32).reshape(1)

    def body(c_ref, *refs):
        p_refs, r_refs = refs[:L], refs[L:2 * L]
        w_ref, m_ref, v_ref, g_out, d_out, nm_out, nv_out = refs[2 * L:]
        for j in range(L):
            @pl.when(pl.program_id(0) == j)
            def _(j=j):
                g = p_refs[j][0].astype(F32)
                for k in range(N_PEERS):
                    g = g + r_refs[j][k].astype(F32)
                g_out[0] = g
                d_out[0], nm_out[0], nv_out[0] = _adamw_update(w_ref[0], g, m_ref[0], v_ref[0])

    row = lambda j: (lambda l, r, c: jnp.where(l == j, r, 0))
    part_specs = [pl.BlockSpec((1, br, C), lambda l, r, c, f=row(j): (c[0], f(l, r, c), 0)) for j in range(L)]
    recv_specs = [pl.BlockSpec((N_PEERS, br, C), lambda l, r, c, f=row(j): (0, f(l, r, c), 0)) for j in range(L)]
    blk = pl.BlockSpec((1, br, C), lambda l, r, c: (l, r, 0))
    outs = pl.pallas_call(
        body, name=name, out_shape=tuple(jax.ShapeDtypeStruct((L, R, C), F32) for _ in range(4)),
        grid_spec=pltpu.PrefetchScalarGridSpec(
            num_scalar_prefetch=1, grid=(L, R // br), in_specs=part_specs + recv_specs + [blk] * 3, out_specs=(blk,) * 4),
        compiler_params=_params("arbitrary", "arbitrary"),
    )(chip, *[p.reshape(8, R, C) for p in parts], *[r.reshape(N_PEERS, R, C) for r in recvs],
      *[t.reshape(L, R, C) for t in (w, m, v)])
    return tuple(o.reshape(w.shape) for o in outs)


def ordered_sum(parts, *, name):
    _, R, C = parts.shape

    def body(p_ref, o_ref):
        acc = p_ref[0]
        for d in range(1, 8):
            acc = acc + p_ref[d]
        o_ref[...] = acc

    return pl.pallas_call(body, name=name, out_shape=jax.ShapeDtypeStruct((R, C), F32))(parts)


W_IN_COLS, W_IN_SHARD = 13128, 1641
W_IN_SEGMENTS = ((0, 832, 0), (832, 3904, Z_FOX), (3904, 3912, FF_COL), (3912, 6984, Z_CH), (6984, 13128, Z_GATE))


def col_gather(src, table, pieces, out_shape, *, name, tr=2048, after=None):
    R, C = src.shape[1:]
    tr = _tile(R, tr)
    width = 2 + 6 * pieces
    nb = table.shape[0] // width
    last_tile, last_valid = C // LANE, C % LANE

    extra = [] if after is None else [after]

    def body(tab, *refs):
        o_ref, acc_ref = refs[-2:]
        base = pl.program_id(1) * width
        lane = lax.broadcasted_iota(jnp.int32, (tr, LANE), 1)
        row = lax.broadcasted_iota(jnp.int32, (2 * LANE, LANE), 0)
        col = lax.broadcasted_iota(jnp.int32, (2 * LANE, LANE), 1)
        acc_ref[...] = jnp.zeros_like(acc_ref)
        for p in range(pieces):
            e = base + 2 + 6 * p
            lo, hi = tab[e + 4], tab[e + 5]

            @pl.when(hi > lo)
            def _(p=p, e=e, lo=lo, hi=hi):
                tiles = []
                for tcol in (1, 2):
                    x = refs[2 * p + tcol - 1][0]
                    if last_valid:
                        x = jnp.where(jnp.logical_or(tab[e + tcol] < last_tile, lane < last_valid), x, jnp.zeros_like(x))
                    tiles.append(x)
                hit = jnp.logical_and(row == col + tab[e + 3], jnp.logical_and(col >= lo, col < hi))
                sel = jnp.where(hit, 1.0, 0.0).astype(src.dtype)
                acc_ref[...] += jnp.dot(jnp.concatenate(tiles, axis=1), sel, preferred_element_type=F32)
        o_ref[0] = acc_ref[...].astype(o_ref.dtype)

    in_specs = []
    for p in range(pieces):
        for tcol in (1, 2):
            in_specs.append(pl.BlockSpec(
                (1, tr, LANE), lambda i, b, tab, p=p, tcol=tcol: (tab[b * width + 2 + 6 * p], i, tab[b * width + 2 + 6 * p + tcol])))
    return pl.pallas_call(
        body, name=name, out_shape=jax.ShapeDtypeStruct(out_shape, src.dtype),
        grid_spec=pltpu.PrefetchScalarGridSpec(
            num_scalar_prefetch=1, grid=(R // tr, nb), in_specs=in_specs + [_ANY] * len(extra),
            out_specs=pl.BlockSpec((1, tr, LANE), lambda i, b, tab: (tab[b * width], i, tab[b * width + 1])),
            scratch_shapes=[pltpu.VMEM((tr, LANE), F32)]),
        compiler_params=_params("parallel", "parallel"),
    )(jnp.asarray(table, jnp.int32), *([src] * (2 * pieces)), *extra)


def _piece(sd, start, lo, hi, last_tile):
    t0 = start // LANE
    return [sd, t0, min(t0 + 1, last_tile), start % LANE - lo, lo, hi]


def _pad_pieces(rows, pieces):
    out, prev = [], [0, 0, 0, 0, 0, 0] * pieces
    for head, pcs in rows:
        full = list(pcs)
        for p in range(len(pcs) // 6, pieces):
            full += prev[6 * p:6 * p + 3] + [0, 0, 0]
        out.append(head + full)
        prev = full
    return np.asarray(out, np.int32).reshape(-1)


def _w_in_table(layer, L):
    rows = []
    for b in range(Z_W // LANE):
        pcs = []
        for first, last, col in W_IN_SEGMENTS:
            lo, hi = max(LANE * b, col), min(LANE * (b + 1), col + last - first)
            while lo < hi:
                c = first + lo - col
                n = min(hi - lo, W_IN_SHARD - c % W_IN_SHARD)
                pcs += _piece((c // W_IN_SHARD) * L + layer, c % W_IN_SHARD, lo - LANE * b, lo - LANE * b + n, W_IN_SHARD // LANE)
                lo += n
        assert len(pcs) <= 12
        rows.append(([0, b], pcs))
    return _pad_pieces(rows, 2)


def _w_in_grad_table():
    rows = []
    for d in range(8):
        for t in range(-(-W_IN_SHARD // LANE)):
            pcs = []
            c0 = d * W_IN_SHARD + LANE * t
            c1 = min(c0 + LANE, (d + 1) * W_IN_SHARD)
            for first, last, col in W_IN_SEGMENTS:
                lo, hi = max(c0, first), min(c1, last)
                if lo < hi:
                    pcs += _piece(0, col + lo - first, lo - c0, hi - c0, Z_W // LANE - 1)
            assert len(pcs) <= 18
            rows.append(([d, t], pcs))
    return _pad_pieces(rows, 3)


def block_copy(src, out_shape, in_blk, out_blk, grid, in_map, out_map, *, name):
    def body(x_ref, o_ref):
        o_ref[(0,) * (len(out_blk) - 2) + (Ellipsis,)] = x_ref[(0,) * (len(in_blk) - 2) + (Ellipsis,)]

    return pl.pallas_call(
        body, name=name, out_shape=jax.ShapeDtypeStruct(out_shape, src.dtype), grid=grid,
        in_specs=[pl.BlockSpec(in_blk, in_map)], out_specs=pl.BlockSpec(out_blk, out_map),
        compiler_params=_params("parallel", "parallel"),
    )(src)


def _columns_from_owners(z, *, name, lead=()):
    K, c = z.shape[-2:]
    tr, nl = _tile(K, 1024), len(lead)
    return block_copy(z, (K, 8 * c), (1,) * (1 + nl) + (tr, c), (tr, c), (8, K // tr),
                      lambda d, i: (d, *lead, i, 0), lambda d, i: (i, d), name=name)


def _owners_from_columns(g, *, name):
    K, c = g.shape[0], g.shape[1] // 8
    tr = _tile(K, 1024)
    return block_copy(g, (8, K, c), (tr, c), (1, tr, c), (8, K // tr), lambda d, i: (i, d), lambda d, i: (d, i, 0), name=name)


def _full_from_shards(k, sh, tag):
    if k not in COL_SHARDED:
        return sh.reshape((-1, sh.shape[-1]))
    if k == 'w_br':
        return [_columns_from_owners(sh, lead=(n,), name=f"{tag}_w_br{n}_layout") for n in range(3)]
    if k == 'w_uq':
        return _columns_from_owners(jnp.pad(sh, ((0, 0), (0, 0), (0, 64))), name=f"{tag}_w_uq_layout")
    if k == 'w_ukv':
        return block_copy(sh, (256, 2048), (1, 256, LANE), (256, LANE), (2, MLA_HEADS),
                          lambda t, h: (h, 0, t), lambda t, h: (0, t * MLA_HEADS + h), name=f"{tag}_w_ukv_layout")
    return _columns_from_owners(sh, name=f"{tag}_{k}_layout")


def _shards_from_full(k, g, tag):
    if k not in COL_SHARDED:
        return g.reshape((8, g.shape[0] // 8, g.shape[1]))
    if k == 'w_br':
        return jnp.stack([_owners_from_columns(g[n], name=f"{tag}_dw_br{n}_layout") for n in range(3)], axis=1)
    if k == 'w_uq':
        return _owners_from_columns(g, name=f"{tag}_dw_uq_layout")[:, :, :MLA_QK]
    if k == 'w_ukv':
        return block_copy(g, (8, 256, 256), (256, LANE), (1, 256, LANE), (2, MLA_HEADS),
                          lambda t, h: (0, t * MLA_HEADS + h), lambda t, h: (h, 0, t), name=f"{tag}_dw_ukv_layout")
    return _owners_from_columns(g, name=f"{tag}_d{k}_layout")


def w_in_full(gathered, layer, *, name, after=None):
    _, L, K, c = gathered.shape
    return col_gather(gathered.reshape(8 * L, K, c), _w_in_table(layer, L), 2, (1, K, Z_W), name=name, after=after)[0]


def w_in_shards(g, *, name):
    return col_gather(g[None], _w_in_grad_table(), 3, (8, g.shape[0], W_IN_SHARD), name=name)


def _layer_fwd(x, mem, W, P, cos, ssin, tag, later=None):
    S = x.shape[0]
    sv = {'x0': x}
    h = rmsnorm_fwd(x, P['g_mix'], name=f"{tag}_norm_mix")
    z = mm(h, W['w_in'], name=f"{tag}_mm_in")
    if later is not None:
        W = later(z)
    sv.update(h=h, z=z)
    cqn = rmsnorm_fwd(z, P['g_cq'], col=0, width=512, name=f"{tag}_norm_cq")
    ckvn = rmsnorm_fwd(z, P['g_ckv'], col=512, width=256, name=f"{tag}_norm_ckv")
    qf = mm(cqn, W['w_uq'], name=f"{tag}_mm_uq")
    kvf = mm(ckvn, W['w_ukv'], name=f"{tag}_mm_ukv")
    qa = mla_prep_fwd(qf, qf, P['g_mla_q'], cos, ssin, n_col=0, n_stride=2 * LANE, r_col=LANE, r_stride=2 * LANE,
                      heads=8, name=f"{tag}_mla_q")
    ka = mla_prep_fwd(kvf, z, P['g_mla_k'], cos, ssin, n_col=0, n_stride=LANE, r_col=KR_COL, r_stride=0,
                      heads=8, name=f"{tag}_mla_k")
    ya, lse_a = causal_attn_fwd(qa, ka, kvf, v_col=1024, chunked=True, scale=MLA_QK ** -0.5, name=f"{tag}_mla_attn")
    sv.update(cqn=cqn, ckvn=ckvn, qf=qf, kvf=kvf, qa=qa, ka=ka, lse_a=lse_a)
    qb = headnorm_fwd(z, P['g_fox_q'], col=Z_FOX, heads=8, name=f"{tag}_fox_qn")
    kb = headnorm_fwd(z, P['g_fox_k'], col=Z_FOX + 1024, heads=8, name=f"{tag}_fox_kn")
    fl = z[:, FF_COL:FF_COL + 8].T
    cum = foxgate_fwd(fl, P['b_f'], name=f"{tag}_fox_gate")
    cq, ck = cum.reshape(8, S, 1), cum.reshape(8, 1, S)
    yb, lse_b = causal_attn_fwd(qb, kb, z, v_col=Z_FOX + 2048, chunked=False, scale=LANE ** -0.5, cq=cq, ck=ck,
                                name=f"{tag}_fox_attn")
    sv.update(qb=qb, kb=kb, fl=fl, cq=cq, ck=ck, lse_b=lse_b)
    qc = headnorm_fwd(z, P['g_ch_q'], col=Z_CH, heads=8, name=f"{tag}_ch_qn")
    kc = headnorm_fwd(z, P['g_ch_k'], col=Z_CH + 1024, heads=8, name=f"{tag}_ch_kn")
    kcp = jnp.pad(kc, ((0, 0), (PAD, CHUNK), (0, 0)))
    vcp = jnp.pad(z[:, Z_CH + 2048:Z_CH + 3072], ((PAD, CHUNK), (0, 0)))
    bias = band_bias(P['rel_bias'], name=f"{tag}_ch_bias")
    yc = band_fwd(qc, kcp, vcp, bias, scale=LANE ** -0.5, name=f"{tag}_ch_attn")
    sv.update(qc=qc, kcp=kcp, vcp=vcp, bias=bias)
    ys = (ya, yb, yc)
    proj = [mm(ys[n], W['w_br'][n], name=f"{tag}_mm_br{n}") for n in range(3)]
    merged = gate_fwd(z, proj, name=f"{tag}_gate")
    x1 = mm(merged, W['w_out'], epi='add', aux=x, name=f"{tag}_mm_out")
    sv.update(ys=ys, proj=proj, merged=merged, x1=x1)
    hc = rmsnorm_fwd(x1, P['g_cross'], name=f"{tag}_norm_cross")
    memn = rmsnorm_fwd(mem, P['g_mem'], name=f"{tag}_norm_mem")
    qx_raw = mm(hc, W['w_xq'], name=f"{tag}_mm_xq")
    memkv = mm(memn, W['w_xkv'], name=f"{tag}_mm_xkv")
    qx = headnorm_fwd(qx_raw, P['g_x_q'], col=0, heads=4, name=f"{tag}_x_qn")
    kx = headnorm_fwd(memkv, P['g_x_k'], col=0, heads=4, name=f"{tag}_x_kn")
    ox = attn_fwd(qx, kx, memkv, v_col=512, scale=LANE ** -0.5, name=f"{tag}_x_attn")
    x2 = mm(ox, W['w_xo'], epi='add', aux=x1, name=f"{tag}_mm_xo")
    sv.update(hc=hc, memn=memn, qx_raw=qx_raw, memkv=memkv, qx=qx, kx=kx, ox=ox, x2=x2)
    hm = rmsnorm_fwd(x2, P['g_mlp'], name=f"{tag}_norm_mlp")
    u, a = mm(hm, W['w_1'], epi='relu2', out_dtype=BF16, name=f"{tag}_mm_w1")
    x3 = mm(a, W['w_2'], epi='add', aux=x2, name=f"{tag}_mm_w2")
    sv.update(hm=hm, u=u, a=a)
    return x3, sv


def _layer_bwd(dx, mem, W, P, sv, cos, ssin, tag, send_off=None):
    S = dx.shape[0]
    z = sv['z']
    gw, gs = {}, {}
    wgrad = lambda a, d, name: mm(a, d, ta=True, out_dtype=BF16, name=name)
    gw['w_2'] = wgrad(sv['a'], dx, f"{tag}_dw2")
    du = mm(dx, W['w_2'], tb=True, epi='mul_drelu2', aux=sv['u'], out_dtype=BF16, name=f"{tag}_du")
    gw['w_1'] = wgrad(sv['hm'], du, f"{tag}_dw1")
    dhm = mm(du, W['w_1'], tb=True, name=f"{tag}_dhm")
    dx, gs['g_mlp'] = rmsnorm_bwd(sv['x2'], P['g_mlp'], dhm, res=dx, name=f"{tag}_dnorm_mlp")
    gw['w_xo'] = wgrad(sv['ox'], dx, f"{tag}_dwxo")
    dox = mm(dx, W['w_xo'], tb=True, out_dtype=BF16, name=f"{tag}_dox")
    dqx, dkx, dvx = attn_bwd(sv['qx'], sv['kx'], sv['memkv'], dox, v_col=512, scale=LANE ** -0.5, name=f"{tag}_x_attn_bwd")
    dqx_raw, gs['g_x_q'] = headnorm_bwd(sv['qx_raw'], P['g_x_q'], dqx, col=0, heads=4, name=f"{tag}_x_qn_bwd")
    dkx_raw, gs['g_x_k'] = headnorm_bwd(sv['memkv'], P['g_x_k'], dkx, col=0, heads=4, name=f"{tag}_x_kn_bwd")
    dqx_b = dqx_raw.astype(BF16)
    gw['w_xq'] = wgrad(sv['hc'], dqx_b, f"{tag}_dwxq")
    dhc = mm(dqx_b, W['w_xq'], tb=True, name=f"{tag}_dhc")
    dx, gs['g_cross'] = rmsnorm_bwd(sv['x1'], P['g_cross'], dhc, res=dx, name=f"{tag}_dnorm_cross")
    dmemkv = jnp.concatenate([dkx_raw, dvx], axis=1).astype(BF16)
    gw['w_xkv'] = wgrad(sv['memn'], dmemkv, f"{tag}_dwxkv")
    dmemn = mm(dmemkv, W['w_xkv'], tb=True, name=f"{tag}_dmemn")
    _, gs['g_mem'] = rmsnorm_bwd(mem, P['g_mem'], dmemn, name=f"{tag}_dnorm_mem")
    gw['w_out'] = wgrad(sv['merged'], dx, f"{tag}_dwout")
    dmerged = mm(dx, W['w_out'], tb=True, name=f"{tag}_dmerged")
    dproj, dgl = gate_bwd(z, sv['proj'], dmerged, name=f"{tag}_gate_bwd")
    gw['w_br'] = [wgrad(sv['ys'][n], dproj[n], f"{tag}_dwbr{n}") for n in range(3)]
    dys = [mm(dproj[n], W['w_br'][n], tb=True, out_dtype=BF16, name=f"{tag}_dys{n}") for n in range(3)]
    dqa, dka, dva = causal_attn_bwd(sv['qa'], sv['ka'], sv['kvf'], sv['ys'][0], dys[0], sv['lse_a'], v_col=1024, chunked=True,
                                    scale=MLA_QK ** -0.5, name=f"{tag}_mla_attn_bwd")
    g_mla_q = P['g_mla_q'] if send_off is None else P['g_mla_q'] + send_off[0](gw)
    dqn, dqr, gs['g_mla_q'] = mla_prep_bwd(sv['qf'], sv['qf'], g_mla_q, cos, ssin, dqa, n_col=0, n_stride=2 * LANE,
                                           r_col=LANE, r_stride=2 * LANE, heads=8, name=f"{tag}_mla_q_bwd")
    dkn, dkr, gs['g_mla_k'] = mla_prep_bwd(sv['kvf'], z, P['g_mla_k'], cos, ssin, dka, n_col=0, n_stride=LANE,
                                           r_col=KR_COL, r_stride=0, heads=8, name=f"{tag}_mla_k_bwd")
    dqf = jnp.stack([dqn.reshape(S, 8, LANE), dqr.reshape(S, 8, LANE)], axis=2).reshape(S, 2048).astype(BF16)
    dkvf = jnp.concatenate([dkn, dva], axis=1).astype(BF16)
    gw['w_uq'] = wgrad(sv['cqn'], dqf, f"{tag}_dwuq")
    gw['w_ukv'] = wgrad(sv['ckvn'], dkvf, f"{tag}_dwukv")
    dcqn = mm(dqf, W['w_uq'], tb=True, name=f"{tag}_dcqn")
    dckvn = mm(dkvf, W['w_ukv'], tb=True, name=f"{tag}_dckvn")
    dcq_raw, gs['g_cq'] = rmsnorm_bwd(z, P['g_cq'], dcqn, col=0, width=512, name=f"{tag}_dnorm_cq")
    dckv_raw, gs['g_ckv'] = rmsnorm_bwd(z, P['g_ckv'], dckvn, col=512, width=256, name=f"{tag}_dnorm_ckv")
    dqb, dkb, dvb, dcq, dck = causal_attn_bwd(sv['qb'], sv['kb'], z, sv['ys'][1], dys[1], sv['lse_b'], v_col=Z_FOX + 2048,
                                              chunked=False, scale=LANE ** -0.5, cq=sv['cq'], ck=sv['ck'],
                                              name=f"{tag}_fox_attn_bwd")
    dqb_raw, gs['g_fox_q'] = headnorm_bwd(z, P['g_fox_q'], dqb, col=Z_FOX, heads=8, name=f"{tag}_fox_qn_bwd")
    dkb_raw, gs['g_fox_k'] = headnorm_bwd(z, P['g_fox_k'], dkb, col=Z_FOX + 1024, heads=8, name=f"{tag}_fox_kn_bwd")
    dfl, gs['b_f'] = foxgate_bwd(sv['fl'], P['b_f'], dcq.reshape(8, S) + dck.reshape(8, S), name=f"{tag}_fox_gate_bwd")
    dqc, dkcp, dvcp, dbias = band_bwd(sv['qc'], sv['kcp'], sv['vcp'], sv['bias'], dys[2], scale=LANE ** -0.5,
                                      name=f"{tag}_ch_attn_bwd")
    dqc_raw, gs['g_ch_q'] = headnorm_bwd(z, P['g_ch_q'], dqc, col=Z_CH, heads=8, name=f"{tag}_ch_qn_bwd")
    dkc_raw, gs['g_ch_k'] = headnorm_bwd(z, P['g_ch_k'], dkcp[:, PAD:PAD + S, :], col=Z_CH + 1024, heads=8,
                                         name=f"{tag}_ch_kn_bwd")
    gs['rel_bias'] = relbias_bwd(chunk_band_dbias(dbias), name=f"{tag}_relbias_bwd")
    b16 = lambda t: t.astype(BF16)
    dz = jnp.concatenate([b16(dcq_raw), b16(dckv_raw), b16(dkr), b16(dfl.T), jnp.zeros((S, 120), BF16),
                          b16(dqb_raw), b16(dkb_raw), b16(dvb), b16(dqc_raw), b16(dkc_raw), b16(dvcp[PAD:PAD + S]),
                          dgl[0], dgl[1], dgl[2]], axis=1)
    gw['w_in'] = wgrad(sv['h'], dz, f"{tag}_dwin")
    dh = mm(dz, W['w_in'], tb=True, name=f"{tag}_dh")
    g_mix = P['g_mix'] if send_off is None else P['g_mix'] + send_off[1](gw)
    dx, gs['g_mix'] = rmsnorm_bwd(sv['x0'], g_mix, dh, res=dx, name=f"{tag}_dnorm_mix")
    return dx, gw, gs


def _local_step(x, mem, target, Ws, Ps):
    S = x.shape[0]
    cos, ssin = _rope_tables(S)
    L = len(Ws)
    saved = []
    for l in range(L):
        x, sv = _layer_fwd(x, mem, Ws[l], Ps[l], cos, ssin, f"l{l}")
        saved.append(sv)
    loss, dx = loss_head(x, target, name="loss_head")
    gws, gss = [None] * L, [None] * L
    for l in reversed(range(L)):
        dx, gws[l], gss[l] = _layer_bwd(dx, mem, Ws[l], Ps[l], saved[l], cos, ssin, f"l{l}")
    return loss, dx, gws, gss


def _pack_small(d):
    flat = jnp.concatenate([d[k].reshape(-1) for k in SMALL])
    n = flat.shape[0]
    rows = -(-n // (8 * LANE)) * 8
    return jnp.pad(flat, (0, rows * LANE - n)).reshape(rows, LANE)


def _unpack_small(packed, like):
    flat, out, off = packed.reshape(-1), {}, 0
    for k in SMALL:
        n = int(np.prod(like[k].shape))
        out[k] = flat[off:off + n].reshape(like[k].shape)
        off += n
    return out


def kernel(x, mem, g_mix, w_in, g_cq, w_uq, g_ckv, w_ukv, g_mla_q, g_mla_k, b_f, g_fox_q, g_fox_k, rel_bias, g_ch_q, g_ch_k, w_br, w_out, g_cross, g_mem, w_xq, w_xkv, g_x_q, g_x_k, w_xo, g_mlp, w_1, w_2, loss_target, m_g_mix, m_w_in, m_g_cq, m_w_uq, m_g_ckv, m_w_ukv, m_g_mla_q, m_g_mla_k, m_b_f, m_g_fox_q, m_g_fox_k, m_rel_bias, m_g_ch_q, m_g_ch_k, m_w_br, m_w_out, m_g_cross, m_g_mem, m_w_xq, m_w_xkv, m_g_x_q, m_g_x_k, m_w_xo, m_g_mlp, m_w_1, m_w_2, v_g_mix, v_w_in, v_g_cq, v_w_uq, v_g_ckv, v_w_ukv, v_g_mla_q, v_g_mla_k, v_b_f, v_g_fox_q, v_g_fox_k, v_rel_bias, v_g_ch_q, v_g_ch_k, v_w_br, v_w_out, v_g_cross, v_g_mem, v_w_xq, v_w_xkv, v_g_x_q, v_g_x_k, v_w_xo, v_g_mlp, v_w_1, v_w_2):
    args = locals()
    w = {k: args[k] for k in WEIGHTS}
    m = {k: args['m_' + k] for k in WEIGHTS}
    v = {k: args['v_' + k] for k in WEIGHTS}
    L = w_in.shape[0]

    Ps = [{k: w[k][l] for k in SMALL} for l in range(L)]
    xs, memv = x[0], mem[0]
    cos, ssin = _rope_tables(xs.shape[0])

    gathers = {}

    def start_gather(l, group, after=None):
        shards = [w[k][l].astype(BF16) for k in group]
        gathers[l, group] = copies_start(_gather_copies, shards, [lax.empty((8,) + s.shape, s.dtype) for s in shards],
                                         name=f"l{l}_ag_start_{group[0]}", after=after)
        return gathers[l, group][4][0, :1]

    Ps[0]['g_mix'] = Ps[0]['g_mix'] + start_gather(0, AG_GROUPS[0])

    def arrived(l, group, after, on_landed=None):
        send_sems, recv_sems, shards, zones, _ = gathers[l, group]
        shards, zones = copies_wait(_gather_copies, send_sems, recv_sems, shards, zones, after, name=f"l{l}_ag_wait_{group[0]}")
        then = None if on_landed is None else on_landed(shards[0])
        zones = forward_to_sibling(zones, name=f"l{l}_ag_forward_{group[0]}")
        zones = [place_mine(z, s, name=f"l{l}_{k}_mine") for k, z, s in zip(group, zones, shards)]
        return {k: w_in_full(z[:, None], 0, name=f"l{l}_w_in_layout", after=then) if k == 'w_in'
                else _full_from_shards(k, z, f"l{l}") for k, z in zip(group, zones)}

    Ws, saved = [], []
    for l in range(L):
        if l == 0:
            def start_the_rest(anchor):
                rest = [(j, group) for j in range(L) for group in AG_GROUPS if (j, group) != (0, AG_GROUPS[0])]
                return sum(start_gather(j, group, after=anchor) for j, group in rest)

            Ws.append(arrived(0, AG_GROUPS[0], Ps[0]['g_mix'], start_the_rest))
        else:
            Ws.append(arrived(l, AG_GROUPS[0], xs))

        def later(anchor, l=l):
            Ws[l].update(arrived(l, AG_GROUPS[1], anchor))
            return Ws[l]

        xs, sv = _layer_fwd(xs, memv, Ws[l], Ps[l], cos, ssin, f"l{l}", later)
        saved.append(sv)

    loss, dx = loss_head(xs, loss_target[0], name="loss_head")
    loss = lax.psum(loss[0, 0], ("x", "y", "c"))

    gss = [None] * L
    scatters = {group: [None] * L for group in RS_GROUPS}
    for l in reversed(range(L)):
        def send_off(gw, group, l=l):
            gdst = [w_in_shards(gw[k], name=f"l{l}_dw_in_layout") if k == 'w_in' else _shards_from_full(k, gw[k], f"l{l}")
                    for k in group]
            started = copies_start(_scatter_copies, gdst, [lax.empty((N_PEERS,) + g.shape[1:], g.dtype) for g in gdst],
                                   name=f"l{l}_rs_start_{group[0]}")
            scatters[group][l] = started[:4]
            return started[4][0, :1]

        hooks = tuple((lambda gw, group=group: send_off(gw, group)) for group in RS_GROUPS)
        dx, _, gss[l] = _layer_bwd(dx, memv, Ws[l], Ps[l], saved[l], cos, ssin, f"l{l}", hooks)
    grad_x = dx

    grads, delta, new_m, new_v = {}, {}, {}, {}
    after = grad_x
    for group in RS_GROUPS:
        done = [copies_wait(_scatter_copies, *scatters[group][l], after, name=f"l{l}_rs_wait_{group[0]}") for l in range(L)]
        for t, k in enumerate(group):
            grads[k], delta[k], new_m[k], new_v[k] = grad_sum_adamw(
                [done[l][0][t] for l in range(L)], [done[l][1][t] for l in range(L)], w[k], m[k], v[k], name=f"adamw_{k}")
        after = sum(delta[k][(0,) * (delta[k].ndim - 1)][:1] for k in group)

    small_part = _pack_small({k: jnp.stack([gss[l][k] for l in range(L)]) for k in SMALL})
    small_all = all_gather([small_part], after, name="ag_small")[0]
    grads.update(_unpack_small(ordered_sum(small_all, name="small_sum"), {k: w[k] for k in SMALL}))
    sd, sm, sv_ = adamw(_pack_small({k: w[k] for k in SMALL}), _pack_small({k: grads[k] for k in SMALL}),
                        _pack_small({k: m[k] for k in SMALL}), _pack_small({k: v[k] for k in SMALL}), name="adamw_small")
    like = {k: w[k] for k in SMALL}
    delta.update(_unpack_small(sd, like))
    new_m.update(_unpack_small(sm, like))
    new_v.update(_unpack_small(sv_, like))

    return (loss, grad_x[None], *[grads[k] for k in WEIGHTS], *[delta[k] for k in WEIGHTS],
            *[new_m[k] for k in WEIGHTS], *[new_v[k] for k in WEIGHTS])
```

```python
import numpy as np
import jax
import jax.numpy as jnp
from jax import lax
from jax.experimental import pallas as pl
from jax.experimental.pallas import tpu as pltpu

F32, BF16 = jnp.float32, jnp.bfloat16
EPS = 1e-6
NEG = -1e30
LANE = 128
VMEM_LIMIT_BYTES = 56 * 2**20
MESH = pl.DeviceIdType.MESH

D_MODEL = 2048
CHUNK = 64
BAND = 9 * CHUNK
PAD = 8 * CHUNK
REL_CLIP = 128
MLA_HEADS, MLA_NOPE, MLA_ROPE, MLA_QK = 8, 128, 64, 192
N_HEADS = 8
X_HEADS = 4
ROPE_THETA = 10000.0
ADAM_LR, ADAM_B1, ADAM_B2, ADAM_EPS, ADAM_WD, ADAM_STEP = 0.001, 0.9, 0.999, 1e-08, 0.01, 10

Z_MAIN, Z_FOX, Z_CH, Z_GATE, Z_W = 0, 1024, 4096, 7168, 13312
KR_COL, FF_COL = 768, 896

BIG = ('w_in', 'w_uq', 'w_ukv', 'w_br', 'w_out', 'w_xq', 'w_xkv', 'w_xo', 'w_1', 'w_2')
COL_SHARDED = ('w_in', 'w_uq', 'w_ukv', 'w_br', 'w_xo', 'w_1')
RS_GROUPS = (('w_2', 'w_1', 'w_xo', 'w_xq', 'w_xkv', 'w_out', 'w_br'), ('w_uq', 'w_ukv', 'w_in'))
AG_GROUPS = (('w_in',), ('w_uq', 'w_ukv', 'w_br', 'w_out', 'w_xq', 'w_xkv', 'w_xo', 'w_1', 'w_2'))
SMALL = ('g_mix', 'g_cq', 'g_ckv', 'g_mla_q', 'g_mla_k', 'b_f', 'g_fox_q', 'g_fox_k', 'rel_bias', 'g_ch_q',
         'g_ch_k', 'g_cross', 'g_mem', 'g_x_q', 'g_x_k', 'g_mlp')
WEIGHTS = ('g_mix', 'w_in', 'g_cq', 'w_uq', 'g_ckv', 'w_ukv', 'g_mla_q', 'g_mla_k', 'b_f', 'g_fox_q', 'g_fox_k',
           'rel_bias', 'g_ch_q', 'g_ch_k', 'w_br', 'w_out', 'g_cross', 'g_mem', 'w_xq', 'w_xkv', 'g_x_q', 'g_x_k',
           'w_xo', 'g_mlp', 'w_1', 'w_2')


def _params(*sem):
    return pltpu.CompilerParams(dimension_semantics=sem, vmem_limit_bytes=VMEM_LIMIT_BYTES)


def _tile(dim, pref):
    if dim <= pref:
        return dim
    for t in range(pref - pref % LANE, 0, -LANE):
        if dim % t == 0:
            return t
    raise ValueError((dim, pref))


def mm(a, b, *, ta=False, tb=False, out_dtype=F32, epi=None, aux=None, name, tm=1024, tn=512, tk=2048):
    M, K = (a.shape[1], a.shape[0]) if ta else a.shape
    N = b.shape[0] if tb else b.shape[1]
    assert (b.shape[1] if tb else b.shape[0]) == K, (a.shape, b.shape, ta, tb)
    tm, tn, tk = _tile(M, tm), _tile(N, tn), _tile(K, tk)
    nk = K // tk
    dn = (((0 if ta else 1,), (1 if tb else 0,)), ((), ()))
    n_aux = 0 if aux is None else 1

    def finish(acc, aux_refs, o_refs):
        if epi is None:
            o_refs[0][...] = acc.astype(o_refs[0].dtype)
        elif epi == 'add':
            o_refs[0][...] = (acc + aux_refs[0][...]).astype(o_refs[0].dtype)
        elif epi == 'relu2':
            o_refs[0][...] = acc
            r = jnp.maximum(acc, 0.0)
            o_refs[1][...] = (r * r).astype(o_refs[1].dtype)
        elif epi == 'mul_drelu2':
            o_refs[0][...] = (acc * (2.0 * jnp.maximum(aux_refs[0][...], 0.0))).astype(o_refs[0].dtype)

    def body(a_ref, b_ref, *rest):
        aux_refs = rest[:n_aux]
        o_refs = rest[n_aux:n_aux + (2 if epi == 'relu2' else 1)]
        part = lax.dot_general(a_ref[...].astype(BF16), b_ref[...].astype(BF16), dn, preferred_element_type=F32)
        if nk == 1:
            finish(part, aux_refs, o_refs)
        else:
            acc_ref = rest[-1]
            k = pl.program_id(2)

            @pl.when(k == 0)
            def _():
                acc_ref[...] = part

            @pl.when(k > 0)
            def _():
                acc_ref[...] += part

            @pl.when(k == nk - 1)
            def _():
                finish(acc_ref[...], aux_refs, o_refs)

    a_spec = pl.BlockSpec((tk, tm), lambda i, j, k: (k, i)) if ta else pl.BlockSpec((tm, tk), lambda i, j, k: (i, k))
    b_spec = pl.BlockSpec((tn, tk), lambda i, j, k: (j, k)) if tb else pl.BlockSpec((tk, tn), lambda i, j, k: (k, j))
    o_spec = pl.BlockSpec((tm, tn), lambda i, j, k: (i, j))
    if epi == 'relu2':
        out_shape = (jax.ShapeDtypeStruct((M, N), F32), jax.ShapeDtypeStruct((M, N), out_dtype))
        out_specs = (o_spec, o_spec)
    else:
        out_shape, out_specs = jax.ShapeDtypeStruct((M, N), out_dtype), o_spec
    return pl.pallas_call(
        body, name=name, out_shape=out_shape, grid=(M // tm, N // tn, nk),
        in_specs=[a_spec, b_spec] + [o_spec] * n_aux, out_specs=out_specs,
        scratch_shapes=[pltpu.VMEM((tm, tn), F32)] if nk > 1 else [],
        compiler_params=_params("parallel", "parallel", "arbitrary"),
    )(a, b, *([aux] if n_aux else []))


def rmsnorm_fwd(x, g, *, col=0, width=None, out_dtype=BF16, name, ts=256):
    S = x.shape[0]
    width = x.shape[1] if width is None else width
    ts, cb = _tile(S, ts), col // width

    def body(x_ref, g_ref, o_ref):
        xf = x_ref[...]
        r = lax.rsqrt(jnp.mean(xf * xf, axis=-1, keepdims=True) + EPS)
        o_ref[...] = (xf * r * g_ref[...]).astype(o_ref.dtype)

    return pl.pallas_call(
        body, name=name, out_shape=jax.ShapeDtypeStruct((S, width), out_dtype), grid=(S // ts,),
        in_specs=[pl.BlockSpec((ts, width), lambda i: (i, cb)), pl.BlockSpec((1, width), lambda i: (0, 0))],
        out_specs=pl.BlockSpec((ts, width), lambda i: (i, 0)), compiler_params=_params("parallel"),
    )(x, g.reshape(1, width))


def rmsnorm_bwd(x, g, dy, *, col=0, width=None, res=None, name, ts=256):
    S = x.shape[0]
    width = x.shape[1] if width is None else width
    ts, cb = _tile(S, ts), col // width
    has_res = res is not None

    def body(x_ref, g_ref, dy_ref, *rest):
        dx_ref, dg_ref = rest[-2:]
        xf = x_ref[...]
        r = lax.rsqrt(jnp.mean(xf * xf, axis=-1, keepdims=True) + EPS)
        dyf = dy_ref[...].astype(F32)
        dyg = dyf * g_ref[...]
        dx = r * dyg - xf * (r * r * r) * jnp.mean(dyg * xf, axis=-1, keepdims=True)
        if has_res:
            dx = dx + rest[0][...]
        dx_ref[...] = dx
        part = jnp.sum(dyf * xf * r, axis=0, keepdims=True)

        @pl.when(pl.program_id(0) == 0)
        def _():
            dg_ref[...] = part

        @pl.when(pl.program_id(0) > 0)
        def _():
            dg_ref[...] += part

    blk = pl.BlockSpec((ts, width), lambda i: (i, 0))
    dx, dg = pl.pallas_call(
        body, name=name,
        out_shape=(jax.ShapeDtypeStruct((S, width), F32), jax.ShapeDtypeStruct((1, width), F32)), grid=(S // ts,),
        in_specs=[pl.BlockSpec((ts, width), lambda i: (i, cb)), pl.BlockSpec((1, width), lambda i: (0, 0)), blk]
        + ([blk] if has_res else []),
        out_specs=(blk, pl.BlockSpec((1, width), lambda i: (0, 0))), compiler_params=_params("arbitrary"),
    )(x, g.reshape(1, width), dy, *([res] if has_res else []))
    return dx, dg.reshape(width)


def headnorm_fwd(x, g, *, col, heads, name, ts=2048):
    S = x.shape[0]
    ts, cb = _tile(S, ts), col // LANE

    def body(x_ref, g_ref, o_ref):
        xf = x_ref[...]
        r = lax.rsqrt(jnp.mean(xf * xf, axis=-1, keepdims=True) + EPS)
        o_ref[0] = (xf * r * g_ref[...]).astype(o_ref.dtype)

    return pl.pallas_call(
        body, name=name, out_shape=jax.ShapeDtypeStruct((heads, S, LANE), BF16), grid=(heads, S // ts),
        in_specs=[pl.BlockSpec((ts, LANE), lambda h, i: (i, cb + h)), pl.BlockSpec((1, LANE), lambda h, i: (0, 0))],
        out_specs=pl.BlockSpec((1, ts, LANE), lambda h, i: (h, i, 0)), compiler_params=_params("parallel", "parallel"),
    )(x, g.reshape(1, LANE))


def headnorm_bwd(x, g, dy, *, col, heads, name, ts=2048):
    S = x.shape[0]
    ts, cb = _tile(S, ts), col // LANE

    def body(x_ref, g_ref, dy_ref, dx_ref, dg_ref):
        xf = x_ref[...]
        r = lax.rsqrt(jnp.mean(xf * xf, axis=-1, keepdims=True) + EPS)
        dyf = dy_ref[0]
        dyg = dyf * g_ref[...]
        dx_ref[...] = r * dyg - xf * (r * r * r) * jnp.mean(dyg * xf, axis=-1, keepdims=True)
        part = jnp.sum(dyf * xf * r, axis=0, keepdims=True)
        first = jnp.logical_and(pl.program_id(0) == 0, pl.program_id(1) == 0)

        @pl.when(first)
        def _():
            dg_ref[...] = part

        @pl.when(jnp.logical_not(first))
        def _():
            dg_ref[...] += part

    dx, dg = pl.pallas_call(
        body, name=name,
        out_shape=(jax.ShapeDtypeStruct((S, heads * LANE), F32), jax.ShapeDtypeStruct((1, LANE), F32)),
        grid=(heads, S // ts),
        in_specs=[pl.BlockSpec((ts, LANE), lambda h, i: (i, cb + h)), pl.BlockSpec((1, LANE), lambda h, i: (0, 0)),
                  pl.BlockSpec((1, ts, LANE), lambda h, i: (h, i, 0))],
        out_specs=(pl.BlockSpec((ts, LANE), lambda h, i: (i, h)), pl.BlockSpec((1, LANE), lambda h, i: (0, 0))),
        compiler_params=_params("arbitrary", "arbitrary"),
    )(x, g.reshape(1, LANE), dy)
    return dx, dg.reshape(LANE)


def _rope_tables(S):
    pos = jnp.arange(S, dtype=F32)
    inv = ROPE_THETA ** (-jnp.arange(0, MLA_ROPE, 2, dtype=F32) / MLA_ROPE)
    ang = pos[:, None] * inv[None, :]
    c, s, z = jnp.cos(ang), jnp.sin(ang), jnp.zeros((S, 64), F32)
    return jnp.concatenate([c, c, z], axis=1), jnp.concatenate([-s, s, z], axis=1)


def _rope(v, cos, ssin, lane):
    partner = jnp.where(lane < 32, pltpu.roll(v, 96, 1), pltpu.roll(v, 32, 1))
    return v * cos + partner * ssin


def mla_prep_fwd(xn, xr, g, cos, ssin, *, n_col, n_stride, r_col, r_stride, heads, name, ts=2048):
    S = xn.shape[0]
    ts = _tile(S, ts)
    nb, ns, rb, rs = n_col // LANE, n_stride // LANE, r_col // LANE, r_stride // LANE
    gn = g[:MLA_NOPE].reshape(1, LANE)
    gr = jnp.concatenate([g[MLA_NOPE:], jnp.zeros((64,), F32)]).reshape(1, LANE)

    def body(n_ref, r_ref, gn_ref, gr_ref, c_ref, s_ref, o_ref):
        n, rr = n_ref[...], r_ref[...]
        ss = jnp.sum(n * n, axis=-1, keepdims=True) + jnp.sum(rr * rr, axis=-1, keepdims=True)
        r = lax.rsqrt(ss * (1.0 / MLA_QK) + EPS)
        lane = lax.broadcasted_iota(jnp.int32, rr.shape, 1)
        o_ref[0, :, :LANE] = (n * r * gn_ref[...]).astype(o_ref.dtype)
        o_ref[0, :, LANE:] = _rope(rr * r * gr_ref[...], c_ref[...], s_ref[...], lane).astype(o_ref.dtype)

    row = lambda h, i: (0, 0)
    return pl.pallas_call(
        body, name=name, out_shape=jax.ShapeDtypeStruct((heads, S, 2 * LANE), BF16), grid=(heads, S // ts),
        in_specs=[pl.BlockSpec((ts, LANE), lambda h, i: (i, nb + ns * h)),
                  pl.BlockSpec((ts, LANE), lambda h, i: (i, rb + rs * h)),
                  pl.BlockSpec((1, LANE), row), pl.BlockSpec((1, LANE), row),
                  pl.BlockSpec((ts, LANE), lambda h, i: (i, 0)), pl.BlockSpec((ts, LANE), lambda h, i: (i, 0))],
        out_specs=pl.BlockSpec((1, ts, 2 * LANE), lambda h, i: (h, i, 0)),
        compiler_params=_params("parallel", "parallel"),
    )(xn, xr, gn, gr, cos, ssin)


def mla_prep_bwd(xn, xr, g, cos, ssin, dy, *, n_col, n_stride, r_col, r_stride, heads, name, ts=2048):
    S = xn.shape[0]
    ts = _tile(S, ts)
    nb, ns, rb, rs = n_col // LANE, n_stride // LANE, r_col // LANE, r_stride // LANE
    shared = r_stride == 0
    gn = g[:MLA_NOPE].reshape(1, LANE)
    gr = jnp.concatenate([g[MLA_NOPE:], jnp.zeros((64,), F32)]).reshape(1, LANE)

    def body(n_ref, r_ref, gn_ref, gr_ref, c_ref, s_ref, dy_ref, dn_ref, dr_ref, dgn_ref, dgr_ref):
        i, h = pl.program_id(0), pl.program_id(1)
        n, rr = n_ref[...], r_ref[...]
        ss = jnp.sum(n * n, axis=-1, keepdims=True) + jnp.sum(rr * rr, axis=-1, keepdims=True)
        r = lax.rsqrt(ss * (1.0 / MLA_QK) + EPS)
        lane = lax.broadcasted_iota(jnp.int32, rr.shape, 1)
        dyn = dy_ref[0, :, :LANE]
        dyr = dy_ref[0, :, LANE:]
        t = dyr * s_ref[...]
        dvr = dyr * c_ref[...] + jnp.where(lane < 32, pltpu.roll(t, 96, 1), pltpu.roll(t, 32, 1))
        dvr = jnp.where(lane < 64, dvr, 0.0)
        dgn_part = jnp.sum(dyn * n * r, axis=0, keepdims=True)
        dgr_part = jnp.sum(dvr * rr * r, axis=0, keepdims=True)
        dyn_g, dvr_g = dyn * gn_ref[...], dvr * gr_ref[...]
        proj = (jnp.sum(dyn_g * n, axis=-1, keepdims=True) + jnp.sum(dvr_g * rr, axis=-1, keepdims=True)) * (1.0 / MLA_QK)
        r3 = r * r * r
        dn_ref[...] = r * dyn_g - n * r3 * proj
        dr = r * dvr_g - rr * r3 * proj
        if shared:
            @pl.when(h == 0)
            def _():
                dr_ref[...] = dr

            @pl.when(h > 0)
            def _():
                dr_ref[...] += dr
        else:
            dr_ref[...] = dr
        first = jnp.logical_and(i == 0, h == 0)

        @pl.when(first)
        def _():
            dgn_ref[...] = dgn_part
            dgr_ref[...] = dgr_part

        @pl.when(jnp.logical_not(first))
        def _():
            dgn_ref[...] += dgn_part
            dgr_ref[...] += dgr_part

    row = lambda i, h: (0, 0)
    dr_cols = LANE if shared else heads * LANE
    dn, dr, dgn, dgr = pl.pallas_call(
        body, name=name,
        out_shape=(jax.ShapeDtypeStruct((S, heads * LANE), F32), jax.ShapeDtypeStruct((S, dr_cols), F32),
                   jax.ShapeDtypeStruct((1, LANE), F32), jax.ShapeDtypeStruct((1, LANE), F32)),
        grid=(S // ts, heads),
        in_specs=[pl.BlockSpec((ts, LANE), lambda i, h: (i, nb + ns * h)),
                  pl.BlockSpec((ts, LANE), lambda i, h: (i, rb + rs * h)),
                  pl.BlockSpec((1, LANE), row), pl.BlockSpec((1, LANE), row),
                  pl.BlockSpec((ts, LANE), lambda i, h: (i, 0)), pl.BlockSpec((ts, LANE), lambda i, h: (i, 0)),
                  pl.BlockSpec((1, ts, 2 * LANE), lambda i, h: (h, i, 0))],
        out_specs=(pl.BlockSpec((ts, LANE), lambda i, h: (i, h)),
                   pl.BlockSpec((ts, LANE), (lambda i, h: (i, 0)) if shared else (lambda i, h: (i, h))),
                   pl.BlockSpec((1, LANE), row), pl.BlockSpec((1, LANE), row)),
        compiler_params=_params("arbitrary", "arbitrary"),
    )(xn, xr, gn, gr, cos, ssin, dy)
    return dn, dr, jnp.concatenate([dgn.reshape(LANE), dgr.reshape(LANE)[:MLA_ROPE]])


_NT = (((1,), (1,)), ((), ()))
_TN = (((0,), (0,)), ((), ()))


def attn_fwd(q, k, v, *, v_col, scale, name, bq=256):
    H, S, dk = q.shape
    Sk = k.shape[1]
    bq, vb = _tile(S, bq), v_col // LANE

    def body(q_ref, k_ref, v_ref, o_ref):
        s = lax.dot_general(q_ref[0], k_ref[0], _NT, preferred_element_type=F32) * scale
        e = jnp.exp(s - jnp.max(s, axis=-1, keepdims=True))
        p = e * (1.0 / jnp.sum(e, axis=-1, keepdims=True))
        o_ref[...] = jnp.dot(p.astype(BF16), v_ref[...].astype(BF16), preferred_element_type=F32).astype(o_ref.dtype)

    return pl.pallas_call(
        body, name=name, out_shape=jax.ShapeDtypeStruct((S, H * LANE), BF16), grid=(H, S // bq),
        in_specs=[pl.BlockSpec((1, bq, dk), lambda h, i: (h, i, 0)), pl.BlockSpec((1, Sk, dk), lambda h, i: (h, 0, 0)),
                  pl.BlockSpec((Sk, LANE), lambda h, i: (0, vb + h))],
        out_specs=pl.BlockSpec((bq, LANE), lambda h, i: (i, h)), compiler_params=_params("parallel", "parallel"),
    )(q, k, v)


def attn_bwd(q, k, v, do, *, v_col, scale, name, bq=256):
    H, S, dk = q.shape
    Sk = k.shape[1]
    bq, vb = _tile(S, bq), v_col // LANE

    def body(q_ref, k_ref, v_ref, do_ref, dq_ref, dk_ref, dv_ref):
        i = pl.program_id(1)
        qb, kb, vv = q_ref[0], k_ref[0], v_ref[...].astype(BF16)
        s = lax.dot_general(qb, kb, _NT, preferred_element_type=F32) * scale
        e = jnp.exp(s - jnp.max(s, axis=-1, keepdims=True))
        p = e * (1.0 / jnp.sum(e, axis=-1, keepdims=True))
        dob = do_ref[...].astype(BF16)
        dv_part = lax.dot_general(p.astype(BF16), dob, _TN, preferred_element_type=F32)
        dp = lax.dot_general(dob, vv, _NT, preferred_element_type=F32)
        ds = p * (dp - jnp.sum(p * dp, axis=-1, keepdims=True))
        dsb = (ds * scale).astype(BF16)
        dq_ref[0] = jnp.dot(dsb, kb, preferred_element_type=F32)
        dk_part = lax.dot_general(dsb, qb, _TN, preferred_element_type=F32)

        @pl.when(i == 0)
        def _():
            dk_ref[0] = dk_part
            dv_ref[...] = dv_part

        @pl.when(i > 0)
        def _():
            dk_ref[0] += dk_part
            dv_ref[...] += dv_part

    return pl.pallas_call(
        body, name=name,
        out_shape=(jax.ShapeDtypeStruct((H, S, dk), F32), jax.ShapeDtypeStruct((H, Sk, dk), F32),
                   jax.ShapeDtypeStruct((Sk, H * LANE), F32)),
        grid=(H, S // bq),
        in_specs=[pl.BlockSpec((1, bq, dk), lambda h, i: (h, i, 0)), pl.BlockSpec((1, Sk, dk), lambda h, i: (h, 0, 0)),
                  pl.BlockSpec((Sk, LANE), lambda h, i: (0, vb + h)), pl.BlockSpec((bq, LANE), lambda h, i: (i, h))],
        out_specs=(pl.BlockSpec((1, bq, dk), lambda h, i: (h, i, 0)), pl.BlockSpec((1, Sk, dk), lambda h, i: (h, 0, 0)),
                   pl.BlockSpec((Sk, LANE), lambda h, i: (0, h))),
        compiler_params=_params("parallel", "arbitrary"),
    )(q, k, v, do)


def _causal_scores(q, kblk, i, start, blk, scale, chunked, cq, ckblk):
    s = lax.dot_general(q, kblk, _NT, preferred_element_type=F32) * scale
    if cq is not None:
        s = s + cq - ckblk
    qpos = i * blk + lax.broadcasted_iota(jnp.int32, s.shape, 0)
    kpos = start + lax.broadcasted_iota(jnp.int32, s.shape, 1)
    ok = (kpos >> 6) <= (qpos >> 6) if chunked else kpos <= qpos
    return jnp.where(ok, s, NEG)


def causal_attn_fwd(q, k, v, *, v_col, chunked, scale, cq=None, ck=None, name, blk=256):
    H, S, dk = q.shape
    blk, vb = _tile(S, blk), v_col // LANE
    fox = cq is not None

    def body(q_ref, k_ref, v_ref, *rest):
        o_ref, lse_ref = rest[-2:]
        s = _causal_scores(q_ref[0], k_ref[0], pl.program_id(1), 0, blk, scale, chunked,
                           rest[0][0] if fox else None, rest[1][0] if fox else None)
        m = jnp.max(s, axis=-1, keepdims=True)
        p = jnp.exp(s - m)
        l = jnp.sum(p, axis=-1, keepdims=True)
        pv = jnp.dot(p.astype(BF16), v_ref[...].astype(BF16), preferred_element_type=F32)
        o_ref[...] = (pv * (1.0 / l)).astype(o_ref.dtype)
        lse_ref[0] = m + jnp.log(l)

    in_specs = [pl.BlockSpec((1, blk, dk), lambda h, i: (h, i, 0)), pl.BlockSpec((1, S, dk), lambda h, i: (h, 0, 0)),
                pl.BlockSpec((S, LANE), lambda h, i: (0, vb + h))]
    args = [q, k, v]
    if fox:
        in_specs += [pl.BlockSpec((1, blk, 1), lambda h, i: (h, i, 0)), pl.BlockSpec((1, 1, S), lambda h, i: (h, 0, 0))]
        args += [cq, ck]
    return pl.pallas_call(
        body, name=name, out_shape=(jax.ShapeDtypeStruct((S, H * LANE), BF16), jax.ShapeDtypeStruct((H, S, 1), F32)),
        grid=(H, S // blk), in_specs=in_specs,
        out_specs=(pl.BlockSpec((blk, LANE), lambda h, i: (i, h)), pl.BlockSpec((1, blk, 1), lambda h, i: (h, i, 0))),
        compiler_params=_params("parallel", "parallel"),
    )(*args)


def causal_attn_bwd(q, k, v, o, do, lse, *, v_col, chunked, scale, cq=None, ck=None, name, blk=256):
    H, S, dk = q.shape
    blk, vb = _tile(S, blk), v_col // LANE
    kc = 2 * blk if S % (2 * blk) == 0 else blk
    fox = cq is not None

    def body(q_ref, k_ref, v_ref, o_ref, do_ref, lse_ref, *rest):
        i = pl.program_id(1)
        if fox:
            cq_ref, ck_ref, dq_ref, dk_ref, dv_ref, dcq_ref, dck_ref = rest
        else:
            dq_ref, dk_ref, dv_ref = rest

        @pl.when(i == 0)
        def _():
            dk_ref[...] = jnp.zeros_like(dk_ref)
            dv_ref[...] = jnp.zeros_like(dv_ref)
            if fox:
                dck_ref[...] = jnp.zeros_like(dck_ref)

        qb, dob, lse_b = q_ref[0], do_ref[...].astype(BF16), lse_ref[0]
        delta = jnp.sum(do_ref[...].astype(F32) * o_ref[...].astype(F32), axis=-1, keepdims=True)
        dq_ref[...] = jnp.zeros_like(dq_ref)
        if fox:
            dcq_ref[...] = jnp.zeros_like(dcq_ref)
        for c in range(S // kc):
            @pl.when(c * kc < (i + 1) * blk)
            def _(ks=slice(c * kc, (c + 1) * kc), start=c * kc):
                kblk = k_ref[0, ks, :]
                s = _causal_scores(qb, kblk, i, start, blk, scale, chunked,
                                   cq_ref[0] if fox else None, ck_ref[0, :, ks] if fox else None)
                p = jnp.exp(s - lse_b)
                dv_ref[ks, :] += lax.dot_general(p.astype(BF16), dob, _TN, preferred_element_type=F32)
                dp = lax.dot_general(dob, v_ref[ks, :].astype(BF16), _NT, preferred_element_type=F32)
                ds = p * (dp - delta)
                dsb = (ds * scale).astype(BF16)
                dq_ref[0] += jnp.dot(dsb, kblk, preferred_element_type=F32)
                dk_ref[0, ks, :] += lax.dot_general(dsb, qb, _TN, preferred_element_type=F32)
                if fox:
                    dcq_ref[0] += jnp.sum(ds, axis=-1, keepdims=True)
                    dck_ref[0, :, ks] += -jnp.sum(ds, axis=0, keepdims=True)

    row = pl.BlockSpec((blk, LANE), lambda h, i: (i, h))
    in_specs = [pl.BlockSpec((1, blk, dk), lambda h, i: (h, i, 0)), pl.BlockSpec((1, S, dk), lambda h, i: (h, 0, 0)),
                pl.BlockSpec((S, LANE), lambda h, i: (0, vb + h)), row, row, pl.BlockSpec((1, blk, 1), lambda h, i: (h, i, 0))]
    args = [q, k, v, o, do, lse]
    out_shape = [jax.ShapeDtypeStruct((H, S, dk), F32), jax.ShapeDtypeStruct((H, S, dk), F32),
                 jax.ShapeDtypeStruct((S, H * LANE), F32)]
    out_specs = [pl.BlockSpec((1, blk, dk), lambda h, i: (h, i, 0)), pl.BlockSpec((1, S, dk), lambda h, i: (h, 0, 0)),
                 pl.BlockSpec((S, LANE), lambda h, i: (0, h))]
    if fox:
        fox_specs = [pl.BlockSpec((1, blk, 1), lambda h, i: (h, i, 0)), pl.BlockSpec((1, 1, S), lambda h, i: (h, 0, 0))]
        in_specs += fox_specs
        args += [cq, ck]
        out_shape += [jax.ShapeDtypeStruct((H, S, 1), F32), jax.ShapeDtypeStruct((H, 1, S), F32)]
        out_specs += fox_specs
    return pl.pallas_call(
        body, name=name, out_shape=tuple(out_shape), grid=(H, S // blk), in_specs=in_specs, out_specs=tuple(out_specs),
        compiler_params=_params("parallel", "arbitrary"),
    )(*args)


CPB = 4
BANDW = BAND + CHUNK
WIN = BAND + (CPB - 1) * CHUNK


def chunk_band_dbias(db):
    H = db.shape[0]
    d4 = db.reshape(H, CPB, CHUNK, WIN)
    return sum(d4[:, c, :, CHUNK * c:CHUNK * c + BAND] for c in range(CPB))


def _band_probs(qb, kb, bias, start, scale):
    s = lax.dot_general(qb, kb, _NT, preferred_element_type=F32) * scale
    real = start + lax.broadcasted_iota(jnp.int32, s.shape, 1) >= PAD
    s = jnp.where(real, s + bias, NEG)
    e = jnp.exp(s - jnp.max(s, axis=-1, keepdims=True))
    return e * (1.0 / jnp.sum(e, axis=-1, keepdims=True))


def band_fwd(q, kp, vp, bias, *, scale, name):
    H, S, _ = q.shape
    Sp, rows = S + PAD + CHUNK, CPB * CHUNK

    def body(q_ref, k_ref, v_ref, b_ref, o_ref):
        start = pl.multiple_of(pl.program_id(1) * rows, rows)
        p = _band_probs(q_ref[0], k_ref[0, pl.ds(start, WIN), :], b_ref[0], start, scale)
        vb = v_ref[pl.ds(start, WIN), :].astype(BF16)
        o_ref[...] = jnp.dot(p.astype(BF16), vb, preferred_element_type=F32).astype(o_ref.dtype)

    return pl.pallas_call(
        body, name=name, out_shape=jax.ShapeDtypeStruct((S, H * LANE), BF16), grid=(H, S // rows),
        in_specs=[pl.BlockSpec((1, rows, LANE), lambda h, j: (h, j, 0)), pl.BlockSpec((1, Sp, LANE), lambda h, j: (h, 0, 0)),
                  pl.BlockSpec((Sp, LANE), lambda h, j: (0, h)), pl.BlockSpec((1, rows, WIN), lambda h, j: (h, 0, 0))],
        out_specs=pl.BlockSpec((rows, LANE), lambda h, j: (j, h)), compiler_params=_params("parallel", "parallel"),
    )(q, kp, vp, bias)


def band_bwd(q, kp, vp, bias, do, *, scale, name):
    H, S, _ = q.shape
    Sp, rows = S + PAD + CHUNK, CPB * CHUNK

    def body(q_ref, k_ref, v_ref, b_ref, do_ref, dq_ref, dk_ref, dv_ref, db_ref):
        j = pl.program_id(1)

        @pl.when(j == 0)
        def _():
            dk_ref[...] = jnp.zeros_like(dk_ref)
            dv_ref[...] = jnp.zeros_like(dv_ref)
            db_ref[...] = jnp.zeros_like(db_ref)

        start = pl.multiple_of(j * rows, rows)
        qb = q_ref[0]
        kb = k_ref[0, pl.ds(start, WIN), :]
        vb = v_ref[pl.ds(start, WIN), :].astype(BF16)
        p = _band_probs(qb, kb, b_ref[0], start, scale)
        dob = do_ref[...].astype(BF16)
        dv_ref[pl.ds(start, WIN), :] += lax.dot_general(p.astype(BF16), dob, _TN, preferred_element_type=F32)
        dp = lax.dot_general(dob, vb, _NT, preferred_element_type=F32)
        ds = p * (dp - jnp.sum(p * dp, axis=-1, keepdims=True))
        db_ref[0] += ds
        dsb = (ds * scale).astype(BF16)
        dq_ref[0] = jnp.dot(dsb, kb, preferred_element_type=F32)
        dk_ref[0, pl.ds(start, WIN), :] += lax.dot_general(dsb, qb, _TN, preferred_element_type=F32)

    blk_b = pl.BlockSpec((1, rows, WIN), lambda h, j: (h, 0, 0))
    return pl.pallas_call(
        body, name=name,
        out_shape=(jax.ShapeDtypeStruct((H, S, LANE), F32), jax.ShapeDtypeStruct((H, Sp, LANE), F32),
                   jax.ShapeDtypeStruct((Sp, H * LANE), F32), jax.ShapeDtypeStruct((H, rows, WIN), F32)),
        grid=(H, S // rows),
        in_specs=[pl.BlockSpec((1, rows, LANE), lambda h, j: (h, j, 0)), pl.BlockSpec((1, Sp, LANE), lambda h, j: (h, 0, 0)),
                  pl.BlockSpec((Sp, LANE), lambda h, j: (0, h)), blk_b, pl.BlockSpec((rows, LANE), lambda h, j: (j, h))],
        out_specs=(pl.BlockSpec((1, rows, LANE), lambda h, j: (h, j, 0)), pl.BlockSpec((1, Sp, LANE), lambda h, j: (h, 0, 0)),
                   pl.BlockSpec((Sp, LANE), lambda h, j: (0, h)), blk_b),
        compiler_params=_params("parallel", "arbitrary"),
    )(q, kp, vp, bias, do)


def band_bias(rel_bias, *, name):
    H, rows, wide = rel_bias.shape[0], CPB * CHUNK, 1024
    last = rel_bias[:, 2 * REL_CLIP:]
    row0 = jnp.concatenate([jnp.tile(last, (1, PAD - REL_CLIP)), rel_bias[:, CHUNK + 1:][:, ::-1],
                            jnp.tile(last, (1, wide - BAND))], axis=1)

    def body(r_ref, o_ref):
        skew = pltpu.roll(jnp.broadcast_to(r_ref[0], (rows, wide)), 0, 1, stride=1, stride_axis=0)[:, :WIN]
        first = (lax.broadcasted_iota(jnp.int32, (rows, WIN), 0) >> 6) * CHUNK
        col = lax.broadcasted_iota(jnp.int32, (rows, WIN), 1)
        o_ref[0] = jnp.where(jnp.logical_and(col >= first, col < first + BAND), skew, NEG)

    return pl.pallas_call(
        body, name=name, out_shape=jax.ShapeDtypeStruct((H, rows, WIN), F32), grid=(H,),
        in_specs=[pl.BlockSpec((1, 1, wide), lambda h: (h, 0, 0))], out_specs=pl.BlockSpec((1, rows, WIN), lambda h: (h, 0, 0)),
        compiler_params=_params("parallel"),
    )(row0.reshape(H, 1, wide))


def relbias_bwd(dbias, *, name):
    H, W = dbias.shape[0], BANDW
    x = jnp.pad(dbias[:, :, ::-1], ((0, 0), (0, 0), (0, CHUNK)))

    def body(x_ref, o_ref):
        skew = pltpu.roll(x_ref[0], 0, 1, stride=1, stride_axis=0)
        f = jnp.broadcast_to(jnp.sum(skew, axis=0, keepdims=True), (8, W))
        lane = lax.broadcasted_iota(jnp.int32, (8, W), 1)
        direct = jnp.where(jnp.logical_and(lane >= 65, lane <= 255), pltpu.roll(f, 65, 1), 0.0)
        tail = jnp.sum(jnp.where(lane >= 191, f, 0.0), axis=-1, keepdims=True)
        o_ref[0] = direct + jnp.where(lane == 2 * REL_CLIP, tail, 0.0)

    out = pl.pallas_call(
        body, name=name, out_shape=jax.ShapeDtypeStruct((H, 8, W), F32), grid=(H,),
        in_specs=[pl.BlockSpec((1, CHUNK, W), lambda h: (h, 0, 0))], out_specs=pl.BlockSpec((1, 8, W), lambda h: (h, 0, 0)),
        compiler_params=_params("parallel"),
    )(x)
    return out[:, 0, :2 * REL_CLIP + 1]


def _split_dot(x, u, dn):
    hi = x.astype(BF16)
    r1 = x - hi.astype(F32)
    mid = r1.astype(BF16)
    lo = (r1 - mid.astype(F32)).astype(BF16)
    d = lambda t: lax.dot_general(t, u, dn, preferred_element_type=F32)
    return d(hi) + d(mid) + d(lo)


def _upper_ones(S):
    return (np.arange(S)[:, None] <= np.arange(S)[None, :]).astype(np.float32)


def foxgate_fwd(fl, b, *, name):
    H, S = fl.shape
    u = jnp.asarray(_upper_ones(S), BF16)

    def body(f_ref, b_ref, u_ref, o_ref):
        x = f_ref[...] + b_ref[...]
        lf = jnp.minimum(x, 0.0) - jnp.log(1.0 + jnp.exp(-jnp.abs(x)))
        o_ref[...] = _split_dot(lf, u_ref[...], (((1,), (0,)), ((), ())))

    return pl.pallas_call(body, name=name, out_shape=jax.ShapeDtypeStruct((H, S), F32),
                          compiler_params=pltpu.CompilerParams(vmem_limit_bytes=VMEM_LIMIT_BYTES))(fl, b.reshape(H, 1), u)


def foxgate_bwd(fl, b, dcum, *, name):
    H, S = fl.shape
    u = jnp.asarray(_upper_ones(S), BF16)

    def body(f_ref, b_ref, u_ref, dc_ref, df_ref, db_ref):
        x = f_ref[...] + b_ref[...]
        dlf = _split_dot(dc_ref[...], u_ref[...], _NT)
        df = dlf * (1.0 / (1.0 + jnp.exp(x)))
        df_ref[...] = df
        db_ref[...] = jnp.sum(df, axis=-1, keepdims=True)

    df, db = pl.pallas_call(body, name=name,
                            out_shape=(jax.ShapeDtypeStruct((H, S), F32), jax.ShapeDtypeStruct((H, 1), F32)),
                            compiler_params=pltpu.CompilerParams(vmem_limit_bytes=VMEM_LIMIT_BYTES))(fl, b.reshape(H, 1), u, dcum)
    return df, db.reshape(H)


def gate_fwd(z, proj, *, name, ts=256, tc=512):
    S, D = proj[0].shape
    ts, gb, nb = _tile(S, ts), Z_GATE // tc, D // tc

    def body(g0, g1, g2, p0, p1, p2, o_ref):
        acc = None
        for g_ref, p_ref in zip((g0, g1, g2), (p0, p1, p2)):
            t = (1.0 / (1.0 + jnp.exp(-g_ref[...]))) * p_ref[...]
            acc = t if acc is None else acc + t
        o_ref[...] = acc.astype(o_ref.dtype)

    blk = pl.BlockSpec((ts, tc), lambda i, j: (i, j))
    return pl.pallas_call(
        body, name=name, out_shape=jax.ShapeDtypeStruct((S, D), BF16), grid=(S // ts, nb),
        in_specs=[pl.BlockSpec((ts, tc), lambda i, j, n=n: (i, gb + n * nb + j)) for n in range(3)] + [blk] * 3,
        out_specs=blk, compiler_params=_params("parallel", "parallel"),
    )(z, z, z, *proj)


def gate_bwd(z, proj, dm, *, name, ts=256, tc=512):
    S, D = proj[0].shape
    ts, gb, nb = _tile(S, ts), Z_GATE // tc, D // tc

    def body(g0, g1, g2, p0, p1, p2, dm_ref, *outs):
        dmv = dm_ref[...]
        for n, (g_ref, p_ref) in enumerate(zip((g0, g1, g2), (p0, p1, p2))):
            sg = 1.0 / (1.0 + jnp.exp(-g_ref[...]))
            outs[n][...] = (dmv * sg).astype(BF16)
            outs[3 + n][...] = (dmv * p_ref[...] * sg * (1.0 - sg)).astype(BF16)

    blk = pl.BlockSpec((ts, tc), lambda i, j: (i, j))
    outs = pl.pallas_call(
        body, name=name, out_shape=tuple(jax.ShapeDtypeStruct((S, D), BF16) for _ in range(6)), grid=(S // ts, nb),
        in_specs=[pl.BlockSpec((ts, tc), lambda i, j, n=n: (i, gb + n * nb + j)) for n in range(3)] + [blk] * 4,
        out_specs=(blk,) * 6, compiler_params=_params("parallel", "parallel"),
    )(z, z, z, *proj, dm)
    return outs[:3], outs[3:]


def loss_head(y, target, *, name, ts=256):
    S, D = y.shape
    ts = _tile(S, ts)

    def body(y_ref, t_ref, l_ref, dy_ref):
        err = y_ref[...] - t_ref[...]
        dy_ref[...] = err * (1.0 / D)
        part = 0.5 * jnp.sum(jnp.mean(err * err, axis=-1, keepdims=True), axis=0, keepdims=True)

        @pl.when(pl.program_id(0) == 0)
        def _():
            l_ref[...] = part

        @pl.when(pl.program_id(0) > 0)
        def _():
            l_ref[...] += part

    blk = pl.BlockSpec((ts, D), lambda i: (i, 0))
    return pl.pallas_call(
        body, name=name, out_shape=(jax.ShapeDtypeStruct((1, 1), F32), jax.ShapeDtypeStruct((S, D), F32)), grid=(S // ts,),
        in_specs=[blk, blk], out_specs=(pl.BlockSpec((1, 1), lambda i: (0, 0)), blk), compiler_params=_params("arbitrary"),
    )(y, target)


def adamw(w, g, m, v, *, name):
    shape = w.shape
    C = shape[-1]
    R = int(np.prod(shape[:-1]))
    br = R
    while br % 16 == 0 and br * C * 4 > 2**20:
        br //= 2
    w2, g2, m2, v2 = (t.reshape(R, C) for t in (w, g, m, v))

    def body(w_ref, g_ref, m_ref, v_ref, d_ref, nm_ref, nv_ref):
        d_ref[...], nm_ref[...], nv_ref[...] = _adamw_update(w_ref[...], g_ref[...], m_ref[...], v_ref[...])

    blk = pl.BlockSpec((br, C), lambda i: (i, 0))
    outs = pl.pallas_call(
        body, name=name, out_shape=tuple(jax.ShapeDtypeStruct((R, C), F32) for _ in range(3)), grid=(R // br,),
        in_specs=[blk] * 4, out_specs=(blk,) * 3, compiler_params=_params("parallel"),
    )(w2, g2, m2, v2)
    return tuple(o.reshape(shape) for o in outs)


_ANY = pl.BlockSpec(memory_space=pl.ANY)


def _place():
    return lax.axis_index("x"), lax.axis_index("y"), lax.axis_index("c")


def all_gather(xs, after, *, name):
    n = len(xs)

    def body(*refs):
        x_refs, o_refs = refs[:n], refs[n + 1:2 * n + 1]
        send_sems, recv_sems, local_sems = refs[2 * n + 1:]
        px, py, pc = _place()
        me, sibling = (px, py, pc), (px, py, 1 - pc)
        chips = [(1 - px, py), (px, 1 - py), (1 - px, 1 - py)]

        def slot(t, dev):
            return o_refs[t].at[4 * dev[0] + 2 * dev[1] + dev[2]]

        def copy(t, k, block, to, src=None):
            return pltpu.make_async_remote_copy(
                src_ref=slot(t, block) if src is None else src, dst_ref=slot(t, block),
                send_sem=send_sems.at[t, k], recv_sem=recv_sems.at[t, k], device_id=to, device_id_type=MESH)

        mine = [pltpu.make_async_copy(x_refs[t], slot(t, me), local_sems.at[t]) for t in range(n)]
        first = []
        for t in range(n):
            mine[t].start()
            first += [copy(t, 1 + j, me, (*chip, pc), src=x_refs[t]) for j, chip in enumerate(chips)]
            first.append(copy(t, 0, me, sibling, src=x_refs[t]))
        for cp in first:
            cp.start()
        passed = []
        for t in range(n):
            for j, chip in enumerate(chips):
                copy(t, 1 + j, (*chip, pc), me).wait_recv()
                fwd = copy(t, 4 + j, (*chip, pc), sibling)
                fwd.start()
                passed.append(fwd)
        for t in range(n):
            copy(t, 0, sibling, me).wait_recv()
            for j, chip in enumerate(chips):
                copy(t, 4 + j, (*chip, 1 - pc), me).wait_recv()
        for cp in first + passed:
            cp.wait_send()
        for cp in mine:
            cp.wait()

    return pl.pallas_call(
        body, name=name, out_shape=tuple(jax.ShapeDtypeStruct((8,) + x.shape, x.dtype) for x in xs),
        in_specs=[_ANY] * (n + 1), out_specs=(_ANY,) * n,
        scratch_shapes=[pltpu.SemaphoreType.DMA((n, 7)), pltpu.SemaphoreType.DMA((n, 7)), pltpu.SemaphoreType.DMA((n,))],
    )(*xs, after)


def forward_to_sibling(zones, *, name):
    n = len(zones)

    def body(*refs):
        z_refs, (send_sems, recv_sems) = refs[n:2 * n], refs[2 * n:]
        px, py, pc = _place()
        copies = []
        for t in range(n):
            for j, chip in enumerate([(1 - px, py), (px, 1 - py), (1 - px, 1 - py)]):
                slot = z_refs[t].at[4 * chip[0] + 2 * chip[1] + pc]
                copies.append(pltpu.make_async_remote_copy(
                    src_ref=slot, dst_ref=slot, send_sem=send_sems.at[t, j], recv_sem=recv_sems.at[t, j],
                    device_id=(px, py, 1 - pc), device_id_type=MESH))
        for cp in copies:
            cp.start()
        for cp in copies:
            cp.wait()

    return pl.pallas_call(
        body, name=name, out_shape=tuple(jax.ShapeDtypeStruct(z.shape, z.dtype) for z in zones),
        in_specs=[_ANY] * n, out_specs=(_ANY,) * n, input_output_aliases={t: t for t in range(n)},
        scratch_shapes=[pltpu.SemaphoreType.DMA((n, 3)), pltpu.SemaphoreType.DMA((n, 3))],
    )(*zones)


_HBM = pl.BlockSpec(memory_space=pltpu.HBM)
_SEM = pl.BlockSpec(memory_space=pltpu.SEMAPHORE)
_EFFECT = pltpu.SideEffectType.DATAFLOW_SIDE_EFFECTING


N_PEERS = 7
GATHER_FLIPS = (1, 2, 4, 6)


def _peers(flips=range(1, N_PEERS + 1)):
    px, py, pc = _place()
    flip = lambda p, bit: 1 - p if bit else p
    return [(flip(px, m >> 2 & 1), flip(py, m >> 1 & 1), flip(pc, m & 1)) for m in flips]


def _gather_copies(src_refs, zone_refs, send_sems, recv_sems):
    px, py, pc = _place()
    return [pltpu.make_async_remote_copy(src_ref=s, dst_ref=z.at[4 * px + 2 * py + pc], send_sem=send_sems.at[k],
                                         recv_sem=recv_sems.at[k], device_id=peer, device_id_type=MESH)
            for k, peer in enumerate(_peers(GATHER_FLIPS)) for s, z in zip(src_refs, zone_refs)]


def _scatter_copies(part_refs, zone_refs, send_sems, recv_sems):
    return [pltpu.make_async_remote_copy(src_ref=p.at[4 * peer[0] + 2 * peer[1] + peer[2]], dst_ref=z.at[k],
                                         send_sem=send_sems.at[k], recv_sem=recv_sems.at[k], device_id=peer, device_id_type=MESH)
            for k, peer in enumerate(_peers()) for p, z in zip(part_refs, zone_refs)]


def copies_start(make, srcs, zones, *, name, after=None):
    n = len(srcs)
    extra = [] if after is None else [after]

    def body(*refs):
        k = 2 * n + len(extra)
        for cp in make(refs[:n], refs[n:2 * n], refs[k], refs[k + 1]):
            cp.start()
        refs[-1][...] = jnp.zeros_like(refs[-1])

    arrays = list(srcs) + list(zones)
    outs = pl.pallas_call(
        body, name=name,
        out_shape=(pltpu.SemaphoreType.DMA((N_PEERS,)), pltpu.SemaphoreType.DMA((N_PEERS,)),
                   *[pltpu.HBM(a.shape, a.dtype) for a in arrays], jax.ShapeDtypeStruct((8, LANE), F32)),
        in_specs=[_HBM] * (2 * n) + [_ANY] * len(extra),
        out_specs=(_SEM, _SEM, *[_HBM] * (2 * n), pl.BlockSpec(memory_space=pltpu.VMEM)),
        input_output_aliases={i: 2 + i for i in range(2 * n)},
        compiler_params=pltpu.CompilerParams(has_side_effects=_EFFECT),
    )(*[pltpu.with_memory_space_constraint(a, pltpu.HBM) for a in arrays], *extra)
    return outs[0], outs[1], list(outs[2:2 + n]), list(outs[2 + n:2 + 2 * n]), outs[-1]


def copies_wait(make, send_sems, recv_sems, srcs, zones, after, *, name):
    n = len(srcs)

    def body(*refs):
        for cp in make(refs[:n], refs[n:2 * n], refs[2 * n], refs[2 * n + 1]):
            cp.wait_send()
            cp.wait_recv()

    arrays = list(srcs) + list(zones)
    outs = pl.pallas_call(
        body, name=name, out_shape=tuple(pltpu.HBM(a.shape, a.dtype) for a in arrays),
        in_specs=[_HBM] * (2 * n) + [_SEM, _SEM, _ANY], out_specs=(_HBM,) * (2 * n),
        input_output_aliases={i: i for i in range(2 * n)},
        compiler_params=pltpu.CompilerParams(has_side_effects=_EFFECT),
    )(*arrays, send_sems, recv_sems, after)
    return list(outs[:n]), list(outs[n:])


def place_mine(zone, mine, *, name):
    C = mine.shape[-1]
    R = int(np.prod(mine.shape[:-1]))
    br = _row_block(R, C, mine.dtype.itemsize, budget=2**21)
    me = (4 * lax.axis_index("x") + 2 * lax.axis_index("y") + lax.axis_index("c")).astype(jnp.int32).reshape(1)

    def body(me_ref, m_ref, z_ref, o_ref):
        o_ref[0] = m_ref[...]

    out = pl.pallas_call(
        body, name=name, out_shape=jax.ShapeDtypeStruct((8, R, C), zone.dtype),
        grid_spec=pltpu.PrefetchScalarGridSpec(
            num_scalar_prefetch=1, grid=(R // br,), in_specs=[pl.BlockSpec((br, C), lambda i, me: (i, 0)), _ANY],
            out_specs=pl.BlockSpec((1, br, C), lambda i, me: (me[0], i, 0))),
        input_output_aliases={2: 0}, compiler_params=_params("parallel"),
    )(me, mine.reshape(R, C), zone.reshape(8, R, C))
    return out.reshape(zone.shape)


def _row_block(rows, cols, itemsize, budget=2**20):
    br = rows
    while br % 32 == 0 and br * cols * itemsize > budget:
        br //= 2
    return br


def _adamw_update(w, g, m, v):
    nm = ADAM_B1 * m + (1.0 - ADAM_B1) * g
    nv = ADAM_B2 * v + (1.0 - ADAM_B2) * (g * g)
    m_hat = nm / (1.0 - ADAM_B1 ** ADAM_STEP)
    v_hat = nv / (1.0 - ADAM_B2 ** ADAM_STEP)
    return -ADAM_LR * (m_hat / (jnp.sqrt(v_hat) + ADAM_EPS) + ADAM_WD * w), nm, nv


def grad_sum_adamw(parts, recvs, w, m, v, *, name):
    L, C = len(parts), w.shape[-1]
    R = int(np.prod(w.shape[1:-1]))
    br = _row_block(R, C, 4, budget=2**19)
    chip = (4 * lax.axis_index("x") + 2 * lax.axis_index("y") + lax.axis_index("c")).astype(jnp.int32).reshape(1)

    def body(c_ref, *refs):
        p_refs, r_refs = refs[:L], refs[L:2 * L]
        w_ref, m_ref, v_ref, g_out, d_out, nm_out, nv_out = refs[2 * L:]
        for j in range(L):
            @pl.when(pl.program_id(0) == j)
            def _(j=j):
                g = p_refs[j][0].astype(F32)
                for k in range(N_PEERS):
                    g = g + r_refs[j][k].astype(F32)
                g_out[0] = g
                d_out[0], nm_out[0], nv_out[0] = _adamw_update(w_ref[0], g, m_ref[0], v_ref[0])

    row = lambda j: (lambda l, r, c: jnp.where(l == j, r, 0))
    part_specs = [pl.BlockSpec((1, br, C), lambda l, r, c, f=row(j): (c[0], f(l, r, c), 0)) for j in range(L)]
    recv_specs = [pl.BlockSpec((N_PEERS, br, C), lambda l, r, c, f=row(j): (0, f(l, r, c), 0)) for j in range(L)]
    blk = pl.BlockSpec((1, br, C), lambda l, r, c: (l, r, 0))
    outs = pl.pallas_call(
        body, name=name, out_shape=tuple(jax.ShapeDtypeStruct((L, R, C), F32) for _ in range(4)),
        grid_spec=pltpu.PrefetchScalarGridSpec(
            num_scalar_prefetch=1, grid=(L, R // br), in_specs=part_specs + recv_specs + [blk] * 3, out_specs=(blk,) * 4),
        compiler_params=_params("arbitrary", "arbitrary"),
    )(chip, *[p.reshape(8, R, C) for p in parts], *[r.reshape(N_PEERS, R, C) for r in recvs],
      *[t.reshape(L, R, C) for t in (w, m, v)])
    return tuple(o.reshape(w.shape) for o in outs)


def ordered_sum(parts, *, name):
    _, R, C = parts.shape

    def body(p_ref, o_ref):
        acc = p_ref[0]
        for d in range(1, 8):
            acc = acc + p_ref[d]
        o_ref[...] = acc

    return pl.pallas_call(body, name=name, out_shape=jax.ShapeDtypeStruct((R, C), F32))(parts)


W_IN_COLS, W_IN_SHARD = 13128, 1641
W_IN_SEGMENTS = ((0, 832, 0), (832, 3904, Z_FOX), (3904, 3912, FF_COL), (3912, 6984, Z_CH), (6984, 13128, Z_GATE))


def col_gather(src, table, pieces, out_shape, *, name, tr=2048, after=None):
    R, C = src.shape[1:]
    tr = _tile(R, tr)
    width = 2 + 6 * pieces
    nb = table.shape[0] // width
    last_tile, last_valid = C // LANE, C % LANE

    extra = [] if after is None else [after]

    def body(tab, *refs):
        o_ref, acc_ref = refs[-2:]
        base = pl.program_id(1) * width
        lane = lax.broadcasted_iota(jnp.int32, (tr, LANE), 1)
        row = lax.broadcasted_iota(jnp.int32, (2 * LANE, LANE), 0)
        col = lax.broadcasted_iota(jnp.int32, (2 * LANE, LANE), 1)
        acc_ref[...] = jnp.zeros_like(acc_ref)
        for p in range(pieces):
            e = base + 2 + 6 * p
            lo, hi = tab[e + 4], tab[e + 5]

            @pl.when(hi > lo)
            def _(p=p, e=e, lo=lo, hi=hi):
                tiles = []
                for tcol in (1, 2):
                    x = refs[2 * p + tcol - 1][0]
                    if last_valid:
                        x = jnp.where(jnp.logical_or(tab[e + tcol] < last_tile, lane < last_valid), x, jnp.zeros_like(x))
                    tiles.append(x)
                hit = jnp.logical_and(row == col + tab[e + 3], jnp.logical_and(col >= lo, col < hi))
                sel = jnp.where(hit, 1.0, 0.0).astype(src.dtype)
                acc_ref[...] += jnp.dot(jnp.concatenate(tiles, axis=1), sel, preferred_element_type=F32)
        o_ref[0] = acc_ref[...].astype(o_ref.dtype)

    in_specs = []
    for p in range(pieces):
        for tcol in (1, 2):
            in_specs.append(pl.BlockSpec(
                (1, tr, LANE), lambda i, b, tab, p=p, tcol=tcol: (tab[b * width + 2 + 6 * p], i, tab[b * width + 2 + 6 * p + tcol])))
    return pl.pallas_call(
        body, name=name, out_shape=jax.ShapeDtypeStruct(out_shape, src.dtype),
        grid_spec=pltpu.PrefetchScalarGridSpec(
            num_scalar_prefetch=1, grid=(R // tr, nb), in_specs=in_specs + [_ANY] * len(extra),
            out_specs=pl.BlockSpec((1, tr, LANE), lambda i, b, tab: (tab[b * width], i, tab[b * width + 1])),
            scratch_shapes=[pltpu.VMEM((tr, LANE), F32)]),
        compiler_params=_params("parallel", "parallel"),
    )(jnp.asarray(table, jnp.int32), *([src] * (2 * pieces)), *extra)


def _piece(sd, start, lo, hi, last_tile):
    t0 = start // LANE
    return [sd, t0, min(t0 + 1, last_tile), start % LANE - lo, lo, hi]


def _pad_pieces(rows, pieces):
    out, prev = [], [0, 0, 0, 0, 0, 0] * pieces
    for head, pcs in rows:
        full = list(pcs)
        for p in range(len(pcs) // 6, pieces):
            full += prev[6 * p:6 * p + 3] + [0, 0, 0]
        out.append(head + full)
        prev = full
    return np.asarray(out, np.int32).reshape(-1)


def _w_in_table(layer, L):
    rows = []
    for b in range(Z_W // LANE):
        pcs = []
        for first, last, col in W_IN_SEGMENTS:
            lo, hi = max(LANE * b, col), min(LANE * (b + 1), col + last - first)
            while lo < hi:
                c = first + lo - col
                n = min(hi - lo, W_IN_SHARD - c % W_IN_SHARD)
                pcs += _piece((c // W_IN_SHARD) * L + layer, c % W_IN_SHARD, lo - LANE * b, lo - LANE * b + n, W_IN_SHARD // LANE)
                lo += n
        assert len(pcs) <= 12
        rows.append(([0, b], pcs))
    return _pad_pieces(rows, 2)


def _w_in_grad_table():
    rows = []
    for d in range(8):
        for t in range(-(-W_IN_SHARD // LANE)):
            pcs = []
            c0 = d * W_IN_SHARD + LANE * t
            c1 = min(c0 + LANE, (d + 1) * W_IN_SHARD)
            for first, last, col in W_IN_SEGMENTS:
                lo, hi = max(c0, first), min(c1, last)
                if lo < hi:
                    pcs += _piece(0, col + lo - first, lo - c0, hi - c0, Z_W // LANE - 1)
            assert len(pcs) <= 18
            rows.append(([d, t], pcs))
    return _pad_pieces(rows, 3)


def block_copy(src, out_shape, in_blk, out_blk, grid, in_map, out_map, *, name):
    def body(x_ref, o_ref):
        o_ref[(0,) * (len(out_blk) - 2) + (Ellipsis,)] = x_ref[(0,) * (len(in_blk) - 2) + (Ellipsis,)]

    return pl.pallas_call(
        body, name=name, out_shape=jax.ShapeDtypeStruct(out_shape, src.dtype), grid=grid,
        in_specs=[pl.BlockSpec(in_blk, in_map)], out_specs=pl.BlockSpec(out_blk, out_map),
        compiler_params=_params("parallel", "parallel"),
    )(src)


def _columns_from_owners(z, *, name, lead=()):
    K, c = z.shape[-2:]
    tr, nl = _tile(K, 1024), len(lead)
    return block_copy(z, (K, 8 * c), (1,) * (1 + nl) + (tr, c), (tr, c), (8, K // tr),
                      lambda d, i: (d, *lead, i, 0), lambda d, i: (i, d), name=name)


def _owners_from_columns(g, *, name):
    K, c = g.shape[0], g.shape[1] // 8
    tr = _tile(K, 1024)
    return block_copy(g, (8, K, c), (tr, c), (1, tr, c), (8, K // tr), lambda d, i: (i, d), lambda d, i: (d, i, 0), name=name)


def _full_from_shards(k, sh, tag):
    if k not in COL_SHARDED:
        return sh.reshape((-1, sh.shape[-1]))
    if k == 'w_br':
        return [_columns_from_owners(sh, lead=(n,), name=f"{tag}_w_br{n}_layout") for n in range(3)]
    if k == 'w_uq':
        return _columns_from_owners(jnp.pad(sh, ((0, 0), (0, 0), (0, 64))), name=f"{tag}_w_uq_layout")
    if k == 'w_ukv':
        return block_copy(sh, (256, 2048), (1, 256, LANE), (256, LANE), (2, MLA_HEADS),
                          lambda t, h: (h, 0, t), lambda t, h: (0, t * MLA_HEADS + h), name=f"{tag}_w_ukv_layout")
    return _columns_from_owners(sh, name=f"{tag}_{k}_layout")


def _shards_from_full(k, g, tag):
    if k not in COL_SHARDED:
        return g.reshape((8, g.shape[0] // 8, g.shape[1]))
    if k == 'w_br':
        return jnp.stack([_owners_from_columns(g[n], name=f"{tag}_dw_br{n}_layout") for n in range(3)], axis=1)
    if k == 'w_uq':
        return _owners_from_columns(g, name=f"{tag}_dw_uq_layout")[:, :, :MLA_QK]
    if k == 'w_ukv':
        return block_copy(g, (8, 256, 256), (256, LANE), (1, 256, LANE), (2, MLA_HEADS),
                          lambda t, h: (0, t * MLA_HEADS + h), lambda t, h: (h, 0, t), name=f"{tag}_dw_ukv_layout")
    return _owners_from_columns(g, name=f"{tag}_d{k}_layout")


def w_in_full(gathered, layer, *, name, after=None):
    _, L, K, c = gathered.shape
    return col_gather(gathered.reshape(8 * L, K, c), _w_in_table(layer, L), 2, (1, K, Z_W), name=name, after=after)[0]


def w_in_shards(g, *, name):
    return col_gather(g[None], _w_in_grad_table(), 3, (8, g.shape[0], W_IN_SHARD), name=name)


def _layer_fwd(x, mem, W, P, cos, ssin, tag, later=None):
    S = x.shape[0]
    sv = {'x0': x}
    h = rmsnorm_fwd(x, P['g_mix'], name=f"{tag}_norm_mix")
    z = mm(h, W['w_in'], name=f"{tag}_mm_in")
    if later is not None:
        W = later(z)
    sv.update(h=h, z=z)
    cqn = rmsnorm_fwd(z, P['g_cq'], col=0, width=512, name=f"{tag}_norm_cq")
    ckvn = rmsnorm_fwd(z, P['g_ckv'], col=512, width=256, name=f"{tag}_norm_ckv")
    qf = mm(cqn, W['w_uq'], name=f"{tag}_mm_uq")
    kvf = mm(ckvn, W['w_ukv'], name=f"{tag}_mm_ukv")
    qa = mla_prep_fwd(qf, qf, P['g_mla_q'], cos, ssin, n_col=0, n_stride=2 * LANE, r_col=LANE, r_stride=2 * LANE,
                      heads=8, name=f"{tag}_mla_q")
    ka = mla_prep_fwd(kvf, z, P['g_mla_k'], cos, ssin, n_col=0, n_stride=LANE, r_col=KR_COL, r_stride=0,
                      heads=8, name=f"{tag}_mla_k")
    ya, lse_a = causal_attn_fwd(qa, ka, kvf, v_col=1024, chunked=True, scale=MLA_QK ** -0.5, name=f"{tag}_mla_attn")
    sv.update(cqn=cqn, ckvn=ckvn, qf=qf, kvf=kvf, qa=qa, ka=ka, lse_a=lse_a)
    qb = headnorm_fwd(z, P['g_fox_q'], col=Z_FOX, heads=8, name=f"{tag}_fox_qn")
    kb = headnorm_fwd(z, P['g_fox_k'], col=Z_FOX + 1024, heads=8, name=f"{tag}_fox_kn")
    fl = z[:, FF_COL:FF_COL + 8].T
    cum = foxgate_fwd(fl, P['b_f'], name=f"{tag}_fox_gate")
    cq, ck = cum.reshape(8, S, 1), cum.reshape(8, 1, S)
    yb, lse_b = causal_attn_fwd(qb, kb, z, v_col=Z_FOX + 2048, chunked=False, scale=LANE ** -0.5, cq=cq, ck=ck,
                                name=f"{tag}_fox_attn")
    sv.update(qb=qb, kb=kb, fl=fl, cq=cq, ck=ck, lse_b=lse_b)
    qc = headnorm_fwd(z, P['g_ch_q'], col=Z_CH, heads=8, name=f"{tag}_ch_qn")
    kc = headnorm_fwd(z, P['g_ch_k'], col=Z_CH + 1024, heads=8, name=f"{tag}_ch_kn")
    kcp = jnp.pad(kc, ((0, 0), (PAD, CHUNK), (0, 0)))
    vcp = jnp.pad(z[:, Z_CH + 2048:Z_CH + 3072], ((PAD, CHUNK), (0, 0)))
    bias = band_bias(P['rel_bias'], name=f"{tag}_ch_bias")
    yc = band_fwd(qc, kcp, vcp, bias, scale=LANE ** -0.5, name=f"{tag}_ch_attn")
    sv.update(qc=qc, kcp=kcp, vcp=vcp, bias=bias)
    ys = (ya, yb, yc)
    proj = [mm(ys[n], W['w_br'][n], name=f"{tag}_mm_br{n}") for n in range(3)]
    merged = gate_fwd(z, proj, name=f"{tag}_gate")
    x1 = mm(merged, W['w_out'], epi='add', aux=x, name=f"{tag}_mm_out")
    sv.update(ys=ys, proj=proj, merged=merged, x1=x1)
    hc = rmsnorm_fwd(x1, P['g_cross'], name=f"{tag}_norm_cross")
    memn = rmsnorm_fwd(mem, P['g_mem'], name=f"{tag}_norm_mem")
    qx_raw = mm(hc, W['w_xq'], name=f"{tag}_mm_xq")
    memkv = mm(memn, W['w_xkv'], name=f"{tag}_mm_xkv")
    qx = headnorm_fwd(qx_raw, P['g_x_q'], col=0, heads=4, name=f"{tag}_x_qn")
    kx = headnorm_fwd(memkv, P['g_x_k'], col=0, heads=4, name=f"{tag}_x_kn")
    ox = attn_fwd(qx, kx, memkv, v_col=512, scale=LANE ** -0.5, name=f"{tag}_x_attn")
    x2 = mm(ox, W['w_xo'], epi='add', aux=x1, name=f"{tag}_mm_xo")
    sv.update(hc=hc, memn=memn, qx_raw=qx_raw, memkv=memkv, qx=qx, kx=kx, ox=ox, x2=x2)
    hm = rmsnorm_fwd(x2, P['g_mlp'], name=f"{tag}_norm_mlp")
    u, a = mm(hm, W['w_1'], epi='relu2', out_dtype=BF16, name=f"{tag}_mm_w1")
    x3 = mm(a, W['w_2'], epi='add', aux=x2, name=f"{tag}_mm_w2")
    sv.update(hm=hm, u=u, a=a)
    return x3, sv


def _layer_bwd(dx, mem, W, P, sv, cos, ssin, tag, send_off=None):
    S = dx.shape[0]
    z = sv['z']
    gw, gs = {}, {}
    wgrad = lambda a, d, name: mm(a, d, ta=True, out_dtype=BF16, name=name)
    gw['w_2'] = wgrad(sv['a'], dx, f"{tag}_dw2")
    du = mm(dx, W['w_2'], tb=True, epi='mul_drelu2', aux=sv['u'], out_dtype=BF16, name=f"{tag}_du")
    gw['w_1'] = wgrad(sv['hm'], du, f"{tag}_dw1")
    dhm = mm(du, W['w_1'], tb=True, name=f"{tag}_dhm")
    dx, gs['g_mlp'] = rmsnorm_bwd(sv['x2'], P['g_mlp'], dhm, res=dx, name=f"{tag}_dnorm_mlp")
    gw['w_xo'] = wgrad(sv['ox'], dx, f"{tag}_dwxo")
    dox = mm(dx, W['w_xo'], tb=True, out_dtype=BF16, name=f"{tag}_dox")
    dqx, dkx, dvx = attn_bwd(sv['qx'], sv['kx'], sv['memkv'], dox, v_col=512, scale=LANE ** -0.5, name=f"{tag}_x_attn_bwd")
    dqx_raw, gs['g_x_q'] = headnorm_bwd(sv['qx_raw'], P['g_x_q'], dqx, col=0, heads=4, name=f"{tag}_x_qn_bwd")
    dkx_raw, gs['g_x_k'] = headnorm_bwd(sv['memkv'], P['g_x_k'], dkx, col=0, heads=4, name=f"{tag}_x_kn_bwd")
    dqx_b = dqx_raw.astype(BF16)
    gw['w_xq'] = wgrad(sv['hc'], dqx_b, f"{tag}_dwxq")
    dhc = mm(dqx_b, W['w_xq'], tb=True, name=f"{tag}_dhc")
    dx, gs['g_cross'] = rmsnorm_bwd(sv['x1'], P['g_cross'], dhc, res=dx, name=f"{tag}_dnorm_cross")
    dmemkv = jnp.concatenate([dkx_raw, dvx], axis=1).astype(BF16)
    gw['w_xkv'] = wgrad(sv['memn'], dmemkv, f"{tag}_dwxkv")
    dmemn = mm(dmemkv, W['w_xkv'], tb=True, name=f"{tag}_dmemn")
    _, gs['g_mem'] = rmsnorm_bwd(mem, P['g_mem'], dmemn, name=f"{tag}_dnorm_mem")
    gw['w_out'] = wgrad(sv['merged'], dx, f"{tag}_dwout")
    dmerged = mm(dx, W['w_out'], tb=True, name=f"{tag}_dmerged")
    dproj, dgl = gate_bwd(z, sv['proj'], dmerged, name=f"{tag}_gate_bwd")
    gw['w_br'] = [wgrad(sv['ys'][n], dproj[n], f"{tag}_dwbr{n}") for n in range(3)]
    dys = [mm(dproj[n], W['w_br'][n], tb=True, out_dtype=BF16, name=f"{tag}_dys{n}") for n in range(3)]
    dqa, dka, dva = causal_attn_bwd(sv['qa'], sv['ka'], sv['kvf'], sv['ys'][0], dys[0], sv['lse_a'], v_col=1024, chunked=True,
                                    scale=MLA_QK ** -0.5, name=f"{tag}_mla_attn_bwd")
    g_mla_q = P['g_mla_q'] if send_off is None else P['g_mla_q'] + send_off[0](gw)
    dqn, dqr, gs['g_mla_q'] = mla_prep_bwd(sv['qf'], sv['qf'], g_mla_q, cos, ssin, dqa, n_col=0, n_stride=2 * LANE,
                                           r_col=LANE, r_stride=2 * LANE, heads=8, name=f"{tag}_mla_q_bwd")
    dkn, dkr, gs['g_mla_k'] = mla_prep_bwd(sv['kvf'], z, P['g_mla_k'], cos, ssin, dka, n_col=0, n_stride=LANE,
                                           r_col=KR_COL, r_stride=0, heads=8, name=f"{tag}_mla_k_bwd")
    dqf = jnp.stack([dqn.reshape(S, 8, LANE), dqr.reshape(S, 8, LANE)], axis=2).reshape(S, 2048).astype(BF16)
    dkvf = jnp.concatenate([dkn, dva], axis=1).astype(BF16)
    gw['w_uq'] = wgrad(sv['cqn'], dqf, f"{tag}_dwuq")
    gw['w_ukv'] = wgrad(sv['ckvn'], dkvf, f"{tag}_dwukv")
    dcqn = mm(dqf, W['w_uq'], tb=True, name=f"{tag}_dcqn")
    dckvn = mm(dkvf, W['w_ukv'], tb=True, name=f"{tag}_dckvn")
    dcq_raw, gs['g_cq'] = rmsnorm_bwd(z, P['g_cq'], dcqn, col=0, width=512, name=f"{tag}_dnorm_cq")
    dckv_raw, gs['g_ckv'] = rmsnorm_bwd(z, P['g_ckv'], dckvn, col=512, width=256, name=f"{tag}_dnorm_ckv")
    dqb, dkb, dvb, dcq, dck = causal_attn_bwd(sv['qb'], sv['kb'], z, sv['ys'][1], dys[1], sv['lse_b'], v_col=Z_FOX + 2048,
                                              chunked=False, scale=LANE ** -0.5, cq=sv['cq'], ck=sv['ck'],
                                              name=f"{tag}_fox_attn_bwd")
    dqb_raw, gs['g_fox_q'] = headnorm_bwd(z, P['g_fox_q'], dqb, col=Z_FOX, heads=8, name=f"{tag}_fox_qn_bwd")
    dkb_raw, gs['g_fox_k'] = headnorm_bwd(z, P['g_fox_k'], dkb, col=Z_FOX + 1024, heads=8, name=f"{tag}_fox_kn_bwd")
    dfl, gs['b_f'] = foxgate_bwd(sv['fl'], P['b_f'], dcq.reshape(8, S) + dck.reshape(8, S), name=f"{tag}_fox_gate_bwd")
    dqc, dkcp, dvcp, dbias = band_bwd(sv['qc'], sv['kcp'], sv['vcp'], sv['bias'], dys[2], scale=LANE ** -0.5,
                                      name=f"{tag}_ch_attn_bwd")
    dqc_raw, gs['g_ch_q'] = headnorm_bwd(z, P['g_ch_q'], dqc, col=Z_CH, heads=8, name=f"{tag}_ch_qn_bwd")
    dkc_raw, gs['g_ch_k'] = headnorm_bwd(z, P['g_ch_k'], dkcp[:, PAD:PAD + S, :], col=Z_CH + 1024, heads=8,
                                         name=f"{tag}_ch_kn_bwd")
    gs['rel_bias'] = relbias_bwd(chunk_band_dbias(dbias), name=f"{tag}_relbias_bwd")
    b16 = lambda t: t.astype(BF16)
    dz = jnp.concatenate([b16(dcq_raw), b16(dckv_raw), b16(dkr), b16(dfl.T), jnp.zeros((S, 120), BF16),
                          b16(dqb_raw), b16(dkb_raw), b16(dvb), b16(dqc_raw), b16(dkc_raw), b16(dvcp[PAD:PAD + S]),
                          dgl[0], dgl[1], dgl[2]], axis=1)
    gw['w_in'] = wgrad(sv['h'], dz, f"{tag}_dwin")
    dh = mm(dz, W['w_in'], tb=True, name=f"{tag}_dh")
    g_mix = P['g_mix'] if send_off is None else P['g_mix'] + send_off[1](gw)
    dx, gs['g_mix'] = rmsnorm_bwd(sv['x0'], g_mix, dh, res=dx, name=f"{tag}_dnorm_mix")
    return dx, gw, gs


def _local_step(x, mem, target, Ws, Ps):
    S = x.shape[0]
    cos, ssin = _rope_tables(S)
    L = len(Ws)
    saved = []
    for l in range(L):
        x, sv = _layer_fwd(x, mem, Ws[l], Ps[l], cos, ssin, f"l{l}")
        saved.append(sv)
    loss, dx = loss_head(x, target, name="loss_head")
    gws, gss = [None] * L, [None] * L
    for l in reversed(range(L)):
        dx, gws[l], gss[l] = _layer_bwd(dx, mem, Ws[l], Ps[l], saved[l], cos, ssin, f"l{l}")
    return loss, dx, gws, gss


def _pack_small(d):
    flat = jnp.concatenate([d[k].reshape(-1) for k in SMALL])
    n = flat.shape[0]
    rows = -(-n // (8 * LANE)) * 8
    return jnp.pad(flat, (0, rows * LANE - n)).reshape(rows, LANE)


def _unpack_small(packed, like):
    flat, out, off = packed.reshape(-1), {}, 0
    for k in SMALL:
        n = int(np.prod(like[k].shape))
        out[k] = flat[off:off + n].reshape(like[k].shape)
        off += n
    return out


def kernel(x, mem, g_mix, w_in, g_cq, w_uq, g_ckv, w_ukv, g_mla_q, g_mla_k, b_f, g_fox_q, g_fox_k, rel_bias, g_ch_q, g_ch_k, w_br, w_out, g_cross, g_mem, w_xq, w_xkv, g_x_q, g_x_k, w_xo, g_mlp, w_1, w_2, loss_target, m_g_mix, m_w_in, m_g_cq, m_w_uq, m_g_ckv, m_w_ukv, m_g_mla_q, m_g_mla_k, m_b_f, m_g_fox_q, m_g_fox_k, m_rel_bias, m_g_ch_q, m_g_ch_k, m_w_br, m_w_out, m_g_cross, m_g_mem, m_w_xq, m_w_xkv, m_g_x_q, m_g_x_k, m_w_xo, m_g_mlp, m_w_1, m_w_2, v_g_mix, v_w_in, v_g_cq, v_w_uq, v_g_ckv, v_w_ukv, v_g_mla_q, v_g_mla_k, v_b_f, v_g_fox_q, v_g_fox_k, v_rel_bias, v_g_ch_q, v_g_ch_k, v_w_br, v_w_out, v_g_cross, v_g_mem, v_w_xq, v_w_xkv, v_g_x_q, v_g_x_k, v_w_xo, v_g_mlp, v_w_1, v_w_2):
    args = locals()
    w = {k: args[k] for k in WEIGHTS}
    m = {k: args['m_' + k] for k in WEIGHTS}
    v = {k: args['v_' + k] for k in WEIGHTS}
    L = w_in.shape[0]

    Ps = [{k: w[k][l] for k in SMALL} for l in range(L)]
    xs, memv = x[0], mem[0]
    cos, ssin = _rope_tables(xs.shape[0])

    gathers = {}

    def start_gather(l, group, after=None):
        shards = [w[k][l].astype(BF16) for k in group]
        gathers[l, group] = copies_start(_gather_copies, shards, [lax.empty((8,) + s.shape, s.dtype) for s in shards],
                                         name=f"l{l}_ag_start_{group[0]}", after=after)
        return gathers[l, group][4][0, :1]

    Ps[0]['g_mix'] = Ps[0]['g_mix'] + start_gather(0, AG_GROUPS[0])

    def arrived(l, group, after, on_landed=None):
        send_sems, recv_sems, shards, zones, _ = gathers[l, group]
        shards, zones = copies_wait(_gather_copies, send_sems, recv_sems, shards, zones, after, name=f"l{l}_ag_wait_{group[0]}")
        then = None if on_landed is None else on_landed(shards[0])
        zones = forward_to_sibling(zones, name=f"l{l}_ag_forward_{group[0]}")
        zones = [place_mine(z, s, name=f"l{l}_{k}_mine") for k, z, s in zip(group, zones, shards)]
        return {k: w_in_full(z[:, None], 0, name=f"l{l}_w_in_layout", after=then) if k == 'w_in'
                else _full_from_shards(k, z, f"l{l}") for k, z in zip(group, zones)}

    Ws, saved = [], []
    for l in range(L):
        if l == 0:
            def start_the_rest(anchor):
                rest = [(j, group) for j in range(L) for group in AG_GROUPS if (j, group) != (0, AG_GROUPS[0])]
                return sum(start_gather(j, group, after=anchor) for j, group in rest)

            Ws.append(arrived(0, AG_GROUPS[0], Ps[0]['g_mix'], start_the_rest))
        else:
            Ws.append(arrived(l, AG_GROUPS[0], xs))

        def later(anchor, l=l):
            Ws[l].update(arrived(l, AG_GROUPS[1], anchor))
            return Ws[l]

        xs, sv = _layer_fwd(xs, memv, Ws[l], Ps[l], cos, ssin, f"l{l}", later)
        saved.append(sv)

    loss, dx = loss_head(xs, loss_target[0], name="loss_head")
    loss = lax.psum(loss[0, 0], ("x", "y", "c"))

    gss = [None] * L
    scatters = {group: [None] * L for group in RS_GROUPS}
    for l in reversed(range(L)):
        def send_off(gw, group, l=l):
            gdst = [w_in_shards(gw[k], name=f"l{l}_dw_in_layout") if k == 'w_in' else _shards_from_full(k, gw[k], f"l{l}")
                    for k in group]
            started = copies_start(_scatter_copies, gdst, [lax.empty((N_PEERS,) + g.shape[1:], g.dtype) for g in gdst],
                                   name=f"l{l}_rs_start_{group[0]}")
            scatters[group][l] = started[:4]
            return started[4][0, :1]

        hooks = tuple((lambda gw, group=group: send_off(gw, group)) for group in RS_GROUPS)
        dx, _, gss[l] = _layer_bwd(dx, memv, Ws[l], Ps[l], saved[l], cos, ssin, f"l{l}", hooks)
    grad_x = dx

    grads, delta, new_m, new_v = {}, {}, {}, {}
    after = grad_x
    for group in RS_GROUPS:
        done = [copies_wait(_scatter_copies, *scatters[group][l], after, name=f"l{l}_rs_wait_{group[0]}") for l in range(L)]
        for t, k in enumerate(group):
            grads[k], delta[k], new_m[k], new_v[k] = grad_sum_adamw(
                [done[l][0][t] for l in range(L)], [done[l][1][t] for l in range(L)], w[k], m[k], v[k], name=f"adamw_{k}")
        after = sum(delta[k][(0,) * (delta[k].ndim - 1)][:1] for k in group)

    small_part = _pack_small({k: jnp.stack([gss[l][k] for l in range(L)]) for k in SMALL})
    small_all = all_gather([small_part], after, name="ag_small")[0]
    grads.update(_unpack_small(ordered_sum(small_all, name="small_sum"), {k: w[k] for k in SMALL}))
    sd, sm, sv_ = adamw(_pack_small({k: w[k] for k in SMALL}), _pack_small({k: grads[k] for k in SMALL}),
                        _pack_small({k: m[k] for k in SMALL}), _pack_small({k: v[k] for k in SMALL}), name="adamw_small")
    like = {k: w[k] for k in SMALL}
    delta.update(_unpack_small(sd, like))
    new_m.update(_unpack_small(sm, like))
    new_v.update(_unpack_small(sv_, like))

    return (loss, grad_x[None], *[grads[k] for k in WEIGHTS], *[delta[k] for k in WEIGHTS],
            *[new_m[k] for k in WEIGHTS], *[new_v[k] for k in WEIGHTS])
```

```python
import numpy as np
import jax
import jax.numpy as jnp
from jax import lax
from jax.experimental import pallas as pl
from jax.experimental.pallas import tpu as pltpu

F32, BF16 = jnp.float32, jnp.bfloat16
EPS = 1e-6
NEG = -1e30
LANE = 128
VMEM_LIMIT_BYTES = 56 * 2**20
MESH = pl.DeviceIdType.MESH

D_MODEL = 2048
CHUNK = 64
BAND = 9 * CHUNK
PAD = 8 * CHUNK
REL_CLIP = 128
MLA_HEADS, MLA_NOPE, MLA_ROPE, MLA_QK = 8, 128, 64, 192
N_HEADS = 8
X_HEADS = 4
ROPE_THETA = 10000.0
ADAM_LR, ADAM_B1, ADAM_B2, ADAM_EPS, ADAM_WD, ADAM_STEP = 0.001, 0.9, 0.999, 1e-08, 0.01, 10

Z_MAIN, Z_FOX, Z_CH, Z_GATE, Z_W = 0, 1024, 4096, 7168, 13312
KR_COL, FF_COL = 768, 896

BIG = ('w_in', 'w_uq', 'w_ukv', 'w_br', 'w_out', 'w_xq', 'w_xkv', 'w_xo', 'w_1', 'w_2')
COL_SHARDED = ('w_in', 'w_uq', 'w_ukv', 'w_br', 'w_xo', 'w_1')
RS_GROUPS = (('w_2', 'w_1', 'w_xo', 'w_xq', 'w_xkv', 'w_out', 'w_br'), ('w_uq', 'w_ukv', 'w_in'))
AG_GROUPS = (('w_in',), ('w_uq', 'w_ukv', 'w_br', 'w_out', 'w_xq', 'w_xkv', 'w_xo', 'w_1', 'w_2'))
SMALL = ('g_mix', 'g_cq', 'g_ckv', 'g_mla_q', 'g_mla_k', 'b_f', 'g_fox_q', 'g_fox_k', 'rel_bias', 'g_ch_q',
         'g_ch_k', 'g_cross', 'g_mem', 'g_x_q', 'g_x_k', 'g_mlp')
WEIGHTS = ('g_mix', 'w_in', 'g_cq', 'w_uq', 'g_ckv', 'w_ukv', 'g_mla_q', 'g_mla_k', 'b_f', 'g_fox_q', 'g_fox_k',
           'rel_bias', 'g_ch_q', 'g_ch_k', 'w_br', 'w_out', 'g_cross', 'g_mem', 'w_xq', 'w_xkv', 'g_x_q', 'g_x_k',
           'w_xo', 'g_mlp', 'w_1', 'w_2')


def _params(*sem):
    return pltpu.CompilerParams(dimension_semantics=sem, vmem_limit_bytes=VMEM_LIMIT_BYTES)


def _tile(dim, pref):
    if dim <= pref:
        return dim
    for t in range(pref - pref % LANE, 0, -LANE):
        if dim % t == 0:
            return t
    raise ValueError((dim, pref))


def mm(a, b, *, ta=False, tb=False, out_dtype=F32, epi=None, aux=None, name, tm=1024, tn=512, tk=2048):
    M, K = (a.shape[1], a.shape[0]) if ta else a.shape
    N = b.shape[0] if tb else b.shape[1]
    assert (b.shape[1] if tb else b.shape[0]) == K, (a.shape, b.shape, ta, tb)
    tm, tn, tk = _tile(M, tm), _tile(N, tn), _tile(K, tk)
    nk = K // tk
    dn = (((0 if ta else 1,), (1 if tb else 0,)), ((), ()))
    n_aux = 0 if aux is None else 1

    def finish(acc, aux_refs, o_refs):
        if epi is None:
            o_refs[0][...] = acc.astype(o_refs[0].dtype)
        elif epi == 'add':
            o_refs[0][...] = (acc + aux_refs[0][...]).astype(o_refs[0].dtype)
        elif epi == 'relu2':
            o_refs[0][...] = acc
            r = jnp.maximum(acc, 0.0)
            o_refs[1][...] = (r * r).astype(o_refs[1].dtype)
        elif epi == 'mul_drelu2':
            o_refs[0][...] = (acc * (2.0 * jnp.maximum(aux_refs[0][...], 0.0))).astype(o_refs[0].dtype)

    def body(a_ref, b_ref, *rest):
        aux_refs = rest[:n_aux]
        o_refs = rest[n_aux:n_aux + (2 if epi == 'relu2' else 1)]
        part = lax.dot_general(a_ref[...].astype(BF16), b_ref[...].astype(BF16), dn, preferred_element_type=F32)
        if nk == 1:
            finish(part, aux_refs, o_refs)
        else:
            acc_ref = rest[-1]
            k = pl.program_id(2)

            @pl.when(k == 0)
            def _():
                acc_ref[...] = part

            @pl.when(k > 0)
            def _():
                acc_ref[...] += part

            @pl.when(k == nk - 1)
            def _():
                finish(acc_ref[...], aux_refs, o_refs)

    a_spec = pl.BlockSpec((tk, tm), lambda i, j, k: (k, i)) if ta else pl.BlockSpec((tm, tk), lambda i, j, k: (i, k))
    b_spec = pl.BlockSpec((tn, tk), lambda i, j, k: (j, k)) if tb else pl.BlockSpec((tk, tn), lambda i, j, k: (k, j))
    o_spec = pl.BlockSpec((tm, tn), lambda i, j, k: (i, j))
    if epi == 'relu2':
        out_shape = (jax.ShapeDtypeStruct((M, N), F32), jax.ShapeDtypeStruct((M, N), out_dtype))
        out_specs = (o_spec, o_spec)
    else:
        out_shape, out_specs = jax.ShapeDtypeStruct((M, N), out_dtype), o_spec
    return pl.pallas_call(
        body, name=name, out_shape=out_shape, grid=(M // tm, N // tn, nk),
        in_specs=[a_spec, b_spec] + [o_spec] * n_aux, out_specs=out_specs,
        scratch_shapes=[pltpu.VMEM((tm, tn), F32)] if nk > 1 else [],
        compiler_params=_params("parallel", "parallel", "arbitrary"),
    )(a, b, *([aux] if n_aux else []))


def rmsnorm_fwd(x, g, *, col=0, width=None, out_dtype=BF16, name, ts=256):
    S = x.shape[0]
    width = x.shape[1] if width is None else width
    ts, cb = _tile(S, ts), col // width

    def body(x_ref, g_ref, o_ref):
        xf = x_ref[...]
        r = lax.rsqrt(jnp.mean(xf * xf, axis=-1, keepdims=True) + EPS)
        o_ref[...] = (xf * r * g_ref[...]).astype(o_ref.dtype)

    return pl.pallas_call(
        body, name=name, out_shape=jax.ShapeDtypeStruct((S, width), out_dtype), grid=(S // ts,),
        in_specs=[pl.BlockSpec((ts, width), lambda i: (i, cb)), pl.BlockSpec((1, width), lambda i: (0, 0))],
        out_specs=pl.BlockSpec((ts, width), lambda i: (i, 0)), compiler_params=_params("parallel"),
    )(x, g.reshape(1, width))


def rmsnorm_bwd(x, g, dy, *, col=0, width=None, res=None, name, ts=256):
    S = x.shape[0]
    width = x.shape[1] if width is None else width
    ts, cb = _tile(S, ts), col // width
    has_res = res is not None

    def body(x_ref, g_ref, dy_ref, *rest):
        dx_ref, dg_ref = rest[-2:]
        xf = x_ref[...]
        r = lax.rsqrt(jnp.mean(xf * xf, axis=-1, keepdims=True) + EPS)
        dyf = dy_ref[...].astype(F32)
        dyg = dyf * g_ref[...]
        dx = r * dyg - xf * (r * r * r) * jnp.mean(dyg * xf, axis=-1, keepdims=True)
        if has_res:
            dx = dx + rest[0][...]
        dx_ref[...] = dx
        part = jnp.sum(dyf * xf * r, axis=0, keepdims=True)

        @pl.when(pl.program_id(0) == 0)
        def _():
            dg_ref[...] = part

        @pl.when(pl.program_id(0) > 0)
        def _():
            dg_ref[...] += part

    blk = pl.BlockSpec((ts, width), lambda i: (i, 0))
    dx, dg = pl.pallas_call(
        body, name=name,
        out_shape=(jax.ShapeDtypeStruct((S, width), F32), jax.ShapeDtypeStruct((1, width), F32)), grid=(S // ts,),
        in_specs=[pl.BlockSpec((ts, width), lambda i: (i, cb)), pl.BlockSpec((1, width), lambda i: (0, 0)), blk]
        + ([blk] if has_res else []),
        out_specs=(blk, pl.BlockSpec((1, width), lambda i: (0, 0))), compiler_params=_params("arbitrary"),
    )(x, g.reshape(1, width), dy, *([res] if has_res else []))
    return dx, dg.reshape(width)


def headnorm_fwd(x, g, *, col, heads, name, ts=2048):
    S = x.shape[0]
    ts, cb = _tile(S, ts), col // LANE

    def body(x_ref, g_ref, o_ref):
        xf = x_ref[...]
        r = lax.rsqrt(jnp.mean(xf * xf, axis=-1, keepdims=True) + EPS)
        o_ref[0] = (xf * r * g_ref[...]).astype(o_ref.dtype)

    return pl.pallas_call(
        body, name=name, out_shape=jax.ShapeDtypeStruct((heads, S, LANE), BF16), grid=(heads, S // ts),
        in_specs=[pl.BlockSpec((ts, LANE), lambda h, i: (i, cb + h)), pl.BlockSpec((1, LANE), lambda h, i: (0, 0))],
        out_specs=pl.BlockSpec((1, ts, LANE), lambda h, i: (h, i, 0)), compiler_params=_params("parallel", "parallel"),
    )(x, g.reshape(1, LANE))


def headnorm_bwd(x, g, dy, *, col, heads, name, ts=2048):
    S = x.shape[0]
    ts, cb = _tile(S, ts), col // LANE

    def body(x_ref, g_ref, dy_ref, dx_ref, dg_ref):
        xf = x_ref[...]
        r = lax.rsqrt(jnp.mean(xf * xf, axis=-1, keepdims=True) + EPS)
        dyf = dy_ref[0]
        dyg = dyf * g_ref[...]
        dx_ref[...] = r * dyg - xf * (r * r * r) * jnp.mean(dyg * xf, axis=-1, keepdims=True)
        part = jnp.sum(dyf * xf * r, axis=0, keepdims=True)
        first = jnp.logical_and(pl.program_id(0) == 0, pl.program_id(1) == 0)

        @pl.when(first)
        def _():
            dg_ref[...] = part

        @pl.when(jnp.logical_not(first))
        def _():
            dg_ref[...] += part

    dx, dg = pl.pallas_call(
        body, name=name,
        out_shape=(jax.ShapeDtypeStruct((S, heads * LANE), F32), jax.ShapeDtypeStruct((1, LANE), F32)),
        grid=(heads, S // ts),
        in_specs=[pl.BlockSpec((ts, LANE), lambda h, i: (i, cb + h)), pl.BlockSpec((1, LANE), lambda h, i: (0, 0)),
                  pl.BlockSpec((1, ts, LANE), lambda h, i: (h, i, 0))],
        out_specs=(pl.BlockSpec((ts, LANE), lambda h, i: (i, h)), pl.BlockSpec((1, LANE), lambda h, i: (0, 0))),
        compiler_params=_params("arbitrary", "arbitrary"),
    )(x, g.reshape(1, LANE), dy)
    return dx, dg.reshape(LANE)


def _rope_tables(S):
    pos = jnp.arange(S, dtype=F32)
    inv = ROPE_THETA ** (-jnp.arange(0, MLA_ROPE, 2, dtype=F32) / MLA_ROPE)
    ang = pos[:, None] * inv[None, :]
    c, s, z = jnp.cos(ang), jnp.sin(ang), jnp.zeros((S, 64), F32)
    return jnp.concatenate([c, c, z], axis=1), jnp.concatenate([-s, s, z], axis=1)


def _rope(v, cos, ssin, lane):
    partner = jnp.where(lane < 32, pltpu.roll(v, 96, 1), pltpu.roll(v, 32, 1))
    return v * cos + partner * ssin


def mla_prep_fwd(xn, xr, g, cos, ssin, *, n_col, n_stride, r_col, r_stride, heads, name, ts=2048):
    S = xn.shape[0]
    ts = _tile(S, ts)
    nb, ns, rb, rs = n_col // LANE, n_stride // LANE, r_col // LANE, r_stride // LANE
    gn = g[:MLA_NOPE].reshape(1, LANE)
    gr = jnp.concatenate([g[MLA_NOPE:], jnp.zeros((64,), F32)]).reshape(1, LANE)

    def body(n_ref, r_ref, gn_ref, gr_ref, c_ref, s_ref, o_ref):
        n, rr = n_ref[...], r_ref[...]
        ss = jnp.sum(n * n, axis=-1, keepdims=True) + jnp.sum(rr * rr, axis=-1, keepdims=True)
        r = lax.rsqrt(ss * (1.0 / MLA_QK) + EPS)
        lane = lax.broadcasted_iota(jnp.int32, rr.shape, 1)
        o_ref[0, :, :LANE] = (n * r * gn_ref[...]).astype(o_ref.dtype)
        o_ref[0, :, LANE:] = _rope(rr * r * gr_ref[...], c_ref[...], s_ref[...], lane).astype(o_ref.dtype)

    row = lambda h, i: (0, 0)
    return pl.pallas_call(
        body, name=name, out_shape=jax.ShapeDtypeStruct((heads, S, 2 * LANE), BF16), grid=(heads, S // ts),
        in_specs=[pl.BlockSpec((ts, LANE), lambda h, i: (i, nb + ns * h)),
                  pl.BlockSpec((ts, LANE), lambda h, i: (i, rb + rs * h)),
                  pl.BlockSpec((1, LANE), row), pl.BlockSpec((1, LANE), row),
                  pl.BlockSpec((ts, LANE), lambda h, i: (i, 0)), pl.BlockSpec((ts, LANE), lambda h, i: (i, 0))],
        out_specs=pl.BlockSpec((1, ts, 2 * LANE), lambda h, i: (h, i, 0)),
        compiler_params=_params("parallel", "parallel"),
    )(xn, xr, gn, gr, cos, ssin)


def mla_prep_bwd(xn, xr, g, cos, ssin, dy, *, n_col, n_stride, r_col, r_stride, heads, name, ts=2048):
    S = xn.shape[0]
    ts = _tile(S, ts)
    nb, ns, rb, rs = n_col // LANE, n_stride // LANE, r_col // LANE, r_stride // LANE
    shared = r_stride == 0
    gn = g[:MLA_NOPE].reshape(1, LANE)
    gr = jnp.concatenate([g[MLA_NOPE:], jnp.zeros((64,), F32)]).reshape(1, LANE)

    def body(n_ref, r_ref, gn_ref, gr_ref, c_ref, s_ref, dy_ref, dn_ref, dr_ref, dgn_ref, dgr_ref):
        i, h = pl.program_id(0), pl.program_id(1)
        n, rr = n_ref[...], r_ref[...]
        ss = jnp.sum(n * n, axis=-1, keepdims=True) + jnp.sum(rr * rr, axis=-1, keepdims=True)
        r = lax.rsqrt(ss * (1.0 / MLA_QK) + EPS)
        lane = lax.broadcasted_iota(jnp.int32, rr.shape, 1)
        dyn = dy_ref[0, :, :LANE]
        dyr = dy_ref[0, :, LANE:]
        t = dyr * s_ref[...]
        dvr = dyr * c_ref[...] + jnp.where(lane < 32, pltpu.roll(t, 96, 1), pltpu.roll(t, 32, 1))
        dvr = jnp.where(lane < 64, dvr, 0.0)
        dgn_part = jnp.sum(dyn * n * r, axis=0, keepdims=True)
        dgr_part = jnp.sum(dvr * rr * r, axis=0, keepdims=True)
        dyn_g, dvr_g = dyn * gn_ref[...], dvr * gr_ref[...]
        proj = (jnp.sum(dyn_g * n, axis=-1, keepdims=True) + jnp.sum(dvr_g * rr, axis=-1, keepdims=True)) * (1.0 / MLA_QK)
        r3 = r * r * r
        dn_ref[...] = r * dyn_g - n * r3 * proj
        dr = r * dvr_g - rr * r3 * proj
        if shared:
            @pl.when(h == 0)
            def _():
                dr_ref[...] = dr

            @pl.when(h > 0)
            def _():
                dr_ref[...] += dr
        else:
            dr_ref[...] = dr
        first = jnp.logical_and(i == 0, h == 0)

        @pl.when(first)
        def _():
            dgn_ref[...] = dgn_part
            dgr_ref[...] = dgr_part

        @pl.when(jnp.logical_not(first))
        def _():
            dgn_ref[...] += dgn_part
            dgr_ref[...] += dgr_part

    row = lambda i, h: (0, 0)
    dr_cols = LANE if shared else heads * LANE
    dn, dr, dgn, dgr = pl.pallas_call(
        body, name=name,
        out_shape=(jax.ShapeDtypeStruct((S, heads * LANE), F32), jax.ShapeDtypeStruct((S, dr_cols), F32),
                   jax.ShapeDtypeStruct((1, LANE), F32), jax.ShapeDtypeStruct((1, LANE), F32)),
        grid=(S // ts, heads),
        in_specs=[pl.BlockSpec((ts, LANE), lambda i, h: (i, nb + ns * h)),
                  pl.BlockSpec((ts, LANE), lambda i, h: (i, rb + rs * h)),
                  pl.BlockSpec((1, LANE), row), pl.BlockSpec((1, LANE), row),
                  pl.BlockSpec((ts, LANE), lambda i, h: (i, 0)), pl.BlockSpec((ts, LANE), lambda i, h: (i, 0)),
                  pl.BlockSpec((1, ts, 2 * LANE), lambda i, h: (h, i, 0))],
        out_specs=(pl.BlockSpec((ts, LANE), lambda i, h: (i, h)),
                   pl.BlockSpec((ts, LANE), (lambda i, h: (i, 0)) if shared else (lambda i, h: (i, h))),
                   pl.BlockSpec((1, LANE), row), pl.BlockSpec((1, LANE), row)),
        compiler_params=_params("arbitrary", "arbitrary"),
    )(xn, xr, gn, gr, cos, ssin, dy)
    return dn, dr, jnp.concatenate([dgn.reshape(LANE), dgr.reshape(LANE)[:MLA_ROPE]])


_NT = (((1,), (1,)), ((), ()))
_TN = (((0,), (0,)), ((), ()))


def attn_fwd(q, k, v, *, v_col, scale, name, bq=256):
    H, S, dk = q.shape
    Sk = k.shape[1]
    bq, vb = _tile(S, bq), v_col // LANE

    def body(q_ref, k_ref, v_ref, o_ref):
        s = lax.dot_general(q_ref[0], k_ref[0], _NT, preferred_element_type=F32) * scale
        e = jnp.exp(s - jnp.max(s, axis=-1, keepdims=True))
        p = e * (1.0 / jnp.sum(e, axis=-1, keepdims=True))
        o_ref[...] = jnp.dot(p.astype(BF16), v_ref[...].astype(BF16), preferred_element_type=F32).astype(o_ref.dtype)

    return pl.pallas_call(
        body, name=name, out_shape=jax.ShapeDtypeStruct((S, H * LANE), BF16), grid=(H, S // bq),
        in_specs=[pl.BlockSpec((1, bq, dk), lambda h, i: (h, i, 0)), pl.BlockSpec((1, Sk, dk), lambda h, i: (h, 0, 0)),
                  pl.BlockSpec((Sk, LANE), lambda h, i: (0, vb + h))],
        out_specs=pl.BlockSpec((bq, LANE), lambda h, i: (i, h)), compiler_params=_params("parallel", "parallel"),
    )(q, k, v)


def attn_bwd(q, k, v, do, *, v_col, scale, name, bq=256):
    H, S, dk = q.shape
    Sk = k.shape[1]
    bq, vb = _tile(S, bq), v_col // LANE

    def body(q_ref, k_ref, v_ref, do_ref, dq_ref, dk_ref, dv_ref):
        i = pl.program_id(1)
        qb, kb, vv = q_ref[0], k_ref[0], v_ref[...].astype(BF16)
        s = lax.dot_general(qb, kb, _NT, preferred_element_type=F32) * scale
        e = jnp.exp(s - jnp.max(s, axis=-1, keepdims=True))
        p = e * (1.0 / jnp.sum(e, axis=-1, keepdims=True))
        dob = do_ref[...].astype(BF16)
        dv_part = lax.dot_general(p.astype(BF16), dob, _TN, preferred_element_type=F32)
        dp = lax.dot_general(dob, vv, _NT, preferred_element_type=F32)
        ds = p * (dp - jnp.sum(p * dp, axis=-1, keepdims=True))
        dsb = (ds * scale).astype(BF16)
        dq_ref[0] = jnp.dot(dsb, kb, preferred_element_type=F32)
        dk_part = lax.dot_general(dsb, qb, _TN, preferred_element_type=F32)

        @pl.when(i == 0)
        def _():
            dk_ref[0] = dk_part
            dv_ref[...] = dv_part

        @pl.when(i > 0)
        def _():
            dk_ref[0] += dk_part
            dv_ref[...] += dv_part

    return pl.pallas_call(
        body, name=name,
        out_shape=(jax.ShapeDtypeStruct((H, S, dk), F32), jax.ShapeDtypeStruct((H, Sk, dk), F32),
                   jax.ShapeDtypeStruct((Sk, H * LANE), F32)),
        grid=(H, S // bq),
        in_specs=[pl.BlockSpec((1, bq, dk), lambda h, i: (h, i, 0)), pl.BlockSpec((1, Sk, dk), lambda h, i: (h, 0, 0)),
                  pl.BlockSpec((Sk, LANE), lambda h, i: (0, vb + h)), pl.BlockSpec((bq, LANE), lambda h, i: (i, h))],
        out_specs=(pl.BlockSpec((1, bq, dk), lambda h, i: (h, i, 0)), pl.BlockSpec((1, Sk, dk), lambda h, i: (h, 0, 0)),
                   pl.BlockSpec((Sk, LANE), lambda h, i: (0, h))),
        compiler_params=_params("parallel", "arbitrary"),
    )(q, k, v, do)


def _causal_scores(q, kblk, i, start, blk, scale, chunked, cq, ckblk):
    s = lax.dot_general(q, kblk, _NT, preferred_element_type=F32) * scale
    if cq is not None:
        s = s + cq - ckblk
    qpos = i * blk + lax.broadcasted_iota(jnp.int32, s.shape, 0)
    kpos = start + lax.broadcasted_iota(jnp.int32, s.shape, 1)
    ok = (kpos >> 6) <= (qpos >> 6) if chunked else kpos <= qpos
    return jnp.where(ok, s, NEG)


def causal_attn_fwd(q, k, v, *, v_col, chunked, scale, cq=None, ck=None, name, blk=256):
    H, S, dk = q.shape
    blk, vb = _tile(S, blk), v_col // LANE
    fox = cq is not None

    def body(q_ref, k_ref, v_ref, *rest):
        o_ref, lse_ref = rest[-2:]
        s = _causal_scores(q_ref[0], k_ref[0], pl.program_id(1), 0, blk, scale, chunked,
                           rest[0][0] if fox else None, rest[1][0] if fox else None)
        m = jnp.max(s, axis=-1, keepdims=True)
        p = jnp.exp(s - m)
        l = jnp.sum(p, axis=-1, keepdims=True)
        pv = jnp.dot(p.astype(BF16), v_ref[...].astype(BF16), preferred_element_type=F32)
        o_ref[...] = (pv * (1.0 / l)).astype(o_ref.dtype)
        lse_ref[0] = m + jnp.log(l)

    in_specs = [pl.BlockSpec((1, blk, dk), lambda h, i: (h, i, 0)), pl.BlockSpec((1, S, dk), lambda h, i: (h, 0, 0)),
                pl.BlockSpec((S, LANE), lambda h, i: (0, vb + h))]
    args = [q, k, v]
    if fox:
        in_specs += [pl.BlockSpec((1, blk, 1), lambda h, i: (h, i, 0)), pl.BlockSpec((1, 1, S), lambda h, i: (h, 0, 0))]
        args += [cq, ck]
    return pl.pallas_call(
        body, name=name, out_shape=(jax.ShapeDtypeStruct((S, H * LANE), BF16), jax.ShapeDtypeStruct((H, S, 1), F32)),
        grid=(H, S // blk), in_specs=in_specs,
        out_specs=(pl.BlockSpec((blk, LANE), lambda h, i: (i, h)), pl.BlockSpec((1, blk, 1), lambda h, i: (h, i, 0))),
        compiler_params=_params("parallel", "parallel"),
    )(*args)


def causal_attn_bwd(q, k, v, o, do, lse, *, v_col, chunked, scale, cq=None, ck=None, name, blk=256):
    H, S, dk = q.shape
    blk, vb = _tile(S, blk), v_col // LANE
    kc = 2 * blk if S % (2 * blk) == 0 else blk
    fox = cq is not None

    def body(q_ref, k_ref, v_ref, o_ref, do_ref, lse_ref, *rest):
        i = pl.program_id(1)
        if fox:
            cq_ref, ck_ref, dq_ref, dk_ref, dv_ref, dcq_ref, dck_ref = rest
        else:
            dq_ref, dk_ref, dv_ref = rest

        @pl.when(i == 0)
        def _():
            dk_ref[...] = jnp.zeros_like(dk_ref)
            dv_ref[...] = jnp.zeros_like(dv_ref)
            if fox:
                dck_ref[...] = jnp.zeros_like(dck_ref)

        qb, dob, lse_b = q_ref[0], do_ref[...].astype(BF16), lse_ref[0]
        delta = jnp.sum(do_ref[...].astype(F32) * o_ref[...].astype(F32), axis=-1, keepdims=True)
        dq_ref[...] = jnp.zeros_like(dq_ref)
        if fox:
            dcq_ref[...] = jnp.zeros_like(dcq_ref)
        for c in range(S // kc):
            @pl.when(c * kc < (i + 1) * blk)
            def _(ks=slice(c * kc, (c + 1) * kc), start=c * kc):
                kblk = k_ref[0, ks, :]
                s = _causal_scores(qb, kblk, i, start, blk, scale, chunked,
                                   cq_ref[0] if fox else None, ck_ref[0, :, ks] if fox else None)
                p = jnp.exp(s - lse_b)
                dv_ref[ks, :] += lax.dot_general(p.astype(BF16), dob, _TN, preferred_element_type=F32)
                dp = lax.dot_general(dob, v_ref[ks, :].astype(BF16), _NT, preferred_element_type=F32)
                ds = p * (dp - delta)
                dsb = (ds * scale).astype(BF16)
                dq_ref[0] += jnp.dot(dsb, kblk, preferred_element_type=F32)
                dk_ref[0, ks, :] += lax.dot_general(dsb, qb, _TN, preferred_element_type=F32)
                if fox:
                    dcq_ref[0] += jnp.sum(ds, axis=-1, keepdims=True)
                    dck_ref[0, :, ks] += -jnp.sum(ds, axis=0, keepdims=True)

    row = pl.BlockSpec((blk, LANE), lambda h, i: (i, h))
    in_specs = [pl.BlockSpec((1, blk, dk), lambda h, i: (h, i, 0)), pl.BlockSpec((1, S, dk), lambda h, i: (h, 0, 0)),
                pl.BlockSpec((S, LANE), lambda h, i: (0, vb + h)), row, row, pl.BlockSpec((1, blk, 1), lambda h, i: (h, i, 0))]
    args = [q, k, v, o, do, lse]
    out_shape = [jax.ShapeDtypeStruct((H, S, dk), F32), jax.ShapeDtypeStruct((H, S, dk), F32),
                 jax.ShapeDtypeStruct((S, H * LANE), F32)]
    out_specs = [pl.BlockSpec((1, blk, dk), lambda h, i: (h, i, 0)), pl.BlockSpec((1, S, dk), lambda h, i: (h, 0, 0)),
                 pl.BlockSpec((S, LANE), lambda h, i: (0, h))]
    if fox:
        fox_specs = [pl.BlockSpec((1, blk, 1), lambda h, i: (h, i, 0)), pl.BlockSpec((1, 1, S), lambda h, i: (h, 0, 0))]
        in_specs += fox_specs
        args += [cq, ck]
        out_shape += [jax.ShapeDtypeStruct((H, S, 1), F32), jax.ShapeDtypeStruct((H, 1, S), F32)]
        out_specs += fox_specs
    return pl.pallas_call(
        body, name=name, out_shape=tuple(out_shape), grid=(H, S // blk), in_specs=in_specs, out_specs=tuple(out_specs),
        compiler_params=_params("parallel", "arbitrary"),
    )(*args)


CPB = 4
BANDW = BAND + CHUNK
WIN = BAND + (CPB - 1) * CHUNK


def chunk_band_dbias(db):
    H = db.shape[0]
    d4 = db.reshape(H, CPB, CHUNK, WIN)
    return sum(d4[:, c, :, CHUNK * c:CHUNK * c + BAND] for c in range(CPB))


def _band_probs(qb, kb, bias, start, scale):
    s = lax.dot_general(qb, kb, _NT, preferred_element_type=F32) * scale
    real = start + lax.broadcasted_iota(jnp.int32, s.shape, 1) >= PAD
    s = jnp.where(real, s + bias, NEG)
    e = jnp.exp(s - jnp.max(s, axis=-1, keepdims=True))
    return e * (1.0 / jnp.sum(e, axis=-1, keepdims=True))


def band_fwd(q, kp, vp, bias, *, scale, name):
    H, S, _ = q.shape
    Sp, rows = S + PAD + CHUNK, CPB * CHUNK

    def body(q_ref, k_ref, v_ref, b_ref, o_ref):
        start = pl.multiple_of(pl.program_id(1) * rows, rows)
        p = _band_probs(q_ref[0], k_ref[0, pl.ds(start, WIN), :], b_ref[0], start, scale)
        vb = v_ref[pl.ds(start, WIN), :].astype(BF16)
        o_ref[...] = jnp.dot(p.astype(BF16), vb, preferred_element_type=F32).astype(o_ref.dtype)

    return pl.pallas_call(
        body, name=name, out_shape=jax.ShapeDtypeStruct((S, H * LANE), BF16), grid=(H, S // rows),
        in_specs=[pl.BlockSpec((1, rows, LANE), lambda h, j: (h, j, 0)), pl.BlockSpec((1, Sp, LANE), lambda h, j: (h, 0, 0)),
                  pl.BlockSpec((Sp, LANE), lambda h, j: (0, h)), pl.BlockSpec((1, rows, WIN), lambda h, j: (h, 0, 0))],
        out_specs=pl.BlockSpec((rows, LANE), lambda h, j: (j, h)), compiler_params=_params("parallel", "parallel"),
    )(q, kp, vp, bias)


def band_bwd(q, kp, vp, bias, do, *, scale, name):
    H, S, _ = q.shape
    Sp, rows = S + PAD + CHUNK, CPB * CHUNK

    def body(q_ref, k_ref, v_ref, b_ref, do_ref, dq_ref, dk_ref, dv_ref, db_ref):
        j = pl.program_id(1)

        @pl.when(j == 0)
        def _():
            dk_ref[...] = jnp.zeros_like(dk_ref)
            dv_ref[...] = jnp.zeros_like(dv_ref)
            db_ref[...] = jnp.zeros_like(db_ref)

        start = pl.multiple_of(j * rows, rows)
        qb = q_ref[0]
        kb = k_ref[0, pl.ds(start, WIN), :]
        vb = v_ref[pl.ds(start, WIN), :].astype(BF16)
        p = _band_probs(qb, kb, b_ref[0], start, scale)
        dob = do_ref[...].astype(BF16)
        dv_ref[pl.ds(start, WIN), :] += lax.dot_general(p.astype(BF16), dob, _TN, preferred_element_type=F32)
        dp = lax.dot_general(dob, vb, _NT, preferred_element_type=F32)
        ds = p * (dp - jnp.sum(p * dp, axis=-1, keepdims=True))
        db_ref[0] += ds
        dsb = (ds * scale).astype(BF16)
        dq_ref[0] = jnp.dot(dsb, kb, preferred_element_type=F32)
        dk_ref[0, pl.ds(start, WIN), :] += lax.dot_general(dsb, qb, _TN, preferred_element_type=F32)

    blk_b = pl.BlockSpec((1, rows, WIN), lambda h, j: (h, 0, 0))
    return pl.pallas_call(
        body, name=name,
        out_shape=(jax.ShapeDtypeStruct((H, S, LANE), F32), jax.ShapeDtypeStruct((H, Sp, LANE), F32),
                   jax.ShapeDtypeStruct((Sp, H * LANE), F32), jax.ShapeDtypeStruct((H, rows, WIN), F32)),
        grid=(H, S // rows),
        in_specs=[pl.BlockSpec((1, rows, LANE), lambda h, j: (h, j, 0)), pl.BlockSpec((1, Sp, LANE), lambda h, j: (h, 0, 0)),
                  pl.BlockSpec((Sp, LANE), lambda h, j: (0, h)), blk_b, pl.BlockSpec((rows, LANE), lambda h, j: (j, h))],
        out_specs=(pl.BlockSpec((1, rows, LANE), lambda h, j: (h, j, 0)), pl.BlockSpec((1, Sp, LANE), lambda h, j: (h, 0, 0)),
                   pl.BlockSpec((Sp, LANE), lambda h, j: (0, h)), blk_b),
        compiler_params=_params("parallel", "arbitrary"),
    )(q, kp, vp, bias, do)


def band_bias(rel_bias, *, name):
    H, rows, wide = rel_bias.shape[0], CPB * CHUNK, 1024
    last = rel_bias[:, 2 * REL_CLIP:]
    row0 = jnp.concatenate([jnp.tile(last, (1, PAD - REL_CLIP)), rel_bias[:, CHUNK + 1:][:, ::-1],
                            jnp.tile(last, (1, wide - BAND))], axis=1)

    def body(r_ref, o_ref):
        skew = pltpu.roll(jnp.broadcast_to(r_ref[0], (rows, wide)), 0, 1, stride=1, stride_axis=0)[:, :WIN]
        first = (lax.broadcasted_iota(jnp.int32, (rows, WIN), 0) >> 6) * CHUNK
        col = lax.broadcasted_iota(jnp.int32, (rows, WIN), 1)
        o_ref[0] = jnp.where(jnp.logical_and(col >= first, col < first + BAND), skew, NEG)

    return pl.pallas_call(
        body, name=name, out_shape=jax.ShapeDtypeStruct((H, rows, WIN), F32), grid=(H,),
        in_specs=[pl.BlockSpec((1, 1, wide), lambda h: (h, 0, 0))], out_specs=pl.BlockSpec((1, rows, WIN), lambda h: (h, 0, 0)),
        compiler_params=_params("parallel"),
    )(row0.reshape(H, 1, wide))


def relbias_bwd(dbias, *, name):
    H, W = dbias.shape[0], BANDW
    x = jnp.pad(dbias[:, :, ::-1], ((0, 0), (0, 0), (0, CHUNK)))

    def body(x_ref, o_ref):
        skew = pltpu.roll(x_ref[0], 0, 1, stride=1, stride_axis=0)
        f = jnp.broadcast_to(jnp.sum(skew, axis=0, keepdims=True), (8, W))
        lane = lax.broadcasted_iota(jnp.int32, (8, W), 1)
        direct = jnp.where(jnp.logical_and(lane >= 65, lane <= 255), pltpu.roll(f, 65, 1), 0.0)
        tail = jnp.sum(jnp.where(lane >= 191, f, 0.0), axis=-1, keepdims=True)
        o_ref[0] = direct + jnp.where(lane == 2 * REL_CLIP, tail, 0.0)

    out = pl.pallas_call(
        body, name=name, out_shape=jax.ShapeDtypeStruct((H, 8, W), F32), grid=(H,),
        in_specs=[pl.BlockSpec((1, CHUNK, W), lambda h: (h, 0, 0))], out_specs=pl.BlockSpec((1, 8, W), lambda h: (h, 0, 0)),
        compiler_params=_params("parallel"),
    )(x)
    return out[:, 0, :2 * REL_CLIP + 1]


def _split_dot(x, u, dn):
    hi = x.astype(BF16)
    r1 = x - hi.astype(F32)
    mid = r1.astype(BF16)
    lo = (r1 - mid.astype(F32)).astype(BF16)
    d = lambda t: lax.dot_general(t, u, dn, preferred_element_type=F32)
    return d(hi) + d(mid) + d(lo)


def _upper_ones(S):
    return (np.arange(S)[:, None] <= np.arange(S)[None, :]).astype(np.float32)


def foxgate_fwd(fl, b, *, name):
    H, S = fl.shape
    u = jnp.asarray(_upper_ones(S), BF16)

    def body(f_ref, b_ref, u_ref, o_ref):
        x = f_ref[...] + b_ref[...]
        lf = jnp.minimum(x, 0.0) - jnp.log(1.0 + jnp.exp(-jnp.abs(x)))
        o_ref[...] = _split_dot(lf, u_ref[...], (((1,), (0,)), ((), ())))

    return pl.pallas_call(body, name=name, out_shape=jax.ShapeDtypeStruct((H, S), F32),
                          compiler_params=pltpu.CompilerParams(vmem_limit_bytes=VMEM_LIMIT_BYTES))(fl, b.reshape(H, 1), u)


def foxgate_bwd(fl, b, dcum, *, name):
    H, S = fl.shape
    u = jnp.asarray(_upper_ones(S), BF16)

    def body(f_ref, b_ref, u_ref, dc_ref, df_ref, db_ref):
        x = f_ref[...] + b_ref[...]
        dlf = _split_dot(dc_ref[...], u_ref[...], _NT)
        df = dlf * (1.0 / (1.0 + jnp.exp(x)))
        df_ref[...] = df
        db_ref[...] = jnp.sum(df, axis=-1, keepdims=True)

    df, db = pl.pallas_call(body, name=name,
                            out_shape=(jax.ShapeDtypeStruct((H, S), F32), jax.ShapeDtypeStruct((H, 1), F32)),
                            compiler_params=pltpu.CompilerParams(vmem_limit_bytes=VMEM_LIMIT_BYTES))(fl, b.reshape(H, 1), u, dcum)
    return df, db.reshape(H)


def gate_fwd(z, proj, *, name, ts=256, tc=512):
    S, D = proj[0].shape
    ts, gb, nb = _tile(S, ts), Z_GATE // tc, D // tc

    def body(g0, g1, g2, p0, p1, p2, o_ref):
        acc = None
        for g_ref, p_ref in zip((g0, g1, g2), (p0, p1, p2)):
            t = (1.0 / (1.0 + jnp.exp(-g_ref[...]))) * p_ref[...]
            acc = t if acc is None else acc + t
        o_ref[...] = acc.astype(o_ref.dtype)

    blk = pl.BlockSpec((ts, tc), lambda i, j: (i, j))
    return pl.pallas_call(
        body, name=name, out_shape=jax.ShapeDtypeStruct((S, D), BF16), grid=(S // ts, nb),
        in_specs=[pl.BlockSpec((ts, tc), lambda i, j, n=n: (i, gb + n * nb + j)) for n in range(3)] + [blk] * 3,
        out_specs=blk, compiler_params=_params("parallel", "parallel"),
    )(z, z, z, *proj)


def gate_bwd(z, proj, dm, *, name, ts=256, tc=512):
    S, D = proj[0].shape
    ts, gb, nb = _tile(S, ts), Z_GATE // tc, D // tc

    def body(g0, g1, g2, p0, p1, p2, dm_ref, *outs):
        dmv = dm_ref[...]
        for n, (g_ref, p_ref) in enumerate(zip((g0, g1, g2), (p0, p1, p2))):
            sg = 1.0 / (1.0 + jnp.exp(-g_ref[...]))
            outs[n][...] = (dmv * sg).astype(BF16)
            outs[3 + n][...] = (dmv * p_ref[...] * sg * (1.0 - sg)).astype(BF16)

    blk = pl.BlockSpec((ts, tc), lambda i, j: (i, j))
    outs = pl.pallas_call(
        body, name=name, out_shape=tuple(jax.ShapeDtypeStruct((S, D), BF16) for _ in range(6)), grid=(S // ts, nb),
        in_specs=[pl.BlockSpec((ts, tc), lambda i, j, n=n: (i, gb + n * nb + j)) for n in range(3)] + [blk] * 4,
        out_specs=(blk,) * 6, compiler_params=_params("parallel", "parallel"),
    )(z, z, z, *proj, dm)
    return outs[:3], outs[3:]


def loss_head(y, target, *, name, ts=256):
    S, D = y.shape
    ts = _tile(S, ts)

    def body(y_ref, t_ref, l_ref, dy_ref):
        err = y_ref[...] - t_ref[...]
        dy_ref[...] = err * (1.0 / D)
        part = 0.5 * jnp.sum(jnp.mean(err * err, axis=-1, keepdims=True), axis=0, keepdims=True)

        @pl.when(pl.program_id(0) == 0)
        def _():
            l_ref[...] = part

        @pl.when(pl.program_id(0) > 0)
        def _():
            l_ref[...] += part

    blk = pl.BlockSpec((ts, D), lambda i: (i, 0))
    return pl.pallas_call(
        body, name=name, out_shape=(jax.ShapeDtypeStruct((1, 1), F32), jax.ShapeDtypeStruct((S, D), F32)), grid=(S // ts,),
        in_specs=[blk, blk], out_specs=(pl.BlockSpec((1, 1), lambda i: (0, 0)), blk), compiler_params=_params("arbitrary"),
    )(y, target)


def adamw(w, g, m, v, *, name):
    shape = w.shape
    C = shape[-1]
    R = int(np.prod(shape[:-1]))
    br = R
    while br % 16 == 0 and br * C * 4 > 2**20:
        br //= 2
    w2, g2, m2, v2 = (t.reshape(R, C) for t in (w, g, m, v))

    def body(w_ref, g_ref, m_ref, v_ref, d_ref, nm_ref, nv_ref):
        d_ref[...], nm_ref[...], nv_ref[...] = _adamw_update(w_ref[...], g_ref[...], m_ref[...], v_ref[...])

    blk = pl.BlockSpec((br, C), lambda i: (i, 0))
    outs = pl.pallas_call(
        body, name=name, out_shape=tuple(jax.ShapeDtypeStruct((R, C), F32) for _ in range(3)), grid=(R // br,),
        in_specs=[blk] * 4, out_specs=(blk,) * 3, compiler_params=_params("parallel"),
    )(w2, g2, m2, v2)
    return tuple(o.reshape(shape) for o in outs)


_ANY = pl.BlockSpec(memory_space=pl.ANY)


def _place():
    return lax.axis_index("x"), lax.axis_index("y"), lax.axis_index("c")


def all_gather(xs, after, *, name):
    n = len(xs)

    def body(*refs):
        x_refs, o_refs = refs[:n], refs[n + 1:2 * n + 1]
        send_sems, recv_sems, local_sems = refs[2 * n + 1:]
        px, py, pc = _place()
        me, sibling = (px, py, pc), (px, py, 1 - pc)
        chips = [(1 - px, py), (px, 1 - py), (1 - px, 1 - py)]

        def slot(t, dev):
            return o_refs[t].at[4 * dev[0] + 2 * dev[1] + dev[2]]

        def copy(t, k, block, to, src=None):
            return pltpu.make_async_remote_copy(
                src_ref=slot(t, block) if src is None else src, dst_ref=slot(t, block),
                send_sem=send_sems.at[t, k], recv_sem=recv_sems.at[t, k], device_id=to, device_id_type=MESH)

        mine = [pltpu.make_async_copy(x_refs[t], slot(t, me), local_sems.at[t]) for t in range(n)]
        first = []
        for t in range(n):
            mine[t].start()
            first += [copy(t, 1 + j, me, (*chip, pc), src=x_refs[t]) for j, chip in enumerate(chips)]
            first.append(copy(t, 0, me, sibling, src=x_refs[t]))
        for cp in first:
            cp.start()
        passed = []
        for t in range(n):
            for j, chip in enumerate(chips):
                copy(t, 1 + j, (*chip, pc), me).wait_recv()
                fwd = copy(t, 4 + j, (*chip, pc), sibling)
                fwd.start()
                passed.append(fwd)
        for t in range(n):
            copy(t, 0, sibling, me).wait_recv()
            for j, chip in enumerate(chips):
                copy(t, 4 + j, (*chip, 1 - pc), me).wait_recv()
        for cp in first + passed:
            cp.wait_send()
        for cp in mine:
            cp.wait()

    return pl.pallas_call(
        body, name=name, out_shape=tuple(jax.ShapeDtypeStruct((8,) + x.shape, x.dtype) for x in xs),
        in_specs=[_ANY] * (n + 1), out_specs=(_ANY,) * n,
        scratch_shapes=[pltpu.SemaphoreType.DMA((n, 7)), pltpu.SemaphoreType.DMA((n, 7)), pltpu.SemaphoreType.DMA((n,))],
    )(*xs, after)


def forward_to_sibling(zones, *, name):
    n = len(zones)

    def body(*refs):
        z_refs, (send_sems, recv_sems) = refs[n:2 * n], refs[2 * n:]
        px, py, pc = _place()
        copies = []
        for t in range(n):
            for j, chip in enumerate([(1 - px, py), (px, 1 - py), (1 - px, 1 - py)]):
                slot = z_refs[t].at[4 * chip[0] + 2 * chip[1] + pc]
                copies.append(pltpu.make_async_remote_copy(
                    src_ref=slot, dst_ref=slot, send_sem=send_sems.at[t, j], recv_sem=recv_sems.at[t, j],
                    device_id=(px, py, 1 - pc), device_id_type=MESH))
        for cp in copies:
            cp.start()
        for cp in copies:
            cp.wait()

    return pl.pallas_call(
        body, name=name, out_shape=tuple(jax.ShapeDtypeStruct(z.shape, z.dtype) for z in zones),
        in_specs=[_ANY] * n, out_specs=(_ANY,) * n, input_output_aliases={t: t for t in range(n)},
        scratch_shapes=[pltpu.SemaphoreType.DMA((n, 3)), pltpu.SemaphoreType.DMA((n, 3))],
    )(*zones)


_HBM = pl.BlockSpec(memory_space=pltpu.HBM)
_SEM = pl.BlockSpec(memory_space=pltpu.SEMAPHORE)
_EFFECT = pltpu.SideEffectType.DATAFLOW_SIDE_EFFECTING


N_PEERS = 7
GATHER_FLIPS = (1, 2, 4, 6)


def _peers(flips=range(1, N_PEERS + 1)):
    px, py, pc = _place()
    flip = lambda p, bit: 1 - p if bit else p
    return [(flip(px, m >> 2 & 1), flip(py, m >> 1 & 1), flip(pc, m & 1)) for m in flips]


def _gather_copies(src_refs, zone_refs, send_sems, recv_sems):
    px, py, pc = _place()
    return [pltpu.make_async_remote_copy(src_ref=s, dst_ref=z.at[4 * px + 2 * py + pc], send_sem=send_sems.at[k],
                                         recv_sem=recv_sems.at[k], device_id=peer, device_id_type=MESH)
            for k, peer in enumerate(_peers(GATHER_FLIPS)) for s, z in zip(src_refs, zone_refs)]


def _scatter_copies(part_refs, zone_refs, send_sems, recv_sems):
    return [pltpu.make_async_remote_copy(src_ref=p.at[4 * peer[0] + 2 * peer[1] + peer[2]], dst_ref=z.at[k],
                                         send_sem=send_sems.at[k], recv_sem=recv_sems.at[k], device_id=peer, device_id_type=MESH)
            for k, peer in enumerate(_peers()) for p, z in zip(part_refs, zone_refs)]


def copies_start(make, srcs, zones, *, name, after=None):
    n = len(srcs)
    extra = [] if after is None else [after]

    def body(*refs):
        k = 2 * n + len(extra)
        for cp in make(refs[:n], refs[n:2 * n], refs[k], refs[k + 1]):
            cp.start()
        refs[-1][...] = jnp.zeros_like(refs[-1])

    arrays = list(srcs) + list(zones)
    outs = pl.pallas_call(
        body, name=name,
        out_shape=(pltpu.SemaphoreType.DMA((N_PEERS,)), pltpu.SemaphoreType.DMA((N_PEERS,)),
                   *[pltpu.HBM(a.shape, a.dtype) for a in arrays], jax.ShapeDtypeStruct((8, LANE), F32)),
        in_specs=[_HBM] * (2 * n) + [_ANY] * len(extra),
        out_specs=(_SEM, _SEM, *[_HBM] * (2 * n), pl.BlockSpec(memory_space=pltpu.VMEM)),
        input_output_aliases={i: 2 + i for i in range(2 * n)},
        compiler_params=pltpu.CompilerParams(has_side_effects=_EFFECT),
    )(*[pltpu.with_memory_space_constraint(a, pltpu.HBM) for a in arrays], *extra)
    return outs[0], outs[1], list(outs[2:2 + n]), list(outs[2 + n:2 + 2 * n]), outs[-1]


def copies_wait(make, send_sems, recv_sems, srcs, zones, after, *, name):
    n = len(srcs)

    def body(*refs):
        for cp in make(refs[:n], refs[n:2 * n], refs[2 * n], refs[2 * n + 1]):
            cp.wait_send()
            cp.wait_recv()

    arrays = list(srcs) + list(zones)
    outs = pl.pallas_call(
        body, name=name, out_shape=tuple(pltpu.HBM(a.shape, a.dtype) for a in arrays),
        in_specs=[_HBM] * (2 * n) + [_SEM, _SEM, _ANY], out_specs=(_HBM,) * (2 * n),
        input_output_aliases={i: i for i in range(2 * n)},
        compiler_params=pltpu.CompilerParams(has_side_effects=_EFFECT),
    )(*arrays, send_sems, recv_sems, after)
    return list(outs[:n]), list(outs[n:])


def place_mine(zone, mine, *, name):
    C = mine.shape[-1]
    R = int(np.prod(mine.shape[:-1]))
    br = _row_block(R, C, mine.dtype.itemsize, budget=2**21)
    me = (4 * lax.axis_index("x") + 2 * lax.axis_index("y") + lax.axis_index("c")).astype(jnp.int32).reshape(1)

    def body(me_ref, m_ref, z_ref, o_ref):
        o_ref[0] = m_ref[...]

    out = pl.pallas_call(
        body, name=name, out_shape=jax.ShapeDtypeStruct((8, R, C), zone.dtype),
        grid_spec=pltpu.PrefetchScalarGridSpec(
            num_scalar_prefetch=1, grid=(R // br,), in_specs=[pl.BlockSpec((br, C), lambda i, me: (i, 0)), _ANY],
            out_specs=pl.BlockSpec((1, br, C), lambda i, me: (me[0], i, 0))),
        input_output_aliases={2: 0}, compiler_params=_params("parallel"),
    )(me, mine.reshape(R, C), zone.reshape(8, R, C))
    return out.reshape(zone.shape)


def _row_block(rows, cols, itemsize, budget=2**20):
    br = rows
    while br % 32 == 0 and br * cols * itemsize > budget:
        br //= 2
    return br


def _adamw_update(w, g, m, v):
    nm = ADAM_B1 * m + (1.0 - ADAM_B1) * g
    nv = ADAM_B2 * v + (1.0 - ADAM_B2) * (g * g)
    m_hat = nm / (1.0 - ADAM_B1 ** ADAM_STEP)
    v_hat = nv / (1.0 - ADAM_B2 ** ADAM_STEP)
    return -ADAM_LR * (m_hat / (jnp.sqrt(v_hat) + ADAM_EPS) + ADAM_WD * w), nm, nv


def grad_sum_adamw(parts, recvs, w, m, v, *, name):
    L, C = len(parts), w.shape[-1]
    R = int(np.prod(w.shape[1:-1]))
    br = _row_block(R, C, 4, budget=2**19)
    chip = (4 * lax.axis_index("x") + 2 * lax.axis_index("y") + lax.axis_index("c")).astype(jnp.int32).reshape(1)

    def body(c_ref, *refs):
        p_refs, r_refs = refs[:L], refs[L:2 * L]
        w_ref, m_ref, v_ref, g_out, d_out, nm_out, nv_out = refs[2 * L:]
        for j in range(L):
            @pl.when(pl.program_id(0) == j)
            def _(j=j):
                g = p_refs[j][0].astype(F32)
                for k in range(N_PEERS):
                    g = g + r_refs[j][k].astype(F32)
                g_out[0] = g
                d_out[0], nm_out[0], nv_out[0] = _adamw_update(w_ref[0], g, m_ref[0], v_ref[0])

    row = lambda j: (lambda l, r, c: jnp.where(l == j, r, 0))
    part_specs = [pl.BlockSpec((1, br, C), lambda l, r, c, f=row(j): (c[0], f(l, r, c), 0)) for j in range(L)]
    recv_specs = [pl.BlockSpec((N_PEERS, br, C), lambda l, r, c, f=row(j): (0, f(l, r, c), 0)) for j in range(L)]
    blk = pl.BlockSpec((1, br, C), lambda l, r, c: (l, r, 0))
    outs = pl.pallas_call(
        body, name=name, out_shape=tuple(jax.ShapeDtypeStruct((L, R, C), F32) for _ in range(4)),
        grid_spec=pltpu.PrefetchScalarGridSpec(
            num_scalar_prefetch=1, grid=(L, R // br), in_specs=part_specs + recv_specs + [blk] * 3, out_specs=(blk,) * 4),
        compiler_params=_params("arbitrary", "arbitrary"),
    )(chip, *[p.reshape(8, R, C) for p in parts], *[r.reshape(N_PEERS, R, C) for r in recvs],
      *[t.reshape(L, R, C) for t in (w, m, v)])
    return tuple(o.reshape(w.shape) for o in outs)


def ordered_sum(parts, *, name):
    _, R, C = parts.shape

    def body(p_ref, o_ref):
        acc = p_ref[0]
        for d in range(1, 8):
            acc = acc + p_ref[d]
        o_ref[...] = acc

    return pl.pallas_call(body, name=name, out_shape=jax.ShapeDtypeStruct((R, C), F32))(parts)


W_IN_COLS, W_IN_SHARD = 13128, 1641
W_IN_SEGMENTS = ((0, 832, 0), (832, 3904, Z_FOX), (3904, 3912, FF_COL), (3912, 6984, Z_CH), (6984, 13128, Z_GATE))


def col_gather(src, table, pieces, out_shape, *, name, tr=2048, after=None):
    R, C = src.shape[1:]
    tr = _tile(R, tr)
    width = 2 + 6 * pieces
    nb = table.shape[0] // width
    last_tile, last_valid = C // LANE, C % LANE

    extra = [] if after is None else [after]

    def body(tab, *refs):
        o_ref, acc_ref = refs[-2:]
        base = pl.program_id(1) * width
        lane = lax.broadcasted_iota(jnp.int32, (tr, LANE), 1)
        row = lax.broadcasted_iota(jnp.int32, (2 * LANE, LANE), 0)
        col = lax.broadcasted_iota(jnp.int32, (2 * LANE, LANE), 1)
        acc_ref[...] = jnp.zeros_like(acc_ref)
        for p in range(pieces):
            e = base + 2 + 6 * p
            lo, hi = tab[e + 4], tab[e + 5]

            @pl.when(hi > lo)
            def _(p=p, e=e, lo=lo, hi=hi):
                tiles = []
                for tcol in (1, 2):
                    x = refs[2 * p + tcol - 1][0]
                    if last_valid:
                        x = jnp.where(jnp.logical_or(tab[e + tcol] < last_tile, lane < last_valid), x, jnp.zeros_like(x))
                    tiles.append(x)
                hit = jnp.logical_and(row == col + tab[e + 3], jnp.logical_and(col >= lo, col < hi))
                sel = jnp.where(hit, 1.0, 0.0).astype(src.dtype)
                acc_ref[...] += jnp.dot(jnp.concatenate(tiles, axis=1), sel, preferred_element_type=F32)
        o_ref[0] = acc_ref[...].astype(o_ref.dtype)

    in_specs = []
    for p in range(pieces):
        for tcol in (1, 2):
            in_specs.append(pl.BlockSpec(
                (1, tr, LANE), lambda i, b, tab, p=p, tcol=tcol: (tab[b * width + 2 + 6 * p], i, tab[b * width + 2 + 6 * p + tcol])))
    return pl.pallas_call(
        body, name=name, out_shape=jax.ShapeDtypeStruct(out_shape, src.dtype),
        grid_spec=pltpu.PrefetchScalarGridSpec(
            num_scalar_prefetch=1, grid=(R // tr, nb), in_specs=in_specs + [_ANY] * len(extra),
            out_specs=pl.BlockSpec((1, tr, LANE), lambda i, b, tab: (tab[b * width], i, tab[b * width + 1])),
            scratch_shapes=[pltpu.VMEM((tr, LANE), F32)]),
        compiler_params=_params("parallel", "parallel"),
    )(jnp.asarray(table, jnp.int32), *([src] * (2 * pieces)), *extra)


def _piece(sd, start, lo, hi, last_tile):
    t0 = start // LANE
    return [sd, t0, min(t0 + 1, last_tile), start % LANE - lo, lo, hi]


def _pad_pieces(rows, pieces):
    out, prev = [], [0, 0, 0, 0, 0, 0] * pieces
    for head, pcs in rows:
        full = list(pcs)
        for p in range(len(pcs) // 6, pieces):
            full += prev[6 * p:6 * p + 3] + [0, 0, 0]
        out.append(head + full)
        prev = full
    return np.asarray(out, np.int32).reshape(-1)


def _w_in_table(layer, L):
    rows = []
    for b in range(Z_W // LANE):
        pcs = []
        for first, last, col in W_IN_SEGMENTS:
            lo, hi = max(LANE * b, col), min(LANE * (b + 1), col + last - first)
            while lo < hi:
                c = first + lo - col
                n = min(hi - lo, W_IN_SHARD - c % W_IN_SHARD)
                pcs += _piece((c // W_IN_SHARD) * L + layer, c % W_IN_SHARD, lo - LANE * b, lo - LANE * b + n, W_IN_SHARD // LANE)
                lo += n
        assert len(pcs) <= 12
        rows.append(([0, b], pcs))
    return _pad_pieces(rows, 2)


def _w_in_grad_table():
    rows = []
    for d in range(8):
        for t in range(-(-W_IN_SHARD // LANE)):
            pcs = []
            c0 = d * W_IN_SHARD + LANE * t
            c1 = min(c0 + LANE, (d + 1) * W_IN_SHARD)
            for first, last, col in W_IN_SEGMENTS:
                lo, hi = max(c0, first), min(c1, last)
                if lo < hi:
                    pcs += _piece(0, col + lo - first, lo - c0, hi - c0, Z_W // LANE - 1)
            assert len(pcs) <= 18
            rows.append(([d, t], pcs))
    return _pad_pieces(rows, 3)


def block_copy(src, out_shape, in_blk, out_blk, grid, in_map, out_map, *, name):
    def body(x_ref, o_ref):
        o_ref[(0,) * (len(out_blk) - 2) + (Ellipsis,)] = x_ref[(0,) * (len(in_blk) - 2) + (Ellipsis,)]

    return pl.pallas_call(
        body, name=name, out_shape=jax.ShapeDtypeStruct(out_shape, src.dtype), grid=grid,
        in_specs=[pl.BlockSpec(in_blk, in_map)], out_specs=pl.BlockSpec(out_blk, out_map),
        compiler_params=_params("parallel", "parallel"),
    )(src)


def _columns_from_owners(z, *, name, lead=()):
    K, c = z.shape[-2:]
    tr, nl = _tile(K, 1024), len(lead)
    return block_copy(z, (K, 8 * c), (1,) * (1 + nl) + (tr, c), (tr, c), (8, K // tr),
                      lambda d, i: (d, *lead, i, 0), lambda d, i: (i, d), name=name)


def _owners_from_columns(g, *, name):
    K, c = g.shape[0], g.shape[1] // 8
    tr = _tile(K, 1024)
    return block_copy(g, (8, K, c), (tr, c), (1, tr, c), (8, K // tr), lambda d, i: (i, d), lambda d, i: (d, i, 0), name=name)


def _full_from_shards(k, sh, tag):
    if k not in COL_SHARDED:
        return sh.reshape((-1, sh.shape[-1]))
    if k == 'w_br':
        return [_columns_from_owners(sh, lead=(n,), name=f"{tag}_w_br{n}_layout") for n in range(3)]
    if k == 'w_uq':
        return _columns_from_owners(jnp.pad(sh, ((0, 0), (0, 0), (0, 64))), name=f"{tag}_w_uq_layout")
    if k == 'w_ukv':
        return block_copy(sh, (256, 2048), (1, 256, LANE), (256, LANE), (2, MLA_HEADS),
                          lambda t, h: (h, 0, t), lambda t, h: (0, t * MLA_HEADS + h), name=f"{tag}_w_ukv_layout")
    return _columns_from_owners(sh, name=f"{tag}_{k}_layout")


def _shards_from_full(k, g, tag):
    if k not in COL_SHARDED:
        return g.reshape((8, g.shape[0] // 8, g.shape[1]))
    if k == 'w_br':
        return jnp.stack([_owners_from_columns(g[n], name=f"{tag}_dw_br{n}_layout") for n in range(3)], axis=1)
    if k == 'w_uq':
        return _owners_from_columns(g, name=f"{tag}_dw_uq_layout")[:, :, :MLA_QK]
    if k == 'w_ukv':
        return block_copy(g, (8, 256, 256), (256, LANE), (1, 256, LANE), (2, MLA_HEADS),
                          lambda t, h: (0, t * MLA_HEADS + h), lambda t, h: (h, 0, t), name=f"{tag}_dw_ukv_layout")
    return _owners_from_columns(g, name=f"{tag}_d{k}_layout")


def w_in_full(gathered, layer, *, name, after=None):
    _, L, K, c = gathered.shape
    return col_gather(gathered.reshape(8 * L, K, c), _w_in_table(layer, L), 2, (1, K, Z_W), name=name, after=after)[0]


def w_in_shards(g, *, name):
    return col_gather(g[None], _w_in_grad_table(), 3, (8, g.shape[0], W_IN_SHARD), name=name)


def _layer_fwd(x, mem, W, P, cos, ssin, tag, later=None):
    S = x.shape[0]
    sv = {'x0': x}
    h = rmsnorm_fwd(x, P['g_mix'], name=f"{tag}_norm_mix")
    z = mm(h, W['w_in'], name=f"{tag}_mm_in")
    if later is not None:
        W = later(z)
    sv.update(h=h, z=z)
    cqn = rmsnorm_fwd(z, P['g_cq'], col=0, width=512, name=f"{tag}_norm_cq")
    ckvn = rmsnorm_fwd(z, P['g_ckv'], col=512, width=256, name=f"{tag}_norm_ckv")
    qf = mm(cqn, W['w_uq'], name=f"{tag}_mm_uq")
    kvf = mm(ckvn, W['w_ukv'], name=f"{tag}_mm_ukv")
    qa = mla_prep_fwd(qf, qf, P['g_mla_q'], cos, ssin, n_col=0, n_stride=2 * LANE, r_col=LANE, r_stride=2 * LANE,
                      heads=8, name=f"{tag}_mla_q")
    ka = mla_prep_fwd(kvf, z, P['g_mla_k'], cos, ssin, n_col=0, n_stride=LANE, r_col=KR_COL, r_stride=0,
                      heads=8, name=f"{tag}_mla_k")
    ya, lse_a = causal_attn_fwd(qa, ka, kvf, v_col=1024, chunked=True, scale=MLA_QK ** -0.5, name=f"{tag}_mla_attn")
    sv.update(cqn=cqn, ckvn=ckvn, qf=qf, kvf=kvf, qa=qa, ka=ka, lse_a=lse_a)
    qb = headnorm_fwd(z, P['g_fox_q'], col=Z_FOX, heads=8, name=f"{tag}_fox_qn")
    kb = headnorm_fwd(z, P['g_fox_k'], col=Z_FOX + 1024, heads=8, name=f"{tag}_fox_kn")
    fl = z[:, FF_COL:FF_COL + 8].T
    cum = foxgate_fwd(fl, P['b_f'], name=f"{tag}_fox_gate")
    cq, ck = cum.reshape(8, S, 1), cum.reshape(8, 1, S)
    yb, lse_b = causal_attn_fwd(qb, kb, z, v_col=Z_FOX + 2048, chunked=False, scale=LANE ** -0.5, cq=cq, ck=ck,
                                name=f"{tag}_fox_attn")
    sv.update(qb=qb, kb=kb, fl=fl, cq=cq, ck=ck, lse_b=lse_b)
    qc = headnorm_fwd(z, P['g_ch_q'], col=Z_CH, heads=8, name=f"{tag}_ch_qn")
    kc = headnorm_fwd(z, P['g_ch_k'], col=Z_CH + 1024, heads=8, name=f"{tag}_ch_kn")
    kcp = jnp.pad(kc, ((0, 0), (PAD, CHUNK), (0, 0)))
    vcp = jnp.pad(z[:, Z_CH + 2048:Z_CH + 3072], ((PAD, CHUNK), (0, 0)))
    bias = band_bias(P['rel_bias'], name=f"{tag}_ch_bias")
    yc = band_fwd(qc, kcp, vcp, bias, scale=LANE ** -0.5, name=f"{tag}_ch_attn")
    sv.update(qc=qc, kcp=kcp, vcp=vcp, bias=bias)
    ys = (ya, yb, yc)
    proj = [mm(ys[n], W['w_br'][n], name=f"{tag}_mm_br{n}") for n in range(3)]
    merged = gate_fwd(z, proj, name=f"{tag}_gate")
    x1 = mm(merged, W['w_out'], epi='add', aux=x, name=f"{tag}_mm_out")
    sv.update(ys=ys, proj=proj, merged=merged, x1=x1)
    hc = rmsnorm_fwd(x1, P['g_cross'], name=f"{tag}_norm_cross")
    memn = rmsnorm_fwd(mem, P['g_mem'], name=f"{tag}_norm_mem")
    qx_raw = mm(hc, W['w_xq'], name=f"{tag}_mm_xq")
    memkv = mm(memn, W['w_xkv'], name=f"{tag}_mm_xkv")
    qx = headnorm_fwd(qx_raw, P['g_x_q'], col=0, heads=4, name=f"{tag}_x_qn")
    kx = headnorm_fwd(memkv, P['g_x_k'], col=0, heads=4, name=f"{tag}_x_kn")
    ox = attn_fwd(qx, kx, memkv, v_col=512, scale=LANE ** -0.5, name=f"{tag}_x_attn")
    x2 = mm(ox, W['w_xo'], epi='add', aux=x1, name=f"{tag}_mm_xo")
    sv.update(hc=hc, memn=memn, qx_raw=qx_raw, memkv=memkv, qx=qx, kx=kx, ox=ox, x2=x2)
    hm = rmsnorm_fwd(x2, P['g_mlp'], name=f"{tag}_norm_mlp")
    u, a = mm(hm, W['w_1'], epi='relu2', out_dtype=BF16, name=f"{tag}_mm_w1")
    x3 = mm(a, W['w_2'], epi='add', aux=x2, name=f"{tag}_mm_w2")
    sv.update(hm=hm, u=u, a=a)
    return x3, sv


def _layer_bwd(dx, mem, W, P, sv, cos, ssin, tag, send_off=None):
    S = dx.shape[0]
    z = sv['z']
    gw, gs = {}, {}
    wgrad = lambda a, d, name: mm(a, d, ta=True, out_dtype=BF16, name=name)
    gw['w_2'] = wgrad(sv['a'], dx, f"{tag}_dw2")
    du = mm(dx, W['w_2'], tb=True, epi='mul_drelu2', aux=sv['u'], out_dtype=BF16, name=f"{tag}_du")
    gw['w_1'] = wgrad(sv['hm'], du, f"{tag}_dw1")
    dhm = mm(du, W['w_1'], tb=True, name=f"{tag}_dhm")
    dx, gs['g_mlp'] = rmsnorm_bwd(sv['x2'], P['g_mlp'], dhm, res=dx, name=f"{tag}_dnorm_mlp")
    gw['w_xo'] = wgrad(sv['ox'], dx, f"{tag}_dwxo")
    dox = mm(dx, W['w_xo'], tb=True, out_dtype=BF16, name=f"{tag}_dox")
    dqx, dkx, dvx = attn_bwd(sv['qx'], sv['kx'], sv['memkv'], dox, v_col=512, scale=LANE ** -0.5, name=f"{tag}_x_attn_bwd")
    dqx_raw, gs['g_x_q'] = headnorm_bwd(sv['qx_raw'], P['g_x_q'], dqx, col=0, heads=4, name=f"{tag}_x_qn_bwd")
    dkx_raw, gs['g_x_k'] = headnorm_bwd(sv['memkv'], P['g_x_k'], dkx, col=0, heads=4, name=f"{tag}_x_kn_bwd")
    dqx_b = dqx_raw.astype(BF16)
    gw['w_xq'] = wgrad(sv['hc'], dqx_b, f"{tag}_dwxq")
    dhc = mm(dqx_b, W['w_xq'], tb=True, name=f"{tag}_dhc")
    dx, gs['g_cross'] = rmsnorm_bwd(sv['x1'], P['g_cross'], dhc, res=dx, name=f"{tag}_dnorm_cross")
    dmemkv = jnp.concatenate([dkx_raw, dvx], axis=1).astype(BF16)
    gw['w_xkv'] = wgrad(sv['memn'], dmemkv, f"{tag}_dwxkv")
    dmemn = mm(dmemkv, W['w_xkv'], tb=True, name=f"{tag}_dmemn")
    _, gs['g_mem'] = rmsnorm_bwd(mem, P['g_mem'], dmemn, name=f"{tag}_dnorm_mem")
    gw['w_out'] = wgrad(sv['merged'], dx, f"{tag}_dwout")
    dmerged = mm(dx, W['w_out'], tb=True, name=f"{tag}_dmerged")
    dproj, dgl = gate_bwd(z, sv['proj'], dmerged, name=f"{tag}_gate_bwd")
    gw['w_br'] = [wgrad(sv['ys'][n], dproj[n], f"{tag}_dwbr{n}") for n in range(3)]
    dys = [mm(dproj[n], W['w_br'][n], tb=True, out_dtype=BF16, name=f"{tag}_dys{n}") for n in range(3)]
    dqa, dka, dva = causal_attn_bwd(sv['qa'], sv['ka'], sv['kvf'], sv['ys'][0], dys[0], sv['lse_a'], v_col=1024, chunked=True,
                                    scale=MLA_QK ** -0.5, name=f"{tag}_mla_attn_bwd")
    g_mla_q = P['g_mla_q'] if send_off is None else P['g_mla_q'] + send_off[0](gw)
    dqn, dqr, gs['g_mla_q'] = mla_prep_bwd(sv['qf'], sv['qf'], g_mla_q, cos, ssin, dqa, n_col=0, n_stride=2 * LANE,
                                           r_col=LANE, r_stride=2 * LANE, heads=8, name=f"{tag}_mla_q_bwd")
    dkn, dkr, gs['g_mla_k'] = mla_prep_bwd(sv['kvf'], z, P['g_mla_k'], cos, ssin, dka, n_col=0, n_stride=LANE,
                                           r_col=KR_COL, r_stride=0, heads=8, name=f"{tag}_mla_k_bwd")
    dqf = jnp.stack([dqn.reshape(S, 8, LANE), dqr.reshape(S, 8, LANE)], axis=2).reshape(S, 2048).astype(BF16)
    dkvf = jnp.concatenate([dkn, dva], axis=1).astype(BF16)
    gw['w_uq'] = wgrad(sv['cqn'], dqf, f"{tag}_dwuq")
    gw['w_ukv'] = wgrad(sv['ckvn'], dkvf, f"{tag}_dwukv")
    dcqn = mm(dqf, W['w_uq'], tb=True, name=f"{tag}_dcqn")
    dckvn = mm(dkvf, W['w_ukv'], tb=True, name=f"{tag}_dckvn")
    dcq_raw, gs['g_cq'] = rmsnorm_bwd(z, P['g_cq'], dcqn, col=0, width=512, name=f"{tag}_dnorm_cq")
    dckv_raw, gs['g_ckv'] = rmsnorm_bwd(z, P['g_ckv'], dckvn, col=512, width=256, name=f"{tag}_dnorm_ckv")
    dqb, dkb, dvb, dcq, dck = causal_attn_bwd(sv['qb'], sv['kb'], z, sv['ys'][1], dys[1], sv['lse_b'], v_col=Z_FOX + 2048,
                                              chunked=False, scale=LANE ** -0.5, cq=sv['cq'], ck=sv['ck'],
                                              name=f"{tag}_fox_attn_bwd")
    dqb_raw, gs['g_fox_q'] = headnorm_bwd(z, P['g_fox_q'], dqb, col=Z_FOX, heads=8, name=f"{tag}_fox_qn_bwd")
    dkb_raw, gs['g_fox_k'] = headnorm_bwd(z, P['g_fox_k'], dkb, col=Z_FOX + 1024, heads=8, name=f"{tag}_fox_kn_bwd")
    dfl, gs['b_f'] = foxgate_bwd(sv['fl'], P['b_f'], dcq.reshape(8, S) + dck.reshape(8, S), name=f"{tag}_fox_gate_bwd")
    dqc, dkcp, dvcp, dbias = band_bwd(sv['qc'], sv['kcp'], sv['vcp'], sv['bias'], dys[2], scale=LANE ** -0.5,
                                      name=f"{tag}_ch_attn_bwd")
    dqc_raw, gs['g_ch_q'] = headnorm_bwd(z, P['g_ch_q'], dqc, col=Z_CH, heads=8, name=f"{tag}_ch_qn_bwd")
    dkc_raw, gs['g_ch_k'] = headnorm_bwd(z, P['g_ch_k'], dkcp[:, PAD:PAD + S, :], col=Z_CH + 1024, heads=8,
                                         name=f"{tag}_ch_kn_bwd")
    gs['rel_bias'] = relbias_bwd(chunk_band_dbias(dbias), name=f"{tag}_relbias_bwd")
    b16 = lambda t: t.astype(BF16)
    dz = jnp.concatenate([b16(dcq_raw), b16(dckv_raw), b16(dkr), b16(dfl.T), jnp.zeros((S, 120), BF16),
                          b16(dqb_raw), b16(dkb_raw), b16(dvb), b16(dqc_raw), b16(dkc_raw), b16(dvcp[PAD:PAD + S]),
                          dgl[0], dgl[1], dgl[2]], axis=1)
    gw['w_in'] = wgrad(sv['h'], dz, f"{tag}_dwin")
    dh = mm(dz, W['w_in'], tb=True, name=f"{tag}_dh")
    g_mix = P['g_mix'] if send_off is None else P['g_mix'] + send_off[1](gw)
    dx, gs['g_mix'] = rmsnorm_bwd(sv['x0'], g_mix, dh, res=dx, name=f"{tag}_dnorm_mix")
    return dx, gw, gs


def _local_step(x, mem, target, Ws, Ps):
    S = x.shape[0]
    cos, ssin = _rope_tables(S)
    L = len(Ws)
    saved = []
    for l in range(L):
        x, sv = _layer_fwd(x, mem, Ws[l], Ps[l], cos, ssin, f"l{l}")
        saved.append(sv)
    loss, dx = loss_head(x, target, name="loss_head")
    gws, gss = [None] * L, [None] * L
    for l in reversed(range(L)):
        dx, gws[l], gss[l] = _layer_bwd(dx, mem, Ws[l], Ps[l], saved[l], cos, ssin, f"l{l}")
    return loss, dx, gws, gss


def _pack_small(d):
    flat = jnp.concatenate([d[k].reshape(-1) for k in SMALL])
    n = flat.shape[0]
    rows = -(-n // (8 * LANE)) * 8
    return jnp.pad(flat, (0, rows * LANE - n)).reshape(rows, LANE)


def _unpack_small(packed, like):
    flat, out, off = packed.reshape(-1), {}, 0
    for k in SMALL:
        n = int(np.prod(like[k].shape))
        out[k] = flat[off:off + n].reshape(like[k].shape)
        off += n
    return out


def kernel(x, mem, g_mix, w_in, g_cq, w_uq, g_ckv, w_ukv, g_mla_q, g_mla_k, b_f, g_fox_q, g_fox_k, rel_bias, g_ch_q, g_ch_k, w_br, w_out, g_cross, g_mem, w_xq, w_xkv, g_x_q, g_x_k, w_xo, g_mlp, w_1, w_2, loss_target, m_g_mix, m_w_in, m_g_cq, m_w_uq, m_g_ckv, m_w_ukv, m_g_mla_q, m_g_mla_k, m_b_f, m_g_fox_q, m_g_fox_k, m_rel_bias, m_g_ch_q, m_g_ch_k, m_w_br, m_w_out, m_g_cross, m_g_mem, m_w_xq, m_w_xkv, m_g_x_q, m_g_x_k, m_w_xo, m_g_mlp, m_w_1, m_w_2, v_g_mix, v_w_in, v_g_cq, v_w_uq, v_g_ckv, v_w_ukv, v_g_mla_q, v_g_mla_k, v_b_f, v_g_fox_q, v_g_fox_k, v_rel_bias, v_g_ch_q, v_g_ch_k, v_w_br, v_w_out, v_g_cross, v_g_mem, v_w_xq, v_w_xkv, v_g_x_q, v_g_x_k, v_w_xo, v_g_mlp, v_w_1, v_w_2):
    args = locals()
    w = {k: args[k] for k in WEIGHTS}
    m = {k: args['m_' + k] for k in WEIGHTS}
    v = {k: args['v_' + k] for k in WEIGHTS}
    L = w_in.shape[0]

    Ps = [{k: w[k][l] for k in SMALL} for l in range(L)]
    xs, memv = x[0], mem[0]
    cos, ssin = _rope_tables(xs.shape[0])

    gathers = {}

    cast = {(k, l): w[k][l].astype(BF16) for k in BIG for l in range(L)}
    all_cast = jnp.concatenate([s[(0,) * (s.ndim - 1)][:1] for s in cast.values()])

    def start_gather(l, group, after=None):
        shards = [cast[k, l] for k in group]
        gathers[l, group] = copies_start(_gather_copies, shards, [lax.empty((8,) + s.shape, s.dtype) for s in shards],
                                         name=f"l{l}_ag_start_{group[0]}", after=after)
        return gathers[l, group][4][0, :1]

    Ps[0]['g_mix'] = Ps[0]['g_mix'] + start_gather(0, AG_GROUPS[0])

    def arrived(l, group, after, on_landed=None):
        send_sems, recv_sems, shards, zones, _ = gathers[l, group]
        shards, zones = copies_wait(_gather_copies, send_sems, recv_sems, shards, zones, after, name=f"l{l}_ag_wait_{group[0]}")
        then = None if on_landed is None else on_landed(shards[0])
        zones = forward_to_sibling(zones, name=f"l{l}_ag_forward_{group[0]}")
        zones = [place_mine(z, s, name=f"l{l}_{k}_mine") for k, z, s in zip(group, zones, shards)]
        return {k: w_in_full(z[:, None], 0, name=f"l{l}_w_in_layout", after=then) if k == 'w_in'
                else _full_from_shards(k, z, f"l{l}") for k, z in zip(group, zones)}

    Ws, saved = [], []
    for l in range(L):
        if l == 0:
            def start_the_rest(anchor):
                rest = [(j, group) for j in range(L) for group in AG_GROUPS if (j, group) != (0, AG_GROUPS[0])]
                return sum(start_gather(j, group, after=anchor) for j, group in rest)

            Ws.append(arrived(0, AG_GROUPS[0], all_cast, start_the_rest))
        else:
            Ws.append(arrived(l, AG_GROUPS[0], xs))

        def later(anchor, l=l):
            Ws[l].update(arrived(l, AG_GROUPS[1], anchor))
            return Ws[l]

        xs, sv = _layer_fwd(xs, memv, Ws[l], Ps[l], cos, ssin, f"l{l}", later)
        saved.append(sv)

    loss, dx = loss_head(xs, loss_target[0], name="loss_head")
    loss = lax.psum(loss[0, 0], ("x", "y", "c"))

    gss = [None] * L
    scatters = {group: [None] * L for group in RS_GROUPS}
    for l in reversed(range(L)):
        def send_off(gw, group, l=l):
            gdst = [w_in_shards(gw[k], name=f"l{l}_dw_in_layout") if k == 'w_in' else _shards_from_full(k, gw[k], f"l{l}")
                    for k in group]
            started = copies_start(_scatter_copies, gdst, [lax.empty((N_PEERS,) + g.shape[1:], g.dtype) for g in gdst],
                                   name=f"l{l}_rs_start_{group[0]}")
            scatters[group][l] = started[:4]
            return started[4][0, :1]

        hooks = tuple((lambda gw, group=group: send_off(gw, group)) for group in RS_GROUPS)
        dx, _, gss[l] = _layer_bwd(dx, memv, Ws[l], Ps[l], saved[l], cos, ssin, f"l{l}", hooks)
    grad_x = dx

    grads, delta, new_m, new_v = {}, {}, {}, {}
    after = grad_x
    for group in RS_GROUPS:
        done = [copies_wait(_scatter_copies, *scatters[group][l], after, name=f"l{l}_rs_wait_{group[0]}") for l in range(L)]
        for t, k in enumerate(group):
            grads[k], delta[k], new_m[k], new_v[k] = grad_sum_adamw(
                [done[l][0][t] for l in range(L)], [done[l][1][t] for l in range(L)], w[k], m[k], v[k], name=f"adamw_{k}")
        after = sum(delta[k][(0,) * (delta[k].ndim - 1)][:1] for k in group)

    small_part = _pack_small({k: jnp.stack([gss[l][k] for l in range(L)]) for k in SMALL})
    small_all = all_gather([small_part], after, name="ag_small")[0]
    grads.update(_unpack_small(ordered_sum(small_all, name="small_sum"), {k: w[k] for k in SMALL}))
    sd, sm, sv_ = adamw(_pack_small({k: w[k] for k in SMALL}), _pack_small({k: grads[k] for k in SMALL}),
                        _pack_small({k: m[k] for k in SMALL}), _pack_small({k: v[k] for k in SMALL}), name="adamw_small")
    like = {k: w[k] for k in SMALL}
    delta.update(_unpack_small(sd, like))
    new_m.update(_unpack_small(sm, like))
    new_v.update(_unpack_small(sv_, like))

    return (loss, grad_x[None], *[grads[k] for k in WEIGHTS], *[delta[k] for k in WEIGHTS],
            *[new_m[k] for k in WEIGHTS], *[new_v[k] for k in WEIGHTS])
```
